```python
import math
import jax, jax.numpy as jnp
from jax import lax
import numpy as np

D_MODEL = 1024
BATCH = 4
SEQ = 8192
DEPTH = 2

HEAD_DIM = 64
A_HEADS = 8
A_KV_HEADS = 2
Q_RANK = 256
IDX_HEADS = 8
IDX_DIM = 64
IDX_TOPK = 256
B_HEADS = 8
SSM_WIDTH = D_MODEL
SSM_GROUP = 16
SSM_GROUPS = SSM_WIDTH // SSM_GROUP
SSM_STATE = 64
SSM_CHUNK = 128
DT_MIN = 1e-3
DT_MAX = 1e-1
MEM_LEN = 256
XA_HEADS = 4
XA_HEAD_DIM = D_MODEL // XA_HEADS
N_EXPERTS = 32
TOP_K = 4
D_EXPERT = D_MODEL
SWIGLU_LIMIT = 7.0
SWIGLU_ALPHA = 1.702
MOE_BLOCK = 128
Q_BLOCK = 128
ROPE_THETA = 500000.0
LN_EPS = 1e-5
N_EVEN = (DEPTH + 1) // 2
N_ODD = DEPTH // 2
DN_ALPHA = (2 * DEPTH) ** 0.25
DN_BETA = (8 * DEPTH) ** -0.25
A_WIDTH = A_HEADS * HEAD_DIM
B_WIDTH = B_HEADS * HEAD_DIM
MIX_WIDTH = A_WIDTH + B_WIDTH
_EVEN_COLS = (Q_RANK, A_KV_HEADS * HEAD_DIM, A_KV_HEADS * HEAD_DIM, IDX_DIM, IDX_HEADS, B_WIDTH, B_WIDTH, B_WIDTH)
_EVEN_VALUE_COLS = (False, False, True, False, False, False, False, True)
EVEN_IN = sum(_EVEN_COLS)
EVEN_SPLITS = tuple(int(c) for c in np.cumsum(_EVEN_COLS)[:-1])

kernel_name = "hybrid_dsa_stickbreak_s5_moe_deepnorm"


def layer_norm(x, g, b):
    xf = x.astype(jnp.float32)
    mu = jnp.mean(xf, axis=-1, keepdims=True)
    var = jnp.mean(jnp.square(xf - mu), axis=-1, keepdims=True)
    y = (xf - mu) * lax.rsqrt(var + LN_EPS) * g.astype(jnp.float32) + b.astype(jnp.float32)
    return y.astype(x.dtype)


def rms_norm(x, g):
    xf = x.astype(jnp.float32)
    y = xf * lax.rsqrt(jnp.mean(jnp.square(xf), axis=-1, keepdims=True) + LN_EPS) * g.astype(jnp.float32)
    return y.astype(x.dtype)


def rope_partial(t, pos):
    rot = t.shape[-1] // 4
    half = rot // 2
    inv = ROPE_THETA ** (-jnp.arange(half, dtype=jnp.float32) / half)
    ang = pos[:, None] * inv[None, :]
    cos = jnp.cos(ang)[:, None, :]
    sin = jnp.sin(ang)[:, None, :]
    t1 = t[..., :half].astype(jnp.float32)
    t2 = t[..., half:rot].astype(jnp.float32)
    r = jnp.concatenate([t1 * cos - t2 * sin, t2 * cos + t1 * sin, t[..., rot:].astype(jnp.float32)], axis=-1)
    return r.astype(t.dtype)


def to_blocks(t, nb):
    return jnp.swapaxes(t.reshape(t.shape[0], nb, Q_BLOCK, *t.shape[2:]), 0, 1)


def from_blocks(t):
    t = jnp.swapaxes(t, 0, 1)
    return t.reshape(t.shape[0], -1, *t.shape[3:])


def dsa_attention(q, k, v, q_idx, k_idx, w_idx):
    bsz, seq = q.shape[0], q.shape[1]
    nb = seq // Q_BLOCK
    topk = min(IDX_TOPK, seq // 4)
    rep = A_HEADS // A_KV_HEADS
    key_pos = jnp.arange(seq)

    def one_block(args):
        blk, qb, qib, wb = args
        qpos = blk * Q_BLOCK + jnp.arange(Q_BLOCK)
        causal = key_pos[None, :] <= qpos[:, None]
        s = jnp.einsum('bqhd,bsd->bqsh', qib, k_idx)
        score = jnp.einsum('bqsh,bqh->bqs', jax.nn.relu(s), wb).astype(jnp.float32)
        score = jnp.where(causal[None], score, -jnp.inf)
        _, idx = lax.top_k(score, topk)
        valid = idx <= qpos[None, :, None]
        kg = jax.vmap(lambda kk, ii: kk[ii])(k, idx)
        vg = jax.vmap(lambda vv, ii: vv[ii])(v, idx)
        qg = qb.reshape(bsz, Q_BLOCK, A_KV_HEADS, rep, HEAD_DIM)
        logits = jnp.einsum('bqgrd,bqkgd->bqgrk', qg, kg).astype(jnp.float32) * (HEAD_DIM ** -0.5)
        logits = jnp.where(valid[:, :, None, None, :], logits, -jnp.inf)
        p = jax.nn.softmax(logits, axis=-1).astype(v.dtype)
        o = jnp.einsum('bqgrk,bqkgd->bqgrd', p, vg)
        return o.reshape(bsz, Q_BLOCK, A_WIDTH)

    out = lax.map(one_block, (jnp.arange(nb), to_blocks(q, nb), to_blocks(q_idx, nb), to_blocks(w_idx, nb)))
    return from_blocks(out)


def stick_breaking_attention(q, k, v):
    bsz, seq = q.shape[0], q.shape[1]
    nb = seq // Q_BLOCK
    key_pos = jnp.arange(seq)

    def one_block(args):
        blk, qb = args
        qpos = blk * Q_BLOCK + jnp.arange(Q_BLOCK)
        strict = key_pos[None, :] < qpos[:, None]
        z = jnp.einsum('bqhd,bshd->bhqs', qb, k).astype(jnp.float32) * (HEAD_DIM ** -0.5)
        log_1mb = jnp.where(strict, jax.nn.log_sigmoid(-z), 0.0)
        after = lax.cumsum(log_1mb, axis=3, reverse=True) - log_1mb
        a = jnp.where(strict, jnp.exp(jax.nn.log_sigmoid(z) + after), 0.0)
        o = jnp.einsum('bhqs,bshd->bqhd', a.astype(v.dtype), v)
        return o.reshape(bsz, Q_BLOCK, B_WIDTH)

    out = lax.map(one_block, (jnp.arange(nb), to_blocks(q, nb)))
    return from_blocks(out)


def even_mixer(x, pos, w_in, qnorm_g, w_uq, w_uq_idx, kidx_g, kidx_b, w_out):
    bsz, seq, _ = x.shape
    c_q, k_a, v_a, k_i, w_i, q_b, k_b, v_b = jnp.split(x @ w_in, EVEN_SPLITS, axis=-1)
    c_q = rms_norm(c_q, qnorm_g)
    q_a = rope_partial((c_q @ w_uq).reshape(bsz, seq, A_HEADS, HEAD_DIM), pos)
    k_a = rope_partial(k_a.reshape(bsz, seq, A_KV_HEADS, HEAD_DIM), pos)
    v_a = v_a.reshape(bsz, seq, A_KV_HEADS, HEAD_DIM)
    q_i = rope_partial((c_q @ w_uq_idx).reshape(bsz, seq, IDX_HEADS, IDX_DIM), pos) * (IDX_DIM ** -0.5)
    k_i = rope_partial(layer_norm(k_i, kidx_g, kidx_b)[:, :, None, :], pos)[:, :, 0, :]
    w_i = w_i * (IDX_HEADS ** -0.5)
    o_a = dsa_attention(q_a, k_a, v_a, q_i, k_i, w_i)
    o_b = stick_breaking_attention(q_b.reshape(bsz, seq, B_HEADS, HEAD_DIM),
                                   k_b.reshape(bsz, seq, B_HEADS, HEAD_DIM),
                                   v_b.reshape(bsz, seq, B_HEADS, HEAD_DIM))
    return jnp.concatenate([o_a, o_b], axis=-1) @ w_out


def _complex_linear_combine(e1, e2):
    a1r, a1i, b1r, b1i = e1
    a2r, a2i, b2r, b2i = e2
    return (a2r * a1r - a2i * a1i, a2r * a1i + a2i * a1r,
            a2r * b1r - a2i * b1i + b2r, a2r * b1i + a2i * b1r + b2i)


def s5_mixer(x, w_in, log_dt, lam_re, lam_im, b_re, b_im, c_re, c_im, d, w_out):
    bsz, seq, _ = x.shape
    f32 = jnp.float32
    u = x @ w_in
    dt = jnp.exp(log_dt.astype(f32))[:, None]
    lr, li = lam_re.astype(f32), lam_im.astype(f32)
    mag = jnp.exp(lr * dt)
    a_re, a_im = mag * jnp.cos(li * dt), mag * jnp.sin(li * dt)
    den = lr * lr + li * li
    coef_re = ((a_re - 1.0) * lr + a_im * li) / den
    coef_im = (a_im * lr - (a_re - 1.0) * li) / den
    br, bi = b_re.astype(f32), b_im.astype(f32)
    bb_re = coef_re[..., None] * br - coef_im[..., None] * bi
    bb_im = coef_re[..., None] * bi + coef_im[..., None] * br
    cr, ci = c_re.astype(f32), c_im.astype(f32)
    nc = seq // SSM_CHUNK
    u_t = jnp.swapaxes(u, 0, 1).astype(f32).reshape(nc, SSM_CHUNK, bsz, SSM_GROUPS, SSM_GROUP)
    a_re_c = jnp.broadcast_to(a_re, (SSM_CHUNK, 1, SSM_GROUPS, SSM_STATE))
    a_im_c = jnp.broadcast_to(a_im, (SSM_CHUNK, 1, SSM_GROUPS, SSM_STATE))

    def chunk(h, u_c):
        h_re, h_im = h
        bu_re = jnp.einsum('lbgc,gpc->lbgp', u_c, bb_re)
        bu_im = jnp.einsum('lbgc,gpc->lbgp', u_c, bb_im)
        A_re, A_im, s_re, s_im = lax.associative_scan(_complex_linear_combine, (a_re_c, a_im_c, bu_re, bu_im), axis=0)
        st_re = s_re + A_re * h_re - A_im * h_im
        st_im = s_im + A_re * h_im + A_im * h_re
        y = jnp.einsum('lbgp,gcp->lbgc', st_re, cr) - jnp.einsum('lbgp,gcp->lbgc', st_im, ci)
        return (st_re[-1], st_im[-1]), y

    h0 = (jnp.zeros((bsz, SSM_GROUPS, SSM_STATE), f32), jnp.zeros((bsz, SSM_GROUPS, SSM_STATE), f32))
    _, ys = lax.scan(chunk, h0, u_t)
    y = jnp.swapaxes(ys.reshape(seq, bsz, SSM_WIDTH), 0, 1)
    y = y + d.astype(f32) * u.astype(f32)
    y = jax.nn.gelu(y).astype(x.dtype)
    z = y @ w_out
    return z[..., :D_MODEL] * jax.nn.sigmoid(z[..., D_MODEL:])


def memory_cross_attention(x, mem, w_q, w_kv, w_o):
    bsz, seq, _ = x.shape
    q = (x @ w_q).reshape(bsz, seq, XA_HEADS, XA_HEAD_DIM)
    k, v = jnp.split(mem @ w_kv, 2, axis=-1)
    k = k.reshape(bsz, -1, XA_HEADS, XA_HEAD_DIM)
    v = v.reshape(bsz, -1, XA_HEADS, XA_HEAD_DIM)
    logits = jnp.einsum('bqhd,bmhd->bhqm', q, k).astype(jnp.float32) * (XA_HEAD_DIM ** -0.5)
    p = jax.nn.softmax(logits, axis=-1).astype(v.dtype)
    o = jnp.einsum('bhqm,bmhd->bqhd', p, v).reshape(bsz, seq, D_MODEL)
    return o @ w_o


def moe_ffn(x, w_router, b_router, w_gu, b_gu, w_down, b_down):
    bsz, seq, _ = x.shape
    n = bsz * seq
    x2 = x.reshape(n, D_MODEL)
    logits = (x2 @ w_router).astype(jnp.float32) + b_router.astype(jnp.float32)
    top_val, top_idx = lax.top_k(logits, TOP_K)
    gates = jax.nn.softmax(top_val, axis=-1).astype(x.dtype)
    e_flat = top_idx.reshape(-1)
    tok_flat = jnp.arange(n * TOP_K) // TOP_K
    order = jnp.argsort(e_flat)
    e_sorted = e_flat[order]
    counts = jnp.bincount(e_flat, length=N_EXPERTS)
    padded = (counts + MOE_BLOCK - 1) // MOE_BLOCK * MOE_BLOCK
    start = jnp.cumsum(counts) - counts
    ends_p = jnp.cumsum(padded)
    pstart = ends_p - padded
    dest = pstart[e_sorted] + (jnp.arange(n * TOP_K) - start[e_sorted])
    n_rows = n * TOP_K + N_EXPERTS * MOE_BLOCK
    n_blocks = n_rows // MOE_BLOCK
    rows_tok = jnp.full((n_rows,), n, jnp.int32).at[dest].set(tok_flat[order].astype(jnp.int32))
    rows_gate = jnp.zeros((n_rows,), x.dtype).at[dest].set(gates.reshape(-1)[order])
    block_exp = jnp.minimum(jnp.searchsorted(ends_p, jnp.arange(n_blocks) * MOE_BLOCK, side='right'), N_EXPERTS - 1)
    x_pad = jnp.concatenate([x2, jnp.zeros((1, D_MODEL), x2.dtype)], axis=0)

    def expert_block(args):
        tok, g, e = args
        h = x_pad[tok] @ w_gu[e] + b_gu[e]
        gate = jnp.minimum(h[:, :D_EXPERT], SWIGLU_LIMIT)
        up = jnp.clip(h[:, D_EXPERT:], -SWIGLU_LIMIT, SWIGLU_LIMIT)
        act = gate * jax.nn.sigmoid(gate * SWIGLU_ALPHA) * (up + 1.0)
        return (act @ w_down[e] + b_down[e]) * g[:, None]

    y_rows = lax.map(expert_block, (rows_tok.reshape(n_blocks, MOE_BLOCK), rows_gate.reshape(n_blocks, MOE_BLOCK), block_exp))
    out = jax.ops.segment_sum(y_rows.reshape(n_rows, D_MODEL), rows_tok, num_segments=n + 1)[:n]
    return out.reshape(bsz, seq, D_MODEL)


def setup_inputs(seed: int = 0) -> dict:
    key = jax.random.key(seed)
    ks = jax.random.split(key, 40)
    f32 = jnp.float32

    def nrm(k, shape, fan_in, scale=1.0):
        return jax.random.normal(k, shape, f32) * (scale * fan_in ** -0.5)

    def gain(k, shape):
        return 1.0 + 0.02 * jax.random.normal(k, shape, f32)

    def small(k, shape, s=0.02):
        return s * jax.random.normal(k, shape, f32)

    even_col_scale = jnp.concatenate([jnp.full((c,), DN_BETA if is_v else 1.0, f32)
                                      for c, is_v in zip(_EVEN_COLS, _EVEN_VALUE_COLS)])
    xa_kv_scale = jnp.concatenate([jnp.ones((D_MODEL,), f32), jnp.full((D_MODEL,), DN_BETA, f32)])
    return {
        "x": jax.random.normal(ks[0], (BATCH, SEQ, D_MODEL), f32),
        "mem": jax.random.normal(ks[1], (BATCH, MEM_LEN, D_MODEL), f32),
        "ev_w_in": nrm(ks[2], (N_EVEN, D_MODEL, EVEN_IN), D_MODEL) * even_col_scale,
        "ev_qnorm_g": gain(ks[3], (N_EVEN, Q_RANK)),
        "ev_w_uq": nrm(ks[4], (N_EVEN, Q_RANK, A_WIDTH), Q_RANK),
        "ev_w_uq_idx": nrm(ks[5], (N_EVEN, Q_RANK, IDX_HEADS * IDX_DIM), Q_RANK),
        "ev_kidx_ln_g": gain(ks[6], (N_EVEN, IDX_DIM)),
        "ev_kidx_ln_b": small(ks[7], (N_EVEN, IDX_DIM)),
        "ev_w_out": nrm(ks[8], (N_EVEN, MIX_WIDTH, D_MODEL), MIX_WIDTH, DN_BETA),
        "od_w_in": nrm(ks[9], (N_ODD, D_MODEL, SSM_WIDTH), D_MODEL),
        "od_log_dt": jax.random.uniform(ks[10], (N_ODD, SSM_GROUPS), f32, math.log(DT_MIN), math.log(DT_MAX)),
        "od_lambda_re": -0.5 + small(ks[11], (N_ODD, SSM_GROUPS, SSM_STATE), 0.01),
        "od_lambda_im": math.pi * jnp.arange(SSM_STATE, dtype=f32) + small(ks[12], (N_ODD, SSM_GROUPS, SSM_STATE), 0.01),
        "od_b_re": nrm(ks[13], (N_ODD, SSM_GROUPS, SSM_STATE, SSM_GROUP), 2 * SSM_GROUP),
        "od_b_im": nrm(ks[14], (N_ODD, SSM_GROUPS, SSM_STATE, SSM_GROUP), 2 * SSM_GROUP),
        "od_c_re": nrm(ks[15], (N_ODD, SSM_GROUPS, SSM_GROUP, SSM_STATE), SSM_STATE, 2.0),
        "od_c_im": nrm(ks[16], (N_ODD, SSM_GROUPS, SSM_GROUP, SSM_STATE), SSM_STATE, 2.0),
        "od_d": jax.random.normal(ks[17], (N_ODD, SSM_WIDTH), f32),
        "od_w_out": nrm(ks[18], (N_ODD, SSM_WIDTH, 2 * D_MODEL), SSM_WIDTH, DN_BETA),
        "mix_ln_g": gain(ks[19], (DEPTH, D_MODEL)),
        "mix_ln_b": small(ks[20], (DEPTH, D_MODEL)),
        "xa_w_q": nrm(ks[21], (DEPTH, D_MODEL, D_MODEL), D_MODEL),
        "xa_w_kv": nrm(ks[22], (DEPTH, D_MODEL, 2 * D_MODEL), D_MODEL) * xa_kv_scale,
        "xa_w_o": nrm(ks[23], (DEPTH, D_MODEL, D_MODEL), D_MODEL, DN_BETA),
        "xa_ln_g": gain(ks[24], (DEPTH, D_MODEL)),
        "xa_ln_b": small(ks[25], (DEPTH, D_MODEL)),
        "moe_w_router": nrm(ks[26], (DEPTH, D_MODEL, N_EXPERTS), D_MODEL),
        "moe_b_router": small(ks[27], (DEPTH, N_EXPERTS), 0.01),
        "moe_w_gu": nrm(ks[28], (DEPTH, N_EXPERTS, D_MODEL, 2 * D_EXPERT), D_MODEL),
        "moe_b_gu": small(ks[29], (DEPTH, N_EXPERTS, 2 * D_EXPERT)),
        "moe_w_down": nrm(ks[30], (DEPTH, N_EXPERTS, D_EXPERT, D_MODEL), D_EXPERT, DN_BETA),
        "moe_b_down": small(ks[31], (DEPTH, N_EXPERTS, D_MODEL)),
        "ffn_ln_g": gain(ks[32], (DEPTH, D_MODEL)),
        "ffn_ln_b": small(ks[33], (DEPTH, D_MODEL)),
    }


def reference(x, mem, ev_w_in, ev_qnorm_g, ev_w_uq, ev_w_uq_idx, ev_kidx_ln_g, ev_kidx_ln_b, ev_w_out,
              od_w_in, od_log_dt, od_lambda_re, od_lambda_im, od_b_re, od_b_im, od_c_re, od_c_im, od_d, od_w_out,
              mix_ln_g, mix_ln_b, xa_w_q, xa_w_kv, xa_w_o, xa_ln_g, xa_ln_b,
              moe_w_router, moe_b_router, moe_w_gu, moe_b_gu, moe_w_down, moe_b_down, ffn_ln_g, ffn_ln_b):
    pos = jnp.arange(x.shape[1], dtype=jnp.float32)
    h = x
    for layer in range(DEPTH):
        j = layer // 2
        if layer % 2 == 0:
            m = even_mixer(h, pos, ev_w_in[j], ev_qnorm_g[j], ev_w_uq[j], ev_w_uq_idx[j],
                           ev_kidx_ln_g[j], ev_kidx_ln_b[j], ev_w_out[j])
        else:
            m = s5_mixer(h, od_w_in[j], od_log_dt[j], od_lambda_re[j], od_lambda_im[j], od_b_re[j], od_b_im[j],
                         od_c_re[j], od_c_im[j], od_d[j], od_w_out[j])
        h = layer_norm(DN_ALPHA * h + m, mix_ln_g[layer], mix_ln_b[layer])
        h = layer_norm(DN_ALPHA * h + memory_cross_attention(h, mem, xa_w_q[layer], xa_w_kv[layer], xa_w_o[layer]),
                       xa_ln_g[layer], xa_ln_b[layer])
        h = layer_norm(DN_ALPHA * h + moe_ffn(h, moe_w_router[layer], moe_b_router[layer], moe_w_gu[layer],
                                              moe_b_gu[layer], moe_w_down[layer], moe_b_down[layer]),
                       ffn_ln_g[layer], ffn_ln_b[layer])
    return h
```

```python
import functools
import math

import jax
import jax.numpy as jnp
from jax import lax
from jax.experimental import pallas as pl
from jax.experimental.pallas import tpu as pltpu

F32 = jnp.float32
BF16 = jnp.bfloat16
I32 = jnp.int32

D_MODEL = 1024
DEPTH = 2
HEAD_DIM = 64
A_HEADS = 8
A_KV_HEADS = 2
A_REP = A_HEADS // A_KV_HEADS
Q_RANK = 256
IDX_HEADS = 8
IDX_DIM = 64
IDX_TOPK = 256
B_HEADS = 8
A_WIDTH = A_HEADS * HEAD_DIM
B_WIDTH = B_HEADS * HEAD_DIM
SSM_GROUP = 16
SSM_GROUPS = D_MODEL // SSM_GROUP
SSM_STATE = 64
XA_HEADS = 4
XA_HEAD_DIM = D_MODEL // XA_HEADS
N_EXPERTS = 32
TOP_K = 4
D_EXPERT = D_MODEL
SWIGLU_LIMIT = 7.0
SWIGLU_ALPHA = 1.702
ROPE_THETA = 500000.0
ROPE_HALF = HEAD_DIM // 8
LN_EPS = 1e-5
DN_ALPHA = (2 * DEPTH) ** 0.25

LANES = 128
SUBLANES = 8
VMEM_LIMIT_BYTES = 56 * 1024 * 1024

Q_BLOCK = 128
DSA_KEY_TILE = 512
SB_KEY_TILE = 256
ROW_TILE = 256
MOE_BLOCK_ROWS = 256
GATHER_ROWS = 1024
S5_CHUNK = 128
S5_LANE_BLOCKS = 4
S5_IN_BLK = D_MODEL // S5_LANE_BLOCKS
S5_ST_BLK = SSM_GROUPS * SSM_STATE // S5_LANE_BLOCKS

NEG_BIG = -1e30
INT_MIN = -(2 ** 31)


def _cparams(*sem):
    return pltpu.CompilerParams(dimension_semantics=sem, vmem_limit_bytes=VMEM_LIMIT_BYTES)


def _dot(a, b):
    return jnp.dot(a, b, preferred_element_type=F32)


def _dot_nt(a, b):
    return lax.dot_general(a, b, (((1,), (1,)), ((), ())), preferred_element_type=F32)


def _layer_norm_rows(y, g, b):
    mu = jnp.mean(y, axis=-1, keepdims=True)
    d = y - mu
    var = jnp.mean(d * d, axis=-1, keepdims=True)
    return d * lax.rsqrt(var + LN_EPS) * g + b


def _mm_kernel(x_ref, w_ref, o_ref):
    o_ref[...] = _dot(x_ref[...].astype(BF16), w_ref[...]).astype(o_ref.dtype)


def _matmul(x, w, *, tm, out_dtype, x_map=None, out_map=None, grid=None, out_shape=None):
    m, k = x.shape
    n = w.shape[1]
    grid = grid or (m // tm,)
    x_map = x_map or (lambda i: (i, 0))
    out_map = out_map or (lambda i: (i, 0))
    out_shape = out_shape or (m, n)
    return pl.pallas_call(
        _mm_kernel,
        grid=grid,
        in_specs=[pl.BlockSpec((tm, k), x_map), pl.BlockSpec((k, n), lambda *a: (0, 0))],
        out_specs=pl.BlockSpec((tm, n), out_map),
        out_shape=jax.ShapeDtypeStruct(out_shape, out_dtype),
        compiler_params=_cparams(*(("parallel",) * len(grid))),
        name="matmul",
    )(x, w)


def _lin_ln_kernel(*refs, n_in, glu):
    xs, ws = refs[:n_in], refs[n_in:2 * n_in]
    res_ref, g_ref, b_ref, o_ref = refs[2 * n_in:]
    acc = _dot(xs[0][...].astype(BF16), ws[0][...])
    for x_ref, w_ref in zip(xs[1:], ws[1:]):
        acc = acc + _dot(x_ref[...].astype(BF16), w_ref[...])
    if glu:
        acc = acc[:, :D_MODEL] * jax.nn.sigmoid(acc[:, D_MODEL:])
    y = DN_ALPHA * res_ref[...] + acc
    o_ref[...] = _layer_norm_rows(y, g_ref[...], b_ref[...])


def _linear_residual_ln(xs, ws, res, g, b, *, tm, glu=False, grid=None, x_maps=None, res_map=None):
    n_rows = res.shape[0]
    grid = grid or (n_rows // tm,)
    x_maps = x_maps or [lambda i: (i, 0)] * len(xs)
    res_map = res_map or (lambda i: (i, 0))
    const = lambda *a: (0, 0)
    in_specs = [pl.BlockSpec((tm, w.shape[0]), m) for w, m in zip(ws, x_maps)]
    in_specs += [pl.BlockSpec(w.shape, const) for w in ws]
    in_specs += [pl.BlockSpec((tm, D_MODEL), res_map), pl.BlockSpec((1, D_MODEL), const),
                 pl.BlockSpec((1, D_MODEL), const)]
    return pl.pallas_call(
        functools.partial(_lin_ln_kernel, n_in=len(xs), glu=glu),
        grid=grid,
        in_specs=in_specs,
        out_specs=pl.BlockSpec((tm, D_MODEL), res_map),
        out_shape=jax.ShapeDtypeStruct((n_rows, D_MODEL), F32),
        compiler_params=_cparams(*(("parallel",) * len(grid))),
        name="linear_residual_ln",
    )(*xs, *ws, res, g.reshape(1, D_MODEL), b.reshape(1, D_MODEL))


_EV_CQ, _EV_KA, _EV_VA, _EV_KI, _EV_QB = 0, 256, 384, 512, 640
_EV_KB = _EV_QB + B_WIDTH
_EV_VB = _EV_KB + B_WIDTH
_EV_COLS = _EV_VB + B_WIDTH


def _rope_tables(seq):
    inv = ROPE_THETA ** (-jnp.arange(ROPE_HALF, dtype=F32) / ROPE_HALF)
    ang = jnp.arange(seq, dtype=F32)[:, None] * inv[None, :]
    cos, sin = jnp.cos(ang), jnp.sin(ang)
    rest = HEAD_DIM - 2 * ROPE_HALF
    zh = jnp.zeros((seq, ROPE_HALF), F32)
    c = jnp.concatenate([cos, cos, jnp.ones((seq, rest), F32)], axis=1)
    s1 = jnp.concatenate([-sin, zh, jnp.zeros((seq, rest), F32)], axis=1)
    s2 = jnp.concatenate([zh, sin, jnp.zeros((seq, rest), F32)], axis=1)
    rep = LANES // HEAD_DIM
    return jnp.tile(c, (1, rep)), jnp.tile(s1, (1, rep)), jnp.tile(s2, (1, rep))


def _even_proj_kernel(x_ref, w_ref, qg_ref, wuq_ref, wuqi_ref, lg_ref, lb_ref, c_ref, s1_ref, s2_ref,
                      qa_ref, qi_ref, ka_ref, va_ref, ki_ref, wi_ref, qb_ref, kb_ref, vb_ref):
    p = _dot(x_ref[...].astype(BF16), w_ref[...])
    c, s1, s2 = c_ref[...], s1_ref[...], s2_ref[...]

    def rope(t):
        return (t * c + pltpu.roll(t, LANES - ROPE_HALF, 1) * s1 + pltpu.roll(t, ROPE_HALF, 1) * s2)

    cq = p[:, _EV_CQ:_EV_CQ + Q_RANK]
    cn = cq * lax.rsqrt(jnp.mean(cq * cq, axis=-1, keepdims=True) + LN_EPS) * qg_ref[...]
    cnb = cn.astype(BF16)
    qa = _dot(cnb, wuq_ref[...])
    qi = _dot(cnb, wuqi_ref[...])
    for j in range(A_WIDTH // LANES):
        sl = slice(j * LANES, (j + 1) * LANES)
        qa_ref[:, sl] = (rope(qa[:, sl]) * (HEAD_DIM ** -0.5)).astype(BF16)
        qi_ref[:, sl] = (rope(qi[:, sl]) * (IDX_DIM ** -0.5)).astype(BF16)
    ka_ref[...] = rope(p[:, _EV_KA:_EV_KA + LANES]).astype(BF16)
    va_ref[...] = p[:, _EV_VA:_EV_VA + LANES].astype(BF16)

    t = p[:, _EV_KI:_EV_KI + LANES]
    lane = lax.broadcasted_iota(I32, t.shape, 1)
    is_k = lane < IDX_DIM
    mu = jnp.sum(jnp.where(is_k, t, 0.0), axis=-1, keepdims=True) * (1.0 / IDX_DIM)
    d = jnp.where(is_k, t - mu, 0.0)
    var = jnp.sum(d * d, axis=-1, keepdims=True) * (1.0 / IDX_DIM)
    kin = d * lax.rsqrt(var + LN_EPS) * lg_ref[...] + lb_ref[...]
    ki_ref[...] = rope(kin)[:, :IDX_DIM].astype(BF16)
    wi_ref[...] = t[:, IDX_DIM:IDX_DIM + IDX_HEADS] * (IDX_HEADS ** -0.5)

    for h in range(B_HEADS):
        sl = slice(h * HEAD_DIM, (h + 1) * HEAD_DIM)
        qb_ref[0, h] = (p[:, _EV_QB:_EV_KB][:, sl] * (HEAD_DIM ** -0.5)).astype(BF16)
        kb_ref[0, h] = p[:, _EV_KB:_EV_VB][:, sl].astype(BF16)
        vb_ref[0, h] = p[:, _EV_VB:_EV_COLS][:, sl].astype(BF16)


def _even_proj(x2d, bsz, seq, w_in, qnorm_g, w_uq, w_uq_idx, kidx_g, kidx_b):
    n = x2d.shape[0]
    tm = ROW_TILE
    per_seq = seq // tm
    c0 = Q_RANK + 2 * A_KV_HEADS * HEAD_DIM + IDX_DIM + IDX_HEADS
    w_pack = jnp.concatenate(
        [w_in[:, :c0], jnp.zeros((D_MODEL, _EV_QB - c0), w_in.dtype), w_in[:, c0:]], axis=1).astype(BF16)
    pad = LANES - IDX_DIM
    lg = jnp.concatenate([kidx_g, jnp.zeros((pad,), F32)]).reshape(1, LANES)
    lb = jnp.concatenate([kidx_b, jnp.zeros((pad,), F32)]).reshape(1, LANES)
    c, s1, s2 = _rope_tables(seq)
    row = lambda i: (i, 0)
    const = lambda i: (0, 0)
    pos = lambda i: (i % per_seq, 0)
    head = lambda i: (i // per_seq, 0, i % per_seq, 0)
    head_shape = jax.ShapeDtypeStruct((bsz, B_HEADS, seq, HEAD_DIM), BF16)
    head_spec = pl.BlockSpec((1, B_HEADS, tm, HEAD_DIM), head)
    return pl.pallas_call(
        _even_proj_kernel,
        grid=(n // tm,),
        in_specs=[pl.BlockSpec((tm, D_MODEL), row), pl.BlockSpec((D_MODEL, _EV_COLS), const),
                  pl.BlockSpec((1, Q_RANK), const), pl.BlockSpec((Q_RANK, A_WIDTH), const),
                  pl.BlockSpec((Q_RANK, IDX_HEADS * IDX_DIM), const),
                  pl.BlockSpec((1, LANES), const), pl.BlockSpec((1, LANES), const),
                  pl.BlockSpec((tm, LANES), pos), pl.BlockSpec((tm, LANES), pos), pl.BlockSpec((tm, LANES), pos)],
        out_specs=[pl.BlockSpec((tm, A_WIDTH), row), pl.BlockSpec((tm, IDX_HEADS * IDX_DIM), row),
                   pl.BlockSpec((tm, LANES), row), pl.BlockSpec((tm, LANES), row),
                   pl.BlockSpec((tm, IDX_DIM), row), pl.BlockSpec((tm, IDX_HEADS), row),
                   head_spec, head_spec, head_spec],
        out_shape=[jax.ShapeDtypeStruct((n, A_WIDTH), BF16), jax.ShapeDtypeStruct((n, IDX_HEADS * IDX_DIM), BF16),
                   jax.ShapeDtypeStruct((n, LANES), BF16), jax.ShapeDtypeStruct((n, LANES), BF16),
                   jax.ShapeDtypeStruct((n, IDX_DIM), BF16), jax.ShapeDtypeStruct((n, IDX_HEADS), F32),
                   head_shape, head_shape, head_shape],
        compiler_params=_cparams("parallel"),
        name="even_proj",
    )(x2d, w_pack, qnorm_g.reshape(1, Q_RANK), w_uq.astype(BF16), w_uq_idx.astype(BF16), lg, lb, c, s1, s2)


def _key_to_float(key):
    bits = key ^ ((key >> 31) & jnp.int32(0x7FFFFFFF))
    return lax.bitcast_convert_type(bits, F32)


def _dsa_kernel(qa_ref, qi_ref, wi_ref, ki_ref, ka_ref, va_ref, o_ref, sc_scr, *, topk, ts):
    qb = pl.program_id(1)
    q0 = qb * Q_BLOCK
    nkt = (q0 + Q_BLOCK - 1) // ts + 1
    t_col = q0 + lax.broadcasted_iota(I32, (Q_BLOCK, 1), 0)
    lane = lax.broadcasted_iota(I32, (Q_BLOCK, ts), 1)
    kf = jnp.float32(topk)

    qi = qi_ref[0]
    wi = wi_ref[0]
    qih = [qi[:, h * IDX_DIM:(h + 1) * IDX_DIM] for h in range(IDX_HEADS)]
    wih = [wi[:, h:h + 1] for h in range(IDX_HEADS)]

    def score_tile(kt, carry):
        off = pl.multiple_of(kt * ts, ts)
        kit = ki_ref[0, pl.ds(off, ts), :]
        acc = jnp.zeros((Q_BLOCK, ts), F32)
        for h in range(IDX_HEADS):
            acc = acc + jnp.maximum(_dot_nt(qih[h], kit), 0.0) * wih[h]
        sc_scr[:, pl.ds(off, ts)] = jnp.where(off + lane <= t_col, acc, -jnp.inf)
        return carry

    lax.fori_loop(0, nkt, score_tile, 0)

    def count(pred):
        def body(kt, acc):
            off = pl.multiple_of(kt * ts, ts)
            ind = pred(sc_scr[:, pl.ds(off, ts)], off + lane)
            for j in range(ts // LANES):
                acc = acc + ind[:, j * LANES:(j + 1) * LANES]
            return acc
        acc = lax.fori_loop(0, nkt, body, jnp.zeros((Q_BLOCK, LANES), F32))
        return jnp.sum(acc, axis=1, keepdims=True)

    def bit_step(i, base):
        cand = base + jnp.left_shift(jnp.int32(1), 31 - i)
        cf = _key_to_float(cand)
        cnt = count(lambda sc, idx: jnp.where(sc >= cf, 1.0, 0.0))
        return jnp.where(cnt >= kf, cand, base)

    base = lax.fori_loop(0, 32, bit_step, jnp.full((Q_BLOCK, 1), INT_MIN, I32))
    thr = jnp.where(base == INT_MIN, -jnp.inf, _key_to_float(base))

    cnt_ge = count(lambda sc, idx: jnp.where(sc >= thr, 1.0, 0.0))
    tied = jnp.logical_and(cnt_ge > kf, thr > -jnp.inf)
    any_tied = jnp.max(jnp.where(tied, 1.0, 0.0)) > 0.0
    seq_bits = max(1, int(math.ceil(math.log2(sc_scr.shape[1]))))

    def tie_cut():
        cnt_gt = count(lambda sc, idx: jnp.where(sc > thr, 1.0, 0.0))
        need = kf - cnt_gt

        def idx_step(i, pos):
            cand = pos + jnp.left_shift(jnp.int32(1), seq_bits - 1 - i)
            cnt = count(lambda sc, idx: jnp.where(sc == thr, jnp.where(idx < cand, 1.0, 0.0), 0.0))
            return jnp.where(cnt < need, cand, pos)

        return lax.fori_loop(0, seq_bits, idx_step, jnp.zeros((Q_BLOCK, 1), I32))

    cut = lax.cond(any_tied, tie_cut, lambda: jnp.full((Q_BLOCK, 1), sc_scr.shape[1], I32))
    cut = jnp.where(tied, cut, sc_scr.shape[1])

    qa = qa_ref[0]
    for g in range(A_KV_HEADS):
        heads = [qa[:, (g * A_REP + r) * HEAD_DIM:(g * A_REP + r + 1) * HEAD_DIM] for r in range(A_REP)]
        qg = jnp.concatenate(heads, axis=0)

        def att_tile(kt, carry):
            m, l, acc = carry
            off = pl.multiple_of(kt * ts, ts)
            sc = sc_scr[:, pl.ds(off, ts)]
            idx = off + lane
            keep = jnp.where(sc > thr, 0.0, jnp.where(sc == thr, jnp.where(idx <= cut, 0.0, NEG_BIG), NEG_BIG))
            bias = jnp.where(idx <= t_col, keep, NEG_BIG)
            kg = ka_ref[0, pl.ds(off, ts), g * HEAD_DIM:(g + 1) * HEAD_DIM]
            vg = va_ref[0, pl.ds(off, ts), g * HEAD_DIM:(g + 1) * HEAD_DIM]
            s = _dot_nt(qg, kg)
            s = (s.reshape(A_REP, Q_BLOCK, ts) + bias[None]).reshape(A_REP * Q_BLOCK, ts)
            m_new = jnp.maximum(m, jnp.max(s, axis=1, keepdims=True))
            p = jnp.exp(s - m_new)
            alpha = jnp.exp(m - m_new)
            l = alpha * l + jnp.sum(p, axis=1, keepdims=True)
            acc = alpha * acc + _dot(p.astype(BF16), vg)
            return m_new, l, acc

        rows = A_REP * Q_BLOCK
        init = (jnp.full((rows, 1), NEG_BIG, F32), jnp.zeros((rows, 1), F32), jnp.zeros((rows, HEAD_DIM), F32))
        _, l, acc = lax.fori_loop(0, nkt, att_tile, init)
        og = acc / l
        for r in range(A_REP):
            hcol = (g * A_REP + r) * HEAD_DIM
            o_ref[0, :, hcol:hcol + HEAD_DIM] = og[r * Q_BLOCK:(r + 1) * Q_BLOCK].astype(BF16)


def _dsa_attention(qa, qi, wi, ki, ka, va, bsz, seq):
    topk = min(IDX_TOPK, seq // 4)
    ts = min(DSA_KEY_TILE, seq)
    blk = lambda b, i: (b, i, 0)
    full = lambda b, i: (b, 0, 0)
    r3 = lambda a: a.reshape(bsz, seq, a.shape[-1])
    return pl.pallas_call(
        functools.partial(_dsa_kernel, topk=topk, ts=ts),
        grid=(bsz, seq // Q_BLOCK),
        in_specs=[pl.BlockSpec((1, Q_BLOCK, A_WIDTH), blk), pl.BlockSpec((1, Q_BLOCK, IDX_HEADS * IDX_DIM), blk),
                  pl.BlockSpec((1, Q_BLOCK, IDX_HEADS), blk), pl.BlockSpec((1, seq, IDX_DIM), full),
                  pl.BlockSpec((1, seq, LANES), full), pl.BlockSpec((1, seq, LANES), full)],
        out_specs=pl.BlockSpec((1, Q_BLOCK, A_WIDTH), blk),
        out_shape=jax.ShapeDtypeStruct((bsz, seq, A_WIDTH), BF16),
        scratch_shapes=[pltpu.VMEM((Q_BLOCK, seq), F32)],
        compiler_params=_cparams("parallel", "parallel"),
        name="dsa_attention",
    )(r3(qa), r3(qi), r3(wi), r3(ki), r3(ka), r3(va)).reshape(bsz * seq, A_WIDTH)


def _sb_kernel(q_ref, k_ref, v_ref, u_ref, o_ref, *, tk):
    q0 = pl.program_id(2) * Q_BLOCK
    q = q_ref[0, 0]
    t_col = q0 + lax.broadcasted_iota(I32, (Q_BLOCK, 1), 0)
    lane = lax.broadcasted_iota(I32, (Q_BLOCK, tk), 1)
    upper = u_ref[...]
    nkt = (q0 + Q_BLOCK - 1) // tk + 1

    def body(i, carry):
        run, acc = carry
        off = pl.multiple_of((nkt - 1 - i) * tk, tk)
        kk = k_ref[0, 0, pl.ds(off, tk), :]
        vv = v_ref[0, 0, pl.ds(off, tk), :]
        z = _dot_nt(q, kk)
        strict = off + lane < t_col
        softplus = jnp.maximum(z, 0.0) + jnp.log(1.0 + jnp.exp(-jnp.abs(z)))
        log_1mb = jnp.where(strict, -softplus, 0.0)
        hi = log_1mb.astype(BF16)
        lo = (log_1mb - hi.astype(F32)).astype(BF16)
        after = _dot(hi, upper) + _dot(lo, upper) + run
        a = jnp.where(strict, jnp.exp(z - softplus + after), 0.0)
        acc = acc + _dot(a.astype(BF16), vv)
        run = run + jnp.sum(log_1mb, axis=1, keepdims=True)
        return run, acc

    init = (jnp.zeros((Q_BLOCK, 1), F32), jnp.zeros((Q_BLOCK, HEAD_DIM), F32))
    _, acc = lax.fori_loop(0, nkt, body, init)
    o_ref[0, 0] = acc.astype(BF16)


def _stick_breaking(qb, kb, vb, bsz, seq):
    tk = min(SB_KEY_TILE, seq)
    r = lax.broadcasted_iota(I32, (tk, tk), 0)
    c = lax.broadcasted_iota(I32, (tk, tk), 1)
    upper = jnp.where(r > c, 1.0, 0.0).astype(BF16)
    blk = lambda b, h, i: (b, h, i, 0)
    full = lambda b, h, i: (b, h, 0, 0)
    return pl.pallas_call(
        functools.partial(_sb_kernel, tk=tk),
        grid=(bsz, B_HEADS, seq // Q_BLOCK),
        in_specs=[pl.BlockSpec((1, 1, Q_BLOCK, HEAD_DIM), blk), pl.BlockSpec((1, 1, seq, HEAD_DIM), full),
                  pl.BlockSpec((1, 1, seq, HEAD_DIM), full), pl.BlockSpec((tk, tk), lambda b, h, i: (0, 0))],
        out_specs=pl.BlockSpec((1, 1, Q_BLOCK, HEAD_DIM), blk),
        out_shape=jax.ShapeDtypeStruct((bsz, B_HEADS, seq, HEAD_DIM), BF16),
        compiler_params=_cparams("parallel", "parallel", "parallel"),
        name="stick_breaking",
    )(qb, kb, vb, upper)


def _xattn_kernel(h_ref, wq_ref, kv_ref, wo_ref, g_ref, b_ref, o_ref):
    h = h_ref[...]
    q = (_dot(h.astype(BF16), wq_ref[...]) * (XA_HEAD_DIM ** -0.5)).astype(BF16)
    kv = kv_ref[0]
    outs = []
    for hd in range(XA_HEADS):
        sl = slice(hd * XA_HEAD_DIM, (hd + 1) * XA_HEAD_DIM)
        s = _dot_nt(q[:, sl], kv[:, sl])
        p = jnp.exp(s - jnp.max(s, axis=1, keepdims=True))
        vh = kv[:, D_MODEL + hd * XA_HEAD_DIM:D_MODEL + (hd + 1) * XA_HEAD_DIM]
        outs.append((_dot(p.astype(BF16), vh) / jnp.sum(p, axis=1, keepdims=True)).astype(BF16))
    y = _dot(jnp.concatenate(outs, axis=1), wo_ref[...])
    o_ref[...] = _layer_norm_rows(DN_ALPHA * h + y, g_ref[...], b_ref[...])


def _cross_attention_block(h2d, mem, bsz, seq, w_q, w_kv, w_o, g, b):
    tm = ROW_TILE
    per_seq = seq // tm
    mem_len = mem.shape[1]
    kv = _matmul(mem.reshape(bsz * mem_len, D_MODEL), w_kv.astype(BF16), tm=mem_len, out_dtype=BF16)
    kv = kv.reshape(bsz, mem_len, 2 * D_MODEL)
    row = lambda i: (i, 0)
    const = lambda i: (0, 0)
    return pl.pallas_call(
        _xattn_kernel,
        grid=(bsz * per_seq,),
        in_specs=[pl.BlockSpec((tm, D_MODEL), row), pl.BlockSpec((D_MODEL, D_MODEL), const),
                  pl.BlockSpec((1, mem_len, 2 * D_MODEL), lambda i: (i // per_seq, 0, 0)),
                  pl.BlockSpec((D_MODEL, D_MODEL), const),
                  pl.BlockSpec((1, D_MODEL), const), pl.BlockSpec((1, D_MODEL), const)],
        out_specs=pl.BlockSpec((tm, D_MODEL), row),
        out_shape=jax.ShapeDtypeStruct(h2d.shape, F32),
        compiler_params=_cparams("parallel"),
        name="cross_attention",
    )(h2d, w_q.astype(BF16), kv, w_o.astype(BF16), g.reshape(1, D_MODEL), b.reshape(1, D_MODEL))


def _router_kernel(h_ref, w_ref, b_ref, idx_ref, gate_ref):
    h = h_ref[...]
    hh = h.astype(BF16)
    hl = (h - hh.astype(F32)).astype(BF16)
    w = w_ref[...]
    wh = w.astype(BF16)
    wl = (w - wh.astype(F32)).astype(BF16)
    logits = _dot(hh, wh) + _dot(hl, wh) + _dot(hh, wl) + b_ref[...]
    lane = lax.broadcasted_iota(I32, logits.shape, 1).astype(F32)
    vals = []
    for k in range(TOP_K):
        m = jnp.max(logits, axis=1, keepdims=True)
        sel = jnp.min(jnp.where(logits == m, lane, float(LANES)), axis=1, keepdims=True)
        idx_ref[:, k:k + 1] = sel.astype(I32)
        vals.append(m)
        logits = jnp.where(lane == sel, -jnp.inf, logits)
    es = [jnp.exp(v - vals[0]) for v in vals]
    tot = es[0] + es[1] + es[2] + es[3]
    for k in range(TOP_K):
        gate_ref[:, k:k + 1] = es[k] / tot


def _router(h2d, w_router, b_router):
    n = h2d.shape[0]
    tm = 2 * ROW_TILE
    pad = LANES - N_EXPERTS
    w = jnp.concatenate([w_router, jnp.zeros((D_MODEL, pad), F32)], axis=1)
    b = jnp.concatenate([b_router, jnp.full((pad,), NEG_BIG, F32)]).reshape(1, LANES)
    row = lambda i: (i, 0)
    const = lambda i: (0, 0)
    return pl.pallas_call(
        _router_kernel,
        grid=(n // tm,),
        in_specs=[pl.BlockSpec((tm, D_MODEL), row), pl.BlockSpec((D_MODEL, LANES), const),
                  pl.BlockSpec((1, LANES), const)],
        out_specs=[pl.BlockSpec((tm, TOP_K), row), pl.BlockSpec((tm, TOP_K), row)],
        out_shape=[jax.ShapeDtypeStruct((n, TOP_K), I32), jax.ShapeDtypeStruct((n, TOP_K), F32)],
        compiler_params=_cparams("parallel"),
        name="moe_router",
    )(h2d, w, b)


def _row_copy(src_hbm, dst_hbm, sem, src_row, dst_row):
    return pltpu.make_async_copy(src_hbm.at[pl.ds(src_row, 1)], dst_hbm.at[pl.ds(dst_row, 1)], sem)


def _gather_kernel(idx_ref, src_hbm, dst_hbm, sem, *, rows):
    base = pl.program_id(0) * rows

    def issue(r, carry):
        _row_copy(src_hbm, dst_hbm, sem, idx_ref[r], base + r).start()
        return carry

    def drain(r, carry):
        _row_copy(src_hbm, dst_hbm, sem, 0, base + r).wait()
        return carry

    lax.fori_loop(0, rows, issue, 0)
    lax.fori_loop(0, rows, drain, 0)


def _gather_rows(src, idx):
    n_out = idx.shape[0]
    rows = GATHER_ROWS
    return pl.pallas_call(
        functools.partial(_gather_kernel, rows=rows),
        grid=(n_out // rows,),
        in_specs=[pl.BlockSpec((rows,), lambda i: (i,), memory_space=pltpu.SMEM),
                  pl.BlockSpec(memory_space=pl.ANY)],
        out_specs=pl.BlockSpec(memory_space=pl.ANY),
        out_shape=jax.ShapeDtypeStruct((n_out, src.shape[1]), src.dtype),
        scratch_shapes=[pltpu.SemaphoreType.DMA(())],
        compiler_params=_cparams("arbitrary"),
        name="gather_rows",
    )(idx, src)


def _expert_kernel(blk_exp_ref, n_used_ref, x_ref, wgu_ref, bgu_ref, wd_ref, bd_ref, o_ref):
    i = pl.program_id(0)

    @pl.when(i < n_used_ref[0])
    def _():
        hgu = _dot(x_ref[...].astype(BF16), wgu_ref[0]) + bgu_ref[0]
        gate = jnp.minimum(hgu[:, :D_EXPERT], SWIGLU_LIMIT)
        up = jnp.clip(hgu[:, D_EXPERT:], -SWIGLU_LIMIT, SWIGLU_LIMIT)
        act = gate * jax.nn.sigmoid(gate * SWIGLU_ALPHA) * (up + 1.0)
        o_ref[...] = _dot(act.astype(BF16), wd_ref[0]) + bd_ref[0]

    @pl.when(i >= n_used_ref[0])
    def _():
        o_ref[...] = jnp.zeros_like(o_ref)


def _expert_mlp(xs, block_exp, n_used, w_gu, b_gu, w_down, b_down):
    n_rows = xs.shape[0]
    bm = MOE_BLOCK_ROWS
    row = lambda i, be, nu: (i, 0)
    exp3 = lambda i, be, nu: (be[i], 0, 0)
    grid_spec = pltpu.PrefetchScalarGridSpec(
        num_scalar_prefetch=2,
        grid=(n_rows // bm,),
        in_specs=[pl.BlockSpec((bm, D_MODEL), row),
                  pl.BlockSpec((1, D_MODEL, 2 * D_EXPERT), exp3), pl.BlockSpec((1, 1, 2 * D_EXPERT), exp3),
                  pl.BlockSpec((1, D_EXPERT, D_MODEL), exp3), pl.BlockSpec((1, 1, D_MODEL), exp3)],
        out_specs=pl.BlockSpec((bm, D_MODEL), row),
    )
    return pl.pallas_call(
        _expert_kernel,
        grid_spec=grid_spec,
        out_shape=jax.ShapeDtypeStruct((n_rows, D_MODEL), F32),
        compiler_params=_cparams("arbitrary"),
        name="moe_experts",
    )(block_exp, n_used, xs, w_gu.astype(BF16), b_gu.reshape(N_EXPERTS, 1, 2 * D_EXPERT),
      w_down.astype(BF16), b_down.reshape(N_EXPERTS, 1, D_MODEL))


def _combine_kernel(y_ref, gate_ref, res_ref, g_ref, b_ref, o_ref):
    gates = gate_ref[...]
    acc = y_ref[:, :D_MODEL] * gates[:, 0:1]
    for k in range(1, TOP_K):
        acc = acc + y_ref[:, k * D_MODEL:(k + 1) * D_MODEL] * gates[:, k:k + 1]
    o_ref[...] = _layer_norm_rows(DN_ALPHA * res_ref[...] + acc, g_ref[...], b_ref[...])


def _moe_block(h2d, w_router, b_router, w_gu, b_gu, w_down, b_down, g, b):
    n = h2d.shape[0]
    n_slots = n * TOP_K
    bm = MOE_BLOCK_ROWS
    top_idx, gates = _router(h2d, w_router, b_router)

    e_flat = top_idx.reshape(-1)
    order = jnp.argsort(e_flat).astype(I32)
    counts = jnp.bincount(e_flat, length=N_EXPERTS).astype(I32)
    padded = (counts + bm - 1) // bm * bm
    start = jnp.cumsum(counts) - counts
    ends_p = jnp.cumsum(padded)
    pstart = ends_p - padded
    n_rows = n_slots + N_EXPERTS * bm
    n_blocks = n_rows // bm
    r = jnp.arange(n_rows, dtype=I32)
    e_r = jnp.minimum(jnp.searchsorted(ends_p, r, side="right"), N_EXPERTS - 1).astype(I32)
    j = r - pstart[e_r]
    valid = j < counts[e_r]
    slot_of_row = order[jnp.where(valid, start[e_r] + j, 0)]
    rows_tok = jnp.where(valid, slot_of_row // TOP_K, 0).astype(I32)
    e_sorted = e_flat[order]
    dest = pstart[e_sorted] + (jnp.arange(n_slots, dtype=I32) - start[e_sorted])
    slot_pos = jnp.zeros((n_slots,), I32).at[order].set(dest, unique_indices=True)
    block_exp = e_r[::bm]
    n_used = (ends_p[-1] // bm).astype(I32).reshape(1)

    xs = _gather_rows(h2d, rows_tok)
    ys = _expert_mlp(xs, block_exp, n_used, w_gu, b_gu, w_down, b_down)
    yk = _gather_rows(ys, slot_pos).reshape(n, TOP_K * D_MODEL)

    tm = ROW_TILE
    row = lambda i: (i, 0)
    const = lambda i: (0, 0)
    return pl.pallas_call(
        _combine_kernel,
        grid=(n // tm,),
        in_specs=[pl.BlockSpec((tm, TOP_K * D_MODEL), row), pl.BlockSpec((tm, TOP_K), row),
                  pl.BlockSpec((tm, D_MODEL), row), pl.BlockSpec((1, D_MODEL), const),
                  pl.BlockSpec((1, D_MODEL), const)],
        out_specs=pl.BlockSpec((tm, D_MODEL), row),
        out_shape=jax.ShapeDtypeStruct((n, D_MODEL), F32),
        compiler_params=_cparams("parallel"),
        name="moe_combine",
    )(yk, gates, h2d, g.reshape(1, D_MODEL), b.reshape(1, D_MODEL))


def _s5_kernel(u_ref, bre_ref, bim_ref, cre_ref, cim_ref, are_ref, aim_ref, d_ref, y_ref,
               bu_re, bu_im, st_re, st_im, h_re, h_im, *, bsz):
    @pl.when(pl.program_id(0) == 0)
    def _():
        h_re[...] = jnp.zeros_like(h_re)
        h_im[...] = jnp.zeros_like(h_im)

    rows = u_ref.shape[0]
    first = lax.broadcasted_iota(I32, (SUBLANES, S5_ST_BLK), 0) < bsz
    for j in range(S5_LANE_BLOCKS):
        cin = slice(j * S5_IN_BLK, (j + 1) * S5_IN_BLK)
        cst = slice(j * S5_ST_BLK, (j + 1) * S5_ST_BLK)
        uj = u_ref[:, cin]
        ujb = uj.astype(BF16)
        bu_re[...] = _dot(ujb, bre_ref[j])
        bu_im[...] = _dot(ujb, bim_ref[j])
        ar = jnp.broadcast_to(are_ref[:, cst], (SUBLANES, S5_ST_BLK))
        ai = jnp.broadcast_to(aim_ref[:, cst], (SUBLANES, S5_ST_BLK))

        def step(i, carry):
            hr, hi = carry
            r0 = pl.multiple_of(i * SUBLANES, SUBLANES)
            vr = bu_re[pl.ds(r0, SUBLANES), :]
            vi = bu_im[pl.ds(r0, SUBLANES), :]
            h1r = ar * hr - ai * hi + vr
            h1i = ar * hi + ai * hr + vi
            h1rs = pltpu.roll(h1r, bsz, 0)
            h1is = pltpu.roll(h1i, bsz, 0)
            h2r = ar * h1rs - ai * h1is + vr
            h2i = ar * h1is + ai * h1rs + vi
            st_re[pl.ds(r0, SUBLANES), :] = jnp.where(first, h1r, h2r)
            st_im[pl.ds(r0, SUBLANES), :] = jnp.where(first, h1i, h2i)
            return pltpu.roll(h2r, bsz, 0), pltpu.roll(h2i, bsz, 0)

        hr, hi = lax.fori_loop(0, rows // SUBLANES, step, (h_re[:, cst], h_im[:, cst]))
        h_re[:, cst] = hr
        h_im[:, cst] = hi
        yj = _dot(st_re[...].astype(BF16), cre_ref[j]) + _dot(st_im[...].astype(BF16), cim_ref[j])
        yj = yj + d_ref[:, cin] * uj
        y_ref[:, cin] = jax.nn.gelu(yj).astype(BF16)


def _s5_block_diag(w, n_in, n_out):
    gpb = SSM_GROUPS // S5_LANE_BLOCKS
    w4 = w.reshape(S5_LANE_BLOCKS, gpb, n_in, n_out)
    eye = jnp.eye(gpb, dtype=w.dtype)
    return jnp.einsum("jgio,gh->jgiho", w4, eye).reshape(S5_LANE_BLOCKS, gpb * n_in, gpb * n_out)


def _s5_mixer_block(h2d, bsz, seq, w_in, log_dt, lam_re, lam_im, b_re, b_im, c_re, c_im, d, w_out, g, b):
    assert 2 * bsz == SUBLANES, "the scan packs two time steps of bsz rows into one 8-row tile"
    tm = ROW_TILE
    per_seq = seq // tm
    u_t = _matmul(h2d, w_in.astype(BF16), tm=tm, out_dtype=F32, grid=(bsz, per_seq),
                  x_map=lambda bb, i: (bb * per_seq + i, 0), out_map=lambda bb, i: (i, bb),
                  out_shape=(seq, bsz * D_MODEL)).reshape(seq * bsz, D_MODEL)

    dt = jnp.exp(log_dt)[:, None]
    mag = jnp.exp(lam_re * dt)
    a_re, a_im = mag * jnp.cos(lam_im * dt), mag * jnp.sin(lam_im * dt)
    den = lam_re * lam_re + lam_im * lam_im
    coef_re = ((a_re - 1.0) * lam_re + a_im * lam_im) / den
    coef_im = (a_im * lam_re - (a_re - 1.0) * lam_im) / den
    bb_re = coef_re[..., None] * b_re - coef_im[..., None] * b_im
    bb_im = coef_re[..., None] * b_im + coef_im[..., None] * b_re
    bre = _s5_block_diag(jnp.swapaxes(bb_re, 1, 2), SSM_GROUP, SSM_STATE).astype(BF16)
    bim = _s5_block_diag(jnp.swapaxes(bb_im, 1, 2), SSM_GROUP, SSM_STATE).astype(BF16)
    cre = _s5_block_diag(jnp.swapaxes(c_re, 1, 2), SSM_STATE, SSM_GROUP).astype(BF16)
    cim = _s5_block_diag(jnp.swapaxes(-c_im, 1, 2), SSM_STATE, SSM_GROUP).astype(BF16)
    n_state = SSM_GROUPS * SSM_STATE

    rows = S5_CHUNK * bsz
    row = lambda c: (c, 0)
    c2 = lambda c: (0, 0)
    c3 = lambda c: (0, 0, 0)
    y_t = pl.pallas_call(
        functools.partial(_s5_kernel, bsz=bsz),
        grid=(seq // S5_CHUNK,),
        in_specs=[pl.BlockSpec((rows, D_MODEL), row),
                  pl.BlockSpec(bre.shape, c3), pl.BlockSpec(bim.shape, c3),
                  pl.BlockSpec(cre.shape, c3), pl.BlockSpec(cim.shape, c3),
                  pl.BlockSpec((1, n_state), c2), pl.BlockSpec((1, n_state), c2), pl.BlockSpec((1, D_MODEL), c2)],
        out_specs=pl.BlockSpec((rows, D_MODEL), row),
        out_shape=jax.ShapeDtypeStruct((seq * bsz, D_MODEL), BF16),
        scratch_shapes=[pltpu.VMEM((rows, S5_ST_BLK), F32)] * 4 + [pltpu.VMEM((SUBLANES, n_state), F32)] * 2,
        compiler_params=_cparams("arbitrary"),
        name="s5_scan",
    )(u_t, bre, bim, cre, cim, a_re.reshape(1, n_state), a_im.reshape(1, n_state), d.reshape(1, D_MODEL))

    y2 = y_t.reshape(seq, bsz * D_MODEL)
    return _linear_residual_ln(
        [y2], [w_out.astype(BF16)], h2d, g, b, tm=tm, glu=True, grid=(bsz, per_seq),
        x_maps=[lambda bb, i: (i, bb)], res_map=lambda bb, i: (bb * per_seq + i, 0))


def _even_mixer_block(h2d, bsz, seq, w_in, qnorm_g, w_uq, w_uq_idx, kidx_g, kidx_b, w_out, g, b):
    qa, qi, ka, va, ki, wi, qb, kb, vb = _even_proj(h2d, bsz, seq, w_in, qnorm_g, w_uq, w_uq_idx, kidx_g, kidx_b)
    o_a = _dsa_attention(qa, qi, wi, ki, ka, va, bsz, seq)
    o_b = _stick_breaking(qb, kb, vb, bsz, seq)
    o_b = jnp.swapaxes(o_b, 1, 2).reshape(bsz * seq, B_WIDTH)
    w_out = w_out.astype(BF16)
    return _linear_residual_ln([o_a, o_b], [w_out[:A_WIDTH], w_out[A_WIDTH:]], h2d, g, b, tm=ROW_TILE)


def kernel(x, mem, ev_w_in, ev_qnorm_g, ev_w_uq, ev_w_uq_idx, ev_kidx_ln_g, ev_kidx_ln_b, ev_w_out, od_w_in, od_log_dt, od_lambda_re, od_lambda_im, od_b_re, od_b_im, od_c_re, od_c_im, od_d, od_w_out, mix_ln_g, mix_ln_b, xa_w_q, xa_w_kv, xa_w_o, xa_ln_g, xa_ln_b, moe_w_router, moe_b_router, moe_w_gu, moe_b_gu, moe_w_down, moe_b_down, ffn_ln_g, ffn_ln_b):
    bsz, seq, _ = x.shape
    h = x.reshape(bsz * seq, D_MODEL)
    for layer in range(DEPTH):
        j = layer // 2
        if layer % 2 == 0:
            h = _even_mixer_block(h, bsz, seq, ev_w_in[j], ev_qnorm_g[j], ev_w_uq[j], ev_w_uq_idx[j],
                                  ev_kidx_ln_g[j], ev_kidx_ln_b[j], ev_w_out[j], mix_ln_g[layer], mix_ln_b[layer])
        else:
            h = _s5_mixer_block(h, bsz, seq, od_w_in[j], od_log_dt[j], od_lambda_re[j], od_lambda_im[j],
                                od_b_re[j], od_b_im[j], od_c_re[j], od_c_im[j], od_d[j], od_w_out[j],
                                mix_ln_g[layer], mix_ln_b[layer])
        h = _cross_attention_block(h, mem, bsz, seq, xa_w_q[layer], xa_w_kv[layer], xa_w_o[layer],
                                   xa_ln_g[layer], xa_ln_b[layer])
        h = _moe_block(h, moe_w_router[layer], moe_b_router[layer], moe_w_gu[layer], moe_b_gu[layer],
                       moe_w_down[layer], moe_b_down[layer], ffn_ln_g[layer], ffn_ln_b[layer])
    return h.reshape(bsz, seq, D_MODEL)
```

```python
import functools
import math

import jax
import jax.numpy as jnp
from jax import lax
from jax.experimental import pallas as pl
from jax.experimental.pallas import tpu as pltpu
from jax.experimental.pallas import tpu_sc as plsc

F32 = jnp.float32
BF16 = jnp.bfloat16
I32 = jnp.int32

D_MODEL = 1024
DEPTH = 2
HEAD_DIM = 64
A_HEADS = 8
A_KV_HEADS = 2
A_REP = A_HEADS // A_KV_HEADS
Q_RANK = 256
IDX_HEADS = 8
IDX_DIM = 64
IDX_TOPK = 256
B_HEADS = 8
A_WIDTH = A_HEADS * HEAD_DIM
B_WIDTH = B_HEADS * HEAD_DIM
SSM_GROUP = 16
SSM_GROUPS = D_MODEL // SSM_GROUP
SSM_STATE = 64
XA_HEADS = 4
XA_HEAD_DIM = D_MODEL // XA_HEADS
N_EXPERTS = 32
TOP_K = 4
D_EXPERT = D_MODEL
SWIGLU_LIMIT = 7.0
SWIGLU_ALPHA = 1.702
ROPE_THETA = 500000.0
ROPE_HALF = HEAD_DIM // 8
LN_EPS = 1e-5
DN_ALPHA = (2 * DEPTH) ** 0.25

LANES = 128
SUBLANES = 8
VMEM_LIMIT_BYTES = 56 * 1024 * 1024

Q_BLOCK = 128
DSA_KEY_TILE = 512
SB_KEY_TILE = 256
ROW_TILE = 256
MOE_BLOCK_ROWS = 256
SC_GATHER_WINDOW = 32
S5_CHUNK = 128
S5_LANE_BLOCKS = 4
S5_IN_BLK = D_MODEL // S5_LANE_BLOCKS
S5_ST_BLK = SSM_GROUPS * SSM_STATE // S5_LANE_BLOCKS

SB_EXIT_LOG = -104.0
NEG_BIG = -1e30
INT_MIN = -(2 ** 31)


def _cparams(*sem):
    return pltpu.CompilerParams(dimension_semantics=sem, vmem_limit_bytes=VMEM_LIMIT_BYTES)


def _dot(a, b):
    return jnp.dot(a, b, preferred_element_type=F32)


def _dot_nt(a, b):
    return lax.dot_general(a, b, (((1,), (1,)), ((), ())), preferred_element_type=F32)


def _layer_norm_rows(y, g, b):
    mu = jnp.mean(y, axis=-1, keepdims=True)
    d = y - mu
    var = jnp.mean(d * d, axis=-1, keepdims=True)
    return d * lax.rsqrt(var + LN_EPS) * g + b


def _mm_kernel(x_ref, w_ref, o_ref):
    o_ref[...] = _dot(x_ref[...].astype(BF16), w_ref[...]).astype(o_ref.dtype)


def _matmul(x, w, *, tm, out_dtype, x_map=None, out_map=None, grid=None, out_shape=None):
    m, k = x.shape
    n = w.shape[1]
    grid = grid or (m // tm,)
    x_map = x_map or (lambda i: (i, 0))
    out_map = out_map or (lambda i: (i, 0))
    out_shape = out_shape or (m, n)
    return pl.pallas_call(
        _mm_kernel,
        grid=grid,
        in_specs=[pl.BlockSpec((tm, k), x_map), pl.BlockSpec((k, n), lambda *a: (0, 0))],
        out_specs=pl.BlockSpec((tm, n), out_map),
        out_shape=jax.ShapeDtypeStruct(out_shape, out_dtype),
        compiler_params=_cparams(*(("parallel",) * len(grid))),
        name="matmul",
    )(x, w)


def _lin_ln_kernel(*refs, n_in, glu):
    xs, ws = refs[:n_in], refs[n_in:2 * n_in]
    res_ref, g_ref, b_ref, o_ref = refs[2 * n_in:]
    acc = _dot(xs[0][...].astype(BF16), ws[0][...])
    for x_ref, w_ref in zip(xs[1:], ws[1:]):
        acc = acc + _dot(x_ref[...].astype(BF16), w_ref[...])
    if glu:
        acc = acc[:, :D_MODEL] * jax.nn.sigmoid(acc[:, D_MODEL:])
    y = DN_ALPHA * res_ref[...] + acc
    o_ref[...] = _layer_norm_rows(y, g_ref[...], b_ref[...])


def _linear_residual_ln(xs, ws, res, g, b, *, tm, glu=False, grid=None, x_maps=None, res_map=None):
    n_rows = res.shape[0]
    grid = grid or (n_rows // tm,)
    x_maps = x_maps or [lambda i: (i, 0)] * len(xs)
    res_map = res_map or (lambda i: (i, 0))
    const = lambda *a: (0, 0)
    in_specs = [pl.BlockSpec((tm, w.shape[0]), m) for w, m in zip(ws, x_maps)]
    in_specs += [pl.BlockSpec(w.shape, const) for w in ws]
    in_specs += [pl.BlockSpec((tm, D_MODEL), res_map), pl.BlockSpec((1, D_MODEL), const),
                 pl.BlockSpec((1, D_MODEL), const)]
    return pl.pallas_call(
        functools.partial(_lin_ln_kernel, n_in=len(xs), glu=glu),
        grid=grid,
        in_specs=in_specs,
        out_specs=pl.BlockSpec((tm, D_MODEL), res_map),
        out_shape=jax.ShapeDtypeStruct((n_rows, D_MODEL), F32),
        compiler_params=_cparams(*(("parallel",) * len(grid))),
        name="linear_residual_ln",
    )(*xs, *ws, res, g.reshape(1, D_MODEL), b.reshape(1, D_MODEL))


_EV_CQ, _EV_KA, _EV_VA, _EV_KI, _EV_QB = 0, 256, 384, 512, 640
_EV_KB = _EV_QB + B_WIDTH
_EV_VB = _EV_KB + B_WIDTH
_EV_COLS = _EV_VB + B_WIDTH


def _rope_tables(seq):
    inv = ROPE_THETA ** (-jnp.arange(ROPE_HALF, dtype=F32) / ROPE_HALF)
    ang = jnp.arange(seq, dtype=F32)[:, None] * inv[None, :]
    cos, sin = jnp.cos(ang), jnp.sin(ang)
    rest = HEAD_DIM - 2 * ROPE_HALF
    zh = jnp.zeros((seq, ROPE_HALF), F32)
    c = jnp.concatenate([cos, cos, jnp.ones((seq, rest), F32)], axis=1)
    s1 = jnp.concatenate([-sin, zh, jnp.zeros((seq, rest), F32)], axis=1)
    s2 = jnp.concatenate([zh, sin, jnp.zeros((seq, rest), F32)], axis=1)
    rep = LANES // HEAD_DIM
    return jnp.tile(c, (1, rep)), jnp.tile(s1, (1, rep)), jnp.tile(s2, (1, rep))


def _even_proj_kernel(x_ref, w_ref, qg_ref, wuq_ref, wuqi_ref, lg_ref, lb_ref, c_ref, s1_ref, s2_ref,
                      qa_ref, qi_ref, ka_ref, va_ref, ki_ref, wi_ref, qb_ref, kb_ref, vb_ref):
    p = _dot(x_ref[...].astype(BF16), w_ref[...])
    c, s1, s2 = c_ref[...], s1_ref[...], s2_ref[...]

    def rope(t):
        return (t * c + pltpu.roll(t, LANES - ROPE_HALF, 1) * s1 + pltpu.roll(t, ROPE_HALF, 1) * s2)

    cq = p[:, _EV_CQ:_EV_CQ + Q_RANK]
    cn = cq * lax.rsqrt(jnp.mean(cq * cq, axis=-1, keepdims=True) + LN_EPS) * qg_ref[...]
    cnb = cn.astype(BF16)
    qa = _dot(cnb, wuq_ref[...])
    qi = _dot(cnb, wuqi_ref[...])
    for j in range(A_WIDTH // LANES):
        sl = slice(j * LANES, (j + 1) * LANES)
        qa_ref[:, sl] = (rope(qa[:, sl]) * (HEAD_DIM ** -0.5)).astype(BF16)
        qi_ref[:, sl] = (rope(qi[:, sl]) * (IDX_DIM ** -0.5)).astype(BF16)
    ka_ref[...] = rope(p[:, _EV_KA:_EV_KA + LANES]).astype(BF16)
    va_ref[...] = p[:, _EV_VA:_EV_VA + LANES].astype(BF16)

    t = p[:, _EV_KI:_EV_KI + LANES]
    lane = lax.broadcasted_iota(I32, t.shape, 1)
    is_k = lane < IDX_DIM
    mu = jnp.sum(jnp.where(is_k, t, 0.0), axis=-1, keepdims=True) * (1.0 / IDX_DIM)
    d = jnp.where(is_k, t - mu, 0.0)
    var = jnp.sum(d * d, axis=-1, keepdims=True) * (1.0 / IDX_DIM)
    kin = d * lax.rsqrt(var + LN_EPS) * lg_ref[...] + lb_ref[...]
    ki_ref[...] = rope(kin)[:, :IDX_DIM].astype(BF16)
    wi_ref[...] = t[:, IDX_DIM:IDX_DIM + IDX_HEADS] * (IDX_HEADS ** -0.5)

    qb_ref[...] = (p[:, _EV_QB:_EV_KB] * (HEAD_DIM ** -0.5)).astype(BF16)
    kb_ref[...] = p[:, _EV_KB:_EV_VB].astype(BF16)
    vb_ref[...] = p[:, _EV_VB:_EV_COLS].astype(BF16)


def _even_proj(x2d, bsz, seq, w_in, qnorm_g, w_uq, w_uq_idx, kidx_g, kidx_b):
    n = x2d.shape[0]
    tm = ROW_TILE
    per_seq = seq // tm
    c0 = Q_RANK + 2 * A_KV_HEADS * HEAD_DIM + IDX_DIM + IDX_HEADS
    w_pack = jnp.concatenate(
        [w_in[:, :c0], jnp.zeros((D_MODEL, _EV_QB - c0), w_in.dtype), w_in[:, c0:]], axis=1).astype(BF16)
    pad = LANES - IDX_DIM
    lg = jnp.concatenate([kidx_g, jnp.zeros((pad,), F32)]).reshape(1, LANES)
    lb = jnp.concatenate([kidx_b, jnp.zeros((pad,), F32)]).reshape(1, LANES)
    c, s1, s2 = _rope_tables(seq)
    row = lambda i: (i, 0)
    const = lambda i: (0, 0)
    pos = lambda i: (i % per_seq, 0)
    head_shape = jax.ShapeDtypeStruct((n, B_WIDTH), BF16)
    head_spec = pl.BlockSpec((tm, B_WIDTH), row)
    return pl.pallas_call(
        _even_proj_kernel,
        grid=(n // tm,),
        in_specs=[pl.BlockSpec((tm, D_MODEL), row), pl.BlockSpec((D_MODEL, _EV_COLS), const),
                  pl.BlockSpec((1, Q_RANK), const), pl.BlockSpec((Q_RANK, A_WIDTH), const),
                  pl.BlockSpec((Q_RANK, IDX_HEADS * IDX_DIM), const),
                  pl.BlockSpec((1, LANES), const), pl.BlockSpec((1, LANES), const),
                  pl.BlockSpec((tm, LANES), pos), pl.BlockSpec((tm, LANES), pos), pl.BlockSpec((tm, LANES), pos)],
        out_specs=[pl.BlockSpec((tm, A_WIDTH), row), pl.BlockSpec((tm, IDX_HEADS * IDX_DIM), row),
                   pl.BlockSpec((tm, LANES), row), pl.BlockSpec((tm, LANES), row),
                   pl.BlockSpec((tm, IDX_DIM), row), pl.BlockSpec((tm, IDX_HEADS), row),
                   head_spec, head_spec, head_spec],
        out_shape=[jax.ShapeDtypeStruct((n, A_WIDTH), BF16), jax.ShapeDtypeStruct((n, IDX_HEADS * IDX_DIM), BF16),
                   jax.ShapeDtypeStruct((n, LANES), BF16), jax.ShapeDtypeStruct((n, LANES), BF16),
                   jax.ShapeDtypeStruct((n, IDX_DIM), BF16), jax.ShapeDtypeStruct((n, IDX_HEADS), F32),
                   head_shape, head_shape, head_shape],
        compiler_params=_cparams("parallel"),
        name="even_proj",
    )(x2d, w_pack, qnorm_g.reshape(1, Q_RANK), w_uq.astype(BF16), w_uq_idx.astype(BF16), lg, lb, c, s1, s2)


def _key_to_float(key):
    bits = key ^ ((key >> 31) & jnp.int32(0x7FFFFFFF))
    return lax.bitcast_convert_type(bits, F32)


def _dsa_kernel(qa_ref, qi_ref, wi_ref, ki_ref, ka_ref, va_ref, o_ref, sc_scr, *, topk, ts):
    qb = pl.program_id(1)
    q0 = qb * Q_BLOCK
    nkt = (q0 + Q_BLOCK - 1) // ts + 1
    t_col = q0 + lax.broadcasted_iota(I32, (Q_BLOCK, 1), 0)
    lane = lax.broadcasted_iota(I32, (Q_BLOCK, ts), 1)
    kf = jnp.float32(topk)

    qi = qi_ref[0]
    wi = wi_ref[0]
    qih = [qi[:, h * IDX_DIM:(h + 1) * IDX_DIM] for h in range(IDX_HEADS)]
    wih = [wi[:, h:h + 1] for h in range(IDX_HEADS)]

    def score_tile(kt, carry):
        off = pl.multiple_of(kt * ts, ts)
        kit = ki_ref[0, pl.ds(off, ts), :]
        acc = jnp.zeros((Q_BLOCK, ts), F32)
        for h in range(IDX_HEADS):
            acc = acc + jnp.maximum(_dot_nt(qih[h], kit), 0.0) * wih[h]
        sc_scr[:, pl.ds(off, ts)] = jnp.where(off + lane <= t_col, acc, -jnp.inf)
        return carry

    lax.fori_loop(0, nkt, score_tile, 0)

    def count(pred):
        def body(kt, acc):
            off = pl.multiple_of(kt * ts, ts)
            ind = pred(sc_scr[:, pl.ds(off, ts)], off + lane)
            for j in range(ts // LANES):
                acc = acc + ind[:, j * LANES:(j + 1) * LANES]
            return acc
        acc = lax.fori_loop(0, nkt, body, jnp.zeros((Q_BLOCK, LANES), F32))
        return jnp.sum(acc, axis=1, keepdims=True)

    def bit_step(i, base):
        cand = base + jnp.left_shift(jnp.int32(1), 31 - i)
        cf = _key_to_float(cand)
        cnt = count(lambda sc, idx: jnp.where(sc >= cf, 1.0, 0.0))
        return jnp.where(cnt >= kf, cand, base)

    base = lax.fori_loop(0, 32, bit_step, jnp.full((Q_BLOCK, 1), INT_MIN, I32))
    thr = jnp.where(base == INT_MIN, -jnp.inf, _key_to_float(base))

    cnt_ge = count(lambda sc, idx: jnp.where(sc >= thr, 1.0, 0.0))
    tied = jnp.logical_and(cnt_ge > kf, thr > -jnp.inf)
    any_tied = jnp.max(jnp.where(tied, 1.0, 0.0)) > 0.0
    seq_bits = max(1, int(math.ceil(math.log2(sc_scr.shape[1]))))

    def tie_cut():
        cnt_gt = count(lambda sc, idx: jnp.where(sc > thr, 1.0, 0.0))
        need = kf - cnt_gt

        def idx_step(i, pos):
            cand = pos + jnp.left_shift(jnp.int32(1), seq_bits - 1 - i)
            cnt = count(lambda sc, idx: jnp.where(sc == thr, jnp.where(idx < cand, 1.0, 0.0), 0.0))
            return jnp.where(cnt < need, cand, pos)

        return lax.fori_loop(0, seq_bits, idx_step, jnp.zeros((Q_BLOCK, 1), I32))

    cut = lax.cond(any_tied, tie_cut, lambda: jnp.full((Q_BLOCK, 1), sc_scr.shape[1], I32))
    cut = jnp.where(tied, cut, sc_scr.shape[1])

    qa = qa_ref[0]
    for g in range(A_KV_HEADS):
        heads = [qa[:, (g * A_REP + r) * HEAD_DIM:(g * A_REP + r + 1) * HEAD_DIM] for r in range(A_REP)]
        qg = jnp.concatenate(heads, axis=0)

        def att_tile(kt, carry):
            m, l, acc = carry
            off = pl.multiple_of(kt * ts, ts)
            sc = sc_scr[:, pl.ds(off, ts)]
            idx = off + lane
            keep = jnp.where(sc > thr, 0.0, jnp.where(sc == thr, jnp.where(idx <= cut, 0.0, NEG_BIG), NEG_BIG))
            bias = jnp.where(idx <= t_col, keep, NEG_BIG)
            kg = ka_ref[0, pl.ds(off, ts), g * HEAD_DIM:(g + 1) * HEAD_DIM]
            vg = va_ref[0, pl.ds(off, ts), g * HEAD_DIM:(g + 1) * HEAD_DIM]
            s = _dot_nt(qg, kg)
            s = (s.reshape(A_REP, Q_BLOCK, ts) + bias[None]).reshape(A_REP * Q_BLOCK, ts)
            m_new = jnp.maximum(m, jnp.max(s, axis=1, keepdims=True))
            p = jnp.exp(s - m_new)
            alpha = jnp.exp(m - m_new)
            l = alpha * l + jnp.sum(p, axis=1, keepdims=True)
            acc = alpha * acc + _dot(p.astype(BF16), vg)
            return m_new, l, acc

        rows = A_REP * Q_BLOCK
        init = (jnp.full((rows, 1), NEG_BIG, F32), jnp.zeros((rows, 1), F32), jnp.zeros((rows, HEAD_DIM), F32))
        _, l, acc = lax.fori_loop(0, nkt, att_tile, init)
        og = acc / l
        for r in range(A_REP):
            hcol = (g * A_REP + r) * HEAD_DIM
            o_ref[0, :, hcol:hcol + HEAD_DIM] = og[r * Q_BLOCK:(r + 1) * Q_BLOCK].astype(BF16)


def _dsa_attention(qa, qi, wi, ki, ka, va, bsz, seq):
    topk = min(IDX_TOPK, seq // 4)
    ts = min(DSA_KEY_TILE, seq)
    blk = lambda b, i: (b, i, 0)
    full = lambda b, i: (b, 0, 0)
    r3 = lambda a: a.reshape(bsz, seq, a.shape[-1])
    return pl.pallas_call(
        functools.partial(_dsa_kernel, topk=topk, ts=ts),
        grid=(bsz, seq // Q_BLOCK),
        in_specs=[pl.BlockSpec((1, Q_BLOCK, A_WIDTH), blk), pl.BlockSpec((1, Q_BLOCK, IDX_HEADS * IDX_DIM), blk),
                  pl.BlockSpec((1, Q_BLOCK, IDX_HEADS), blk), pl.BlockSpec((1, seq, IDX_DIM), full),
                  pl.BlockSpec((1, seq, LANES), full), pl.BlockSpec((1, seq, LANES), full)],
        out_specs=pl.BlockSpec((1, Q_BLOCK, A_WIDTH), blk),
        out_shape=jax.ShapeDtypeStruct((bsz, seq, A_WIDTH), BF16),
        scratch_shapes=[pltpu.VMEM((Q_BLOCK, seq), F32)],
        compiler_params=_cparams("parallel", "parallel"),
        name="dsa_attention",
    )(r3(qa), r3(qi), r3(wi), r3(ki), r3(ka), r3(va)).reshape(bsz * seq, A_WIDTH)


def _sb_kernel(q_ref, k_ref, v_ref, u_ref, o_ref, acc_scr, run_scr, *, tk):
    q0 = pl.program_id(1) * Q_BLOCK
    t_col = q0 + lax.broadcasted_iota(I32, (Q_BLOCK, 1), 0)
    lane = lax.broadcasted_iota(I32, (Q_BLOCK, tk), 1)
    low = lax.broadcasted_iota(I32, (Q_BLOCK, LANES), 1) < HEAD_DIM
    upper = u_ref[...]
    nkt = (q0 + Q_BLOCK - 1) // tk + 1
    q = q_ref[0]
    zero = jnp.zeros((Q_BLOCK, LANES), BF16)
    qm = []
    for h in range(B_HEADS):
        pair = q[:, (h // 2) * LANES:(h // 2 + 1) * LANES]
        qm.append(jnp.where(low, pair, zero) if h % 2 == 0 else jnp.where(low, zero, pair))
    acc_scr[...] = jnp.zeros_like(acc_scr)
    run_scr[...] = jnp.zeros_like(run_scr)

    def cond(carry):
        i, worst = carry
        return jnp.logical_and(i < nkt, worst >= SB_EXIT_LOG)

    def body(carry):
        i, _ = carry
        off = pl.multiple_of((nkt - 1 - i) * tk, tk)
        strict = off + lane < t_col
        worst = None
        for p in range(B_HEADS // 2):
            cols = slice(p * LANES, (p + 1) * LANES)
            kp = k_ref[0, pl.ds(off, tk), cols]
            vp = v_ref[0, pl.ds(off, tk), cols]
            outs = []
            for e in range(2):
                h = 2 * p + e
                run = run_scr[h]
                z = _dot_nt(qm[h], kp)
                softplus = jnp.maximum(z, 0.0) + jnp.log(1.0 + jnp.exp(-jnp.abs(z)))
                log_1mb = jnp.where(strict, -softplus, 0.0)
                hi = log_1mb.astype(BF16)
                lo = (log_1mb - hi.astype(F32)).astype(BF16)
                after = _dot(hi, upper) + _dot(lo, upper) + run
                a = jnp.where(strict, jnp.exp(z - softplus + after), 0.0)
                outs.append(_dot(a.astype(BF16), vp))
                run = run + jnp.sum(log_1mb, axis=1, keepdims=True)
                run_scr[h] = run
                worst = run if worst is None else jnp.maximum(worst, run)
            acc_scr[:, cols] += jnp.where(low, outs[0], outs[1])
        return i + 1, jnp.max(worst)

    lax.while_loop(cond, body, (jnp.int32(0), jnp.float32(0.0)))
    o_ref[0] = acc_scr[...].astype(BF16)


def _stick_breaking(qb, kb, vb, bsz, seq):
    tk = min(SB_KEY_TILE, seq)
    r = lax.broadcasted_iota(I32, (tk, tk), 0)
    c = lax.broadcasted_iota(I32, (tk, tk), 1)
    upper = jnp.where(r > c, 1.0, 0.0).astype(BF16)
    blk = lambda b, i: (b, i, 0)
    full = lambda b, i: (b, 0, 0)
    r3 = lambda a: a.reshape(bsz, seq, B_WIDTH)
    return pl.pallas_call(
        functools.partial(_sb_kernel, tk=tk),
        grid=(bsz, seq // Q_BLOCK),
        in_specs=[pl.BlockSpec((1, Q_BLOCK, B_WIDTH), blk), pl.BlockSpec((1, seq, B_WIDTH), full),
                  pl.BlockSpec((1, seq, B_WIDTH), full), pl.BlockSpec((tk, tk), lambda b, i: (0, 0))],
        out_specs=pl.BlockSpec((1, Q_BLOCK, B_WIDTH), blk),
        out_shape=jax.ShapeDtypeStruct((bsz, seq, B_WIDTH), BF16),
        scratch_shapes=[pltpu.VMEM((Q_BLOCK, B_WIDTH), F32), pltpu.VMEM((B_HEADS, Q_BLOCK, 1), F32)],
        compiler_params=_cparams("parallel", "arbitrary"),
        name="stick_breaking",
    )(r3(qb), r3(kb), r3(vb), upper).reshape(bsz * seq, B_WIDTH)


def _xattn_kernel(h_ref, wq_ref, kv_ref, wo_ref, g_ref, b_ref, o_ref):
    h = h_ref[...]
    q = (_dot(h.astype(BF16), wq_ref[...]) * (XA_HEAD_DIM ** -0.5)).astype(BF16)
    kv = kv_ref[0]
    outs = []
    for hd in range(XA_HEADS):
        sl = slice(hd * XA_HEAD_DIM, (hd + 1) * XA_HEAD_DIM)
        s = _dot_nt(q[:, sl], kv[:, sl])
        p = jnp.exp(s - jnp.max(s, axis=1, keepdims=True))
        vh = kv[:, D_MODEL + hd * XA_HEAD_DIM:D_MODEL + (hd + 1) * XA_HEAD_DIM]
        outs.append((_dot(p.astype(BF16), vh) / jnp.sum(p, axis=1, keepdims=True)).astype(BF16))
    y = _dot(jnp.concatenate(outs, axis=1), wo_ref[...])
    o_ref[...] = _layer_norm_rows(DN_ALPHA * h + y, g_ref[...], b_ref[...])


def _cross_attention_block(h2d, mem, bsz, seq, w_q, w_kv, w_o, g, b):
    tm = ROW_TILE
    per_seq = seq // tm
    mem_len = mem.shape[1]
    kv = _matmul(mem.reshape(bsz * mem_len, D_MODEL), w_kv.astype(BF16), tm=mem_len, out_dtype=BF16)
    kv = kv.reshape(bsz, mem_len, 2 * D_MODEL)
    row = lambda i: (i, 0)
    const = lambda i: (0, 0)
    return pl.pallas_call(
        _xattn_kernel,
        grid=(bsz * per_seq,),
        in_specs=[pl.BlockSpec((tm, D_MODEL), row), pl.BlockSpec((D_MODEL, D_MODEL), const),
                  pl.BlockSpec((1, mem_len, 2 * D_MODEL), lambda i: (i // per_seq, 0, 0)),
                  pl.BlockSpec((D_MODEL, D_MODEL), const),
                  pl.BlockSpec((1, D_MODEL), const), pl.BlockSpec((1, D_MODEL), const)],
        out_specs=pl.BlockSpec((tm, D_MODEL), row),
        out_shape=jax.ShapeDtypeStruct(h2d.shape, F32),
        compiler_params=_cparams("parallel"),
        name="cross_attention",
    )(h2d, w_q.astype(BF16), kv, w_o.astype(BF16), g.reshape(1, D_MODEL), b.reshape(1, D_MODEL))


def _router_kernel(h_ref, w_ref, b_ref, idx_ref, gate_ref):
    h = h_ref[...]
    hh = h.astype(BF16)
    hl = (h - hh.astype(F32)).astype(BF16)
    w = w_ref[...]
    wh = w.astype(BF16)
    wl = (w - wh.astype(F32)).astype(BF16)
    logits = _dot(hh, wh) + _dot(hl, wh) + _dot(hh, wl) + b_ref[...]
    lane = lax.broadcasted_iota(I32, logits.shape, 1).astype(F32)
    vals = []
    for k in range(TOP_K):
        m = jnp.max(logits, axis=1, keepdims=True)
        sel = jnp.min(jnp.where(logits == m, lane, float(LANES)), axis=1, keepdims=True)
        idx_ref[:, k:k + 1] = sel.astype(I32)
        vals.append(m)
        logits = jnp.where(lane == sel, -jnp.inf, logits)
    es = [jnp.exp(v - vals[0]) for v in vals]
    tot = es[0] + es[1] + es[2] + es[3]
    for k in range(TOP_K):
        gate_ref[:, k:k + 1] = es[k] / tot


def _router(h2d, w_router, b_router):
    n = h2d.shape[0]
    tm = 2 * ROW_TILE
    pad = LANES - N_EXPERTS
    w = jnp.concatenate([w_router, jnp.zeros((D_MODEL, pad), F32)], axis=1)
    b = jnp.concatenate([b_router, jnp.full((pad,), NEG_BIG, F32)]).reshape(1, LANES)
    row = lambda i: (i, 0)
    const = lambda i: (0, 0)
    return pl.pallas_call(
        _router_kernel,
        grid=(n // tm,),
        in_specs=[pl.BlockSpec((tm, D_MODEL), row), pl.BlockSpec((D_MODEL, LANES), const),
                  pl.BlockSpec((1, LANES), const)],
        out_specs=[pl.BlockSpec((tm, TOP_K), row), pl.BlockSpec((tm, TOP_K), row)],
        out_shape=[jax.ShapeDtypeStruct((n, TOP_K), I32), jax.ShapeDtypeStruct((n, TOP_K), F32)],
        compiler_params=_cparams("parallel"),
        name="moe_router",
    )(h2d, w, b)


def _gather_rows(src, idx):
    n_out = idx.shape[0]
    width = src.shape[1]
    win = SC_GATHER_WINDOW
    mesh = plsc.VectorSubcoreMesh(core_axis_name="core", subcore_axis_name="subcore")
    n_workers = mesh.num_cores * mesh.num_subcores
    per_worker = n_out // n_workers
    steps = per_worker // win
    assert per_worker * n_workers == n_out and steps * win == per_worker and steps % 2 == 0

    @functools.partial(
        pl.kernel, out_type=jax.ShapeDtypeStruct((n_out, width), src.dtype), mesh=mesh,
        scratch_types=[pltpu.VMEM((per_worker,), I32), pltpu.VMEM((2, win, width), src.dtype),
                       pltpu.SemaphoreType.DMA, pltpu.SemaphoreType.DMA])
    def gather_kernel(src_hbm, idx_hbm, dst_hbm, idx_v, rows_v, sem0, sem1):
        worker = lax.axis_index("subcore") * mesh.num_cores + lax.axis_index("core")
        base = worker * per_worker
        sems = (sem0, sem1)
        pltpu.sync_copy(idx_hbm.at[pl.ds(base, per_worker)], idx_v)

        def gather(step, slot):
            return pltpu.make_async_copy(src_hbm.at[idx_v.at[pl.ds(step * win, win)]], rows_v.at[slot], sems[slot])

        gather(0, 0).start()

        @pl.loop(0, steps, step=2)
        def _(s):
            for slot in range(2):
                step = s + slot
                gather(step, slot).wait()

                @pl.when(step + 1 < steps)
                def _():
                    gather(step + 1, 1 - slot).start()

                pltpu.sync_copy(rows_v.at[slot], dst_hbm.at[pl.ds(base + step * win, win)])

    return gather_kernel(src, idx)


def _expert_kernel(blk_exp_ref, n_used_ref, x_ref, wgu_ref, bgu_ref, wd_ref, bd_ref, o_ref):
    i = pl.program_id(0)

    @pl.when(i < n_used_ref[0])
    def _():
        hgu = _dot(x_ref[...].astype(BF16), wgu_ref[0]) + bgu_ref[0]
        gate = jnp.minimum(hgu[:, :D_EXPERT], SWIGLU_LIMIT)
        up = jnp.clip(hgu[:, D_EXPERT:], -SWIGLU_LIMIT, SWIGLU_LIMIT)
        act = gate * jax.nn.sigmoid(gate * SWIGLU_ALPHA) * (up + 1.0)
        o_ref[...] = _dot(act.astype(BF16), wd_ref[0]) + bd_ref[0]

    @pl.when(i >= n_used_ref[0])
    def _():
        o_ref[...] = jnp.zeros_like(o_ref)


def _expert_mlp(xs, block_exp, n_used, w_gu, b_gu, w_down, b_down):
    n_rows = xs.shape[0]
    bm = MOE_BLOCK_ROWS
    row = lambda i, be, nu: (i, 0)
    exp3 = lambda i, be, nu: (be[i], 0, 0)
    grid_spec = pltpu.PrefetchScalarGridSpec(
        num_scalar_prefetch=2,
        grid=(n_rows // bm,),
        in_specs=[pl.BlockSpec((bm, D_MODEL), row),
                  pl.BlockSpec((1, D_MODEL, 2 * D_EXPERT), exp3), pl.BlockSpec((1, 1, 2 * D_EXPERT), exp3),
                  pl.BlockSpec((1, D_EXPERT, D_MODEL), exp3), pl.BlockSpec((1, 1, D_MODEL), exp3)],
        out_specs=pl.BlockSpec((bm, D_MODEL), row),
    )
    return pl.pallas_call(
        _expert_kernel,
        grid_spec=grid_spec,
        out_shape=jax.ShapeDtypeStruct((n_rows, D_MODEL), F32),
        compiler_params=_cparams("arbitrary"),
        name="moe_experts",
    )(block_exp, n_used, xs, w_gu.astype(BF16), b_gu.reshape(N_EXPERTS, 1, 2 * D_EXPERT),
      w_down.astype(BF16), b_down.reshape(N_EXPERTS, 1, D_MODEL))


def _combine_kernel(y0_ref, y1_ref, y2_ref, y3_ref, gate_ref, res_ref, g_ref, b_ref, o_ref):
    gates = gate_ref[...]
    acc = y0_ref[...] * gates[:, 0:1]
    for k, y_ref in enumerate((y1_ref, y2_ref, y3_ref), start=1):
        acc = acc + y_ref[...] * gates[:, k:k + 1]
    o_ref[...] = _layer_norm_rows(DN_ALPHA * res_ref[...] + acc, g_ref[...], b_ref[...])


def _moe_block(h2d, w_router, b_router, w_gu, b_gu, w_down, b_down, g, b):
    n = h2d.shape[0]
    n_slots = n * TOP_K
    bm = MOE_BLOCK_ROWS
    top_idx, gates = _router(h2d, w_router, b_router)

    e_flat = top_idx.reshape(-1)
    order = jnp.argsort(e_flat).astype(I32)
    counts = jnp.bincount(e_flat, length=N_EXPERTS).astype(I32)
    padded = (counts + bm - 1) // bm * bm
    start = jnp.cumsum(counts) - counts
    ends_p = jnp.cumsum(padded)
    pstart = ends_p - padded
    n_rows = n_slots + N_EXPERTS * bm
    n_blocks = n_rows // bm
    r = jnp.arange(n_rows, dtype=I32)
    e_r = jnp.minimum(jnp.searchsorted(ends_p, r, side="right"), N_EXPERTS - 1).astype(I32)
    j = r - pstart[e_r]
    valid = j < counts[e_r]
    slot_of_row = order[jnp.where(valid, start[e_r] + j, 0)]
    rows_tok = jnp.where(valid, slot_of_row // TOP_K, 0).astype(I32)
    e_sorted = e_flat[order]
    dest = pstart[e_sorted] + (jnp.arange(n_slots, dtype=I32) - start[e_sorted])
    slot_pos = jnp.zeros((n_slots,), I32).at[order].set(dest, unique_indices=True)
    block_exp = e_r[::bm]
    n_used = (ends_p[-1] // bm).astype(I32).reshape(1)

    xs = _gather_rows(h2d, rows_tok)
    ys = _expert_mlp(xs, block_exp, n_used, w_gu, b_gu, w_down, b_down)
    yk = _gather_rows(ys, slot_pos.reshape(n, TOP_K).T.reshape(-1))

    tm = ROW_TILE
    row = lambda i: (i, 0)
    const = lambda i: (0, 0)
    choice = lambda k: (lambda i: (k * (n // tm) + i, 0))
    return pl.pallas_call(
        _combine_kernel,
        grid=(n // tm,),
        in_specs=[pl.BlockSpec((tm, D_MODEL), choice(k)) for k in range(TOP_K)] + [
                  pl.BlockSpec((tm, TOP_K), row),
                  pl.BlockSpec((tm, D_MODEL), row), pl.BlockSpec((1, D_MODEL), const),
                  pl.BlockSpec((1, D_MODEL), const)],
        out_specs=pl.BlockSpec((tm, D_MODEL), row),
        out_shape=jax.ShapeDtypeStruct((n, D_MODEL), F32),
        compiler_params=_cparams("parallel"),
        name="moe_combine",
    )(yk, yk, yk, yk, gates, h2d, g.reshape(1, D_MODEL), b.reshape(1, D_MODEL))


def _s5_kernel(u_ref, bre_ref, bim_ref, cre_ref, cim_ref, are_ref, aim_ref, d_ref, y_ref,
               bu_re, bu_im, st_re, st_im, h_re, h_im, *, bsz):
    @pl.when(pl.program_id(0) == 0)
    def _():
        h_re[...] = jnp.zeros_like(h_re)
        h_im[...] = jnp.zeros_like(h_im)

    rows = u_ref.shape[0]
    first = lax.broadcasted_iota(I32, (SUBLANES, S5_ST_BLK), 0) < bsz
    for j in range(S5_LANE_BLOCKS):
        cin = slice(j * S5_IN_BLK, (j + 1) * S5_IN_BLK)
        cst = slice(j * S5_ST_BLK, (j + 1) * S5_ST_BLK)
        uj = u_ref[:, cin]
        ujb = uj.astype(BF16)
        bu_re[...] = _dot(ujb, bre_ref[j])
        bu_im[...] = _dot(ujb, bim_ref[j])
        ar = jnp.broadcast_to(are_ref[:, cst], (SUBLANES, S5_ST_BLK))
        ai = jnp.broadcast_to(aim_ref[:, cst], (SUBLANES, S5_ST_BLK))

        def step(i, carry):
            hr, hi = carry
            r0 = pl.multiple_of(i * SUBLANES, SUBLANES)
            vr = bu_re[pl.ds(r0, SUBLANES), :]
            vi = bu_im[pl.ds(r0, SUBLANES), :]
            h1r = ar * hr - ai * hi + vr
            h1i = ar * hi + ai * hr + vi
            h1rs = pltpu.roll(h1r, bsz, 0)
            h1is = pltpu.roll(h1i, bsz, 0)
            h2r = ar * h1rs - ai * h1is + vr
            h2i = ar * h1is + ai * h1rs + vi
            st_re[pl.ds(r0, SUBLANES), :] = jnp.where(first, h1r, h2r)
            st_im[pl.ds(r0, SUBLANES), :] = jnp.where(first, h1i, h2i)
            return pltpu.roll(h2r, bsz, 0), pltpu.roll(h2i, bsz, 0)

        hr, hi = lax.fori_loop(0, rows // SUBLANES, step, (h_re[:, cst], h_im[:, cst]))
        h_re[:, cst] = hr
        h_im[:, cst] = hi
        yj = _dot(st_re[...].astype(BF16), cre_ref[j]) + _dot(st_im[...].astype(BF16), cim_ref[j])
        yj = yj + d_ref[:, cin] * uj
        y_ref[:, cin] = jax.nn.gelu(yj).astype(BF16)


def _s5_block_diag(w, n_in, n_out):
    gpb = SSM_GROUPS // S5_LANE_BLOCKS
    w4 = w.reshape(S5_LANE_BLOCKS, gpb, n_in, n_out)
    eye = jnp.eye(gpb, dtype=w.dtype)
    return jnp.einsum("jgio,gh->jgiho", w4, eye).reshape(S5_LANE_BLOCKS, gpb * n_in, gpb * n_out)


def _s5_mixer_block(h2d, bsz, seq, w_in, log_dt, lam_re, lam_im, b_re, b_im, c_re, c_im, d, w_out, g, b):
    assert 2 * bsz == SUBLANES, "the scan packs two time steps of bsz rows into one 8-row tile"
    tm = ROW_TILE
    per_seq = seq // tm
    u_t = _matmul(h2d, w_in.astype(BF16), tm=tm, out_dtype=F32, grid=(bsz, per_seq),
                  x_map=lambda bb, i: (bb * per_seq + i, 0), out_map=lambda bb, i: (i, bb),
                  out_shape=(seq, bsz * D_MODEL)).reshape(seq * bsz, D_MODEL)

    dt = jnp.exp(log_dt)[:, None]
    mag = jnp.exp(lam_re * dt)
    a_re, a_im = mag * jnp.cos(lam_im * dt), mag * jnp.sin(lam_im * dt)
    den = lam_re * lam_re + lam_im * lam_im
    coef_re = ((a_re - 1.0) * lam_re + a_im * lam_im) / den
    coef_im = (a_im * lam_re - (a_re - 1.0) * lam_im) / den
    bb_re = coef_re[..., None] * b_re - coef_im[..., None] * b_im
    bb_im = coef_re[..., None] * b_im + coef_im[..., None] * b_re
    bre = _s5_block_diag(jnp.swapaxes(bb_re, 1, 2), SSM_GROUP, SSM_STATE).astype(BF16)
    bim = _s5_block_diag(jnp.swapaxes(bb_im, 1, 2), SSM_GROUP, SSM_STATE).astype(BF16)
    cre = _s5_block_diag(jnp.swapaxes(c_re, 1, 2), SSM_STATE, SSM_GROUP).astype(BF16)
    cim = _s5_block_diag(jnp.swapaxes(-c_im, 1, 2), SSM_STATE, SSM_GROUP).astype(BF16)
    n_state = SSM_GROUPS * SSM_STATE

    rows = S5_CHUNK * bsz
    row = lambda c: (c, 0)
    c2 = lambda c: (0, 0)
    c3 = lambda c: (0, 0, 0)
    y_t = pl.pallas_call(
        functools.partial(_s5_kernel, bsz=bsz),
        grid=(seq // S5_CHUNK,),
        in_specs=[pl.BlockSpec((rows, D_MODEL), row),
                  pl.BlockSpec(bre.shape, c3), pl.BlockSpec(bim.shape, c3),
                  pl.BlockSpec(cre.shape, c3), pl.BlockSpec(cim.shape, c3),
                  pl.BlockSpec((1, n_state), c2), pl.BlockSpec((1, n_state), c2), pl.BlockSpec((1, D_MODEL), c2)],
        out_specs=pl.BlockSpec((rows, D_MODEL), row),
        out_shape=jax.ShapeDtypeStruct((seq * bsz, D_MODEL), BF16),
        scratch_shapes=[pltpu.VMEM((rows, S5_ST_BLK), F32)] * 4 + [pltpu.VMEM((SUBLANES, n_state), F32)] * 2,
        compiler_params=_cparams("arbitrary"),
        name="s5_scan",
    )(u_t, bre, bim, cre, cim, a_re.reshape(1, n_state), a_im.reshape(1, n_state), d.reshape(1, D_MODEL))

    y2 = y_t.reshape(seq, bsz * D_MODEL)
    return _linear_residual_ln(
        [y2], [w_out.astype(BF16)], h2d, g, b, tm=tm, glu=True, grid=(bsz, per_seq),
        x_maps=[lambda bb, i: (i, bb)], res_map=lambda bb, i: (bb * per_seq + i, 0))


def _even_mixer_block(h2d, bsz, seq, w_in, qnorm_g, w_uq, w_uq_idx, kidx_g, kidx_b, w_out, g, b):
    qa, qi, ka, va, ki, wi, qb, kb, vb = _even_proj(h2d, bsz, seq, w_in, qnorm_g, w_uq, w_uq_idx, kidx_g, kidx_b)
    o_a = _dsa_attention(qa, qi, wi, ki, ka, va, bsz, seq)
    o_b = _stick_breaking(qb, kb, vb, bsz, seq)
    w_out = w_out.astype(BF16)
    return _linear_residual_ln([o_a, o_b], [w_out[:A_WIDTH], w_out[A_WIDTH:]], h2d, g, b, tm=ROW_TILE)


def kernel(x, mem, ev_w_in, ev_qnorm_g, ev_w_uq, ev_w_uq_idx, ev_kidx_ln_g, ev_kidx_ln_b, ev_w_out, od_w_in, od_log_dt, od_lambda_re, od_lambda_im, od_b_re, od_b_im, od_c_re, od_c_im, od_d, od_w_out, mix_ln_g, mix_ln_b, xa_w_q, xa_w_kv, xa_w_o, xa_ln_g, xa_ln_b, moe_w_router, moe_b_router, moe_w_gu, moe_b_gu, moe_w_down, moe_b_down, ffn_ln_g, ffn_ln_b):
    bsz, seq, _ = x.shape
    h = x.reshape(bsz * seq, D_MODEL)
    for layer in range(DEPTH):
        j = layer // 2
        if layer % 2 == 0:
            h = _even_mixer_block(h, bsz, seq, ev_w_in[j], ev_qnorm_g[j], ev_w_uq[j], ev_w_uq_idx[j],
                                  ev_kidx_ln_g[j], ev_kidx_ln_b[j], ev_w_out[j], mix_ln_g[layer], mix_ln_b[layer])
        else:
            h = _s5_mixer_block(h, bsz, seq, od_w_in[j], od_log_dt[j], od_lambda_re[j], od_lambda_im[j],
                                od_b_re[j], od_b_im[j], od_c_re[j], od_c_im[j], od_d[j], od_w_out[j],
                                mix_ln_g[layer], mix_ln_b[layer])
        h = _cross_attention_block(h, mem, bsz, seq, xa_w_q[layer], xa_w_kv[layer], xa_w_o[layer],
                                   xa_ln_g[layer], xa_ln_b[layer])
        h = _moe_block(h, moe_w_router[layer], moe_b_router[layer], moe_w_gu[layer], moe_b_gu[layer],
                       moe_w_down[layer], moe_b_down[layer], ffn_ln_g[layer], ffn_ln_b[layer])
    return h.reshape(bsz, seq, D_MODEL)
```

```python
import functools
import math

import jax
import jax.numpy as jnp
from jax import lax
from jax.experimental import pallas as pl
from jax.experimental.pallas import tpu as pltpu
from jax.experimental.pallas import tpu_sc as plsc

F32 = jnp.float32
BF16 = jnp.bfloat16
I32 = jnp.int32

D_MODEL = 1024
DEPTH = 2
HEAD_DIM = 64
A_HEADS = 8
A_KV_HEADS = 2
A_REP = A_HEADS // A_KV_HEADS
Q_RANK = 256
IDX_HEADS = 8
IDX_DIM = 64
IDX_TOPK = 256
B_HEADS = 8
A_WIDTH = A_HEADS * HEAD_DIM
B_WIDTH = B_HEADS * HEAD_DIM
SSM_GROUP = 16
SSM_GROUPS = D_MODEL // SSM_GROUP
SSM_STATE = 64
XA_HEADS = 4
XA_HEAD_DIM = D_MODEL // XA_HEADS
N_EXPERTS = 32
TOP_K = 4
D_EXPERT = D_MODEL
SWIGLU_LIMIT = 7.0
SWIGLU_ALPHA = 1.702
ROPE_THETA = 500000.0
ROPE_HALF = HEAD_DIM // 8
LN_EPS = 1e-5
DN_ALPHA = (2 * DEPTH) ** 0.25

LANES = 128
SUBLANES = 8
VMEM_LIMIT_BYTES = 56 * 1024 * 1024

Q_BLOCK = 128
DSA_KEY_TILE = 512
DSA_ATT_TILE = 256
SB_KEY_TILE = 256
ROW_TILE = 256
MOE_BLOCK_ROWS = 256
SC_GATHER_WINDOW = 32
S5_CHUNK = 128
S5_LANE_BLOCKS = 4
S5_IN_BLK = D_MODEL // S5_LANE_BLOCKS
S5_ST_BLK = SSM_GROUPS * SSM_STATE // S5_LANE_BLOCKS

SB_EXIT_LOG = -104.0
NEG_BIG = -1e30
INT_MIN = -(2 ** 31)


def _cparams(*sem):
    return pltpu.CompilerParams(dimension_semantics=sem, vmem_limit_bytes=VMEM_LIMIT_BYTES)


def _dot(a, b):
    return jnp.dot(a, b, preferred_element_type=F32)


def _dot_nt(a, b):
    return lax.dot_general(a, b, (((1,), (1,)), ((), ())), preferred_element_type=F32)


def _layer_norm_rows(y, g, b):
    mu = jnp.mean(y, axis=-1, keepdims=True)
    d = y - mu
    var = jnp.mean(d * d, axis=-1, keepdims=True)
    return d * lax.rsqrt(var + LN_EPS) * g + b


def _mm_kernel(x_ref, w_ref, o_ref):
    o_ref[...] = _dot(x_ref[...].astype(BF16), w_ref[...]).astype(o_ref.dtype)


def _matmul(x, w, *, tm, out_dtype, x_map=None, out_map=None, grid=None, out_shape=None):
    m, k = x.shape
    n = w.shape[1]
    grid = grid or (m // tm,)
    x_map = x_map or (lambda i: (i, 0))
    out_map = out_map or (lambda i: (i, 0))
    out_shape = out_shape or (m, n)
    return pl.pallas_call(
        _mm_kernel,
        grid=grid,
        in_specs=[pl.BlockSpec((tm, k), x_map), pl.BlockSpec((k, n), lambda *a: (0, 0))],
        out_specs=pl.BlockSpec((tm, n), out_map),
        out_shape=jax.ShapeDtypeStruct(out_shape, out_dtype),
        compiler_params=_cparams(*(("parallel",) * len(grid))),
        name="matmul",
    )(x, w)


def _lin_ln_kernel(*refs, n_in, glu):
    xs, ws = refs[:n_in], refs[n_in:2 * n_in]
    res_ref, g_ref, b_ref, o_ref = refs[2 * n_in:]
    acc = _dot(xs[0][...].astype(BF16), ws[0][...])
    for x_ref, w_ref in zip(xs[1:], ws[1:]):
        acc = acc + _dot(x_ref[...].astype(BF16), w_ref[...])
    if glu:
        acc = acc[:, :D_MODEL] * jax.nn.sigmoid(acc[:, D_MODEL:])
    y = DN_ALPHA * res_ref[...] + acc
    o_ref[...] = _layer_norm_rows(y, g_ref[...], b_ref[...])


def _linear_residual_ln(xs, ws, res, g, b, *, tm, glu=False, grid=None, x_maps=None, res_map=None):
    n_rows = res.shape[0]
    grid = grid or (n_rows // tm,)
    x_maps = x_maps or [lambda i: (i, 0)] * len(xs)
    res_map = res_map or (lambda i: (i, 0))
    const = lambda *a: (0, 0)
    in_specs = [pl.BlockSpec((tm, w.shape[0]), m) for w, m in zip(ws, x_maps)]
    in_specs += [pl.BlockSpec(w.shape, const) for w in ws]
    in_specs += [pl.BlockSpec((tm, D_MODEL), res_map), pl.BlockSpec((1, D_MODEL), const),
                 pl.BlockSpec((1, D_MODEL), const)]
    return pl.pallas_call(
        functools.partial(_lin_ln_kernel, n_in=len(xs), glu=glu),
        grid=grid,
        in_specs=in_specs,
        out_specs=pl.BlockSpec((tm, D_MODEL), res_map),
        out_shape=jax.ShapeDtypeStruct((n_rows, D_MODEL), F32),
        compiler_params=_cparams(*(("parallel",) * len(grid))),
        name="linear_residual_ln",
    )(*xs, *ws, res, g.reshape(1, D_MODEL), b.reshape(1, D_MODEL))


_EV_CQ, _EV_KA, _EV_VA, _EV_KI, _EV_QB = 0, 256, 384, 512, 640
_EV_KB = _EV_QB + B_WIDTH
_EV_VB = _EV_KB + B_WIDTH
_EV_COLS = _EV_VB + B_WIDTH


def _rope_tables(seq):
    inv = ROPE_THETA ** (-jnp.arange(ROPE_HALF, dtype=F32) / ROPE_HALF)
    ang = jnp.arange(seq, dtype=F32)[:, None] * inv[None, :]
    cos, sin = jnp.cos(ang), jnp.sin(ang)
    rest = HEAD_DIM - 2 * ROPE_HALF
    zh = jnp.zeros((seq, ROPE_HALF), F32)
    c = jnp.concatenate([cos, cos, jnp.ones((seq, rest), F32)], axis=1)
    s1 = jnp.concatenate([-sin, zh, jnp.zeros((seq, rest), F32)], axis=1)
    s2 = jnp.concatenate([zh, sin, jnp.zeros((seq, rest), F32)], axis=1)
    rep = LANES // HEAD_DIM
    return jnp.tile(c, (1, rep)), jnp.tile(s1, (1, rep)), jnp.tile(s2, (1, rep))


def _even_proj_kernel(x_ref, w_ref, qg_ref, wuq_ref, wuqi_ref, lg_ref, lb_ref, c_ref, s1_ref, s2_ref,
                      qa_ref, qi_ref, ka_ref, va_ref, ki_ref, wi_ref, qb_ref, kb_ref, vb_ref):
    p = _dot(x_ref[...].astype(BF16), w_ref[...])
    c, s1, s2 = c_ref[...], s1_ref[...], s2_ref[...]

    def rope(t):
        return (t * c + pltpu.roll(t, LANES - ROPE_HALF, 1) * s1 + pltpu.roll(t, ROPE_HALF, 1) * s2)

    cq = p[:, _EV_CQ:_EV_CQ + Q_RANK]
    cn = cq * lax.rsqrt(jnp.mean(cq * cq, axis=-1, keepdims=True) + LN_EPS) * qg_ref[...]
    cnb = cn.astype(BF16)
    qa = _dot(cnb, wuq_ref[...])
    qi = _dot(cnb, wuqi_ref[...])
    low = lax.broadcasted_iota(I32, c.shape, 1) < HEAD_DIM
    for j in range(A_WIDTH // LANES):
        sl = slice(j * LANES, (j + 1) * LANES)
        pair = rope(qa[:, sl]) * (HEAD_DIM ** -0.5)
        swapped = pltpu.roll(pair, HEAD_DIM, 1)
        for e in range(2):
            h = 2 * j + e
            g = h // A_REP
            src = pair if g == e else swapped
            out = jnp.where(low, src, 0.0) if g == 0 else jnp.where(low, 0.0, src)
            qa_ref[:, h * LANES:(h + 1) * LANES] = out.astype(BF16)
        qi_ref[:, sl] = (rope(qi[:, sl]) * (IDX_DIM ** -0.5)).astype(BF16)
    ka_ref[...] = rope(p[:, _EV_KA:_EV_KA + LANES]).astype(BF16)
    va_ref[...] = p[:, _EV_VA:_EV_VA + LANES].astype(BF16)

    t = p[:, _EV_KI:_EV_KI + LANES]
    lane = lax.broadcasted_iota(I32, t.shape, 1)
    is_k = lane < IDX_DIM
    mu = jnp.sum(jnp.where(is_k, t, 0.0), axis=-1, keepdims=True) * (1.0 / IDX_DIM)
    d = jnp.where(is_k, t - mu, 0.0)
    var = jnp.sum(d * d, axis=-1, keepdims=True) * (1.0 / IDX_DIM)
    kin = d * lax.rsqrt(var + LN_EPS) * lg_ref[...] + lb_ref[...]
    ki_ref[...] = rope(kin)[:, :IDX_DIM].astype(BF16)
    wi_ref[...] = t[:, IDX_DIM:IDX_DIM + IDX_HEADS] * (IDX_HEADS ** -0.5)

    qb_ref[...] = (p[:, _EV_QB:_EV_KB] * (HEAD_DIM ** -0.5)).astype(BF16)
    kb_ref[...] = p[:, _EV_KB:_EV_VB].astype(BF16)
    vb_ref[...] = p[:, _EV_VB:_EV_COLS].astype(BF16)


def _even_proj(x2d, bsz, seq, w_in, qnorm_g, w_uq, w_uq_idx, kidx_g, kidx_b):
    n = x2d.shape[0]
    tm = ROW_TILE
    per_seq = seq // tm
    c0 = Q_RANK + 2 * A_KV_HEADS * HEAD_DIM + IDX_DIM + IDX_HEADS
    w_pack = jnp.concatenate(
        [w_in[:, :c0], jnp.zeros((D_MODEL, _EV_QB - c0), w_in.dtype), w_in[:, c0:]], axis=1).astype(BF16)
    pad = LANES - IDX_DIM
    lg = jnp.concatenate([kidx_g, jnp.zeros((pad,), F32)]).reshape(1, LANES)
    lb = jnp.concatenate([kidx_b, jnp.zeros((pad,), F32)]).reshape(1, LANES)
    c, s1, s2 = _rope_tables(seq)
    row = lambda i: (i, 0)
    const = lambda i: (0, 0)
    pos = lambda i: (i % per_seq, 0)
    head_shape = jax.ShapeDtypeStruct((n, B_WIDTH), BF16)
    head_spec = pl.BlockSpec((tm, B_WIDTH), row)
    return pl.pallas_call(
        _even_proj_kernel,
        grid=(n // tm,),
        in_specs=[pl.BlockSpec((tm, D_MODEL), row), pl.BlockSpec((D_MODEL, _EV_COLS), const),
                  pl.BlockSpec((1, Q_RANK), const), pl.BlockSpec((Q_RANK, A_WIDTH), const),
                  pl.BlockSpec((Q_RANK, IDX_HEADS * IDX_DIM), const),
                  pl.BlockSpec((1, LANES), const), pl.BlockSpec((1, LANES), const),
                  pl.BlockSpec((tm, LANES), pos), pl.BlockSpec((tm, LANES), pos), pl.BlockSpec((tm, LANES), pos)],
        out_specs=[pl.BlockSpec((tm, A_HEADS * LANES), row), pl.BlockSpec((tm, IDX_HEADS * IDX_DIM), row),
                   pl.BlockSpec((tm, LANES), row), pl.BlockSpec((tm, LANES), row),
                   pl.BlockSpec((tm, IDX_DIM), row), pl.BlockSpec((tm, IDX_HEADS), row),
                   head_spec, head_spec, head_spec],
        out_shape=[jax.ShapeDtypeStruct((n, A_HEADS * LANES), BF16), jax.ShapeDtypeStruct((n, IDX_HEADS * IDX_DIM), BF16),
                   jax.ShapeDtypeStruct((n, LANES), BF16), jax.ShapeDtypeStruct((n, LANES), BF16),
                   jax.ShapeDtypeStruct((n, IDX_DIM), BF16), jax.ShapeDtypeStruct((n, IDX_HEADS), F32),
                   head_shape, head_shape, head_shape],
        compiler_params=_cparams("parallel"),
        name="even_proj",
    )(x2d, w_pack, qnorm_g.reshape(1, Q_RANK), w_uq.astype(BF16), w_uq_idx.astype(BF16), lg, lb, c, s1, s2)


def _key_to_float(key):
    bits = key ^ ((key >> 31) & jnp.int32(0x7FFFFFFF))
    return lax.bitcast_convert_type(bits, F32)


def _dsa_kernel(qa_ref, qi_ref, wi_ref, ki_ref, ka_ref, va_ref, o_ref, sc_scr, m_scr, l_scr, acc_scr,
                *, topk, ts, ta):
    qb = pl.program_id(1)
    q0 = qb * Q_BLOCK
    nkt = (q0 + Q_BLOCK - 1) // ts + 1
    t_col = q0 + lax.broadcasted_iota(I32, (Q_BLOCK, 1), 0)
    lane = lax.broadcasted_iota(I32, (Q_BLOCK, ts), 1)
    kf = jnp.float32(topk)

    qi = qi_ref[0]
    wi = wi_ref[0]
    qih = [qi[:, h * IDX_DIM:(h + 1) * IDX_DIM] for h in range(IDX_HEADS)]
    wih = [wi[:, h:h + 1] for h in range(IDX_HEADS)]

    def score_tile(kt, carry):
        off = pl.multiple_of(kt * ts, ts)
        kit = ki_ref[0, pl.ds(off, ts), :]
        acc = jnp.zeros((Q_BLOCK, ts), F32)
        for h in range(IDX_HEADS):
            acc = acc + jnp.maximum(_dot_nt(qih[h], kit), 0.0) * wih[h]
        sc_scr[:, pl.ds(off, ts)] = jnp.where(off + lane <= t_col, acc, -jnp.inf)
        return carry

    lax.fori_loop(0, nkt, score_tile, 0)

    def count(pred):
        def body(kt, acc):
            off = pl.multiple_of(kt * ts, ts)
            ind = pred(sc_scr[:, pl.ds(off, ts)], off + lane)
            for j in range(ts // LANES):
                acc = acc + ind[:, j * LANES:(j + 1) * LANES]
            return acc
        acc = lax.fori_loop(0, nkt, body, jnp.zeros((Q_BLOCK, LANES), F32))
        return jnp.sum(acc, axis=1, keepdims=True)

    def bit_step(i, base):
        cand = base + jnp.left_shift(jnp.int32(1), 31 - i)
        cf = _key_to_float(cand)
        cnt = count(lambda sc, idx: jnp.where(sc >= cf, 1.0, 0.0))
        return jnp.where(cnt >= kf, cand, base)

    base = lax.fori_loop(0, 32, bit_step, jnp.full((Q_BLOCK, 1), INT_MIN, I32))
    thr = jnp.where(base == INT_MIN, -jnp.inf, _key_to_float(base))

    cnt_ge = count(lambda sc, idx: jnp.where(sc >= thr, 1.0, 0.0))
    tied = jnp.logical_and(cnt_ge > kf, thr > -jnp.inf)
    any_tied = jnp.max(jnp.where(tied, 1.0, 0.0)) > 0.0
    seq_bits = max(1, int(math.ceil(math.log2(sc_scr.shape[1]))))

    def tie_cut():
        cnt_gt = count(lambda sc, idx: jnp.where(sc > thr, 1.0, 0.0))
        need = kf - cnt_gt

        def idx_step(i, pos):
            cand = pos + jnp.left_shift(jnp.int32(1), seq_bits - 1 - i)
            cnt = count(lambda sc, idx: jnp.where(sc == thr, jnp.where(idx < cand, 1.0, 0.0), 0.0))
            return jnp.where(cnt < need, cand, pos)

        return lax.fori_loop(0, seq_bits, idx_step, jnp.zeros((Q_BLOCK, 1), I32))

    cut = lax.cond(any_tied, tie_cut, lambda: jnp.full((Q_BLOCK, 1), sc_scr.shape[1], I32))
    cut = jnp.where(tied, cut, sc_scr.shape[1])

    nkt_a = (q0 + Q_BLOCK - 1) // ta + 1
    lane_a = lax.broadcasted_iota(I32, (Q_BLOCK, ta), 1)
    m_scr[...] = jnp.full(m_scr.shape, NEG_BIG, F32)
    l_scr[...] = jnp.zeros_like(l_scr)
    acc_scr[...] = jnp.zeros_like(acc_scr)

    def att_tile(kt, carry):
        off = pl.multiple_of(kt * ta, ta)
        sc = sc_scr[:, pl.ds(off, ta)]
        idx = off + lane_a
        keep = jnp.where(sc > thr, 0.0, jnp.where(sc == thr, jnp.where(idx <= cut, 0.0, NEG_BIG), NEG_BIG))
        bias = jnp.where(idx <= t_col, keep, NEG_BIG)
        kt2 = ka_ref[0, pl.ds(off, ta), :]
        vt2 = va_ref[0, pl.ds(off, ta), :]
        for h in range(A_HEADS):
            s = _dot_nt(qa_ref[0, :, h * LANES:(h + 1) * LANES], kt2) + bias
            m_old = m_scr[h]
            m_new = jnp.maximum(m_old, jnp.max(s, axis=1, keepdims=True))
            p = jnp.exp(s - m_new)
            alpha = jnp.exp(m_old - m_new)
            l_scr[h] = alpha * l_scr[h] + jnp.sum(p, axis=1, keepdims=True)
            acc_scr[h] = alpha * acc_scr[h] + _dot(p.astype(BF16), vt2)
            m_scr[h] = m_new
        return carry

    lax.fori_loop(0, nkt_a, att_tile, 0)
    low = lax.broadcasted_iota(I32, (Q_BLOCK, LANES), 1) < HEAD_DIM
    for j in range(A_HEADS // 2):
        halves = []
        for h in (2 * j, 2 * j + 1):
            o = acc_scr[h] / l_scr[h]
            halves.append(o if (h // A_REP) == (h % 2) else pltpu.roll(o, HEAD_DIM, 1))
        o_ref[0, :, j * LANES:(j + 1) * LANES] = jnp.where(low, halves[0], halves[1]).astype(BF16)


def _dsa_attention(qa, qi, wi, ki, ka, va, bsz, seq):
    topk = min(IDX_TOPK, seq // 4)
    ts = min(DSA_KEY_TILE, seq)
    ta = min(DSA_ATT_TILE, seq)
    blk = lambda b, i: (b, i, 0)
    full = lambda b, i: (b, 0, 0)
    r3 = lambda a: a.reshape(bsz, seq, a.shape[-1])
    return pl.pallas_call(
        functools.partial(_dsa_kernel, topk=topk, ts=ts, ta=ta),
        grid=(bsz, seq // Q_BLOCK),
        in_specs=[pl.BlockSpec((1, Q_BLOCK, A_HEADS * LANES), blk),
                  pl.BlockSpec((1, Q_BLOCK, IDX_HEADS * IDX_DIM), blk),
                  pl.BlockSpec((1, Q_BLOCK, IDX_HEADS), blk), pl.BlockSpec((1, seq, IDX_DIM), full),
                  pl.BlockSpec((1, seq, LANES), full), pl.BlockSpec((1, seq, LANES), full)],
        out_specs=pl.BlockSpec((1, Q_BLOCK, A_WIDTH), blk),
        out_shape=jax.ShapeDtypeStruct((bsz, seq, A_WIDTH), BF16),
        scratch_shapes=[pltpu.VMEM((Q_BLOCK, seq), F32), pltpu.VMEM((A_HEADS, Q_BLOCK, 1), F32),
                        pltpu.VMEM((A_HEADS, Q_BLOCK, 1), F32), pltpu.VMEM((A_HEADS, Q_BLOCK, LANES), F32)],
        compiler_params=_cparams("parallel", "parallel"),
        name="dsa_attention",
    )(r3(qa), r3(qi), r3(wi), r3(ki), r3(ka), r3(va)).reshape(bsz * seq, A_WIDTH)


def _sb_kernel(q_ref, k_ref, v_ref, u_ref, o_ref, acc_scr, run_scr, *, tk):
    q0 = pl.program_id(1) * Q_BLOCK
    t_col = q0 + lax.broadcasted_iota(I32, (Q_BLOCK, 1), 0)
    lane = lax.broadcasted_iota(I32, (Q_BLOCK, tk), 1)
    low = lax.broadcasted_iota(I32, (Q_BLOCK, LANES), 1) < HEAD_DIM
    upper = u_ref[...]
    nkt = (q0 + Q_BLOCK - 1) // tk + 1
    q = q_ref[0]
    zero = jnp.zeros((Q_BLOCK, LANES), BF16)
    qm = []
    for h in range(B_HEADS):
        pair = q[:, (h // 2) * LANES:(h // 2 + 1) * LANES]
        qm.append(jnp.where(low, pair, zero) if h % 2 == 0 else jnp.where(low, zero, pair))
    acc_scr[...] = jnp.zeros_like(acc_scr)
    run_scr[...] = jnp.zeros_like(run_scr)

    def cond(carry):
        i, worst = carry
        return jnp.logical_and(i < nkt, worst >= SB_EXIT_LOG)

    def body(carry):
        i, _ = carry
        off = pl.multiple_of((nkt - 1 - i) * tk, tk)
        strict = off + lane < t_col
        worst = None
        for p in range(B_HEADS // 2):
            cols = slice(p * LANES, (p + 1) * LANES)
            kp = k_ref[0, pl.ds(off, tk), cols]
            vp = v_ref[0, pl.ds(off, tk), cols]
            outs = []
            for e in range(2):
                h = 2 * p + e
                run = run_scr[h]
                z = _dot_nt(qm[h], kp)
                softplus = jnp.maximum(z, 0.0) + jnp.log(1.0 + jnp.exp(-jnp.abs(z)))
                log_1mb = jnp.where(strict, -softplus, 0.0)
                hi = log_1mb.astype(BF16)
                lo = (log_1mb - hi.astype(F32)).astype(BF16)
                after = _dot(hi, upper) + _dot(lo, upper) + run
                a = jnp.where(strict, jnp.exp(z - softplus + after), 0.0)
                outs.append(_dot(a.astype(BF16), vp))
                run = run + jnp.sum(log_1mb, axis=1, keepdims=True)
                run_scr[h] = run
                worst = run if worst is None else jnp.maximum(worst, run)
            acc_scr[:, cols] += jnp.where(low, outs[0], outs[1])
        return i + 1, jnp.max(worst)

    lax.while_loop(cond, body, (jnp.int32(0), jnp.float32(0.0)))
    o_ref[0] = acc_scr[...].astype(BF16)


def _stick_breaking(qb, kb, vb, bsz, seq):
    tk = min(SB_KEY_TILE, seq)
    r = lax.broadcasted_iota(I32, (tk, tk), 0)
    c = lax.broadcasted_iota(I32, (tk, tk), 1)
    upper = jnp.where(r > c, 1.0, 0.0).astype(BF16)
    blk = lambda b, i: (b, i, 0)
    full = lambda b, i: (b, 0, 0)
    r3 = lambda a: a.reshape(bsz, seq, B_WIDTH)
    return pl.pallas_call(
        functools.partial(_sb_kernel, tk=tk),
        grid=(bsz, seq // Q_BLOCK),
        in_specs=[pl.BlockSpec((1, Q_BLOCK, B_WIDTH), blk), pl.BlockSpec((1, seq, B_WIDTH), full),
                  pl.BlockSpec((1, seq, B_WIDTH), full), pl.BlockSpec((tk, tk), lambda b, i: (0, 0))],
        out_specs=pl.BlockSpec((1, Q_BLOCK, B_WIDTH), blk),
        out_shape=jax.ShapeDtypeStruct((bsz, seq, B_WIDTH), BF16),
        scratch_shapes=[pltpu.VMEM((Q_BLOCK, B_WIDTH), F32), pltpu.VMEM((B_HEADS, Q_BLOCK, 1), F32)],
        compiler_params=_cparams("parallel", "arbitrary"),
        name="stick_breaking",
    )(r3(qb), r3(kb), r3(vb), upper).reshape(bsz * seq, B_WIDTH)


def _xattn_kernel(h_ref, wq_ref, kv_ref, wo_ref, g_ref, b_ref, o_ref):
    h = h_ref[...]
    q = (_dot(h.astype(BF16), wq_ref[...]) * (XA_HEAD_DIM ** -0.5)).astype(BF16)
    kv = kv_ref[0]
    outs = []
    for hd in range(XA_HEADS):
        sl = slice(hd * XA_HEAD_DIM, (hd + 1) * XA_HEAD_DIM)
        s = _dot_nt(q[:, sl], kv[:, sl])
        p = jnp.exp(s - jnp.max(s, axis=1, keepdims=True))
        vh = kv[:, D_MODEL + hd * XA_HEAD_DIM:D_MODEL + (hd + 1) * XA_HEAD_DIM]
        outs.append((_dot(p.astype(BF16), vh) / jnp.sum(p, axis=1, keepdims=True)).astype(BF16))
    y = _dot(jnp.concatenate(outs, axis=1), wo_ref[...])
    o_ref[...] = _layer_norm_rows(DN_ALPHA * h + y, g_ref[...], b_ref[...])


def _cross_attention_block(h2d, mem, bsz, seq, w_q, w_kv, w_o, g, b):
    tm = ROW_TILE
    per_seq = seq // tm
    mem_len = mem.shape[1]
    kv = _matmul(mem.reshape(bsz * mem_len, D_MODEL), w_kv.astype(BF16), tm=mem_len, out_dtype=BF16)
    kv = kv.reshape(bsz, mem_len, 2 * D_MODEL)
    row = lambda i: (i, 0)
    const = lambda i: (0, 0)
    return pl.pallas_call(
        _xattn_kernel,
        grid=(bsz * per_seq,),
        in_specs=[pl.BlockSpec((tm, D_MODEL), row), pl.BlockSpec((D_MODEL, D_MODEL), const),
                  pl.BlockSpec((1, mem_len, 2 * D_MODEL), lambda i: (i // per_seq, 0, 0)),
                  pl.BlockSpec((D_MODEL, D_MODEL), const),
                  pl.BlockSpec((1, D_MODEL), const), pl.BlockSpec((1, D_MODEL), const)],
        out_specs=pl.BlockSpec((tm, D_MODEL), row),
        out_shape=jax.ShapeDtypeStruct(h2d.shape, F32),
        compiler_params=_cparams("parallel"),
        name="cross_attention",
    )(h2d, w_q.astype(BF16), kv, w_o.astype(BF16), g.reshape(1, D_MODEL), b.reshape(1, D_MODEL))


def _router_kernel(h_ref, w_ref, b_ref, tri_ref, idx_ref, gate_ref, rank_ref, cnt_ref, run_scr):
    @pl.when(pl.program_id(0) == 0)
    def _():
        run_scr[...] = jnp.zeros_like(run_scr)

    h = h_ref[...]
    hh = h.astype(BF16)
    hl = (h - hh.astype(F32)).astype(BF16)
    w = w_ref[...]
    wh = w.astype(BF16)
    wl = (w - wh.astype(F32)).astype(BF16)
    logits = _dot(hh, wh) + _dot(hl, wh) + _dot(hh, wl) + b_ref[...]
    lane = lax.broadcasted_iota(I32, logits.shape, 1).astype(F32)
    vals, sels = [], []
    onehot = jnp.zeros(logits.shape, F32)
    for k in range(TOP_K):
        m = jnp.max(logits, axis=1, keepdims=True)
        sel = jnp.min(jnp.where(logits == m, lane, float(LANES)), axis=1, keepdims=True)
        idx_ref[:, k:k + 1] = sel.astype(I32)
        vals.append(m)
        sels.append(sel)
        onehot = onehot + jnp.where(lane == sel, 1.0, 0.0)
        logits = jnp.where(lane == sel, -jnp.inf, logits)
    es = [jnp.exp(v - vals[0]) for v in vals]
    tot = es[0] + es[1] + es[2] + es[3]
    for k in range(TOP_K):
        gate_ref[:, k:k + 1] = es[k] / tot

    earlier = _dot(tri_ref[...], onehot.astype(BF16)) + run_scr[...]
    for k in range(TOP_K):
        rank = jnp.sum(jnp.where(lane == sels[k], earlier, 0.0), axis=1, keepdims=True)
        rank_ref[:, k:k + 1] = rank.astype(I32)
    run = run_scr[...] + jnp.sum(onehot, axis=0, keepdims=True)
    run_scr[...] = run
    cnt_ref[...] = run


def _router(h2d, w_router, b_router):
    n = h2d.shape[0]
    tm = 2 * ROW_TILE
    pad = LANES - N_EXPERTS
    w = jnp.concatenate([w_router, jnp.zeros((D_MODEL, pad), F32)], axis=1)
    b = jnp.concatenate([b_router, jnp.full((pad,), NEG_BIG, F32)]).reshape(1, LANES)
    r = lax.broadcasted_iota(I32, (tm, tm), 0)
    c = lax.broadcasted_iota(I32, (tm, tm), 1)
    tri = jnp.where(c < r, 1.0, 0.0).astype(BF16)
    row = lambda i: (i, 0)
    const = lambda i: (0, 0)
    return pl.pallas_call(
        _router_kernel,
        grid=(n // tm,),
        in_specs=[pl.BlockSpec((tm, D_MODEL), row), pl.BlockSpec((D_MODEL, LANES), const),
                  pl.BlockSpec((1, LANES), const), pl.BlockSpec((tm, tm), const)],
        out_specs=[pl.BlockSpec((tm, TOP_K), row), pl.BlockSpec((tm, TOP_K), row),
                   pl.BlockSpec((tm, TOP_K), row), pl.BlockSpec((1, LANES), const)],
        out_shape=[jax.ShapeDtypeStruct((n, TOP_K), I32), jax.ShapeDtypeStruct((n, TOP_K), F32),
                   jax.ShapeDtypeStruct((n, TOP_K), I32), jax.ShapeDtypeStruct((1, LANES), F32)],
        scratch_shapes=[pltpu.VMEM((1, LANES), F32)],
        compiler_params=_cparams("arbitrary"),
        name="moe_router",
    )(h2d, w, b, tri)


def _gather_rows(src, idx):
    n_out = idx.shape[0]
    width = src.shape[1]
    win = SC_GATHER_WINDOW
    mesh = plsc.VectorSubcoreMesh(core_axis_name="core", subcore_axis_name="subcore")
    n_workers = mesh.num_cores * mesh.num_subcores
    per_worker = n_out // n_workers
    steps = per_worker // win
    assert per_worker * n_workers == n_out and steps * win == per_worker and steps % 2 == 0

    @functools.partial(
        pl.kernel, out_type=jax.ShapeDtypeStruct((n_out, width), src.dtype), mesh=mesh,
        scratch_types=[pltpu.VMEM((per_worker,), I32), pltpu.VMEM((2, win, width), src.dtype),
                       pltpu.SemaphoreType.DMA, pltpu.SemaphoreType.DMA])
    def gather_kernel(src_hbm, idx_hbm, dst_hbm, idx_v, rows_v, sem0, sem1):
        worker = lax.axis_index("subcore") * mesh.num_cores + lax.axis_index("core")
        base = worker * per_worker
        sems = (sem0, sem1)
        pltpu.sync_copy(idx_hbm.at[pl.ds(base, per_worker)], idx_v)

        def gather(step, slot):
            return pltpu.make_async_copy(src_hbm.at[idx_v.at[pl.ds(step * win, win)]], rows_v.at[slot], sems[slot])

        gather(0, 0).start()

        @pl.loop(0, steps, step=2)
        def _(s):
            for slot in range(2):
                step = s + slot
                gather(step, slot).wait()

                @pl.when(step + 1 < steps)
                def _():
                    gather(step + 1, 1 - slot).start()

                pltpu.sync_copy(rows_v.at[slot], dst_hbm.at[pl.ds(base + step * win, win)])

    return gather_kernel(src, idx)


def _expert_kernel(blk_exp_ref, n_used_ref, x_ref, wgu_ref, bgu_ref, wd_ref, bd_ref, o_ref, wgu_bf, wd_bf):
    i = pl.program_id(0)

    @pl.when(jnp.logical_or(i == 0, blk_exp_ref[i] != blk_exp_ref[jnp.maximum(i - 1, 0)]))
    def _():
        wgu_bf[...] = wgu_ref[0].astype(BF16)
        wd_bf[...] = wd_ref[0].astype(BF16)

    @pl.when(i < n_used_ref[0])
    def _():
        hgu = _dot(x_ref[...].astype(BF16), wgu_bf[...]) + bgu_ref[0]
        gate = jnp.minimum(hgu[:, :D_EXPERT], SWIGLU_LIMIT)
        up = jnp.clip(hgu[:, D_EXPERT:], -SWIGLU_LIMIT, SWIGLU_LIMIT)
        act = gate * jax.nn.sigmoid(gate * SWIGLU_ALPHA) * (up + 1.0)
        o_ref[...] = _dot(act.astype(BF16), wd_bf[...]) + bd_ref[0]

    @pl.when(i >= n_used_ref[0])
    def _():
        o_ref[...] = jnp.zeros_like(o_ref)


def _expert_mlp(xs, block_exp, n_used, w_gu, b_gu, w_down, b_down):
    n_rows = xs.shape[0]
    bm = MOE_BLOCK_ROWS
    row = lambda i, be, nu: (i, 0)
    exp3 = lambda i, be, nu: (be[i], 0, 0)
    grid_spec = pltpu.PrefetchScalarGridSpec(
        num_scalar_prefetch=2,
        grid=(n_rows // bm,),
        in_specs=[pl.BlockSpec((bm, D_MODEL), row),
                  pl.BlockSpec((1, D_MODEL, 2 * D_EXPERT), exp3), pl.BlockSpec((1, 1, 2 * D_EXPERT), exp3),
                  pl.BlockSpec((1, D_EXPERT, D_MODEL), exp3), pl.BlockSpec((1, 1, D_MODEL), exp3)],
        out_specs=pl.BlockSpec((bm, D_MODEL), row),
        scratch_shapes=[pltpu.VMEM((D_MODEL, 2 * D_EXPERT), BF16), pltpu.VMEM((D_EXPERT, D_MODEL), BF16)],
    )
    return pl.pallas_call(
        _expert_kernel,
        grid_spec=grid_spec,
        out_shape=jax.ShapeDtypeStruct((n_rows, D_MODEL), F32),
        compiler_params=_cparams("arbitrary"),
        name="moe_experts",
    )(block_exp, n_used, xs, w_gu, b_gu.reshape(N_EXPERTS, 1, 2 * D_EXPERT),
      w_down, b_down.reshape(N_EXPERTS, 1, D_MODEL))


def _combine_kernel(y0_ref, y1_ref, y2_ref, y3_ref, gate_ref, res_ref, g_ref, b_ref, o_ref):
    gates = gate_ref[...]
    acc = y0_ref[...] * gates[:, 0:1]
    for k, y_ref in enumerate((y1_ref, y2_ref, y3_ref), start=1):
        acc = acc + y_ref[...] * gates[:, k:k + 1]
    o_ref[...] = _layer_norm_rows(DN_ALPHA * res_ref[...] + acc, g_ref[...], b_ref[...])


def _moe_block(h2d, w_router, b_router, w_gu, b_gu, w_down, b_down, g, b):
    n = h2d.shape[0]
    n_slots = n * TOP_K
    bm = MOE_BLOCK_ROWS
    top_idx, gates, rank, totals = _router(h2d, w_router, b_router)

    e_flat = top_idx.reshape(-1)
    order = jnp.argsort(e_flat).astype(I32)
    counts = totals[0, :N_EXPERTS].astype(I32)
    padded = (counts + bm - 1) // bm * bm
    start = jnp.cumsum(counts) - counts
    ends_p = jnp.cumsum(padded)
    pstart = ends_p - padded
    n_rows = n_slots + N_EXPERTS * bm
    n_blocks = n_rows // bm
    r = jnp.arange(n_rows, dtype=I32)
    e_r = jnp.minimum(jnp.searchsorted(ends_p, r, side="right"), N_EXPERTS - 1).astype(I32)
    j = r - pstart[e_r]
    valid = j < counts[e_r]
    slot_of_row = order[jnp.where(valid, start[e_r] + j, 0)]
    rows_tok = jnp.where(valid, slot_of_row // TOP_K, 0).astype(I32)
    slot_pos = pstart[e_flat] + rank.reshape(-1)
    block_exp = e_r[::bm]
    n_used = (ends_p[-1] // bm).astype(I32).reshape(1)

    xs = _gather_rows(h2d, rows_tok)
    ys = _expert_mlp(xs, block_exp, n_used, w_gu, b_gu, w_down, b_down)
    yk = _gather_rows(ys, slot_pos.reshape(n, TOP_K).T.reshape(-1))

    tm = ROW_TILE
    row = lambda i: (i, 0)
    const = lambda i: (0, 0)
    choice = lambda k: (lambda i: (k * (n // tm) + i, 0))
    return pl.pallas_call(
        _combine_kernel,
        grid=(n // tm,),
        in_specs=[pl.BlockSpec((tm, D_MODEL), choice(k)) for k in range(TOP_K)] + [
                  pl.BlockSpec((tm, TOP_K), row),
                  pl.BlockSpec((tm, D_MODEL), row), pl.BlockSpec((1, D_MODEL), const),
                  pl.BlockSpec((1, D_MODEL), const)],
        out_specs=pl.BlockSpec((tm, D_MODEL), row),
        out_shape=jax.ShapeDtypeStruct((n, D_MODEL), F32),
        compiler_params=_cparams("parallel"),
        name="moe_combine",
    )(yk, yk, yk, yk, gates, h2d, g.reshape(1, D_MODEL), b.reshape(1, D_MODEL))


def _s5_kernel(u_ref, bre_ref, bim_ref, cre_ref, cim_ref, are_ref, aim_ref, d_ref, y_ref,
               bu_re, bu_im, st_re, st_im, h_re, h_im, *, bsz):
    @pl.when(pl.program_id(0) == 0)
    def _():
        h_re[...] = jnp.zeros_like(h_re)
        h_im[...] = jnp.zeros_like(h_im)

    rows = u_ref.shape[0]
    first = lax.broadcasted_iota(I32, (SUBLANES, S5_ST_BLK), 0) < bsz
    for j in range(S5_LANE_BLOCKS):
        cin = slice(j * S5_IN_BLK, (j + 1) * S5_IN_BLK)
        cst = slice(j * S5_ST_BLK, (j + 1) * S5_ST_BLK)
        uj = u_ref[:, cin]
        ujb = uj.astype(BF16)
        bu_re[...] = _dot(ujb, bre_ref[j])
        bu_im[...] = _dot(ujb, bim_ref[j])
        ar = jnp.broadcast_to(are_ref[:, cst], (SUBLANES, S5_ST_BLK))
        ai = jnp.broadcast_to(aim_ref[:, cst], (SUBLANES, S5_ST_BLK))

        def step(i, carry):
            hr, hi = carry
            r0 = pl.multiple_of(i * SUBLANES, SUBLANES)
            vr = bu_re[pl.ds(r0, SUBLANES), :]
            vi = bu_im[pl.ds(r0, SUBLANES), :]
            h1r = ar * hr - ai * hi + vr
            h1i = ar * hi + ai * hr + vi
            h1rs = pltpu.roll(h1r, bsz, 0)
            h1is = pltpu.roll(h1i, bsz, 0)
            h2r = ar * h1rs - ai * h1is + vr
            h2i = ar * h1is + ai * h1rs + vi
            st_re[pl.ds(r0, SUBLANES), :] = jnp.where(first, h1r, h2r)
            st_im[pl.ds(r0, SUBLANES), :] = jnp.where(first, h1i, h2i)
            return pltpu.roll(h2r, bsz, 0), pltpu.roll(h2i, bsz, 0)

        hr, hi = lax.fori_loop(0, rows // SUBLANES, step, (h_re[:, cst], h_im[:, cst]))
        h_re[:, cst] = hr
        h_im[:, cst] = hi
        yj = _dot(st_re[...].astype(BF16), cre_ref[j]) + _dot(st_im[...].astype(BF16), cim_ref[j])
        yj = yj + d_ref[:, cin] * uj
        y_ref[:, cin] = jax.nn.gelu(yj).astype(BF16)


def _s5_block_diag(w, n_in, n_out):
    gpb = SSM_GROUPS // S5_LANE_BLOCKS
    w4 = w.reshape(S5_LANE_BLOCKS, gpb, n_in, n_out)
    eye = jnp.eye(gpb, dtype=w.dtype)
    return jnp.einsum("jgio,gh->jgiho", w4, eye).reshape(S5_LANE_BLOCKS, gpb * n_in, gpb * n_out)


def _s5_mixer_block(h2d, bsz, seq, w_in, log_dt, lam_re, lam_im, b_re, b_im, c_re, c_im, d, w_out, g, b):
    assert 2 * bsz == SUBLANES, "the scan packs two time steps of bsz rows into one 8-row tile"
    tm = ROW_TILE
    per_seq = seq // tm
    u_t = _matmul(h2d, w_in.astype(BF16), tm=tm, out_dtype=F32, grid=(bsz, per_seq),
                  x_map=lambda bb, i: (bb * per_seq + i, 0), out_map=lambda bb, i: (i, bb),
                  out_shape=(seq, bsz * D_MODEL)).reshape(seq * bsz, D_MODEL)

    dt = jnp.exp(log_dt)[:, None]
    mag = jnp.exp(lam_re * dt)
    a_re, a_im = mag * jnp.cos(lam_im * dt), mag * jnp.sin(lam_im * dt)
    den = lam_re * lam_re + lam_im * lam_im
    coef_re = ((a_re - 1.0) * lam_re + a_im * lam_im) / den
    coef_im = (a_im * lam_re - (a_re - 1.0) * lam_im) / den
    bb_re = coef_re[..., None] * b_re - coef_im[..., None] * b_im
    bb_im = coef_re[..., None] * b_im + coef_im[..., None] * b_re
    bre = _s5_block_diag(jnp.swapaxes(bb_re, 1, 2), SSM_GROUP, SSM_STATE).astype(BF16)
    bim = _s5_block_diag(jnp.swapaxes(bb_im, 1, 2), SSM_GROUP, SSM_STATE).astype(BF16)
    cre = _s5_block_diag(jnp.swapaxes(c_re, 1, 2), SSM_STATE, SSM_GROUP).astype(BF16)
    cim = _s5_block_diag(jnp.swapaxes(-c_im, 1, 2), SSM_STATE, SSM_GROUP).astype(BF16)
    n_state = SSM_GROUPS * SSM_STATE

    rows = S5_CHUNK * bsz
    row = lambda c: (c, 0)
    c2 = lambda c: (0, 0)
    c3 = lambda c: (0, 0, 0)
    y_t = pl.pallas_call(
        functools.partial(_s5_kernel, bsz=bsz),
        grid=(seq // S5_CHUNK,),
        in_specs=[pl.BlockSpec((rows, D_MODEL), row),
                  pl.BlockSpec(bre.shape, c3), pl.BlockSpec(bim.shape, c3),
                  pl.BlockSpec(cre.shape, c3), pl.BlockSpec(cim.shape, c3),
                  pl.BlockSpec((1, n_state), c2), pl.BlockSpec((1, n_state), c2), pl.BlockSpec((1, D_MODEL), c2)],
        out_specs=pl.BlockSpec((rows, D_MODEL), row),
        out_shape=jax.ShapeDtypeStruct((seq * bsz, D_MODEL), BF16),
        scratch_shapes=[pltpu.VMEM((rows, S5_ST_BLK), F32)] * 4 + [pltpu.VMEM((SUBLANES, n_state), F32)] * 2,
        compiler_params=_cparams("arbitrary"),
        name="s5_scan",
    )(u_t, bre, bim, cre, cim, a_re.reshape(1, n_state), a_im.reshape(1, n_state), d.reshape(1, D_MODEL))

    y2 = y_t.reshape(seq, bsz * D_MODEL)
    return _linear_residual_ln(
        [y2], [w_out.astype(BF16)], h2d, g, b, tm=tm, glu=True, grid=(bsz, per_seq),
        x_maps=[lambda bb, i: (i, bb)], res_map=lambda bb, i: (bb * per_seq + i, 0))


def _even_mixer_block(h2d, bsz, seq, w_in, qnorm_g, w_uq, w_uq_idx, kidx_g, kidx_b, w_out, g, b):
    qa, qi, ka, va, ki, wi, qb, kb, vb = _even_proj(h2d, bsz, seq, w_in, qnorm_g, w_uq, w_uq_idx, kidx_g, kidx_b)
    o_a = _dsa_attention(qa, qi, wi, ki, ka, va, bsz, seq)
    o_b = _stick_breaking(qb, kb, vb, bsz, seq)
    w_out = w_out.astype(BF16)
    return _linear_residual_ln([o_a, o_b], [w_out[:A_WIDTH], w_out[A_WIDTH:]], h2d, g, b, tm=ROW_TILE)


def kernel(x, mem, ev_w_in, ev_qnorm_g, ev_w_uq, ev_w_uq_idx, ev_kidx_ln_g, ev_kidx_ln_b, ev_w_out, od_w_in, od_log_dt, od_lambda_re, od_lambda_im, od_b_re, od_b_im, od_c_re, od_c_im, od_d, od_w_out, mix_ln_g, mix_ln_b, xa_w_q, xa_w_kv, xa_w_o, xa_ln_g, xa_ln_b, moe_w_router, moe_b_router, moe_w_gu, moe_b_gu, moe_w_down, moe_b_down, ffn_ln_g, ffn_ln_b):
    bsz, seq, _ = x.shape
    h = x.reshape(bsz * seq, D_MODEL)
    for layer in range(DEPTH):
        j = layer // 2
        if layer % 2 == 0:
            h = _even_mixer_block(h, bsz, seq, ev_w_in[j], ev_qnorm_g[j], ev_w_uq[j], ev_w_uq_idx[j],
                                  ev_kidx_ln_g[j], ev_kidx_ln_b[j], ev_w_out[j], mix_ln_g[layer], mix_ln_b[layer])
        else:
            h = _s5_mixer_block(h, bsz, seq, od_w_in[j], od_log_dt[j], od_lambda_re[j], od_lambda_im[j],
                                od_b_re[j], od_b_im[j], od_c_re[j], od_c_im[j], od_d[j], od_w_out[j],
                                mix_ln_g[layer], mix_ln_b[layer])
        h = _cross_attention_block(h, mem, bsz, seq, xa_w_q[layer], xa_w_kv[layer], xa_w_o[layer],
                                   xa_ln_g[layer], xa_ln_b[layer])
        h = _moe_block(h, moe_w_router[layer], moe_b_router[layer], moe_w_gu[layer], moe_b_gu[layer],
                       moe_w_down[layer], moe_b_down[layer], ffn_ln_g[layer], ffn_ln_b[layer])
    return h.reshape(bsz, seq, D_MODEL)
```

```python
import functools
import math

import jax
import jax.numpy as jnp
from jax import lax
from jax.experimental import pallas as pl
from jax.experimental.pallas import tpu as pltpu
from jax.experimental.pallas import tpu_sc as plsc

F32 = jnp.float32
BF16 = jnp.bfloat16
I32 = jnp.int32

D_MODEL = 1024
DEPTH = 2
HEAD_DIM = 64
A_HEADS = 8
A_KV_HEADS = 2
A_REP = A_HEADS // A_KV_HEADS
Q_RANK = 256
IDX_HEADS = 8
IDX_DIM = 64
IDX_TOPK = 256
B_HEADS = 8
A_WIDTH = A_HEADS * HEAD_DIM
B_WIDTH = B_HEADS * HEAD_DIM
SSM_GROUP = 16
SSM_GROUPS = D_MODEL // SSM_GROUP
SSM_STATE = 64
XA_HEADS = 4
XA_HEAD_DIM = D_MODEL // XA_HEADS
N_EXPERTS = 32
TOP_K = 4
D_EXPERT = D_MODEL
SWIGLU_LIMIT = 7.0
SWIGLU_ALPHA = 1.702
ROPE_THETA = 500000.0
ROPE_HALF = HEAD_DIM // 8
LN_EPS = 1e-5
DN_ALPHA = (2 * DEPTH) ** 0.25

LANES = 128
SUBLANES = 8
VMEM_LIMIT_BYTES = 56 * 1024 * 1024

Q_BLOCK = 128
DSA_KEY_TILE = 512
DSA_ATT_TILE = 256
SB_KEY_TILE = 256
ROW_TILE = 256
MOE_BLOCK_ROWS = 256
SC_GATHER_WINDOW = 32
S5_CHUNK = 128
S5_LANE_BLOCKS = 4
S5_IN_BLK = D_MODEL // S5_LANE_BLOCKS
S5_ST_BLK = SSM_GROUPS * SSM_STATE // S5_LANE_BLOCKS

SB_EXIT_LOG = -104.0
NEG_BIG = -1e30
INT_MIN = -(2 ** 31)


def _cparams(*sem):
    return pltpu.CompilerParams(dimension_semantics=sem, vmem_limit_bytes=VMEM_LIMIT_BYTES)


def _dot(a, b):
    return jnp.dot(a, b, preferred_element_type=F32)


def _dot_nt(a, b):
    return lax.dot_general(a, b, (((1,), (1,)), ((), ())), preferred_element_type=F32)


def _layer_norm_rows(y, g, b):
    mu = jnp.mean(y, axis=-1, keepdims=True)
    d = y - mu
    var = jnp.mean(d * d, axis=-1, keepdims=True)
    return d * lax.rsqrt(var + LN_EPS) * g + b


def _mm_kernel(x_ref, w_ref, o_ref):
    o_ref[...] = _dot(x_ref[...].astype(BF16), w_ref[...]).astype(o_ref.dtype)


def _matmul(x, w, *, tm, out_dtype, x_map=None, out_map=None, grid=None, out_shape=None):
    m, k = x.shape
    n = w.shape[1]
    grid = grid or (m // tm,)
    x_map = x_map or (lambda i: (i, 0))
    out_map = out_map or (lambda i: (i, 0))
    out_shape = out_shape or (m, n)
    return pl.pallas_call(
        _mm_kernel,
        grid=grid,
        in_specs=[pl.BlockSpec((tm, k), x_map), pl.BlockSpec((k, n), lambda *a: (0, 0))],
        out_specs=pl.BlockSpec((tm, n), out_map),
        out_shape=jax.ShapeDtypeStruct(out_shape, out_dtype),
        compiler_params=_cparams(*(("parallel",) * len(grid))),
        name="matmul",
    )(x, w)


def _lin_ln_kernel(*refs, n_in, glu):
    xs, ws = refs[:n_in], refs[n_in:2 * n_in]
    res_ref, g_ref, b_ref, o_ref = refs[2 * n_in:]
    acc = _dot(xs[0][...].astype(BF16), ws[0][...])
    for x_ref, w_ref in zip(xs[1:], ws[1:]):
        acc = acc + _dot(x_ref[...].astype(BF16), w_ref[...])
    if glu:
        acc = acc[:, :D_MODEL] * jax.nn.sigmoid(acc[:, D_MODEL:])
    y = DN_ALPHA * res_ref[...] + acc
    o_ref[...] = _layer_norm_rows(y, g_ref[...], b_ref[...])


def _linear_residual_ln(xs, ws, res, g, b, *, tm, glu=False, grid=None, x_maps=None, res_map=None):
    n_rows = res.shape[0]
    grid = grid or (n_rows // tm,)
    x_maps = x_maps or [lambda i: (i, 0)] * len(xs)
    res_map = res_map or (lambda i: (i, 0))
    const = lambda *a: (0, 0)
    in_specs = [pl.BlockSpec((tm, w.shape[0]), m) for w, m in zip(ws, x_maps)]
    in_specs += [pl.BlockSpec(w.shape, const) for w in ws]
    in_specs += [pl.BlockSpec((tm, D_MODEL), res_map), pl.BlockSpec((1, D_MODEL), const),
                 pl.BlockSpec((1, D_MODEL), const)]
    return pl.pallas_call(
        functools.partial(_lin_ln_kernel, n_in=len(xs), glu=glu),
        grid=grid,
        in_specs=in_specs,
        out_specs=pl.BlockSpec((tm, D_MODEL), res_map),
        out_shape=jax.ShapeDtypeStruct((n_rows, D_MODEL), F32),
        compiler_params=_cparams(*(("parallel",) * len(grid))),
        name="linear_residual_ln",
    )(*xs, *ws, res, g.reshape(1, D_MODEL), b.reshape(1, D_MODEL))


_EV_CQ, _EV_KA, _EV_VA, _EV_KI, _EV_QB = 0, 256, 384, 512, 640
_EV_KB = _EV_QB + B_WIDTH
_EV_VB = _EV_KB + B_WIDTH
_EV_COLS = _EV_VB + B_WIDTH


def _rope_tables(seq):
    inv = ROPE_THETA ** (-jnp.arange(ROPE_HALF, dtype=F32) / ROPE_HALF)
    ang = jnp.arange(seq, dtype=F32)[:, None] * inv[None, :]
    cos, sin = jnp.cos(ang), jnp.sin(ang)
    rest = HEAD_DIM - 2 * ROPE_HALF
    zh = jnp.zeros((seq, ROPE_HALF), F32)
    c = jnp.concatenate([cos, cos, jnp.ones((seq, rest), F32)], axis=1)
    s1 = jnp.concatenate([-sin, zh, jnp.zeros((seq, rest), F32)], axis=1)
    s2 = jnp.concatenate([zh, sin, jnp.zeros((seq, rest), F32)], axis=1)
    rep = LANES // HEAD_DIM
    return jnp.tile(c, (1, rep)), jnp.tile(s1, (1, rep)), jnp.tile(s2, (1, rep))


def _even_proj_kernel(x_ref, w_ref, qg_ref, wuq_ref, wuqi_ref, lg_ref, lb_ref, c_ref, s1_ref, s2_ref,
                      qa_ref, qi_ref, ka_ref, va_ref, ki_ref, wi_ref, qb_ref, kb_ref, vb_ref):
    p = _dot(x_ref[...].astype(BF16), w_ref[...])
    c, s1, s2 = c_ref[...], s1_ref[...], s2_ref[...]

    def rope(t):
        return (t * c + pltpu.roll(t, LANES - ROPE_HALF, 1) * s1 + pltpu.roll(t, ROPE_HALF, 1) * s2)

    cq = p[:, _EV_CQ:_EV_CQ + Q_RANK]
    cn = cq * lax.rsqrt(jnp.mean(cq * cq, axis=-1, keepdims=True) + LN_EPS) * qg_ref[...]
    cnb = cn.astype(BF16)
    qa = _dot(cnb, wuq_ref[...])
    qi = _dot(cnb, wuqi_ref[...])
    low = lax.broadcasted_iota(I32, c.shape, 1) < HEAD_DIM
    for j in range(A_WIDTH // LANES):
        sl = slice(j * LANES, (j + 1) * LANES)
        pair = rope(qa[:, sl]) * (HEAD_DIM ** -0.5)
        for e, src in enumerate((pair, pltpu.roll(pair, HEAD_DIM, 1))):
            h = 2 * j + e
            qa_ref[:, h * LANES:(h + 1) * LANES] = jnp.where(low, src, 0.0).astype(BF16)
        qi_ref[:, sl] = (rope(qi[:, sl]) * (IDX_DIM ** -0.5)).astype(BF16)
    kpair = rope(p[:, _EV_KA:_EV_KA + LANES])
    vpair = p[:, _EV_VA:_EV_VA + LANES]
    ones_lane = lax.broadcasted_iota(I32, c.shape, 1) == HEAD_DIM
    for g, (ks, vs) in enumerate(((kpair, vpair), (pltpu.roll(kpair, HEAD_DIM, 1), pltpu.roll(vpair, HEAD_DIM, 1)))):
        ka_ref[:, g * LANES:(g + 1) * LANES] = jnp.where(low, ks, 0.0).astype(BF16)
        va_ref[:, g * LANES:(g + 1) * LANES] = jnp.where(low, vs, jnp.where(ones_lane, 1.0, 0.0)).astype(BF16)

    t = p[:, _EV_KI:_EV_KI + LANES]
    lane = lax.broadcasted_iota(I32, t.shape, 1)
    is_k = lane < IDX_DIM
    mu = jnp.sum(jnp.where(is_k, t, 0.0), axis=-1, keepdims=True) * (1.0 / IDX_DIM)
    d = jnp.where(is_k, t - mu, 0.0)
    var = jnp.sum(d * d, axis=-1, keepdims=True) * (1.0 / IDX_DIM)
    kin = d * lax.rsqrt(var + LN_EPS) * lg_ref[...] + lb_ref[...]
    ki_ref[...] = rope(kin)[:, :IDX_DIM].astype(BF16)
    wi_ref[...] = t[:, IDX_DIM:IDX_DIM + IDX_HEADS] * (IDX_HEADS ** -0.5)

    qb_ref[...] = (p[:, _EV_QB:_EV_KB] * (HEAD_DIM ** -0.5)).astype(BF16)
    kb_ref[...] = p[:, _EV_KB:_EV_VB].astype(BF16)
    vb_ref[...] = p[:, _EV_VB:_EV_COLS].astype(BF16)


def _even_proj(x2d, bsz, seq, w_in, qnorm_g, w_uq, w_uq_idx, kidx_g, kidx_b):
    n = x2d.shape[0]
    tm = ROW_TILE
    per_seq = seq // tm
    c0 = Q_RANK + 2 * A_KV_HEADS * HEAD_DIM + IDX_DIM + IDX_HEADS
    w_pack = jnp.concatenate(
        [w_in[:, :c0], jnp.zeros((D_MODEL, _EV_QB - c0), w_in.dtype), w_in[:, c0:]], axis=1).astype(BF16)
    pad = LANES - IDX_DIM
    lg = jnp.concatenate([kidx_g, jnp.zeros((pad,), F32)]).reshape(1, LANES)
    lb = jnp.concatenate([kidx_b, jnp.zeros((pad,), F32)]).reshape(1, LANES)
    c, s1, s2 = _rope_tables(seq)
    row = lambda i: (i, 0)
    const = lambda i: (0, 0)
    pos = lambda i: (i % per_seq, 0)
    head_shape = jax.ShapeDtypeStruct((n, B_WIDTH), BF16)
    head_spec = pl.BlockSpec((tm, B_WIDTH), row)
    return pl.pallas_call(
        _even_proj_kernel,
        grid=(n // tm,),
        in_specs=[pl.BlockSpec((tm, D_MODEL), row), pl.BlockSpec((D_MODEL, _EV_COLS), const),
                  pl.BlockSpec((1, Q_RANK), const), pl.BlockSpec((Q_RANK, A_WIDTH), const),
                  pl.BlockSpec((Q_RANK, IDX_HEADS * IDX_DIM), const),
                  pl.BlockSpec((1, LANES), const), pl.BlockSpec((1, LANES), const),
                  pl.BlockSpec((tm, LANES), pos), pl.BlockSpec((tm, LANES), pos), pl.BlockSpec((tm, LANES), pos)],
        out_specs=[pl.BlockSpec((tm, A_HEADS * LANES), row), pl.BlockSpec((tm, IDX_HEADS * IDX_DIM), row),
                   pl.BlockSpec((tm, A_KV_HEADS * LANES), row), pl.BlockSpec((tm, A_KV_HEADS * LANES), row),
                   pl.BlockSpec((tm, IDX_DIM), row), pl.BlockSpec((tm, IDX_HEADS), row),
                   head_spec, head_spec, head_spec],
        out_shape=[jax.ShapeDtypeStruct((n, A_HEADS * LANES), BF16), jax.ShapeDtypeStruct((n, IDX_HEADS * IDX_DIM), BF16),
                   jax.ShapeDtypeStruct((n, A_KV_HEADS * LANES), BF16),
                   jax.ShapeDtypeStruct((n, A_KV_HEADS * LANES), BF16),
                   jax.ShapeDtypeStruct((n, IDX_DIM), BF16), jax.ShapeDtypeStruct((n, IDX_HEADS), F32),
                   head_shape, head_shape, head_shape],
        compiler_params=_cparams("parallel"),
        name="even_proj",
    )(x2d, w_pack, qnorm_g.reshape(1, Q_RANK), w_uq.astype(BF16), w_uq_idx.astype(BF16), lg, lb, c, s1, s2)


def _key_to_float(key):
    bits = key ^ ((key >> 31) & jnp.int32(0x7FFFFFFF))
    return lax.bitcast_convert_type(bits, F32)


def _dsa_kernel(qa_ref, qi_ref, wi_ref, ki_ref, ka_ref, va_ref, o_ref, sc_scr, *, topk, ts, ta):
    qb = pl.program_id(1)
    q0 = qb * Q_BLOCK
    nkt = (q0 + Q_BLOCK - 1) // ts + 1
    t_col = q0 + lax.broadcasted_iota(I32, (Q_BLOCK, 1), 0)
    lane = lax.broadcasted_iota(I32, (Q_BLOCK, ts), 1)
    kf = jnp.float32(topk)

    qi = qi_ref[0]
    wi = wi_ref[0]
    qih = [qi[:, h * IDX_DIM:(h + 1) * IDX_DIM] for h in range(IDX_HEADS)]
    wih = [wi[:, h:h + 1] for h in range(IDX_HEADS)]

    def score_tile(kt, carry):
        off = pl.multiple_of(kt * ts, ts)
        kit = ki_ref[0, pl.ds(off, ts), :]
        acc = jnp.zeros((Q_BLOCK, ts), F32)
        for h in range(IDX_HEADS):
            acc = acc + jnp.maximum(_dot_nt(qih[h], kit), 0.0) * wih[h]
        sc_scr[:, pl.ds(off, ts)] = jnp.where(off + lane <= t_col, acc, -jnp.inf)
        return carry

    lax.fori_loop(0, nkt, score_tile, 0)

    def count(pred):
        def body(kt, acc):
            off = pl.multiple_of(kt * ts, ts)
            ind = pred(sc_scr[:, pl.ds(off, ts)], off + lane)
            for j in range(ts // LANES):
                acc = acc + ind[:, j * LANES:(j + 1) * LANES]
            return acc
        acc = lax.fori_loop(0, nkt, body, jnp.zeros((Q_BLOCK, LANES), F32))
        return jnp.sum(acc, axis=1, keepdims=True)

    def bit_step(i, base):
        cand = base + jnp.left_shift(jnp.int32(1), 31 - i)
        cf = _key_to_float(cand)
        cnt = count(lambda sc, idx: jnp.where(sc >= cf, 1.0, 0.0))
        return jnp.where(cnt >= kf, cand, base)

    base = lax.fori_loop(0, 32, bit_step, jnp.full((Q_BLOCK, 1), INT_MIN, I32))
    thr = jnp.where(base == INT_MIN, -jnp.inf, _key_to_float(base))

    cnt_ge = count(lambda sc, idx: jnp.where(sc >= thr, 1.0, 0.0))
    tied = jnp.logical_and(cnt_ge > kf, thr > -jnp.inf)
    any_tied = jnp.max(jnp.where(tied, 1.0, 0.0)) > 0.0
    seq_bits = max(1, int(math.ceil(math.log2(sc_scr.shape[1]))))

    def tie_cut():
        cnt_gt = count(lambda sc, idx: jnp.where(sc > thr, 1.0, 0.0))
        need = kf - cnt_gt

        def idx_step(i, pos):
            cand = pos + jnp.left_shift(jnp.int32(1), seq_bits - 1 - i)
            cnt = count(lambda sc, idx: jnp.where(sc == thr, jnp.where(idx < cand, 1.0, 0.0), 0.0))
            return jnp.where(cnt < need, cand, pos)

        return lax.fori_loop(0, seq_bits, idx_step, jnp.zeros((Q_BLOCK, 1), I32))

    cut = lax.cond(any_tied, tie_cut, lambda: jnp.full((Q_BLOCK, 1), sc_scr.shape[1], I32))
    cut = jnp.where(tied, cut, sc_scr.shape[1])

    nkt_a = (q0 + Q_BLOCK - 1) // ta + 1
    lane_a = lax.broadcasted_iota(I32, (Q_BLOCK, ta), 1)
    rows = A_REP * Q_BLOCK
    qg = [jnp.concatenate([qa_ref[0, :, (g * A_REP + r) * LANES:(g * A_REP + r + 1) * LANES]
                           for r in range(A_REP)], axis=0) for g in range(A_KV_HEADS)]

    def att_tile(kt, carry):
        off = pl.multiple_of(kt * ta, ta)
        sc = sc_scr[:, pl.ds(off, ta)]
        idx = off + lane_a
        keep = jnp.where(sc > thr, 0.0, jnp.where(sc == thr, jnp.where(idx <= cut, 0.0, NEG_BIG), NEG_BIG))
        bias = jnp.where(idx <= t_col, keep, NEG_BIG)
        bias = jnp.concatenate([bias] * A_REP, axis=0)
        out = []
        for g in range(A_KV_HEADS):
            m, acc = carry[g]
            kg = ka_ref[0, pl.ds(off, ta), g * LANES:(g + 1) * LANES]
            vg = va_ref[0, pl.ds(off, ta), g * LANES:(g + 1) * LANES]
            s = _dot_nt(qg[g], kg) + bias
            m_new = jnp.maximum(m, jnp.max(s, axis=1, keepdims=True))
            p = jnp.exp(s - m_new)
            acc = jnp.exp(m - m_new) * acc + _dot(p.astype(BF16), vg)
            out.append((m_new, acc))
        return tuple(out)

    init = tuple((jnp.full((rows, 1), NEG_BIG, F32), jnp.zeros((rows, LANES), F32)) for _ in range(A_KV_HEADS))
    final = lax.fori_loop(0, nkt_a, att_tile, init)
    low = lax.broadcasted_iota(I32, (Q_BLOCK, LANES), 1) < HEAD_DIM
    outs = []
    for g in range(A_KV_HEADS):
        acc = final[g][1]
        og = acc / acc[:, HEAD_DIM:HEAD_DIM + 1]
        outs += [og[r * Q_BLOCK:(r + 1) * Q_BLOCK] for r in range(A_REP)]
    for j in range(A_HEADS // 2):
        pair = jnp.where(low, outs[2 * j], pltpu.roll(outs[2 * j + 1], HEAD_DIM, 1))
        o_ref[0, :, j * LANES:(j + 1) * LANES] = pair.astype(BF16)


def _dsa_attention(qa, qi, wi, ki, ka, va, bsz, seq):
    topk = min(IDX_TOPK, seq // 4)
    ts = min(DSA_KEY_TILE, seq)
    ta = min(DSA_ATT_TILE, seq)
    blk = lambda b, i: (b, i, 0)
    full = lambda b, i: (b, 0, 0)
    r3 = lambda a: a.reshape(bsz, seq, a.shape[-1])
    return pl.pallas_call(
        functools.partial(_dsa_kernel, topk=topk, ts=ts, ta=ta),
        grid=(bsz, seq // Q_BLOCK),
        in_specs=[pl.BlockSpec((1, Q_BLOCK, A_HEADS * LANES), blk),
                  pl.BlockSpec((1, Q_BLOCK, IDX_HEADS * IDX_DIM), blk),
                  pl.BlockSpec((1, Q_BLOCK, IDX_HEADS), blk), pl.BlockSpec((1, seq, IDX_DIM), full),
                  pl.BlockSpec((1, seq, A_KV_HEADS * LANES), full), pl.BlockSpec((1, seq, A_KV_HEADS * LANES), full)],
        out_specs=pl.BlockSpec((1, Q_BLOCK, A_WIDTH), blk),
        out_shape=jax.ShapeDtypeStruct((bsz, seq, A_WIDTH), BF16),
        scratch_shapes=[pltpu.VMEM((Q_BLOCK, seq), F32)],
        compiler_params=_cparams("parallel", "parallel"),
        name="dsa_attention",
    )(r3(qa), r3(qi), r3(wi), r3(ki), r3(ka), r3(va)).reshape(bsz * seq, A_WIDTH)


def _sb_kernel(q_ref, k_ref, v_ref, u_ref, o_ref, acc_scr, run_scr, *, tk):
    q0 = pl.program_id(1) * Q_BLOCK
    t_col = q0 + lax.broadcasted_iota(I32, (Q_BLOCK, 1), 0)
    lane = lax.broadcasted_iota(I32, (Q_BLOCK, tk), 1)
    low = lax.broadcasted_iota(I32, (Q_BLOCK, LANES), 1) < HEAD_DIM
    upper = u_ref[...]
    nkt = (q0 + Q_BLOCK - 1) // tk + 1
    q = q_ref[0]
    zero = jnp.zeros((Q_BLOCK, LANES), BF16)
    qm = []
    for h in range(B_HEADS):
        pair = q[:, (h // 2) * LANES:(h // 2 + 1) * LANES]
        qm.append(jnp.where(low, pair, zero) if h % 2 == 0 else jnp.where(low, zero, pair))
    acc_scr[...] = jnp.zeros_like(acc_scr)
    run_scr[...] = jnp.zeros_like(run_scr)

    def cond(carry):
        i, worst = carry
        return jnp.logical_and(i < nkt, worst >= SB_EXIT_LOG)

    def body(carry):
        i, _ = carry
        off = pl.multiple_of((nkt - 1 - i) * tk, tk)
        strict = off + lane < t_col
        worst = None
        for p in range(B_HEADS // 2):
            cols = slice(p * LANES, (p + 1) * LANES)
            kp = k_ref[0, pl.ds(off, tk), cols]
            vp = v_ref[0, pl.ds(off, tk), cols]
            outs = []
            for e in range(2):
                h = 2 * p + e
                run = run_scr[h]
                z = _dot_nt(qm[h], kp)
                softplus = jnp.maximum(z, 0.0) + jnp.log(1.0 + jnp.exp(-jnp.abs(z)))
                log_1mb = jnp.where(strict, -softplus, 0.0)
                hi = log_1mb.astype(BF16)
                lo = (log_1mb - hi.astype(F32)).astype(BF16)
                after = _dot(hi, upper) + _dot(lo, upper) + run
                a = jnp.where(strict, jnp.exp(z - softplus + after), 0.0)
                outs.append(_dot(a.astype(BF16), vp))
                run = run + jnp.sum(log_1mb, axis=1, keepdims=True)
                run_scr[h] = run
                worst = run if worst is None else jnp.maximum(worst, run)
            acc_scr[:, cols] += jnp.where(low, outs[0], outs[1])
        return i + 1, jnp.max(worst)

    lax.while_loop(cond, body, (jnp.int32(0), jnp.float32(0.0)))
    o_ref[0] = acc_scr[...].astype(BF16)


def _stick_breaking(qb, kb, vb, bsz, seq):
    tk = min(SB_KEY_TILE, seq)
    r = lax.broadcasted_iota(I32, (tk, tk), 0)
    c = lax.broadcasted_iota(I32, (tk, tk), 1)
    upper = jnp.where(r > c, 1.0, 0.0).astype(BF16)
    blk = lambda b, i: (b, i, 0)
    full = lambda b, i: (b, 0, 0)
    r3 = lambda a: a.reshape(bsz, seq, B_WIDTH)
    return pl.pallas_call(
        functools.partial(_sb_kernel, tk=tk),
        grid=(bsz, seq // Q_BLOCK),
        in_specs=[pl.BlockSpec((1, Q_BLOCK, B_WIDTH), blk), pl.BlockSpec((1, seq, B_WIDTH), full),
                  pl.BlockSpec((1, seq, B_WIDTH), full), pl.BlockSpec((tk, tk), lambda b, i: (0, 0))],
        out_specs=pl.BlockSpec((1, Q_BLOCK, B_WIDTH), blk),
        out_shape=jax.ShapeDtypeStruct((bsz, seq, B_WIDTH), BF16),
        scratch_shapes=[pltpu.VMEM((Q_BLOCK, B_WIDTH), F32), pltpu.VMEM((B_HEADS, Q_BLOCK, 1), F32)],
        compiler_params=_cparams("parallel", "arbitrary"),
        name="stick_breaking",
    )(r3(qb), r3(kb), r3(vb), upper).reshape(bsz * seq, B_WIDTH)


def _xattn_kernel(h_ref, wq_ref, kv_ref, wo_ref, g_ref, b_ref, o_ref):
    h = h_ref[...]
    q = (_dot(h.astype(BF16), wq_ref[...]) * (XA_HEAD_DIM ** -0.5)).astype(BF16)
    kv = kv_ref[0]
    outs = []
    for hd in range(XA_HEADS):
        sl = slice(hd * XA_HEAD_DIM, (hd + 1) * XA_HEAD_DIM)
        s = _dot_nt(q[:, sl], kv[:, sl])
        p = jnp.exp(s - jnp.max(s, axis=1, keepdims=True))
        vh = kv[:, D_MODEL + hd * XA_HEAD_DIM:D_MODEL + (hd + 1) * XA_HEAD_DIM]
        outs.append((_dot(p.astype(BF16), vh) / jnp.sum(p, axis=1, keepdims=True)).astype(BF16))
    y = _dot(jnp.concatenate(outs, axis=1), wo_ref[...])
    o_ref[...] = _layer_norm_rows(DN_ALPHA * h + y, g_ref[...], b_ref[...])


def _cross_attention_block(h2d, mem, bsz, seq, w_q, w_kv, w_o, g, b):
    tm = ROW_TILE
    per_seq = seq // tm
    mem_len = mem.shape[1]
    kv = _matmul(mem.reshape(bsz * mem_len, D_MODEL), w_kv.astype(BF16), tm=mem_len, out_dtype=BF16)
    kv = kv.reshape(bsz, mem_len, 2 * D_MODEL)
    row = lambda i: (i, 0)
    const = lambda i: (0, 0)
    return pl.pallas_call(
        _xattn_kernel,
        grid=(bsz * per_seq,),
        in_specs=[pl.BlockSpec((tm, D_MODEL), row), pl.BlockSpec((D_MODEL, D_MODEL), const),
                  pl.BlockSpec((1, mem_len, 2 * D_MODEL), lambda i: (i // per_seq, 0, 0)),
                  pl.BlockSpec((D_MODEL, D_MODEL), const),
                  pl.BlockSpec((1, D_MODEL), const), pl.BlockSpec((1, D_MODEL), const)],
        out_specs=pl.BlockSpec((tm, D_MODEL), row),
        out_shape=jax.ShapeDtypeStruct(h2d.shape, F32),
        compiler_params=_cparams("parallel"),
        name="cross_attention",
    )(h2d, w_q.astype(BF16), kv, w_o.astype(BF16), g.reshape(1, D_MODEL), b.reshape(1, D_MODEL))


def _router_kernel(h_ref, w_ref, b_ref, tri_ref, idx_ref, gate_ref, rank_ref, cnt_ref, run_scr):
    @pl.when(pl.program_id(0) == 0)
    def _():
        run_scr[...] = jnp.zeros_like(run_scr)

    h = h_ref[...]
    hh = h.astype(BF16)
    hl = (h - hh.astype(F32)).astype(BF16)
    w = w_ref[...]
    wh = w.astype(BF16)
    wl = (w - wh.astype(F32)).astype(BF16)
    logits = _dot(hh, wh) + _dot(hl, wh) + _dot(hh, wl) + b_ref[...]
    lane = lax.broadcasted_iota(I32, logits.shape, 1).astype(F32)
    vals, sels = [], []
    onehot = jnp.zeros(logits.shape, F32)
    for k in range(TOP_K):
        m = jnp.max(logits, axis=1, keepdims=True)
        sel = jnp.min(jnp.where(logits == m, lane, float(LANES)), axis=1, keepdims=True)
        idx_ref[:, k:k + 1] = sel.astype(I32)
        vals.append(m)
        sels.append(sel)
        onehot = onehot + jnp.where(lane == sel, 1.0, 0.0)
        logits = jnp.where(lane == sel, -jnp.inf, logits)
    es = [jnp.exp(v - vals[0]) for v in vals]
    tot = es[0] + es[1] + es[2] + es[3]
    for k in range(TOP_K):
        gate_ref[:, k:k + 1] = es[k] / tot

    earlier = _dot(tri_ref[...], onehot.astype(BF16)) + run_scr[...]
    for k in range(TOP_K):
        rank = jnp.sum(jnp.where(lane == sels[k], earlier, 0.0), axis=1, keepdims=True)
        rank_ref[:, k:k + 1] = rank.astype(I32)
    run = run_scr[...] + jnp.sum(onehot, axis=0, keepdims=True)
    run_scr[...] = run
    cnt_ref[...] = run


def _router(h2d, w_router, b_router):
    n = h2d.shape[0]
    tm = 2 * ROW_TILE
    pad = LANES - N_EXPERTS
    w = jnp.concatenate([w_router, jnp.zeros((D_MODEL, pad), F32)], axis=1)
    b = jnp.concatenate([b_router, jnp.full((pad,), NEG_BIG, F32)]).reshape(1, LANES)
    r = lax.broadcasted_iota(I32, (tm, tm), 0)
    c = lax.broadcasted_iota(I32, (tm, tm), 1)
    tri = jnp.where(c < r, 1.0, 0.0).astype(BF16)
    row = lambda i: (i, 0)
    const = lambda i: (0, 0)
    return pl.pallas_call(
        _router_kernel,
        grid=(n // tm,),
        in_specs=[pl.BlockSpec((tm, D_MODEL), row), pl.BlockSpec((D_MODEL, LANES), const),
                  pl.BlockSpec((1, LANES), const), pl.BlockSpec((tm, tm), const)],
        out_specs=[pl.BlockSpec((tm, TOP_K), row), pl.BlockSpec((tm, TOP_K), row),
                   pl.BlockSpec((tm, TOP_K), row), pl.BlockSpec((1, LANES), const)],
        out_shape=[jax.ShapeDtypeStruct((n, TOP_K), I32), jax.ShapeDtypeStruct((n, TOP_K), F32),
                   jax.ShapeDtypeStruct((n, TOP_K), I32), jax.ShapeDtypeStruct((1, LANES), F32)],
        scratch_shapes=[pltpu.VMEM((1, LANES), F32)],
        compiler_params=_cparams("arbitrary"),
        name="moe_router",
    )(h2d, w, b, tri)


def _gather_rows(src, idx):
    n_out = idx.shape[0]
    width = src.shape[1]
    win = SC_GATHER_WINDOW
    mesh = plsc.VectorSubcoreMesh(core_axis_name="core", subcore_axis_name="subcore")
    n_workers = mesh.num_cores * mesh.num_subcores
    per_worker = n_out // n_workers
    steps = per_worker // win
    assert per_worker * n_workers == n_out and steps * win == per_worker and steps % 2 == 0

    @functools.partial(
        pl.kernel, out_type=jax.ShapeDtypeStruct((n_out, width), src.dtype), mesh=mesh,
        scratch_types=[pltpu.VMEM((per_worker,), I32), pltpu.VMEM((2, win, width), src.dtype),
                       pltpu.SemaphoreType.DMA, pltpu.SemaphoreType.DMA])
    def gather_kernel(src_hbm, idx_hbm, dst_hbm, idx_v, rows_v, sem0, sem1):
        worker = lax.axis_index("subcore") * mesh.num_cores + lax.axis_index("core")
        base = worker * per_worker
        sems = (sem0, sem1)
        pltpu.sync_copy(idx_hbm.at[pl.ds(base, per_worker)], idx_v)

        def gather(step, slot):
            return pltpu.make_async_copy(src_hbm.at[idx_v.at[pl.ds(step * win, win)]], rows_v.at[slot], sems[slot])

        gather(0, 0).start()

        @pl.loop(0, steps, step=2)
        def _(s):
            for slot in range(2):
                step = s + slot
                gather(step, slot).wait()

                @pl.when(step + 1 < steps)
                def _():
                    gather(step + 1, 1 - slot).start()

                pltpu.sync_copy(rows_v.at[slot], dst_hbm.at[pl.ds(base + step * win, win)])

    return gather_kernel(src, idx)


def _expert_kernel(blk_exp_ref, n_used_ref, x_ref, wgu_ref, bgu_ref, wd_ref, bd_ref, o_ref, wgu_bf, wd_bf):
    i = pl.program_id(0)

    @pl.when(jnp.logical_or(i == 0, blk_exp_ref[i] != blk_exp_ref[jnp.maximum(i - 1, 0)]))
    def _():
        wgu_bf[...] = wgu_ref[0, 0].astype(BF16)
        wd_bf[...] = wd_ref[0, 0].astype(BF16)

    @pl.when(i < n_used_ref[0])
    def _():
        hgu = _dot(x_ref[...].astype(BF16), wgu_bf[...]) + bgu_ref[0]
        gate = jnp.minimum(hgu[:, :D_EXPERT], SWIGLU_LIMIT)
        up = jnp.clip(hgu[:, D_EXPERT:], -SWIGLU_LIMIT, SWIGLU_LIMIT)
        act = gate * jax.nn.sigmoid(gate * SWIGLU_ALPHA) * (up + 1.0)
        o_ref[...] = _dot(act.astype(BF16), wd_bf[...]) + bd_ref[0]

    @pl.when(i >= n_used_ref[0])
    def _():
        o_ref[...] = jnp.zeros_like(o_ref)


def _expert_mlp(xs, block_exp, n_used, layer, w_gu, b_gu, w_down, b_down):
    n_rows = xs.shape[0]
    bm = MOE_BLOCK_ROWS
    row = lambda i, be, nu: (i, 0)
    exp3 = lambda i, be, nu: (be[i], 0, 0)
    exp4 = lambda i, be, nu: (layer, be[i], 0, 0)
    grid_spec = pltpu.PrefetchScalarGridSpec(
        num_scalar_prefetch=2,
        grid=(n_rows // bm,),
        in_specs=[pl.BlockSpec((bm, D_MODEL), row),
                  pl.BlockSpec((1, 1, D_MODEL, 2 * D_EXPERT), exp4), pl.BlockSpec((1, 1, 2 * D_EXPERT), exp3),
                  pl.BlockSpec((1, 1, D_EXPERT, D_MODEL), exp4), pl.BlockSpec((1, 1, D_MODEL), exp3)],
        out_specs=pl.BlockSpec((bm, D_MODEL), row),
        scratch_shapes=[pltpu.VMEM((D_MODEL, 2 * D_EXPERT), BF16), pltpu.VMEM((D_EXPERT, D_MODEL), BF16)],
    )
    return pl.pallas_call(
        _expert_kernel,
        grid_spec=grid_spec,
        out_shape=jax.ShapeDtypeStruct((n_rows, D_MODEL), F32),
        compiler_params=_cparams("arbitrary"),
        name="moe_experts",
    )(block_exp, n_used, xs, w_gu, b_gu.reshape(N_EXPERTS, 1, 2 * D_EXPERT),
      w_down, b_down.reshape(N_EXPERTS, 1, D_MODEL))


def _combine_kernel(y0_ref, y1_ref, y2_ref, y3_ref, gate_ref, res_ref, g_ref, b_ref, o_ref):
    gates = gate_ref[...]
    acc = y0_ref[...] * gates[:, 0:1]
    for k, y_ref in enumerate((y1_ref, y2_ref, y3_ref), start=1):
        acc = acc + y_ref[...] * gates[:, k:k + 1]
    o_ref[...] = _layer_norm_rows(DN_ALPHA * res_ref[...] + acc, g_ref[...], b_ref[...])


def _moe_block(h2d, layer, w_router, b_router, w_gu, b_gu, w_down, b_down, g, b):
    n = h2d.shape[0]
    n_slots = n * TOP_K
    bm = MOE_BLOCK_ROWS
    top_idx, gates, rank, totals = _router(h2d, w_router, b_router)

    e_flat = top_idx.reshape(-1)
    order = jnp.argsort(e_flat).astype(I32)
    counts = totals[0, :N_EXPERTS].astype(I32)
    padded = (counts + bm - 1) // bm * bm
    start = jnp.cumsum(counts) - counts
    ends_p = jnp.cumsum(padded)
    pstart = ends_p - padded
    n_rows = n_slots + N_EXPERTS * bm
    n_blocks = n_rows // bm
    r = jnp.arange(n_rows, dtype=I32)
    e_r = jnp.minimum(jnp.searchsorted(ends_p, r, side="right"), N_EXPERTS - 1).astype(I32)
    j = r - pstart[e_r]
    valid = j < counts[e_r]
    slot_of_row = order[jnp.where(valid, start[e_r] + j, 0)]
    rows_tok = jnp.where(valid, slot_of_row // TOP_K, 0).astype(I32)
    slot_pos = pstart[e_flat] + rank.reshape(-1)
    block_exp = e_r[::bm]
    n_used = (ends_p[-1] // bm).astype(I32).reshape(1)

    xs = _gather_rows(h2d, rows_tok)
    ys = _expert_mlp(xs, block_exp, n_used, layer, w_gu, b_gu, w_down, b_down)
    yk = _gather_rows(ys, slot_pos.reshape(n, TOP_K).T.reshape(-1))

    tm = ROW_TILE
    row = lambda i: (i, 0)
    const = lambda i: (0, 0)
    choice = lambda k: (lambda i: (k * (n // tm) + i, 0))
    return pl.pallas_call(
        _combine_kernel,
        grid=(n // tm,),
        in_specs=[pl.BlockSpec((tm, D_MODEL), choice(k)) for k in range(TOP_K)] + [
                  pl.BlockSpec((tm, TOP_K), row),
                  pl.BlockSpec((tm, D_MODEL), row), pl.BlockSpec((1, D_MODEL), const),
                  pl.BlockSpec((1, D_MODEL), const)],
        out_specs=pl.BlockSpec((tm, D_MODEL), row),
        out_shape=jax.ShapeDtypeStruct((n, D_MODEL), F32),
        compiler_params=_cparams("parallel"),
        name="moe_combine",
    )(yk, yk, yk, yk, gates, h2d, g.reshape(1, D_MODEL), b.reshape(1, D_MODEL))


def _s5_kernel(u_ref, bre_ref, bim_ref, cre_ref, cim_ref, are_ref, aim_ref, d_ref, y_ref,
               bu_re, bu_im, st_re, st_im, h_re, h_im, *, bsz):
    @pl.when(pl.program_id(0) == 0)
    def _():
        h_re[...] = jnp.zeros_like(h_re)
        h_im[...] = jnp.zeros_like(h_im)

    rows = u_ref.shape[0]
    first = lax.broadcasted_iota(I32, (SUBLANES, S5_ST_BLK), 0) < bsz
    for j in range(S5_LANE_BLOCKS):
        cin = slice(j * S5_IN_BLK, (j + 1) * S5_IN_BLK)
        cst = slice(j * S5_ST_BLK, (j + 1) * S5_ST_BLK)
        uj = u_ref[:, cin]
        ujb = uj.astype(BF16)
        bu_re[...] = _dot(ujb, bre_ref[j])
        bu_im[...] = _dot(ujb, bim_ref[j])
        ar = jnp.broadcast_to(are_ref[:, cst], (SUBLANES, S5_ST_BLK))
        ai = jnp.broadcast_to(aim_ref[:, cst], (SUBLANES, S5_ST_BLK))

        def step(i, carry):
            hr, hi = carry
            r0 = pl.multiple_of(i * SUBLANES, SUBLANES)
            vr = bu_re[pl.ds(r0, SUBLANES), :]
            vi = bu_im[pl.ds(r0, SUBLANES), :]
            h1r = ar * hr - ai * hi + vr
            h1i = ar * hi + ai * hr + vi
            h1rs = pltpu.roll(h1r, bsz, 0)
            h1is = pltpu.roll(h1i, bsz, 0)
            h2r = ar * h1rs - ai * h1is + vr
            h2i = ar * h1is + ai * h1rs + vi
            st_re[pl.ds(r0, SUBLANES), :] = jnp.where(first, h1r, h2r)
            st_im[pl.ds(r0, SUBLANES), :] = jnp.where(first, h1i, h2i)
            return pltpu.roll(h2r, bsz, 0), pltpu.roll(h2i, bsz, 0)

        hr, hi = lax.fori_loop(0, rows // SUBLANES, step, (h_re[:, cst], h_im[:, cst]))
        h_re[:, cst] = hr
        h_im[:, cst] = hi
        yj = _dot(st_re[...].astype(BF16), cre_ref[j]) + _dot(st_im[...].astype(BF16), cim_ref[j])
        yj = yj + d_ref[:, cin] * uj
        y_ref[:, cin] = jax.nn.gelu(yj).astype(BF16)


def _s5_block_diag(w, n_in, n_out):
    gpb = SSM_GROUPS // S5_LANE_BLOCKS
    w4 = w.reshape(S5_LANE_BLOCKS, gpb, n_in, n_out)
    eye = jnp.eye(gpb, dtype=w.dtype)
    return jnp.einsum("jgio,gh->jgiho", w4, eye).reshape(S5_LANE_BLOCKS, gpb * n_in, gpb * n_out)


def _s5_mixer_block(h2d, bsz, seq, w_in, log_dt, lam_re, lam_im, b_re, b_im, c_re, c_im, d, w_out, g, b):
    assert 2 * bsz == SUBLANES, "the scan packs two time steps of bsz rows into one 8-row tile"
    tm = ROW_TILE
    per_seq = seq // tm
    u_t = _matmul(h2d, w_in.astype(BF16), tm=tm, out_dtype=F32, grid=(bsz, per_seq),
                  x_map=lambda bb, i: (bb * per_seq + i, 0), out_map=lambda bb, i: (i, bb),
                  out_shape=(seq, bsz * D_MODEL)).reshape(seq * bsz, D_MODEL)

    dt = jnp.exp(log_dt)[:, None]
    mag = jnp.exp(lam_re * dt)
    a_re, a_im = mag * jnp.cos(lam_im * dt), mag * jnp.sin(lam_im * dt)
    den = lam_re * lam_re + lam_im * lam_im
    coef_re = ((a_re - 1.0) * lam_re + a_im * lam_im) / den
    coef_im = (a_im * lam_re - (a_re - 1.0) * lam_im) / den
    bb_re = coef_re[..., None] * b_re - coef_im[..., None] * b_im
    bb_im = coef_re[..., None] * b_im + coef_im[..., None] * b_re
    bre = _s5_block_diag(jnp.swapaxes(bb_re, 1, 2), SSM_GROUP, SSM_STATE).astype(BF16)
    bim = _s5_block_diag(jnp.swapaxes(bb_im, 1, 2), SSM_GROUP, SSM_STATE).astype(BF16)
    cre = _s5_block_diag(jnp.swapaxes(c_re, 1, 2), SSM_STATE, SSM_GROUP).astype(BF16)
    cim = _s5_block_diag(jnp.swapaxes(-c_im, 1, 2), SSM_STATE, SSM_GROUP).astype(BF16)
    n_state = SSM_GROUPS * SSM_STATE

    rows = S5_CHUNK * bsz
    row = lambda c: (c, 0)
    c2 = lambda c: (0, 0)
    c3 = lambda c: (0, 0, 0)
    y_t = pl.pallas_call(
        functools.partial(_s5_kernel, bsz=bsz),
        grid=(seq // S5_CHUNK,),
        in_specs=[pl.BlockSpec((rows, D_MODEL), row),
                  pl.BlockSpec(bre.shape, c3), pl.BlockSpec(bim.shape, c3),
                  pl.BlockSpec(cre.shape, c3), pl.BlockSpec(cim.shape, c3),
                  pl.BlockSpec((1, n_state), c2), pl.BlockSpec((1, n_state), c2), pl.BlockSpec((1, D_MODEL), c2)],
        out_specs=pl.BlockSpec((rows, D_MODEL), row),
        out_shape=jax.ShapeDtypeStruct((seq * bsz, D_MODEL), BF16),
        scratch_shapes=[pltpu.VMEM((rows, S5_ST_BLK), F32)] * 4 + [pltpu.VMEM((SUBLANES, n_state), F32)] * 2,
        compiler_params=_cparams("arbitrary"),
        name="s5_scan",
    )(u_t, bre, bim, cre, cim, a_re.reshape(1, n_state), a_im.reshape(1, n_state), d.reshape(1, D_MODEL))

    y2 = y_t.reshape(seq, bsz * D_MODEL)
    return _linear_residual_ln(
        [y2], [w_out.astype(BF16)], h2d, g, b, tm=tm, glu=True, grid=(bsz, per_seq),
        x_maps=[lambda bb, i: (i, bb)], res_map=lambda bb, i: (bb * per_seq + i, 0))


def _even_mixer_block(h2d, bsz, seq, w_in, qnorm_g, w_uq, w_uq_idx, kidx_g, kidx_b, w_out, g, b):
    qa, qi, ka, va, ki, wi, qb, kb, vb = _even_proj(h2d, bsz, seq, w_in, qnorm_g, w_uq, w_uq_idx, kidx_g, kidx_b)
    o_a = _dsa_attention(qa, qi, wi, ki, ka, va, bsz, seq)
    o_b = _stick_breaking(qb, kb, vb, bsz, seq)
    w_out = w_out.astype(BF16)
    return _linear_residual_ln([o_a, o_b], [w_out[:A_WIDTH], w_out[A_WIDTH:]], h2d, g, b, tm=ROW_TILE)


def kernel(x, mem, ev_w_in, ev_qnorm_g, ev_w_uq, ev_w_uq_idx, ev_kidx_ln_g, ev_kidx_ln_b, ev_w_out, od_w_in, od_log_dt, od_lambda_re, od_lambda_im, od_b_re, od_b_im, od_c_re, od_c_im, od_d, od_w_out, mix_ln_g, mix_ln_b, xa_w_q, xa_w_kv, xa_w_o, xa_ln_g, xa_ln_b, moe_w_router, moe_b_router, moe_w_gu, moe_b_gu, moe_w_down, moe_b_down, ffn_ln_g, ffn_ln_b):
    bsz, seq, _ = x.shape
    h = x.reshape(bsz * seq, D_MODEL)
    for layer in range(DEPTH):
        j = layer // 2
        if layer % 2 == 0:
            h = _even_mixer_block(h, bsz, seq, ev_w_in[j], ev_qnorm_g[j], ev_w_uq[j], ev_w_uq_idx[j],
                                  ev_kidx_ln_g[j], ev_kidx_ln_b[j], ev_w_out[j], mix_ln_g[layer], mix_ln_b[layer])
        else:
            h = _s5_mixer_block(h, bsz, seq, od_w_in[j], od_log_dt[j], od_lambda_re[j], od_lambda_im[j],
                                od_b_re[j], od_b_im[j], od_c_re[j], od_c_im[j], od_d[j], od_w_out[j],
                                mix_ln_g[layer], mix_ln_b[layer])
        h = _cross_attention_block(h, mem, bsz, seq, xa_w_q[layer], xa_w_kv[layer], xa_w_o[layer],
                                   xa_ln_g[layer], xa_ln_b[layer])
        h = _moe_block(h, layer, moe_w_router[layer], moe_b_router[layer], moe_w_gu, moe_b_gu[layer],
                       moe_w_down, moe_b_down[layer], ffn_ln_g[layer], ffn_ln_b[layer])
    return h.reshape(bsz, seq, D_MODEL)
```

```python
import functools
import math

import jax
import jax.numpy as jnp
from jax import lax
from jax.experimental import pallas as pl
from jax.experimental.pallas import tpu as pltpu
from jax.experimental.pallas import tpu_sc as plsc

F32 = jnp.float32
BF16 = jnp.bfloat16
I32 = jnp.int32

D_MODEL = 1024
DEPTH = 2
HEAD_DIM = 64
A_HEADS = 8
A_KV_HEADS = 2
A_REP = A_HEADS // A_KV_HEADS
Q_RANK = 256
IDX_HEADS = 8
IDX_DIM = 64
IDX_TOPK = 256
B_HEADS = 8
A_WIDTH = A_HEADS * HEAD_DIM
B_WIDTH = B_HEADS * HEAD_DIM
SSM_GROUP = 16
SSM_GROUPS = D_MODEL // SSM_GROUP
SSM_STATE = 64
XA_HEADS = 4
XA_HEAD_DIM = D_MODEL // XA_HEADS
N_EXPERTS = 32
TOP_K = 4
D_EXPERT = D_MODEL
SWIGLU_LIMIT = 7.0
SWIGLU_ALPHA = 1.702
ROPE_THETA = 500000.0
ROPE_HALF = HEAD_DIM // 8
LN_EPS = 1e-5
DN_ALPHA = (2 * DEPTH) ** 0.25

LANES = 128
SUBLANES = 8
VMEM_LIMIT_BYTES = 56 * 1024 * 1024

Q_BLOCK = 128
DSA_KEY_TILE = 512
DSA_ATT_TILE = 256
DSA_SHIFT_LANE = HEAD_DIM + 1
SB_KEY_TILE = 256
ROW_TILE = 256
MOE_BLOCK_ROWS = 256
SC_GATHER_SLOT_BYTES = 128 * 1024
S5_CHUNK = 128
S5_LANE_BLOCKS = 4
S5_IN_BLK = D_MODEL // S5_LANE_BLOCKS
S5_ST_BLK = SSM_GROUPS * SSM_STATE // S5_LANE_BLOCKS

SB_EXIT_LOG = -104.0
NEG_BIG = -1e30
INT_MIN = -(2 ** 31)


def _cparams(*sem):
    return pltpu.CompilerParams(dimension_semantics=sem, vmem_limit_bytes=VMEM_LIMIT_BYTES)


def _dot(a, b):
    return jnp.dot(a, b, preferred_element_type=F32)


def _dot_nt(a, b):
    return lax.dot_general(a, b, (((1,), (1,)), ((), ())), preferred_element_type=F32)


def _layer_norm_rows(y, g, b):
    mu = jnp.mean(y, axis=-1, keepdims=True)
    d = y - mu
    var = jnp.mean(d * d, axis=-1, keepdims=True)
    return d * lax.rsqrt(var + LN_EPS) * g + b


def _mm_kernel(x_ref, w_ref, o_ref):
    o_ref[...] = _dot(x_ref[...].astype(BF16), w_ref[...]).astype(o_ref.dtype)


def _matmul(x, w, *, tm, out_dtype, x_map=None, out_map=None, grid=None, out_shape=None):
    m, k = x.shape
    n = w.shape[1]
    grid = grid or (m // tm,)
    x_map = x_map or (lambda i: (i, 0))
    out_map = out_map or (lambda i: (i, 0))
    out_shape = out_shape or (m, n)
    return pl.pallas_call(
        _mm_kernel,
        grid=grid,
        in_specs=[pl.BlockSpec((tm, k), x_map), pl.BlockSpec((k, n), lambda *a: (0, 0))],
        out_specs=pl.BlockSpec((tm, n), out_map),
        out_shape=jax.ShapeDtypeStruct(out_shape, out_dtype),
        compiler_params=_cparams(*(("parallel",) * len(grid))),
        name="matmul",
    )(x, w)


def _lin_ln_kernel(*refs, n_in, glu):
    xs, ws = refs[:n_in], refs[n_in:2 * n_in]
    res_ref, g_ref, b_ref, o_ref = refs[2 * n_in:]
    acc = _dot(xs[0][...].astype(BF16), ws[0][...])
    for x_ref, w_ref in zip(xs[1:], ws[1:]):
        acc = acc + _dot(x_ref[...].astype(BF16), w_ref[...])
    if glu:
        acc = acc[:, :D_MODEL] * jax.nn.sigmoid(acc[:, D_MODEL:])
    y = DN_ALPHA * res_ref[...] + acc
    o_ref[...] = _layer_norm_rows(y, g_ref[...], b_ref[...])


def _linear_residual_ln(xs, ws, res, g, b, *, tm, glu=False, grid=None, x_maps=None, res_map=None):
    n_rows = res.shape[0]
    grid = grid or (n_rows // tm,)
    x_maps = x_maps or [lambda i: (i, 0)] * len(xs)
    res_map = res_map or (lambda i: (i, 0))
    const = lambda *a: (0, 0)
    in_specs = [pl.BlockSpec((tm, w.shape[0]), m) for w, m in zip(ws, x_maps)]
    in_specs += [pl.BlockSpec(w.shape, const) for w in ws]
    in_specs += [pl.BlockSpec((tm, D_MODEL), res_map), pl.BlockSpec((1, D_MODEL), const),
                 pl.BlockSpec((1, D_MODEL), const)]
    return pl.pallas_call(
        functools.partial(_lin_ln_kernel, n_in=len(xs), glu=glu),
        grid=grid,
        in_specs=in_specs,
        out_specs=pl.BlockSpec((tm, D_MODEL), res_map),
        out_shape=jax.ShapeDtypeStruct((n_rows, D_MODEL), F32),
        compiler_params=_cparams(*(("parallel",) * len(grid))),
        name="linear_residual_ln",
    )(*xs, *ws, res, g.reshape(1, D_MODEL), b.reshape(1, D_MODEL))


_EV_CQ, _EV_KA, _EV_VA, _EV_KI, _EV_QB = 0, 256, 384, 512, 640
_EV_KB = _EV_QB + B_WIDTH
_EV_VB = _EV_KB + B_WIDTH
_EV_COLS = _EV_VB + B_WIDTH


def _rope_tables(seq):
    inv = ROPE_THETA ** (-jnp.arange(ROPE_HALF, dtype=F32) / ROPE_HALF)
    ang = jnp.arange(seq, dtype=F32)[:, None] * inv[None, :]
    cos, sin = jnp.cos(ang), jnp.sin(ang)
    rest = HEAD_DIM - 2 * ROPE_HALF
    zh = jnp.zeros((seq, ROPE_HALF), F32)
    c = jnp.concatenate([cos, cos, jnp.ones((seq, rest), F32)], axis=1)
    s1 = jnp.concatenate([-sin, zh, jnp.zeros((seq, rest), F32)], axis=1)
    s2 = jnp.concatenate([zh, sin, jnp.zeros((seq, rest), F32)], axis=1)
    rep = LANES // HEAD_DIM
    return jnp.tile(c, (1, rep)), jnp.tile(s1, (1, rep)), jnp.tile(s2, (1, rep))


def _even_proj_kernel(x_ref, w_ref, qg_ref, wuq_ref, wuqi_ref, lg_ref, lb_ref, c_ref, s1_ref, s2_ref,
                      qa_ref, qi_ref, ka_ref, va_ref, ki_ref, wi_ref, qb_ref, kb_ref, vb_ref):
    p = _dot(x_ref[...].astype(BF16), w_ref[...])
    c, s1, s2 = c_ref[...], s1_ref[...], s2_ref[...]

    def rope(t):
        return (t * c + pltpu.roll(t, LANES - ROPE_HALF, 1) * s1 + pltpu.roll(t, ROPE_HALF, 1) * s2)

    cq = p[:, _EV_CQ:_EV_CQ + Q_RANK]
    cn = cq * lax.rsqrt(jnp.mean(cq * cq, axis=-1, keepdims=True) + LN_EPS) * qg_ref[...]
    cnb = cn.astype(BF16)
    qa = _dot(cnb, wuq_ref[...])
    qi = _dot(cnb, wuqi_ref[...])
    low = lax.broadcasted_iota(I32, c.shape, 1) < HEAD_DIM
    for j in range(A_WIDTH // LANES):
        sl = slice(j * LANES, (j + 1) * LANES)
        pair = rope(qa[:, sl]) * (HEAD_DIM ** -0.5)
        for e, src in enumerate((pair, pltpu.roll(pair, HEAD_DIM, 1))):
            h = 2 * j + e
            qa_ref[:, h * LANES:(h + 1) * LANES] = jnp.where(low, src, 0.0).astype(BF16)
        qi_ref[:, sl] = (rope(qi[:, sl]) * (IDX_DIM ** -0.5)).astype(BF16)
    kpair = rope(p[:, _EV_KA:_EV_KA + LANES])
    vpair = p[:, _EV_VA:_EV_VA + LANES]
    lane_c = lax.broadcasted_iota(I32, c.shape, 1)
    k_pad = jnp.where(lane_c == DSA_SHIFT_LANE, 1.0, 0.0)
    v_pad = jnp.where(lane_c == HEAD_DIM, 1.0, 0.0)
    for g, (ks, vs) in enumerate(((kpair, vpair), (pltpu.roll(kpair, HEAD_DIM, 1), pltpu.roll(vpair, HEAD_DIM, 1)))):
        ka_ref[:, g * LANES:(g + 1) * LANES] = jnp.where(low, ks, k_pad).astype(BF16)
        va_ref[:, g * LANES:(g + 1) * LANES] = jnp.where(low, vs, v_pad).astype(BF16)

    t = p[:, _EV_KI:_EV_KI + LANES]
    lane = lax.broadcasted_iota(I32, t.shape, 1)
    is_k = lane < IDX_DIM
    mu = jnp.sum(jnp.where(is_k, t, 0.0), axis=-1, keepdims=True) * (1.0 / IDX_DIM)
    d = jnp.where(is_k, t - mu, 0.0)
    var = jnp.sum(d * d, axis=-1, keepdims=True) * (1.0 / IDX_DIM)
    kin = d * lax.rsqrt(var + LN_EPS) * lg_ref[...] + lb_ref[...]
    ki_ref[...] = rope(kin)[:, :IDX_DIM].astype(BF16)
    wi_ref[...] = t[:, IDX_DIM:IDX_DIM + IDX_HEADS] * (IDX_HEADS ** -0.5)

    qb_ref[...] = (p[:, _EV_QB:_EV_KB] * (HEAD_DIM ** -0.5)).astype(BF16)
    kb_ref[...] = p[:, _EV_KB:_EV_VB].astype(BF16)
    vb_ref[...] = p[:, _EV_VB:_EV_COLS].astype(BF16)


def _even_proj(x2d, bsz, seq, w_in, qnorm_g, w_uq, w_uq_idx, kidx_g, kidx_b):
    n = x2d.shape[0]
    tm = ROW_TILE
    per_seq = seq // tm
    c0 = Q_RANK + 2 * A_KV_HEADS * HEAD_DIM + IDX_DIM + IDX_HEADS
    w_pack = jnp.concatenate(
        [w_in[:, :c0], jnp.zeros((D_MODEL, _EV_QB - c0), w_in.dtype), w_in[:, c0:]], axis=1).astype(BF16)
    pad = LANES - IDX_DIM
    lg = jnp.concatenate([kidx_g, jnp.zeros((pad,), F32)]).reshape(1, LANES)
    lb = jnp.concatenate([kidx_b, jnp.zeros((pad,), F32)]).reshape(1, LANES)
    c, s1, s2 = _rope_tables(seq)
    row = lambda i: (i, 0)
    const = lambda i: (0, 0)
    pos = lambda i: (i % per_seq, 0)
    head_shape = jax.ShapeDtypeStruct((n, B_WIDTH), BF16)
    head_spec = pl.BlockSpec((tm, B_WIDTH), row)
    return pl.pallas_call(
        _even_proj_kernel,
        grid=(n // tm,),
        in_specs=[pl.BlockSpec((tm, D_MODEL), row), pl.BlockSpec((D_MODEL, _EV_COLS), const),
                  pl.BlockSpec((1, Q_RANK), const), pl.BlockSpec((Q_RANK, A_WIDTH), const),
                  pl.BlockSpec((Q_RANK, IDX_HEADS * IDX_DIM), const),
                  pl.BlockSpec((1, LANES), const), pl.BlockSpec((1, LANES), const),
                  pl.BlockSpec((tm, LANES), pos), pl.BlockSpec((tm, LANES), pos), pl.BlockSpec((tm, LANES), pos)],
        out_specs=[pl.BlockSpec((tm, A_HEADS * LANES), row), pl.BlockSpec((tm, IDX_HEADS * IDX_DIM), row),
                   pl.BlockSpec((tm, A_KV_HEADS * LANES), row), pl.BlockSpec((tm, A_KV_HEADS * LANES), row),
                   pl.BlockSpec((tm, IDX_DIM), row), pl.BlockSpec((tm, IDX_HEADS), row),
                   head_spec, head_spec, head_spec],
        out_shape=[jax.ShapeDtypeStruct((n, A_HEADS * LANES), BF16), jax.ShapeDtypeStruct((n, IDX_HEADS * IDX_DIM), BF16),
                   jax.ShapeDtypeStruct((n, A_KV_HEADS * LANES), BF16),
                   jax.ShapeDtypeStruct((n, A_KV_HEADS * LANES), BF16),
                   jax.ShapeDtypeStruct((n, IDX_DIM), BF16), jax.ShapeDtypeStruct((n, IDX_HEADS), F32),
                   head_shape, head_shape, head_shape],
        compiler_params=_cparams("parallel"),
        name="even_proj",
    )(x2d, w_pack, qnorm_g.reshape(1, Q_RANK), w_uq.astype(BF16), w_uq_idx.astype(BF16), lg, lb, c, s1, s2)


def _key_to_float(key):
    bits = key ^ ((key >> 31) & jnp.int32(0x7FFFFFFF))
    return lax.bitcast_convert_type(bits, F32)


def _dsa_kernel(qa_ref, qi_ref, wi_ref, ki_ref, ka_ref, va_ref, o_ref, sc_scr, *, topk, ts, ta):
    qb = pl.program_id(1)
    q0 = qb * Q_BLOCK
    nkt = (q0 + Q_BLOCK - 1) // ts + 1
    t_col = q0 + lax.broadcasted_iota(I32, (Q_BLOCK, 1), 0)
    lane = lax.broadcasted_iota(I32, (Q_BLOCK, ts), 1)
    kf = jnp.float32(topk)

    qi = qi_ref[0]
    wi = wi_ref[0]
    qih = [qi[:, h * IDX_DIM:(h + 1) * IDX_DIM] for h in range(IDX_HEADS)]
    wih = [wi[:, h:h + 1] for h in range(IDX_HEADS)]

    def score_tile(kt, carry):
        off = pl.multiple_of(kt * ts, ts)
        kit = ki_ref[0, pl.ds(off, ts), :]
        acc = jnp.zeros((Q_BLOCK, ts), F32)
        for h in range(IDX_HEADS):
            acc = acc + jnp.maximum(_dot_nt(qih[h], kit), 0.0) * wih[h]
        sc_scr[:, pl.ds(off, ts)] = jnp.where(off + lane <= t_col, acc, -jnp.inf)
        return carry

    lax.fori_loop(0, nkt, score_tile, 0)

    def count(pred):
        def body(kt, acc):
            off = pl.multiple_of(kt * ts, ts)
            ind = pred(sc_scr[:, pl.ds(off, ts)], off + lane)
            for j in range(ts // LANES):
                acc = acc + ind[:, j * LANES:(j + 1) * LANES]
            return acc
        acc = lax.fori_loop(0, nkt, body, jnp.zeros((Q_BLOCK, LANES), F32))
        return jnp.sum(acc, axis=1, keepdims=True)

    def bit_step(i, base):
        cand = base + jnp.left_shift(jnp.int32(1), 31 - i)
        cf = _key_to_float(cand)
        cnt = count(lambda sc, idx: jnp.where(sc >= cf, 1.0, 0.0))
        return jnp.where(cnt >= kf, cand, base)

    base = lax.fori_loop(0, 32, bit_step, jnp.full((Q_BLOCK, 1), INT_MIN, I32))
    thr = jnp.where(base == INT_MIN, -jnp.inf, _key_to_float(base))

    cnt_ge = count(lambda sc, idx: jnp.where(sc >= thr, 1.0, 0.0))
    tied = jnp.logical_and(cnt_ge > kf, thr > -jnp.inf)
    any_tied = jnp.max(jnp.where(tied, 1.0, 0.0)) > 0.0
    seq_bits = max(1, int(math.ceil(math.log2(sc_scr.shape[1]))))

    def tie_cut():
        cnt_gt = count(lambda sc, idx: jnp.where(sc > thr, 1.0, 0.0))
        need = kf - cnt_gt

        def idx_step(i, pos):
            cand = pos + jnp.left_shift(jnp.int32(1), seq_bits - 1 - i)
            cnt = count(lambda sc, idx: jnp.where(sc == thr, jnp.where(idx < cand, 1.0, 0.0), 0.0))
            return jnp.where(cnt < need, cand, pos)

        return lax.fori_loop(0, seq_bits, idx_step, jnp.zeros((Q_BLOCK, 1), I32))

    cut = lax.cond(any_tied, tie_cut, lambda: jnp.full((Q_BLOCK, 1), sc_scr.shape[1], I32))
    cut = jnp.where(tied, cut, sc_scr.shape[1])

    nkt_a = (q0 + Q_BLOCK - 1) // ta + 1
    lane_a = lax.broadcasted_iota(I32, (Q_BLOCK, ta), 1)
    rows = A_REP * Q_BLOCK
    qg = [jnp.concatenate([qa_ref[0, :, (g * A_REP + r) * LANES:(g * A_REP + r + 1) * LANES]
                           for r in range(A_REP)], axis=0) for g in range(A_KV_HEADS)]

    def max_tile(kt, carry):
        off = pl.multiple_of(kt * ta, ta)
        sc = sc_scr[:, pl.ds(off, ta)]
        idx = off + lane_a
        keep = jnp.where(sc > thr, 0.0, jnp.where(sc == thr, jnp.where(idx <= cut, 0.0, NEG_BIG), NEG_BIG))
        bias = jnp.where(idx <= t_col, keep, NEG_BIG)
        sc_scr[:, pl.ds(off, ta)] = bias
        bias = jnp.concatenate([bias] * A_REP, axis=0)
        out = []
        for g in range(A_KV_HEADS):
            s = _dot_nt(qg[g], ka_ref[0, pl.ds(off, ta), g * LANES:(g + 1) * LANES]) + bias
            mm = carry[g]
            for j in range(ta // LANES):
                mm = jnp.maximum(mm, s[:, j * LANES:(j + 1) * LANES])
            out.append(mm)
        return tuple(out)

    lane_max = lax.fori_loop(0, nkt_a, max_tile,
                             tuple(jnp.full((rows, LANES), NEG_BIG, F32) for _ in range(A_KV_HEADS)))
    shift_lane = lax.broadcasted_iota(I32, (rows, LANES), 1) == DSA_SHIFT_LANE
    q_shift = []
    for g in range(A_KV_HEADS):
        row_max = jnp.max(lane_max[g], axis=1, keepdims=True)
        q_shift.append(jnp.where(shift_lane, -row_max, qg[g].astype(F32)).astype(BF16))

    def att_tile(kt, carry):
        off = pl.multiple_of(kt * ta, ta)
        bias = jnp.concatenate([sc_scr[:, pl.ds(off, ta)]] * A_REP, axis=0)
        out = []
        for g in range(A_KV_HEADS):
            kg = ka_ref[0, pl.ds(off, ta), g * LANES:(g + 1) * LANES]
            vg = va_ref[0, pl.ds(off, ta), g * LANES:(g + 1) * LANES]
            p = jnp.exp(_dot_nt(q_shift[g], kg) + bias)
            out.append(carry[g] + _dot(p.astype(BF16), vg))
        return tuple(out)

    final = lax.fori_loop(0, nkt_a, att_tile, tuple(jnp.zeros((rows, LANES), F32) for _ in range(A_KV_HEADS)))
    low = lax.broadcasted_iota(I32, (Q_BLOCK, LANES), 1) < HEAD_DIM
    outs = []
    for g in range(A_KV_HEADS):
        acc = final[g]
        og = acc / acc[:, HEAD_DIM:HEAD_DIM + 1]
        outs += [og[r * Q_BLOCK:(r + 1) * Q_BLOCK] for r in range(A_REP)]
    for j in range(A_HEADS // 2):
        pair = jnp.where(low, outs[2 * j], pltpu.roll(outs[2 * j + 1], HEAD_DIM, 1))
        o_ref[0, :, j * LANES:(j + 1) * LANES] = pair.astype(BF16)


def _dsa_attention(qa, qi, wi, ki, ka, va, bsz, seq):
    topk = min(IDX_TOPK, seq // 4)
    ts = min(DSA_KEY_TILE, seq)
    ta = min(DSA_ATT_TILE, seq)
    blk = lambda b, i: (b, i, 0)
    full = lambda b, i: (b, 0, 0)
    r3 = lambda a: a.reshape(bsz, seq, a.shape[-1])
    return pl.pallas_call(
        functools.partial(_dsa_kernel, topk=topk, ts=ts, ta=ta),
        grid=(bsz, seq // Q_BLOCK),
        in_specs=[pl.BlockSpec((1, Q_BLOCK, A_HEADS * LANES), blk),
                  pl.BlockSpec((1, Q_BLOCK, IDX_HEADS * IDX_DIM), blk),
                  pl.BlockSpec((1, Q_BLOCK, IDX_HEADS), blk), pl.BlockSpec((1, seq, IDX_DIM), full),
                  pl.BlockSpec((1, seq, A_KV_HEADS * LANES), full), pl.BlockSpec((1, seq, A_KV_HEADS * LANES), full)],
        out_specs=pl.BlockSpec((1, Q_BLOCK, A_WIDTH), blk),
        out_shape=jax.ShapeDtypeStruct((bsz, seq, A_WIDTH), BF16),
        scratch_shapes=[pltpu.VMEM((Q_BLOCK, seq), F32)],
        compiler_params=_cparams("parallel", "parallel"),
        name="dsa_attention",
    )(r3(qa), r3(qi), r3(wi), r3(ki), r3(ka), r3(va)).reshape(bsz * seq, A_WIDTH)


def _sb_kernel(q_ref, k_ref, v_ref, u_ref, o_ref, acc_scr, run_scr, *, tk):
    q0 = pl.program_id(1) * Q_BLOCK
    t_col = q0 + lax.broadcasted_iota(I32, (Q_BLOCK, 1), 0)
    lane = lax.broadcasted_iota(I32, (Q_BLOCK, tk), 1)
    low = lax.broadcasted_iota(I32, (Q_BLOCK, LANES), 1) < HEAD_DIM
    upper = u_ref[...]
    nkt = (q0 + Q_BLOCK - 1) // tk + 1
    q = q_ref[0]
    zero = jnp.zeros((Q_BLOCK, LANES), BF16)
    qm = []
    for h in range(B_HEADS):
        pair = q[:, (h // 2) * LANES:(h // 2 + 1) * LANES]
        qm.append(jnp.where(low, pair, zero) if h % 2 == 0 else jnp.where(low, zero, pair))
    acc_scr[...] = jnp.zeros_like(acc_scr)
    run_scr[...] = jnp.zeros_like(run_scr)

    def cond(carry):
        i, worst = carry
        return jnp.logical_and(i < nkt, worst >= SB_EXIT_LOG)

    def body(carry):
        i, _ = carry
        off = pl.multiple_of((nkt - 1 - i) * tk, tk)
        strict = off + lane < t_col
        worst = None
        for p in range(B_HEADS // 2):
            cols = slice(p * LANES, (p + 1) * LANES)
            kp = k_ref[0, pl.ds(off, tk), cols]
            vp = v_ref[0, pl.ds(off, tk), cols]
            outs = []
            for e in range(2):
                h = 2 * p + e
                run = run_scr[h]
                z = _dot_nt(qm[h], kp)
                softplus = jnp.maximum(z, 0.0) + jnp.log(1.0 + jnp.exp(-jnp.abs(z)))
                log_1mb = jnp.where(strict, -softplus, 0.0)
                hi = log_1mb.astype(BF16)
                lo = (log_1mb - hi.astype(F32)).astype(BF16)
                after = _dot(hi, upper) + _dot(lo, upper) + run
                a = jnp.where(strict, jnp.exp(z - softplus + after), 0.0)
                outs.append(_dot(a.astype(BF16), vp))
                run = run + jnp.sum(log_1mb, axis=1, keepdims=True)
                run_scr[h] = run
                worst = run if worst is None else jnp.maximum(worst, run)
            acc_scr[:, cols] += jnp.where(low, outs[0], outs[1])
        return i + 1, jnp.max(worst)

    lax.while_loop(cond, body, (jnp.int32(0), jnp.float32(0.0)))
    o_ref[0] = acc_scr[...].astype(BF16)


def _stick_breaking(qb, kb, vb, bsz, seq):
    tk = min(SB_KEY_TILE, seq)
    r = lax.broadcasted_iota(I32, (tk, tk), 0)
    c = lax.broadcasted_iota(I32, (tk, tk), 1)
    upper = jnp.where(r > c, 1.0, 0.0).astype(BF16)
    blk = lambda b, i: (b, i, 0)
    full = lambda b, i: (b, 0, 0)
    r3 = lambda a: a.reshape(bsz, seq, B_WIDTH)
    return pl.pallas_call(
        functools.partial(_sb_kernel, tk=tk),
        grid=(bsz, seq // Q_BLOCK),
        in_specs=[pl.BlockSpec((1, Q_BLOCK, B_WIDTH), blk), pl.BlockSpec((1, seq, B_WIDTH), full),
                  pl.BlockSpec((1, seq, B_WIDTH), full), pl.BlockSpec((tk, tk), lambda b, i: (0, 0))],
        out_specs=pl.BlockSpec((1, Q_BLOCK, B_WIDTH), blk),
        out_shape=jax.ShapeDtypeStruct((bsz, seq, B_WIDTH), BF16),
        scratch_shapes=[pltpu.VMEM((Q_BLOCK, B_WIDTH), F32), pltpu.VMEM((B_HEADS, Q_BLOCK, 1), F32)],
        compiler_params=_cparams("parallel", "arbitrary"),
        name="stick_breaking",
    )(r3(qb), r3(kb), r3(vb), upper).reshape(bsz * seq, B_WIDTH)


HALF_D = D_MODEL // 2
U32 = jnp.uint32
HIGH16 = 0xFFFF0000


def _pack_bf16_pairs(x):
    def bits(v):
        return lax.bitcast_convert_type(v.astype(BF16).astype(F32), U32)
    word = (bits(x[:, HALF_D:]) & U32(HIGH16)) | (bits(x[:, :HALF_D]) >> 16)
    return lax.bitcast_convert_type(word, I32)


def _unpack_bf16_pairs(word):
    u = lax.bitcast_convert_type(word, U32)
    lo = lax.bitcast_convert_type(u << 16, F32)
    hi = lax.bitcast_convert_type(u & U32(HIGH16), F32)
    return lo.astype(BF16), hi.astype(BF16)


def _xattn_kernel(h_ref, wq_ref, kv_ref, wo_ref, g_ref, b_ref, o_ref, packed_ref):
    h = h_ref[...]
    q = (_dot(h.astype(BF16), wq_ref[...]) * (XA_HEAD_DIM ** -0.5)).astype(BF16)
    kv = kv_ref[0]
    outs = []
    for hd in range(XA_HEADS):
        sl = slice(hd * XA_HEAD_DIM, (hd + 1) * XA_HEAD_DIM)
        s = _dot_nt(q[:, sl], kv[:, sl])
        p = jnp.exp(s - jnp.max(s, axis=1, keepdims=True))
        vh = kv[:, D_MODEL + hd * XA_HEAD_DIM:D_MODEL + (hd + 1) * XA_HEAD_DIM]
        outs.append((_dot(p.astype(BF16), vh) / jnp.sum(p, axis=1, keepdims=True)).astype(BF16))
    y = _dot(jnp.concatenate(outs, axis=1), wo_ref[...])
    out = _layer_norm_rows(DN_ALPHA * h + y, g_ref[...], b_ref[...])
    o_ref[...] = out
    packed_ref[...] = _pack_bf16_pairs(out)


def _cross_attention_block(h2d, mem, bsz, seq, w_q, w_kv, w_o, g, b):
    tm = ROW_TILE
    per_seq = seq // tm
    mem_len = mem.shape[1]
    kv = _matmul(mem.reshape(bsz * mem_len, D_MODEL), w_kv.astype(BF16), tm=mem_len, out_dtype=BF16)
    kv = kv.reshape(bsz, mem_len, 2 * D_MODEL)
    row = lambda i: (i, 0)
    const = lambda i: (0, 0)
    return pl.pallas_call(
        _xattn_kernel,
        grid=(bsz * per_seq,),
        in_specs=[pl.BlockSpec((tm, D_MODEL), row), pl.BlockSpec((D_MODEL, D_MODEL), const),
                  pl.BlockSpec((1, mem_len, 2 * D_MODEL), lambda i: (i // per_seq, 0, 0)),
                  pl.BlockSpec((D_MODEL, D_MODEL), const),
                  pl.BlockSpec((1, D_MODEL), const), pl.BlockSpec((1, D_MODEL), const)],
        out_specs=[pl.BlockSpec((tm, D_MODEL), row), pl.BlockSpec((tm, HALF_D), row)],
        out_shape=[jax.ShapeDtypeStruct(h2d.shape, F32), jax.ShapeDtypeStruct((h2d.shape[0], HALF_D), I32)],
        compiler_params=_cparams("parallel"),
        name="cross_attention",
    )(h2d, w_q.astype(BF16), kv, w_o.astype(BF16), g.reshape(1, D_MODEL), b.reshape(1, D_MODEL))


def _router_kernel(h_ref, w_ref, b_ref, tri_ref, idx_ref, gate_ref, rank_ref, cnt_ref, run_scr):
    @pl.when(pl.program_id(0) == 0)
    def _():
        run_scr[...] = jnp.zeros_like(run_scr)

    h = h_ref[...]
    hh = h.astype(BF16)
    hl = (h - hh.astype(F32)).astype(BF16)
    w = w_ref[...]
    wh = w.astype(BF16)
    wl = (w - wh.astype(F32)).astype(BF16)
    logits = _dot(hh, wh) + _dot(hl, wh) + _dot(hh, wl) + b_ref[...]
    lane = lax.broadcasted_iota(I32, logits.shape, 1).astype(F32)
    vals, sels = [], []
    onehot = jnp.zeros(logits.shape, F32)
    for k in range(TOP_K):
        m = jnp.max(logits, axis=1, keepdims=True)
        sel = jnp.min(jnp.where(logits == m, lane, float(LANES)), axis=1, keepdims=True)
        idx_ref[:, k:k + 1] = sel.astype(I32)
        vals.append(m)
        sels.append(sel)
        onehot = onehot + jnp.where(lane == sel, 1.0, 0.0)
        logits = jnp.where(lane == sel, -jnp.inf, logits)
    es = [jnp.exp(v - vals[0]) for v in vals]
    tot = es[0] + es[1] + es[2] + es[3]
    for k in range(TOP_K):
        gate_ref[:, k:k + 1] = es[k] / tot

    earlier = _dot(tri_ref[...], onehot.astype(BF16)) + run_scr[...]
    for k in range(TOP_K):
        rank = jnp.sum(jnp.where(lane == sels[k], earlier, 0.0), axis=1, keepdims=True)
        rank_ref[:, k:k + 1] = rank.astype(I32)
    run = run_scr[...] + jnp.sum(onehot, axis=0, keepdims=True)
    run_scr[...] = run
    cnt_ref[...] = run


def _router(h2d, w_router, b_router):
    n = h2d.shape[0]
    tm = 2 * ROW_TILE
    pad = LANES - N_EXPERTS
    w = jnp.concatenate([w_router, jnp.zeros((D_MODEL, pad), F32)], axis=1)
    b = jnp.concatenate([b_router, jnp.full((pad,), NEG_BIG, F32)]).reshape(1, LANES)
    r = lax.broadcasted_iota(I32, (tm, tm), 0)
    c = lax.broadcasted_iota(I32, (tm, tm), 1)
    tri = jnp.where(c < r, 1.0, 0.0).astype(BF16)
    row = lambda i: (i, 0)
    const = lambda i: (0, 0)
    return pl.pallas_call(
        _router_kernel,
        grid=(n // tm,),
        in_specs=[pl.BlockSpec((tm, D_MODEL), row), pl.BlockSpec((D_MODEL, LANES), const),
                  pl.BlockSpec((1, LANES), const), pl.BlockSpec((tm, tm), const)],
        out_specs=[pl.BlockSpec((tm, TOP_K), row), pl.BlockSpec((tm, TOP_K), row),
                   pl.BlockSpec((tm, TOP_K), row), pl.BlockSpec((1, LANES), const)],
        out_shape=[jax.ShapeDtypeStruct((n, TOP_K), I32), jax.ShapeDtypeStruct((n, TOP_K), F32),
                   jax.ShapeDtypeStruct((n, TOP_K), I32), jax.ShapeDtypeStruct((1, LANES), F32)],
        scratch_shapes=[pltpu.VMEM((1, LANES), F32)],
        compiler_params=_cparams("arbitrary"),
        name="moe_router",
    )(h2d, w, b, tri)


def _gather_rows(src, idx):
    n_out = idx.shape[0]
    width = src.shape[1]
    win = SC_GATHER_SLOT_BYTES // (width * src.dtype.itemsize)
    mesh = plsc.VectorSubcoreMesh(core_axis_name="core", subcore_axis_name="subcore")
    n_workers = mesh.num_cores * mesh.num_subcores
    per_worker = n_out // n_workers
    steps = per_worker // win
    assert per_worker * n_workers == n_out and steps * win == per_worker and steps % 2 == 0

    @functools.partial(
        pl.kernel, out_type=jax.ShapeDtypeStruct((n_out, width), src.dtype), mesh=mesh,
        scratch_types=[pltpu.VMEM((per_worker,), I32), pltpu.VMEM((2, win, width), src.dtype),
                       pltpu.SemaphoreType.DMA, pltpu.SemaphoreType.DMA])
    def gather_kernel(src_hbm, idx_hbm, dst_hbm, idx_v, rows_v, sem0, sem1):
        worker = lax.axis_index("subcore") * mesh.num_cores + lax.axis_index("core")
        base = worker * per_worker
        sems = (sem0, sem1)
        pltpu.sync_copy(idx_hbm.at[pl.ds(base, per_worker)], idx_v)

        def gather(step, slot):
            return pltpu.make_async_copy(src_hbm.at[idx_v.at[pl.ds(step * win, win)]], rows_v.at[slot], sems[slot])

        gather(0, 0).start()

        @pl.loop(0, steps, step=2)
        def _(s):
            for slot in range(2):
                step = s + slot
                gather(step, slot).wait()

                @pl.when(step + 1 < steps)
                def _():
                    gather(step + 1, 1 - slot).start()

                pltpu.sync_copy(rows_v.at[slot], dst_hbm.at[pl.ds(base + step * win, win)])

    return gather_kernel(src, idx)


def _expert_kernel(blk_exp_ref, n_used_ref, x_ref, wgu_ref, bgu_ref, wd_ref, bd_ref, o_ref, wgu_bf, wd_bf):
    i = pl.program_id(0)

    @pl.when(jnp.logical_or(i == 0, blk_exp_ref[i] != blk_exp_ref[jnp.maximum(i - 1, 0)]))
    def _():
        wgu_bf[...] = wgu_ref[0, 0].astype(BF16)
        wd_bf[...] = wd_ref[0, 0].astype(BF16)

    @pl.when(i < n_used_ref[0])
    def _():
        x_lo, x_hi = _unpack_bf16_pairs(x_ref[...])
        hgu = _dot(x_lo, wgu_bf[:HALF_D, :]) + _dot(x_hi, wgu_bf[HALF_D:, :]) + bgu_ref[0]
        gate = jnp.minimum(hgu[:, :D_EXPERT], SWIGLU_LIMIT)
        up = jnp.clip(hgu[:, D_EXPERT:], -SWIGLU_LIMIT, SWIGLU_LIMIT)
        act = gate * jax.nn.sigmoid(gate * SWIGLU_ALPHA) * (up + 1.0)
        o_ref[...] = _dot(act.astype(BF16), wd_bf[...]) + bd_ref[0]

    @pl.when(i >= n_used_ref[0])
    def _():
        o_ref[...] = jnp.zeros_like(o_ref)


def _expert_mlp(xs, block_exp, n_used, layer, w_gu, b_gu, w_down, b_down):
    n_rows = xs.shape[0]
    bm = MOE_BLOCK_ROWS
    row = lambda i, be, nu: (i, 0)
    exp3 = lambda i, be, nu: (be[i], 0, 0)
    exp4 = lambda i, be, nu: (layer, be[i], 0, 0)
    grid_spec = pltpu.PrefetchScalarGridSpec(
        num_scalar_prefetch=2,
        grid=(n_rows // bm,),
        in_specs=[pl.BlockSpec((bm, HALF_D), row),
                  pl.BlockSpec((1, 1, D_MODEL, 2 * D_EXPERT), exp4), pl.BlockSpec((1, 1, 2 * D_EXPERT), exp3),
                  pl.BlockSpec((1, 1, D_EXPERT, D_MODEL), exp4), pl.BlockSpec((1, 1, D_MODEL), exp3)],
        out_specs=pl.BlockSpec((bm, D_MODEL), row),
        scratch_shapes=[pltpu.VMEM((D_MODEL, 2 * D_EXPERT), BF16), pltpu.VMEM((D_EXPERT, D_MODEL), BF16)],
    )
    return pl.pallas_call(
        _expert_kernel,
        grid_spec=grid_spec,
        out_shape=jax.ShapeDtypeStruct((n_rows, D_MODEL), F32),
        compiler_params=_cparams("arbitrary"),
        name="moe_experts",
    )(block_exp, n_used, xs, w_gu, b_gu.reshape(N_EXPERTS, 1, 2 * D_EXPERT),
      w_down, b_down.reshape(N_EXPERTS, 1, D_MODEL))


def _combine_kernel(y0_ref, y1_ref, y2_ref, y3_ref, gate_ref, res_ref, g_ref, b_ref, o_ref):
    gates = gate_ref[...]
    acc = y0_ref[...] * gates[:, 0:1]
    for k, y_ref in enumerate((y1_ref, y2_ref, y3_ref), start=1):
        acc = acc + y_ref[...] * gates[:, k:k + 1]
    o_ref[...] = _layer_norm_rows(DN_ALPHA * res_ref[...] + acc, g_ref[...], b_ref[...])


def _moe_block(h2d, h_packed, layer, w_router, b_router, w_gu, b_gu, w_down, b_down, g, b):
    n = h2d.shape[0]
    n_slots = n * TOP_K
    bm = MOE_BLOCK_ROWS
    top_idx, gates, rank, totals = _router(h2d, w_router, b_router)

    e_flat = top_idx.reshape(-1)
    order = jnp.argsort(e_flat).astype(I32)
    counts = totals[0, :N_EXPERTS].astype(I32)
    padded = (counts + bm - 1) // bm * bm
    start = jnp.cumsum(counts) - counts
    ends_p = jnp.cumsum(padded)
    pstart = ends_p - padded
    n_rows = n_slots + N_EXPERTS * bm
    n_blocks = n_rows // bm
    r = jnp.arange(n_rows, dtype=I32)
    e_r = jnp.minimum(jnp.searchsorted(ends_p, r, side="right"), N_EXPERTS - 1).astype(I32)
    j = r - pstart[e_r]
    valid = j < counts[e_r]
    slot_of_row = order[jnp.where(valid, start[e_r] + j, 0)]
    rows_tok = jnp.where(valid, slot_of_row // TOP_K, 0).astype(I32)
    slot_pos = pstart[e_flat] + rank.reshape(-1)
    block_exp = e_r[::bm]
    n_used = (ends_p[-1] // bm).astype(I32).reshape(1)

    xs = _gather_rows(h_packed, rows_tok)
    ys = _expert_mlp(xs, block_exp, n_used, layer, w_gu, b_gu, w_down, b_down)
    yk = _gather_rows(ys, slot_pos.reshape(n, TOP_K).T.reshape(-1))

    tm = ROW_TILE
    row = lambda i: (i, 0)
    const = lambda i: (0, 0)
    choice = lambda k: (lambda i: (k * (n // tm) + i, 0))
    return pl.pallas_call(
        _combine_kernel,
        grid=(n // tm,),
        in_specs=[pl.BlockSpec((tm, D_MODEL), choice(k)) for k in range(TOP_K)] + [
                  pl.BlockSpec((tm, TOP_K), row),
                  pl.BlockSpec((tm, D_MODEL), row), pl.BlockSpec((1, D_MODEL), const),
                  pl.BlockSpec((1, D_MODEL), const)],
        out_specs=pl.BlockSpec((tm, D_MODEL), row),
        out_shape=jax.ShapeDtypeStruct((n, D_MODEL), F32),
        compiler_params=_cparams("parallel"),
        name="moe_combine",
    )(yk, yk, yk, yk, gates, h2d, g.reshape(1, D_MODEL), b.reshape(1, D_MODEL))


def _s5_kernel(u_ref, bre_ref, bim_ref, cre_ref, cim_ref, are_ref, aim_ref, d_ref, y_ref,
               bu_re, bu_im, st_re, st_im, h_re, h_im, *, bsz):
    @pl.when(pl.program_id(0) == 0)
    def _():
        h_re[...] = jnp.zeros_like(h_re)
        h_im[...] = jnp.zeros_like(h_im)

    rows = u_ref.shape[0]
    first = lax.broadcasted_iota(I32, (SUBLANES, S5_ST_BLK), 0) < bsz
    for j in range(S5_LANE_BLOCKS):
        cin = slice(j * S5_IN_BLK, (j + 1) * S5_IN_BLK)
        cst = slice(j * S5_ST_BLK, (j + 1) * S5_ST_BLK)
        uj = u_ref[:, cin]
        ujb = uj.astype(BF16)
        bu_re[...] = _dot(ujb, bre_ref[j])
        bu_im[...] = _dot(ujb, bim_ref[j])
        ar = jnp.broadcast_to(are_ref[:, cst], (SUBLANES, S5_ST_BLK))
        ai = jnp.broadcast_to(aim_ref[:, cst], (SUBLANES, S5_ST_BLK))

        def step(i, carry):
            hr, hi = carry
            r0 = pl.multiple_of(i * SUBLANES, SUBLANES)
            vr = bu_re[pl.ds(r0, SUBLANES), :]
            vi = bu_im[pl.ds(r0, SUBLANES), :]
            h1r = ar * hr - ai * hi + vr
            h1i = ar * hi + ai * hr + vi
            h1rs = pltpu.roll(h1r, bsz, 0)
            h1is = pltpu.roll(h1i, bsz, 0)
            h2r = ar * h1rs - ai * h1is + vr
            h2i = ar * h1is + ai * h1rs + vi
            st_re[pl.ds(r0, SUBLANES), :] = jnp.where(first, h1r, h2r)
            st_im[pl.ds(r0, SUBLANES), :] = jnp.where(first, h1i, h2i)
            return pltpu.roll(h2r, bsz, 0), pltpu.roll(h2i, bsz, 0)

        hr, hi = lax.fori_loop(0, rows // SUBLANES, step, (h_re[:, cst], h_im[:, cst]))
        h_re[:, cst] = hr
        h_im[:, cst] = hi
        yj = _dot(st_re[...].astype(BF16), cre_ref[j]) + _dot(st_im[...].astype(BF16), cim_ref[j])
        yj = yj + d_ref[:, cin] * uj
        y_ref[:, cin] = jax.nn.gelu(yj).astype(BF16)


def _s5_block_diag(w, n_in, n_out):
    gpb = SSM_GROUPS // S5_LANE_BLOCKS
    w4 = w.reshape(S5_LANE_BLOCKS, gpb, n_in, n_out)
    eye = jnp.eye(gpb, dtype=w.dtype)
    return jnp.einsum("jgio,gh->jgiho", w4, eye).reshape(S5_LANE_BLOCKS, gpb * n_in, gpb * n_out)


def _s5_mixer_block(h2d, bsz, seq, w_in, log_dt, lam_re, lam_im, b_re, b_im, c_re, c_im, d, w_out, g, b):
    assert 2 * bsz == SUBLANES, "the scan packs two time steps of bsz rows into one 8-row tile"
    tm = ROW_TILE
    per_seq = seq // tm
    u_t = _matmul(h2d, w_in.astype(BF16), tm=tm, out_dtype=F32, grid=(bsz, per_seq),
                  x_map=lambda bb, i: (bb * per_seq + i, 0), out_map=lambda bb, i: (i, bb),
                  out_shape=(seq, bsz * D_MODEL)).reshape(seq * bsz, D_MODEL)

    dt = jnp.exp(log_dt)[:, None]
    mag = jnp.exp(lam_re * dt)
    a_re, a_im = mag * jnp.cos(lam_im * dt), mag * jnp.sin(lam_im * dt)
    den = lam_re * lam_re + lam_im * lam_im
    coef_re = ((a_re - 1.0) * lam_re + a_im * lam_im) / den
    coef_im = (a_im * lam_re - (a_re - 1.0) * lam_im) / den
    bb_re = coef_re[..., None] * b_re - coef_im[..., None] * b_im
    bb_im = coef_re[..., None] * b_im + coef_im[..., None] * b_re
    bre = _s5_block_diag(jnp.swapaxes(bb_re, 1, 2), SSM_GROUP, SSM_STATE).astype(BF16)
    bim = _s5_block_diag(jnp.swapaxes(bb_im, 1, 2), SSM_GROUP, SSM_STATE).astype(BF16)
    cre = _s5_block_diag(jnp.swapaxes(c_re, 1, 2), SSM_STATE, SSM_GROUP).astype(BF16)
    cim = _s5_block_diag(jnp.swapaxes(-c_im, 1, 2), SSM_STATE, SSM_GROUP).astype(BF16)
    n_state = SSM_GROUPS * SSM_STATE

    rows = S5_CHUNK * bsz
    row = lambda c: (c, 0)
    c2 = lambda c: (0, 0)
    c3 = lambda c: (0, 0, 0)
    y_t = pl.pallas_call(
        functools.partial(_s5_kernel, bsz=bsz),
        grid=(seq // S5_CHUNK,),
        in_specs=[pl.BlockSpec((rows, D_MODEL), row),
                  pl.BlockSpec(bre.shape, c3), pl.BlockSpec(bim.shape, c3),
                  pl.BlockSpec(cre.shape, c3), pl.BlockSpec(cim.shape, c3),
                  pl.BlockSpec((1, n_state), c2), pl.BlockSpec((1, n_state), c2), pl.BlockSpec((1, D_MODEL), c2)],
        out_specs=pl.BlockSpec((rows, D_MODEL), row),
        out_shape=jax.ShapeDtypeStruct((seq * bsz, D_MODEL), BF16),
        scratch_shapes=[pltpu.VMEM((rows, S5_ST_BLK), F32)] * 4 + [pltpu.VMEM((SUBLANES, n_state), F32)] * 2,
        compiler_params=_cparams("arbitrary"),
        name="s5_scan",
    )(u_t, bre, bim, cre, cim, a_re.reshape(1, n_state), a_im.reshape(1, n_state), d.reshape(1, D_MODEL))

    y2 = y_t.reshape(seq, bsz * D_MODEL)
    return _linear_residual_ln(
        [y2], [w_out.astype(BF16)], h2d, g, b, tm=tm, glu=True, grid=(bsz, per_seq),
        x_maps=[lambda bb, i: (i, bb)], res_map=lambda bb, i: (bb * per_seq + i, 0))


def _even_mixer_block(h2d, bsz, seq, w_in, qnorm_g, w_uq, w_uq_idx, kidx_g, kidx_b, w_out, g, b):
    qa, qi, ka, va, ki, wi, qb, kb, vb = _even_proj(h2d, bsz, seq, w_in, qnorm_g, w_uq, w_uq_idx, kidx_g, kidx_b)
    o_a = _dsa_attention(qa, qi, wi, ki, ka, va, bsz, seq)
    o_b = _stick_breaking(qb, kb, vb, bsz, seq)
    w_out = w_out.astype(BF16)
    return _linear_residual_ln([o_a, o_b], [w_out[:A_WIDTH], w_out[A_WIDTH:]], h2d, g, b, tm=ROW_TILE)


def kernel(x, mem, ev_w_in, ev_qnorm_g, ev_w_uq, ev_w_uq_idx, ev_kidx_ln_g, ev_kidx_ln_b, ev_w_out, od_w_in, od_log_dt, od_lambda_re, od_lambda_im, od_b_re, od_b_im, od_c_re, od_c_im, od_d, od_w_out, mix_ln_g, mix_ln_b, xa_w_q, xa_w_kv, xa_w_o, xa_ln_g, xa_ln_b, moe_w_router, moe_b_router, moe_w_gu, moe_b_gu, moe_w_down, moe_b_down, ffn_ln_g, ffn_ln_b):
    bsz, seq, _ = x.shape
    h = x.reshape(bsz * seq, D_MODEL)
    for layer in range(DEPTH):
        j = layer // 2
        if layer % 2 == 0:
            h = _even_mixer_block(h, bsz, seq, ev_w_in[j], ev_qnorm_g[j], ev_w_uq[j], ev_w_uq_idx[j],
                                  ev_kidx_ln_g[j], ev_kidx_ln_b[j], ev_w_out[j], mix_ln_g[layer], mix_ln_b[layer])
        else:
            h = _s5_mixer_block(h, bsz, seq, od_w_in[j], od_log_dt[j], od_lambda_re[j], od_lambda_im[j],
                                od_b_re[j], od_b_im[j], od_c_re[j], od_c_im[j], od_d[j], od_w_out[j],
                                mix_ln_g[layer], mix_ln_b[layer])
        h, h_packed = _cross_attention_block(h, mem, bsz, seq, xa_w_q[layer], xa_w_kv[layer], xa_w_o[layer],
                                             xa_ln_g[layer], xa_ln_b[layer])
        h = _moe_block(h, h_packed, layer, moe_w_router[layer], moe_b_router[layer], moe_w_gu, moe_b_gu[layer],
                       moe_w_down, moe_b_down[layer], ffn_ln_g[layer], ffn_ln_b[layer])
    return h.reshape(bsz, seq, D_MODEL)
```

```python
import functools
import math

import jax
import jax.numpy as jnp
from jax import lax
from jax.experimental import pallas as pl
from jax.experimental.pallas import tpu as pltpu
from jax.experimental.pallas import tpu_sc as plsc

F32 = jnp.float32
BF16 = jnp.bfloat16
I32 = jnp.int32

D_MODEL = 1024
DEPTH = 2
HEAD_DIM = 64
A_HEADS = 8
A_KV_HEADS = 2
A_REP = A_HEADS // A_KV_HEADS
Q_RANK = 256
IDX_HEADS = 8
IDX_DIM = 64
IDX_TOPK = 256
B_HEADS = 8
A_WIDTH = A_HEADS * HEAD_DIM
B_WIDTH = B_HEADS * HEAD_DIM
SSM_GROUP = 16
SSM_GROUPS = D_MODEL // SSM_GROUP
SSM_STATE = 64
XA_HEADS = 4
XA_HEAD_DIM = D_MODEL // XA_HEADS
N_EXPERTS = 32
TOP_K = 4
D_EXPERT = D_MODEL
SWIGLU_LIMIT = 7.0
SWIGLU_ALPHA = 1.702
ROPE_THETA = 500000.0
ROPE_HALF = HEAD_DIM // 8
LN_EPS = 1e-5
DN_ALPHA = (2 * DEPTH) ** 0.25

LANES = 128
SUBLANES = 8
VMEM_LIMIT_BYTES = 56 * 1024 * 1024

Q_BLOCK = 128
DSA_KEY_TILE = 512
DSA_ATT_TILE = 256
DSA_COUNT_ROWS = 8 * SUBLANES
SB_KEY_TILE = 256
ROW_TILE = 256
MOE_BLOCK_ROWS = 256
SC_GATHER_SLOT_BYTES = 128 * 1024
S5_CHUNK = 128
S5_LANE_BLOCKS = 4
S5_IN_BLK = D_MODEL // S5_LANE_BLOCKS
S5_ST_BLK = SSM_GROUPS * SSM_STATE // S5_LANE_BLOCKS

SB_EXIT_LOG = -104.0
NEG_BIG = -1e30
INT_MIN = -(2 ** 31)


def _cparams(*sem):
    return pltpu.CompilerParams(dimension_semantics=sem, vmem_limit_bytes=VMEM_LIMIT_BYTES)


def _dot(a, b):
    return jnp.dot(a, b, preferred_element_type=F32)


def _dot_nt(a, b):
    return lax.dot_general(a, b, (((1,), (1,)), ((), ())), preferred_element_type=F32)


def _layer_norm_rows(y, g, b):
    mu = jnp.mean(y, axis=-1, keepdims=True)
    d = y - mu
    var = jnp.mean(d * d, axis=-1, keepdims=True)
    return d * lax.rsqrt(var + LN_EPS) * g + b


def _mm_kernel(x_ref, w_ref, o_ref):
    o_ref[...] = _dot(x_ref[...].astype(BF16), w_ref[...]).astype(o_ref.dtype)


def _matmul(x, w, *, tm, out_dtype, x_map=None, out_map=None, grid=None, out_shape=None):
    m, k = x.shape
    n = w.shape[1]
    grid = grid or (m // tm,)
    x_map = x_map or (lambda i: (i, 0))
    out_map = out_map or (lambda i: (i, 0))
    out_shape = out_shape or (m, n)
    return pl.pallas_call(
        _mm_kernel,
        grid=grid,
        in_specs=[pl.BlockSpec((tm, k), x_map), pl.BlockSpec((k, n), lambda *a: (0, 0))],
        out_specs=pl.BlockSpec((tm, n), out_map),
        out_shape=jax.ShapeDtypeStruct(out_shape, out_dtype),
        compiler_params=_cparams(*(("parallel",) * len(grid))),
        name="matmul",
    )(x, w)


def _lin_ln_kernel(*refs, n_in, glu):
    xs, ws = refs[:n_in], refs[n_in:2 * n_in]
    res_ref, g_ref, b_ref, o_ref = refs[2 * n_in:]
    acc = _dot(xs[0][...].astype(BF16), ws[0][...])
    for x_ref, w_ref in zip(xs[1:], ws[1:]):
        acc = acc + _dot(x_ref[...].astype(BF16), w_ref[...])
    if glu:
        acc = acc[:, :D_MODEL] * jax.nn.sigmoid(acc[:, D_MODEL:])
    y = DN_ALPHA * res_ref[...] + acc
    o_ref[...] = _layer_norm_rows(y, g_ref[...], b_ref[...])


def _linear_residual_ln(xs, ws, res, g, b, *, tm, glu=False, grid=None, x_maps=None, res_map=None):
    n_rows = res.shape[0]
    grid = grid or (n_rows // tm,)
    x_maps = x_maps or [lambda i: (i, 0)] * len(xs)
    res_map = res_map or (lambda i: (i, 0))
    const = lambda *a: (0, 0)
    in_specs = [pl.BlockSpec((tm, w.shape[0]), m) for w, m in zip(ws, x_maps)]
    in_specs += [pl.BlockSpec(w.shape, const) for w in ws]
    in_specs += [pl.BlockSpec((tm, D_MODEL), res_map), pl.BlockSpec((1, D_MODEL), const),
                 pl.BlockSpec((1, D_MODEL), const)]
    return pl.pallas_call(
        functools.partial(_lin_ln_kernel, n_in=len(xs), glu=glu),
        grid=grid,
        in_specs=in_specs,
        out_specs=pl.BlockSpec((tm, D_MODEL), res_map),
        out_shape=jax.ShapeDtypeStruct((n_rows, D_MODEL), F32),
        compiler_params=_cparams(*(("parallel",) * len(grid))),
        name="linear_residual_ln",
    )(*xs, *ws, res, g.reshape(1, D_MODEL), b.reshape(1, D_MODEL))


_EV_CQ, _EV_KA, _EV_VA, _EV_KI, _EV_QB = 0, 256, 384, 512, 640
_EV_KB = _EV_QB + B_WIDTH
_EV_VB = _EV_KB + B_WIDTH
_EV_COLS = _EV_VB + B_WIDTH


def _rope_tables(seq):
    inv = ROPE_THETA ** (-jnp.arange(ROPE_HALF, dtype=F32) / ROPE_HALF)
    ang = jnp.arange(seq, dtype=F32)[:, None] * inv[None, :]
    cos, sin = jnp.cos(ang), jnp.sin(ang)
    rest = HEAD_DIM - 2 * ROPE_HALF
    zh = jnp.zeros((seq, ROPE_HALF), F32)
    c = jnp.concatenate([cos, cos, jnp.ones((seq, rest), F32)], axis=1)
    s1 = jnp.concatenate([-sin, zh, jnp.zeros((seq, rest), F32)], axis=1)
    s2 = jnp.concatenate([zh, sin, jnp.zeros((seq, rest), F32)], axis=1)
    rep = LANES // HEAD_DIM
    return jnp.tile(c, (1, rep)), jnp.tile(s1, (1, rep)), jnp.tile(s2, (1, rep))


def _even_proj_kernel(x_ref, w_ref, qg_ref, wuq_ref, wuqi_ref, lg_ref, lb_ref, c_ref, s1_ref, s2_ref,
                      qa_ref, qi_ref, ka_ref, va_ref, ki_ref, wi_ref, qb_ref, kb_ref, vb_ref):
    p = _dot(x_ref[...].astype(BF16), w_ref[...])
    c, s1, s2 = c_ref[...], s1_ref[...], s2_ref[...]

    def rope(t):
        return (t * c + pltpu.roll(t, LANES - ROPE_HALF, 1) * s1 + pltpu.roll(t, ROPE_HALF, 1) * s2)

    cq = p[:, _EV_CQ:_EV_CQ + Q_RANK]
    cn = cq * lax.rsqrt(jnp.mean(cq * cq, axis=-1, keepdims=True) + LN_EPS) * qg_ref[...]
    cnb = cn.astype(BF16)
    qa = _dot(cnb, wuq_ref[...])
    qi = _dot(cnb, wuqi_ref[...])
    low = lax.broadcasted_iota(I32, c.shape, 1) < HEAD_DIM
    for j in range(A_WIDTH // LANES):
        sl = slice(j * LANES, (j + 1) * LANES)
        pair = rope(qa[:, sl]) * (HEAD_DIM ** -0.5)
        for e, src in enumerate((pair, pltpu.roll(pair, HEAD_DIM, 1))):
            h = 2 * j + e
            qa_ref[:, h * LANES:(h + 1) * LANES] = jnp.where(low, src, 0.0).astype(BF16)
        qi_ref[:, sl] = (rope(qi[:, sl]) * (IDX_DIM ** -0.5)).astype(BF16)
    kpair = rope(p[:, _EV_KA:_EV_KA + LANES])
    vpair = p[:, _EV_VA:_EV_VA + LANES]
    v_pad = jnp.where(lax.broadcasted_iota(I32, c.shape, 1) == HEAD_DIM, 1.0, 0.0)
    for g, (ks, vs) in enumerate(((kpair, vpair), (pltpu.roll(kpair, HEAD_DIM, 1), pltpu.roll(vpair, HEAD_DIM, 1)))):
        ka_ref[:, g * LANES:(g + 1) * LANES] = jnp.where(low, ks, 0.0).astype(BF16)
        va_ref[:, g * LANES:(g + 1) * LANES] = jnp.where(low, vs, v_pad).astype(BF16)

    t = p[:, _EV_KI:_EV_KI + LANES]
    lane = lax.broadcasted_iota(I32, t.shape, 1)
    is_k = lane < IDX_DIM
    mu = jnp.sum(jnp.where(is_k, t, 0.0), axis=-1, keepdims=True) * (1.0 / IDX_DIM)
    d = jnp.where(is_k, t - mu, 0.0)
    var = jnp.sum(d * d, axis=-1, keepdims=True) * (1.0 / IDX_DIM)
    kin = d * lax.rsqrt(var + LN_EPS) * lg_ref[...] + lb_ref[...]
    ki_ref[...] = rope(kin)[:, :IDX_DIM].astype(BF16)
    wi_ref[...] = t[:, IDX_DIM:IDX_DIM + IDX_HEADS] * (IDX_HEADS ** -0.5)

    qb_ref[...] = (p[:, _EV_QB:_EV_KB] * (HEAD_DIM ** -0.5)).astype(BF16)
    kb_ref[...] = p[:, _EV_KB:_EV_VB].astype(BF16)
    vb_ref[...] = p[:, _EV_VB:_EV_COLS].astype(BF16)


def _even_proj(x2d, bsz, seq, w_in, qnorm_g, w_uq, w_uq_idx, kidx_g, kidx_b):
    n = x2d.shape[0]
    tm = ROW_TILE
    per_seq = seq // tm
    c0 = Q_RANK + 2 * A_KV_HEADS * HEAD_DIM + IDX_DIM + IDX_HEADS
    w_pack = jnp.concatenate(
        [w_in[:, :c0], jnp.zeros((D_MODEL, _EV_QB - c0), w_in.dtype), w_in[:, c0:]], axis=1).astype(BF16)
    pad = LANES - IDX_DIM
    lg = jnp.concatenate([kidx_g, jnp.zeros((pad,), F32)]).reshape(1, LANES)
    lb = jnp.concatenate([kidx_b, jnp.zeros((pad,), F32)]).reshape(1, LANES)
    c, s1, s2 = _rope_tables(seq)
    row = lambda i: (i, 0)
    const = lambda i: (0, 0)
    pos = lambda i: (i % per_seq, 0)
    head_shape = jax.ShapeDtypeStruct((n, B_WIDTH), BF16)
    head_spec = pl.BlockSpec((tm, B_WIDTH), row)
    return pl.pallas_call(
        _even_proj_kernel,
        grid=(n // tm,),
        in_specs=[pl.BlockSpec((tm, D_MODEL), row), pl.BlockSpec((D_MODEL, _EV_COLS), const),
                  pl.BlockSpec((1, Q_RANK), const), pl.BlockSpec((Q_RANK, A_WIDTH), const),
                  pl.BlockSpec((Q_RANK, IDX_HEADS * IDX_DIM), const),
                  pl.BlockSpec((1, LANES), const), pl.BlockSpec((1, LANES), const),
                  pl.BlockSpec((tm, LANES), pos), pl.BlockSpec((tm, LANES), pos), pl.BlockSpec((tm, LANES), pos)],
        out_specs=[pl.BlockSpec((tm, A_HEADS * LANES), row), pl.BlockSpec((tm, IDX_HEADS * IDX_DIM), row),
                   pl.BlockSpec((tm, A_KV_HEADS * LANES), row), pl.BlockSpec((tm, A_KV_HEADS * LANES), row),
                   pl.BlockSpec((tm, IDX_DIM), row), pl.BlockSpec((tm, IDX_HEADS), row),
                   head_spec, head_spec, head_spec],
        out_shape=[jax.ShapeDtypeStruct((n, A_HEADS * LANES), BF16), jax.ShapeDtypeStruct((n, IDX_HEADS * IDX_DIM), BF16),
                   jax.ShapeDtypeStruct((n, A_KV_HEADS * LANES), BF16),
                   jax.ShapeDtypeStruct((n, A_KV_HEADS * LANES), BF16),
                   jax.ShapeDtypeStruct((n, IDX_DIM), BF16), jax.ShapeDtypeStruct((n, IDX_HEADS), F32),
                   head_shape, head_shape, head_shape],
        compiler_params=_cparams("parallel"),
        name="even_proj",
    )(x2d, w_pack, qnorm_g.reshape(1, Q_RANK), w_uq.astype(BF16), w_uq_idx.astype(BF16), lg, lb, c, s1, s2)


def _key_to_float(key):
    bits = key ^ ((key >> 31) & jnp.int32(0x7FFFFFFF))
    return lax.bitcast_convert_type(bits, F32)


def _dsa_kernel(qa_ref, qi_ref, wit_ref, ki_ref, ka_ref, vat_ref, o_ref, sc_scr, *, topk, ts, ta):
    seq = sc_scr.shape[0]
    qb = pl.program_id(1)
    q0 = qb * Q_BLOCK
    nkt = (q0 + Q_BLOCK - 1) // ts + 1
    t_row = q0 + lax.broadcasted_iota(I32, (1, Q_BLOCK), 1)
    key = lax.broadcasted_iota(I32, (ts, Q_BLOCK), 0)
    kf = jnp.float32(topk)

    qi = qi_ref[0]
    qs = jnp.concatenate([qi[:, h * IDX_DIM:(h + 1) * IDX_DIM] for h in range(IDX_HEADS)], axis=0)
    wit = wit_ref[0]

    def score_tile(kt, carry):
        off = pl.multiple_of(kt * ts, ts)
        s_all = _dot_nt(ki_ref[0, pl.ds(off, ts), :], qs)
        acc = jnp.zeros((ts, Q_BLOCK), F32)
        for h in range(IDX_HEADS):
            acc = acc + jnp.maximum(s_all[:, h * Q_BLOCK:(h + 1) * Q_BLOCK], 0.0) * wit[h:h + 1, :]
        sc_scr[pl.ds(off, ts), :] = jnp.where(off + key <= t_row, acc, -jnp.inf)
        return carry

    lax.fori_loop(0, nkt, score_tile, 0)

    def count(pred):
        def body(kt, acc):
            off = pl.multiple_of(kt * ts, ts)
            ind = pred(sc_scr[pl.ds(off, ts), :], off + key)
            return acc + jnp.sum(ind.reshape(ts // DSA_COUNT_ROWS, DSA_COUNT_ROWS, Q_BLOCK), axis=0)
        acc = lax.fori_loop(0, nkt, body, jnp.zeros((DSA_COUNT_ROWS, Q_BLOCK), F32))
        return jnp.sum(acc, axis=0, keepdims=True)

    def bit_step(i, base):
        cand = base + jnp.left_shift(jnp.int32(1), 31 - i)
        cf = _key_to_float(cand)
        cnt = count(lambda sc, idx: jnp.where(sc >= cf, 1.0, 0.0))
        return jnp.where(cnt >= kf, cand, base)

    base = lax.fori_loop(0, 32, bit_step, jnp.full((1, Q_BLOCK), INT_MIN, I32))
    thr = jnp.where(base == INT_MIN, -jnp.inf, _key_to_float(base))

    cnt_ge = count(lambda sc, idx: jnp.where(sc >= thr, 1.0, 0.0))
    tied = jnp.logical_and(cnt_ge > kf, thr > -jnp.inf)
    any_tied = jnp.max(jnp.where(tied, 1.0, 0.0)) > 0.0
    seq_bits = max(1, int(math.ceil(math.log2(seq))))

    def tie_cut():
        cnt_gt = count(lambda sc, idx: jnp.where(sc > thr, 1.0, 0.0))
        need = kf - cnt_gt

        def idx_step(i, pos):
            cand = pos + jnp.left_shift(jnp.int32(1), seq_bits - 1 - i)
            cnt = count(lambda sc, idx: jnp.where(sc == thr, jnp.where(idx < cand, 1.0, 0.0), 0.0))
            return jnp.where(cnt < need, cand, pos)

        return lax.fori_loop(0, seq_bits, idx_step, jnp.zeros((1, Q_BLOCK), I32))

    cut = lax.cond(any_tied, tie_cut, lambda: jnp.full((1, Q_BLOCK), seq, I32))
    cut = jnp.where(tied, cut, seq)

    nkt_a = (q0 + Q_BLOCK - 1) // ta + 1
    key_a = lax.broadcasted_iota(I32, (ta, Q_BLOCK), 0)
    cols = A_REP * Q_BLOCK
    qg = [jnp.concatenate([qa_ref[0, :, (g * A_REP + r) * LANES:(g * A_REP + r + 1) * LANES]
                           for r in range(A_REP)], axis=0) for g in range(A_KV_HEADS)]

    def att_pair(i, carry):
        offs = [pl.multiple_of((2 * i + e) * ta, ta) for e in range(2)]
        logits = [_dot_nt(ka_ref[0, pl.ds(offs[e], ta), g * LANES:(g + 1) * LANES], qg[g])
                  for e in range(2) for g in range(A_KV_HEADS)]
        out = []
        for e in range(2):
            sc = sc_scr[pl.ds(offs[e], ta), :]
            idx = offs[e] + key_a
            keep = jnp.where(sc > thr, 0.0, jnp.where(sc == thr, jnp.where(idx <= cut, 0.0, NEG_BIG), NEG_BIG))
            bias = jnp.where(idx <= t_row, keep, NEG_BIG)
            bias = jnp.concatenate([bias] * A_REP, axis=1)
            for g in range(A_KV_HEADS):
                m, acc = carry[e * A_KV_HEADS + g]
                s = logits[e * A_KV_HEADS + g] + bias
                m_new = jnp.maximum(m, jnp.max(s, axis=0, keepdims=True))
                p = jnp.exp(s - m_new)
                vt = vat_ref[0, g * LANES:(g + 1) * LANES, pl.ds(offs[e], ta)]
                out.append((m_new, jnp.exp(m - m_new) * acc + _dot(vt, p.astype(BF16))))
        return tuple(out)

    init = tuple((jnp.full((1, cols), NEG_BIG, F32), jnp.zeros((LANES, cols), F32))
                 for _ in range(2 * A_KV_HEADS))
    final = lax.fori_loop(0, (nkt_a + 1) // 2, att_pair, init)
    low = lax.broadcasted_iota(I32, (Q_BLOCK, LANES), 1) < HEAD_DIM
    outs = []
    for g in range(A_KV_HEADS):
        (m0, acc0), (m1, acc1) = final[g], final[A_KV_HEADS + g]
        m = jnp.maximum(m0, m1)
        acc = jnp.exp(m0 - m) * acc0 + jnp.exp(m1 - m) * acc1
        og = acc / acc[HEAD_DIM:HEAD_DIM + 1, :]
        outs += [og[:, r * Q_BLOCK:(r + 1) * Q_BLOCK].T for r in range(A_REP)]
    for j in range(A_HEADS // 2):
        pair = jnp.where(low, outs[2 * j], pltpu.roll(outs[2 * j + 1], HEAD_DIM, 1))
        o_ref[0, :, j * LANES:(j + 1) * LANES] = pair.astype(BF16)


def _dsa_attention(qa, qi, wi, ki, ka, va, bsz, seq):
    topk = min(IDX_TOPK, seq // 4)
    ts = min(DSA_KEY_TILE, seq)
    ta = min(DSA_ATT_TILE, seq // 2)
    assert seq % (2 * ta) == 0 and seq % ts == 0, "the attention loop walks the key tiles in pairs"
    blk = lambda b, i: (b, i, 0)
    full = lambda b, i: (b, 0, 0)
    r3 = lambda a: a.reshape(bsz, seq, a.shape[-1])
    wit = jnp.swapaxes(r3(wi), 1, 2)
    vat = jnp.swapaxes(r3(va), 1, 2)
    return pl.pallas_call(
        functools.partial(_dsa_kernel, topk=topk, ts=ts, ta=ta),
        grid=(bsz, seq // Q_BLOCK),
        in_specs=[pl.BlockSpec((1, Q_BLOCK, A_HEADS * LANES), blk),
                  pl.BlockSpec((1, Q_BLOCK, IDX_HEADS * IDX_DIM), blk),
                  pl.BlockSpec((1, IDX_HEADS, Q_BLOCK), lambda b, i: (b, 0, i)),
                  pl.BlockSpec((1, seq, IDX_DIM), full),
                  pl.BlockSpec((1, seq, A_KV_HEADS * LANES), full),
                  pl.BlockSpec((1, A_KV_HEADS * LANES, seq), full)],
        out_specs=pl.BlockSpec((1, Q_BLOCK, A_WIDTH), blk),
        out_shape=jax.ShapeDtypeStruct((bsz, seq, A_WIDTH), BF16),
        scratch_shapes=[pltpu.VMEM((seq, Q_BLOCK), F32)],
        compiler_params=_cparams("parallel", "parallel"),
        name="dsa_attention",
    )(r3(qa), r3(qi), wit, r3(ki), r3(ka), vat).reshape(bsz * seq, A_WIDTH)


def _sb_kernel(q_ref, k_ref, v_ref, u_ref, o_ref, acc_scr, run_scr, *, tk):
    q0 = pl.program_id(1) * Q_BLOCK
    t_col = q0 + lax.broadcasted_iota(I32, (Q_BLOCK, 1), 0)
    lane = lax.broadcasted_iota(I32, (Q_BLOCK, tk), 1)
    low = lax.broadcasted_iota(I32, (Q_BLOCK, LANES), 1) < HEAD_DIM
    upper = u_ref[...]
    nkt = (q0 + Q_BLOCK - 1) // tk + 1
    q = q_ref[0]
    zero = jnp.zeros((Q_BLOCK, LANES), BF16)
    qm = []
    for h in range(B_HEADS):
        pair = q[:, (h // 2) * LANES:(h // 2 + 1) * LANES]
        qm.append(jnp.where(low, pair, zero) if h % 2 == 0 else jnp.where(low, zero, pair))
    acc_scr[...] = jnp.zeros_like(acc_scr)
    run_scr[...] = jnp.zeros_like(run_scr)

    def cond(carry):
        i, worst = carry
        return jnp.logical_and(i < nkt, worst >= SB_EXIT_LOG)

    def body(carry):
        i, _ = carry
        off = pl.multiple_of((nkt - 1 - i) * tk, tk)
        strict = off + lane < t_col
        worst = None
        for p in range(B_HEADS // 2):
            cols = slice(p * LANES, (p + 1) * LANES)
            kp = k_ref[0, pl.ds(off, tk), cols]
            vp = v_ref[0, pl.ds(off, tk), cols]
            outs = []
            for e in range(2):
                h = 2 * p + e
                run = run_scr[h]
                z = _dot_nt(qm[h], kp)
                softplus = jnp.maximum(z, 0.0) + jnp.log(1.0 + jnp.exp(-jnp.abs(z)))
                log_1mb = jnp.where(strict, -softplus, 0.0)
                hi = log_1mb.astype(BF16)
                lo = (log_1mb - hi.astype(F32)).astype(BF16)
                after = _dot(hi, upper) + _dot(lo, upper) + run
                a = jnp.where(strict, jnp.exp(z - softplus + after), 0.0)
                outs.append(_dot(a.astype(BF16), vp))
                run = run + jnp.sum(log_1mb, axis=1, keepdims=True)
                run_scr[h] = run
                worst = run if worst is None else jnp.maximum(worst, run)
            acc_scr[:, cols] += jnp.where(low, outs[0], outs[1])
        return i + 1, jnp.max(worst)

    lax.while_loop(cond, body, (jnp.int32(0), jnp.float32(0.0)))
    o_ref[0] = acc_scr[...].astype(BF16)


def _stick_breaking(qb, kb, vb, bsz, seq):
    tk = min(SB_KEY_TILE, seq)
    r = lax.broadcasted_iota(I32, (tk, tk), 0)
    c = lax.broadcasted_iota(I32, (tk, tk), 1)
    upper = jnp.where(r > c, 1.0, 0.0).astype(BF16)
    blk = lambda b, i: (b, i, 0)
    full = lambda b, i: (b, 0, 0)
    r3 = lambda a: a.reshape(bsz, seq, B_WIDTH)
    return pl.pallas_call(
        functools.partial(_sb_kernel, tk=tk),
        grid=(bsz, seq // Q_BLOCK),
        in_specs=[pl.BlockSpec((1, Q_BLOCK, B_WIDTH), blk), pl.BlockSpec((1, seq, B_WIDTH), full),
                  pl.BlockSpec((1, seq, B_WIDTH), full), pl.BlockSpec((tk, tk), lambda b, i: (0, 0))],
        out_specs=pl.BlockSpec((1, Q_BLOCK, B_WIDTH), blk),
        out_shape=jax.ShapeDtypeStruct((bsz, seq, B_WIDTH), BF16),
        scratch_shapes=[pltpu.VMEM((Q_BLOCK, B_WIDTH), F32), pltpu.VMEM((B_HEADS, Q_BLOCK, 1), F32)],
        compiler_params=_cparams("parallel", "arbitrary"),
        name="stick_breaking",
    )(r3(qb), r3(kb), r3(vb), upper).reshape(bsz * seq, B_WIDTH)


HALF_D = D_MODEL // 2
U32 = jnp.uint32
HIGH16 = 0xFFFF0000


def _pack_bf16_pairs(x):
    def bits(v):
        return lax.bitcast_convert_type(v.astype(BF16).astype(F32), U32)
    word = (bits(x[:, HALF_D:]) & U32(HIGH16)) | (bits(x[:, :HALF_D]) >> 16)
    return lax.bitcast_convert_type(word, I32)


def _unpack_bf16_pairs(word):
    u = lax.bitcast_convert_type(word, U32)
    lo = lax.bitcast_convert_type(u << 16, F32)
    hi = lax.bitcast_convert_type(u & U32(HIGH16), F32)
    return lo.astype(BF16), hi.astype(BF16)


def _xattn_kernel(h_ref, wq_ref, kv_ref, wo_ref, g_ref, b_ref, o_ref, packed_ref):
    h = h_ref[...]
    q = (_dot(h.astype(BF16), wq_ref[...]) * (XA_HEAD_DIM ** -0.5)).astype(BF16)
    kv = kv_ref[0]
    outs = []
    for hd in range(XA_HEADS):
        sl = slice(hd * XA_HEAD_DIM, (hd + 1) * XA_HEAD_DIM)
        s = _dot_nt(q[:, sl], kv[:, sl])
        p = jnp.exp(s - jnp.max(s, axis=1, keepdims=True))
        vh = kv[:, D_MODEL + hd * XA_HEAD_DIM:D_MODEL + (hd + 1) * XA_HEAD_DIM]
        outs.append((_dot(p.astype(BF16), vh) / jnp.sum(p, axis=1, keepdims=True)).astype(BF16))
    y = _dot(jnp.concatenate(outs, axis=1), wo_ref[...])
    out = _layer_norm_rows(DN_ALPHA * h + y, g_ref[...], b_ref[...])
    o_ref[...] = out
    packed_ref[...] = _pack_bf16_pairs(out)


def _cross_attention_block(h2d, mem, bsz, seq, w_q, w_kv, w_o, g, b):
    tm = ROW_TILE
    per_seq = seq // tm
    mem_len = mem.shape[1]
    kv = _matmul(mem.reshape(bsz * mem_len, D_MODEL), w_kv.astype(BF16), tm=mem_len, out_dtype=BF16)
    kv = kv.reshape(bsz, mem_len, 2 * D_MODEL)
    row = lambda i: (i, 0)
    const = lambda i: (0, 0)
    return pl.pallas_call(
        _xattn_kernel,
        grid=(bsz * per_seq,),
        in_specs=[pl.BlockSpec((tm, D_MODEL), row), pl.BlockSpec((D_MODEL, D_MODEL), const),
                  pl.BlockSpec((1, mem_len, 2 * D_MODEL), lambda i: (i // per_seq, 0, 0)),
                  pl.BlockSpec((D_MODEL, D_MODEL), const),
                  pl.BlockSpec((1, D_MODEL), const), pl.BlockSpec((1, D_MODEL), const)],
        out_specs=[pl.BlockSpec((tm, D_MODEL), row), pl.BlockSpec((tm, HALF_D), row)],
        out_shape=[jax.ShapeDtypeStruct(h2d.shape, F32), jax.ShapeDtypeStruct((h2d.shape[0], HALF_D), I32)],
        compiler_params=_cparams("parallel"),
        name="cross_attention",
    )(h2d, w_q.astype(BF16), kv, w_o.astype(BF16), g.reshape(1, D_MODEL), b.reshape(1, D_MODEL))


def _router_kernel(h_ref, w_ref, b_ref, tri_ref, idx_ref, gate_ref, rank_ref, cnt_ref, run_scr):
    @pl.when(pl.program_id(0) == 0)
    def _():
        run_scr[...] = jnp.zeros_like(run_scr)

    h = h_ref[...]
    hh = h.astype(BF16)
    hl = (h - hh.astype(F32)).astype(BF16)
    w = w_ref[...]
    wh = w.astype(BF16)
    wl = (w - wh.astype(F32)).astype(BF16)
    logits = _dot(hh, wh) + _dot(hl, wh) + _dot(hh, wl) + b_ref[...]
    lane = lax.broadcasted_iota(I32, logits.shape, 1).astype(F32)
    vals, sels = [], []
    onehot = jnp.zeros(logits.shape, F32)
    for k in range(TOP_K):
        m = jnp.max(logits, axis=1, keepdims=True)
        sel = jnp.min(jnp.where(logits == m, lane, float(LANES)), axis=1, keepdims=True)
        idx_ref[:, k:k + 1] = sel.astype(I32)
        vals.append(m)
        sels.append(sel)
        onehot = onehot + jnp.where(lane == sel, 1.0, 0.0)
        logits = jnp.where(lane == sel, -jnp.inf, logits)
    es = [jnp.exp(v - vals[0]) for v in vals]
    tot = es[0] + es[1] + es[2] + es[3]
    for k in range(TOP_K):
        gate_ref[:, k:k + 1] = es[k] / tot

    earlier = _dot(tri_ref[...], onehot.astype(BF16)) + run_scr[...]
    for k in range(TOP_K):
        rank = jnp.sum(jnp.where(lane == sels[k], earlier, 0.0), axis=1, keepdims=True)
        rank_ref[:, k:k + 1] = rank.astype(I32)
    run = run_scr[...] + jnp.sum(onehot, axis=0, keepdims=True)
    run_scr[...] = run
    cnt_ref[...] = run


def _router(h2d, w_router, b_router):
    n = h2d.shape[0]
    tm = 2 * ROW_TILE
    pad = LANES - N_EXPERTS
    w = jnp.concatenate([w_router, jnp.zeros((D_MODEL, pad), F32)], axis=1)
    b = jnp.concatenate([b_router, jnp.full((pad,), NEG_BIG, F32)]).reshape(1, LANES)
    r = lax.broadcasted_iota(I32, (tm, tm), 0)
    c = lax.broadcasted_iota(I32, (tm, tm), 1)
    tri = jnp.where(c < r, 1.0, 0.0).astype(BF16)
    row = lambda i: (i, 0)
    const = lambda i: (0, 0)
    return pl.pallas_call(
        _router_kernel,
        grid=(n // tm,),
        in_specs=[pl.BlockSpec((tm, D_MODEL), row), pl.BlockSpec((D_MODEL, LANES), const),
                  pl.BlockSpec((1, LANES), const), pl.BlockSpec((tm, tm), const)],
        out_specs=[pl.BlockSpec((tm, TOP_K), row), pl.BlockSpec((tm, TOP_K), row),
                   pl.BlockSpec((tm, TOP_K), row), pl.BlockSpec((1, LANES), const)],
        out_shape=[jax.ShapeDtypeStruct((n, TOP_K), I32), jax.ShapeDtypeStruct((n, TOP_K), F32),
                   jax.ShapeDtypeStruct((n, TOP_K), I32), jax.ShapeDtypeStruct((1, LANES), F32)],
        scratch_shapes=[pltpu.VMEM((1, LANES), F32)],
        compiler_params=_cparams("arbitrary"),
        name="moe_router",
    )(h2d, w, b, tri)


def _gather_rows(src, idx):
    n_out = idx.shape[0]
    width = src.shape[1]
    win = SC_GATHER_SLOT_BYTES // (width * src.dtype.itemsize)
    mesh = plsc.VectorSubcoreMesh(core_axis_name="core", subcore_axis_name="subcore")
    n_workers = mesh.num_cores * mesh.num_subcores
    per_worker = n_out // n_workers
    steps = per_worker // win
    assert per_worker * n_workers == n_out and steps * win == per_worker and steps % 2 == 0

    @functools.partial(
        pl.kernel, out_type=jax.ShapeDtypeStruct((n_out, width), src.dtype), mesh=mesh,
        scratch_types=[pltpu.VMEM((per_worker,), I32), pltpu.VMEM((2, win, width), src.dtype),
                       pltpu.SemaphoreType.DMA, pltpu.SemaphoreType.DMA])
    def gather_kernel(src_hbm, idx_hbm, dst_hbm, idx_v, rows_v, sem0, sem1):
        worker = lax.axis_index("subcore") * mesh.num_cores + lax.axis_index("core")
        base = worker * per_worker
        sems = (sem0, sem1)
        pltpu.sync_copy(idx_hbm.at[pl.ds(base, per_worker)], idx_v)

        def gather(step, slot):
            return pltpu.make_async_copy(src_hbm.at[idx_v.at[pl.ds(step * win, win)]], rows_v.at[slot], sems[slot])

        gather(0, 0).start()

        @pl.loop(0, steps, step=2)
        def _(s):
            for slot in range(2):
                step = s + slot
                gather(step, slot).wait()

                @pl.when(step + 1 < steps)
                def _():
                    gather(step + 1, 1 - slot).start()

                pltpu.sync_copy(rows_v.at[slot], dst_hbm.at[pl.ds(base + step * win, win)])

    return gather_kernel(src, idx)


def _expert_kernel(blk_exp_ref, n_used_ref, x_ref, wgu_ref, bgu_ref, wd_ref, bd_ref, o_ref, wgu_bf, wd_bf):
    i = pl.program_id(0)

    @pl.when(jnp.logical_or(i == 0, blk_exp_ref[i] != blk_exp_ref[jnp.maximum(i - 1, 0)]))
    def _():
        wgu_bf[...] = wgu_ref[0, 0].astype(BF16)
        wd_bf[...] = wd_ref[0, 0].astype(BF16)

    @pl.when(i < n_used_ref[0])
    def _():
        x_lo, x_hi = _unpack_bf16_pairs(x_ref[...])
        hgu = _dot(x_lo, wgu_bf[:HALF_D, :]) + _dot(x_hi, wgu_bf[HALF_D:, :]) + bgu_ref[0]
        gate = jnp.minimum(hgu[:, :D_EXPERT], SWIGLU_LIMIT)
        up = jnp.clip(hgu[:, D_EXPERT:], -SWIGLU_LIMIT, SWIGLU_LIMIT)
        act = gate * jax.nn.sigmoid(gate * SWIGLU_ALPHA) * (up + 1.0)
        o_ref[...] = _dot(act.astype(BF16), wd_bf[...]) + bd_ref[0]

    @pl.when(i >= n_used_ref[0])
    def _():
        o_ref[...] = jnp.zeros_like(o_ref)


def _expert_mlp(xs, block_exp, n_used, layer, w_gu, b_gu, w_down, b_down):
    n_rows = xs.shape[0]
    bm = MOE_BLOCK_ROWS
    row = lambda i, be, nu: (i, 0)
    exp3 = lambda i, be, nu: (be[i], 0, 0)
    exp4 = lambda i, be, nu: (layer, be[i], 0, 0)
    grid_spec = pltpu.PrefetchScalarGridSpec(
        num_scalar_prefetch=2,
        grid=(n_rows // bm,),
        in_specs=[pl.BlockSpec((bm, HALF_D), row),
                  pl.BlockSpec((1, 1, D_MODEL, 2 * D_EXPERT), exp4), pl.BlockSpec((1, 1, 2 * D_EXPERT), exp3),
                  pl.BlockSpec((1, 1, D_EXPERT, D_MODEL), exp4), pl.BlockSpec((1, 1, D_MODEL), exp3)],
        out_specs=pl.BlockSpec((bm, D_MODEL), row),
        scratch_shapes=[pltpu.VMEM((D_MODEL, 2 * D_EXPERT), BF16), pltpu.VMEM((D_EXPERT, D_MODEL), BF16)],
    )
    return pl.pallas_call(
        _expert_kernel,
        grid_spec=grid_spec,
        out_shape=jax.ShapeDtypeStruct((n_rows, D_MODEL), F32),
        compiler_params=_cparams("arbitrary"),
        name="moe_experts",
    )(block_exp, n_used, xs, w_gu, b_gu.reshape(N_EXPERTS, 1, 2 * D_EXPERT),
      w_down, b_down.reshape(N_EXPERTS, 1, D_MODEL))


def _combine_kernel(y0_ref, y1_ref, y2_ref, y3_ref, gate_ref, res_ref, g_ref, b_ref, o_ref):
    gates = gate_ref[...]
    acc = y0_ref[...] * gates[:, 0:1]
    for k, y_ref in enumerate((y1_ref, y2_ref, y3_ref), start=1):
        acc = acc + y_ref[...] * gates[:, k:k + 1]
    o_ref[...] = _layer_norm_rows(DN_ALPHA * res_ref[...] + acc, g_ref[...], b_ref[...])


def _moe_block(h2d, h_packed, layer, w_router, b_router, w_gu, b_gu, w_down, b_down, g, b):
    n = h2d.shape[0]
    n_slots = n * TOP_K
    bm = MOE_BLOCK_ROWS
    top_idx, gates, rank, totals = _router(h2d, w_router, b_router)

    e_flat = top_idx.reshape(-1)
    order = jnp.argsort(e_flat).astype(I32)
    counts = totals[0, :N_EXPERTS].astype(I32)
    padded = (counts + bm - 1) // bm * bm
    start = jnp.cumsum(counts) - counts
    ends_p = jnp.cumsum(padded)
    pstart = ends_p - padded
    n_rows = n_slots + N_EXPERTS * bm
    n_blocks = n_rows // bm
    r = jnp.arange(n_rows, dtype=I32)
    e_r = jnp.minimum(jnp.searchsorted(ends_p, r, side="right"), N_EXPERTS - 1).astype(I32)
    j = r - pstart[e_r]
    valid = j < counts[e_r]
    slot_of_row = order[jnp.where(valid, start[e_r] + j, 0)]
    rows_tok = jnp.where(valid, slot_of_row // TOP_K, 0).astype(I32)
    slot_pos = pstart[e_flat] + rank.reshape(-1)
    block_exp = e_r[::bm]
    n_used = (ends_p[-1] // bm).astype(I32).reshape(1)

    xs = _gather_rows(h_packed, rows_tok)
    ys = _expert_mlp(xs, block_exp, n_used, layer, w_gu, b_gu, w_down, b_down)
    yk = _gather_rows(ys, slot_pos.reshape(n, TOP_K).T.reshape(-1))

    tm = ROW_TILE
    row = lambda i: (i, 0)
    const = lambda i: (0, 0)
    choice = lambda k: (lambda i: (k * (n // tm) + i, 0))
    return pl.pallas_call(
        _combine_kernel,
        grid=(n // tm,),
        in_specs=[pl.BlockSpec((tm, D_MODEL), choice(k)) for k in range(TOP_K)] + [
                  pl.BlockSpec((tm, TOP_K), row),
                  pl.BlockSpec((tm, D_MODEL), row), pl.BlockSpec((1, D_MODEL), const),
                  pl.BlockSpec((1, D_MODEL), const)],
        out_specs=pl.BlockSpec((tm, D_MODEL), row),
        out_shape=jax.ShapeDtypeStruct((n, D_MODEL), F32),
        compiler_params=_cparams("parallel"),
        name="moe_combine",
    )(yk, yk, yk, yk, gates, h2d, g.reshape(1, D_MODEL), b.reshape(1, D_MODEL))


def _s5_kernel(u_ref, bre_ref, bim_ref, cre_ref, cim_ref, are_ref, aim_ref, d_ref, y_ref,
               bu_re, bu_im, st_re, st_im, h_re, h_im, *, bsz):
    @pl.when(pl.program_id(0) == 0)
    def _():
        h_re[...] = jnp.zeros_like(h_re)
        h_im[...] = jnp.zeros_like(h_im)

    rows = u_ref.shape[0]
    first = lax.broadcasted_iota(I32, (SUBLANES, S5_ST_BLK), 0) < bsz
    for j in range(S5_LANE_BLOCKS):
        cin = slice(j * S5_IN_BLK, (j + 1) * S5_IN_BLK)
        cst = slice(j * S5_ST_BLK, (j + 1) * S5_ST_BLK)
        uj = u_ref[:, cin]
        ujb = uj.astype(BF16)
        bu_re[...] = _dot(ujb, bre_ref[j])
        bu_im[...] = _dot(ujb, bim_ref[j])
        ar = jnp.broadcast_to(are_ref[:, cst], (SUBLANES, S5_ST_BLK))
        ai = jnp.broadcast_to(aim_ref[:, cst], (SUBLANES, S5_ST_BLK))

        def step(i, carry):
            hr, hi = carry
            r0 = pl.multiple_of(i * SUBLANES, SUBLANES)
            vr = bu_re[pl.ds(r0, SUBLANES), :]
            vi = bu_im[pl.ds(r0, SUBLANES), :]
            h1r = ar * hr - ai * hi + vr
            h1i = ar * hi + ai * hr + vi
            h1rs = pltpu.roll(h1r, bsz, 0)
            h1is = pltpu.roll(h1i, bsz, 0)
            h2r = ar * h1rs - ai * h1is + vr
            h2i = ar * h1is + ai * h1rs + vi
            st_re[pl.ds(r0, SUBLANES), :] = jnp.where(first, h1r, h2r)
            st_im[pl.ds(r0, SUBLANES), :] = jnp.where(first, h1i, h2i)
            return pltpu.roll(h2r, bsz, 0), pltpu.roll(h2i, bsz, 0)

        hr, hi = lax.fori_loop(0, rows // SUBLANES, step, (h_re[:, cst], h_im[:, cst]))
        h_re[:, cst] = hr
        h_im[:, cst] = hi
        yj = _dot(st_re[...].astype(BF16), cre_ref[j]) + _dot(st_im[...].astype(BF16), cim_ref[j])
        yj = yj + d_ref[:, cin] * uj
        y_ref[:, cin] = jax.nn.gelu(yj).astype(BF16)


def _s5_block_diag(w, n_in, n_out):
    gpb = SSM_GROUPS // S5_LANE_BLOCKS
    w4 = w.reshape(S5_LANE_BLOCKS, gpb, n_in, n_out)
    eye = jnp.eye(gpb, dtype=w.dtype)
    return jnp.einsum("jgio,gh->jgiho", w4, eye).reshape(S5_LANE_BLOCKS, gpb * n_in, gpb * n_out)


def _s5_mixer_block(h2d, bsz, seq, w_in, log_dt, lam_re, lam_im, b_re, b_im, c_re, c_im, d, w_out, g, b):
    assert 2 * bsz == SUBLANES, "the scan packs two time steps of bsz rows into one 8-row tile"
    tm = ROW_TILE
    per_seq = seq // tm
    u_t = _matmul(h2d, w_in.astype(BF16), tm=tm, out_dtype=F32, grid=(bsz, per_seq),
                  x_map=lambda bb, i: (bb * per_seq + i, 0), out_map=lambda bb, i: (i, bb),
                  out_shape=(seq, bsz * D_MODEL)).reshape(seq * bsz, D_MODEL)

    dt = jnp.exp(log_dt)[:, None]
    mag = jnp.exp(lam_re * dt)
    a_re, a_im = mag * jnp.cos(lam_im * dt), mag * jnp.sin(lam_im * dt)
    den = lam_re * lam_re + lam_im * lam_im
    coef_re = ((a_re - 1.0) * lam_re + a_im * lam_im) / den
    coef_im = (a_im * lam_re - (a_re - 1.0) * lam_im) / den
    bb_re = coef_re[..., None] * b_re - coef_im[..., None] * b_im
    bb_im = coef_re[..., None] * b_im + coef_im[..., None] * b_re
    bre = _s5_block_diag(jnp.swapaxes(bb_re, 1, 2), SSM_GROUP, SSM_STATE).astype(BF16)
    bim = _s5_block_diag(jnp.swapaxes(bb_im, 1, 2), SSM_GROUP, SSM_STATE).astype(BF16)
    cre = _s5_block_diag(jnp.swapaxes(c_re, 1, 2), SSM_STATE, SSM_GROUP).astype(BF16)
    cim = _s5_block_diag(jnp.swapaxes(-c_im, 1, 2), SSM_STATE, SSM_GROUP).astype(BF16)
    n_state = SSM_GROUPS * SSM_STATE

    rows = S5_CHUNK * bsz
    row = lambda c: (c, 0)
    c2 = lambda c: (0, 0)
    c3 = lambda c: (0, 0, 0)
    y_t = pl.pallas_call(
        functools.partial(_s5_kernel, bsz=bsz),
        grid=(seq // S5_CHUNK,),
        in_specs=[pl.BlockSpec((rows, D_MODEL), row),
                  pl.BlockSpec(bre.shape, c3), pl.BlockSpec(bim.shape, c3),
                  pl.BlockSpec(cre.shape, c3), pl.BlockSpec(cim.shape, c3),
                  pl.BlockSpec((1, n_state), c2), pl.BlockSpec((1, n_state), c2), pl.BlockSpec((1, D_MODEL), c2)],
        out_specs=pl.BlockSpec((rows, D_MODEL), row),
        out_shape=jax.ShapeDtypeStruct((seq * bsz, D_MODEL), BF16),
        scratch_shapes=[pltpu.VMEM((rows, S5_ST_BLK), F32)] * 4 + [pltpu.VMEM((SUBLANES, n_state), F32)] * 2,
        compiler_params=_cparams("arbitrary"),
        name="s5_scan",
    )(u_t, bre, bim, cre, cim, a_re.reshape(1, n_state), a_im.reshape(1, n_state), d.reshape(1, D_MODEL))

    y2 = y_t.reshape(seq, bsz * D_MODEL)
    return _linear_residual_ln(
        [y2], [w_out.astype(BF16)], h2d, g, b, tm=tm, glu=True, grid=(bsz, per_seq),
        x_maps=[lambda bb, i: (i, bb)], res_map=lambda bb, i: (bb * per_seq + i, 0))


def _even_mixer_block(h2d, bsz, seq, w_in, qnorm_g, w_uq, w_uq_idx, kidx_g, kidx_b, w_out, g, b):
    qa, qi, ka, va, ki, wi, qb, kb, vb = _even_proj(h2d, bsz, seq, w_in, qnorm_g, w_uq, w_uq_idx, kidx_g, kidx_b)
    o_a = _dsa_attention(qa, qi, wi, ki, ka, va, bsz, seq)
    o_b = _stick_breaking(qb, kb, vb, bsz, seq)
    w_out = w_out.astype(BF16)
    return _linear_residual_ln([o_a, o_b], [w_out[:A_WIDTH], w_out[A_WIDTH:]], h2d, g, b, tm=ROW_TILE)


def kernel(x, mem, ev_w_in, ev_qnorm_g, ev_w_uq, ev_w_uq_idx, ev_kidx_ln_g, ev_kidx_ln_b, ev_w_out, od_w_in, od_log_dt, od_lambda_re, od_lambda_im, od_b_re, od_b_im, od_c_re, od_c_im, od_d, od_w_out, mix_ln_g, mix_ln_b, xa_w_q, xa_w_kv, xa_w_o, xa_ln_g, xa_ln_b, moe_w_router, moe_b_router, moe_w_gu, moe_b_gu, moe_w_down, moe_b_down, ffn_ln_g, ffn_ln_b):
    bsz, seq, _ = x.shape
    h = x.reshape(bsz * seq, D_MODEL)
    for layer in range(DEPTH):
        j = layer // 2
        if layer % 2 == 0:
            h = _even_mixer_block(h, bsz, seq, ev_w_in[j], ev_qnorm_g[j], ev_w_uq[j], ev_w_uq_idx[j],
                                  ev_kidx_ln_g[j], ev_kidx_ln_b[j], ev_w_out[j], mix_ln_g[layer], mix_ln_b[layer])
        else:
            h = _s5_mixer_block(h, bsz, seq, od_w_in[j], od_log_dt[j], od_lambda_re[j], od_lambda_im[j],
                                od_b_re[j], od_b_im[j], od_c_re[j], od_c_im[j], od_d[j], od_w_out[j],
                                mix_ln_g[layer], mix_ln_b[layer])
        h, h_packed = _cross_attention_block(h, mem, bsz, seq, xa_w_q[layer], xa_w_kv[layer], xa_w_o[layer],
                                             xa_ln_g[layer], xa_ln_b[layer])
        h = _moe_block(h, h_packed, layer, moe_w_router[layer], moe_b_router[layer], moe_w_gu, moe_b_gu[layer],
                       moe_w_down, moe_b_down[layer], ffn_ln_g[layer], ffn_ln_b[layer])
    return h.reshape(bsz, seq, D_MODEL)
```

```python
import functools
import math

import jax
import jax.numpy as jnp
from jax import lax
from jax.experimental import pallas as pl
from jax.experimental.pallas import tpu as pltpu
from jax.experimental.pallas import tpu_sc as plsc

F32 = jnp.float32
BF16 = jnp.bfloat16
I32 = jnp.int32

D_MODEL = 1024
DEPTH = 2
HEAD_DIM = 64
A_HEADS = 8
A_KV_HEADS = 2
A_REP = A_HEADS // A_KV_HEADS
Q_RANK = 256
IDX_HEADS = 8
IDX_DIM = 64
IDX_TOPK = 256
B_HEADS = 8
A_WIDTH = A_HEADS * HEAD_DIM
B_WIDTH = B_HEADS * HEAD_DIM
SSM_GROUP = 16
SSM_GROUPS = D_MODEL // SSM_GROUP
SSM_STATE = 64
XA_HEADS = 4
XA_HEAD_DIM = D_MODEL // XA_HEADS
N_EXPERTS = 32
TOP_K = 4
D_EXPERT = D_MODEL
SWIGLU_LIMIT = 7.0
SWIGLU_ALPHA = 1.702
ROPE_THETA = 500000.0
ROPE_HALF = HEAD_DIM // 8
LN_EPS = 1e-5
DN_ALPHA = (2 * DEPTH) ** 0.25

LANES = 128
SUBLANES = 8
VMEM_LIMIT_BYTES = 56 * 1024 * 1024

Q_BLOCK = 256
DSA_KEY_TILE = 512
DSA_ATT_TILE = 256
DSA_COUNT_ROWS = 8 * SUBLANES
SB_KEY_TILE = 256
ROW_TILE = 256
MOE_BLOCK_ROWS = 512
SC_GATHER_SLOT_BYTES = 128 * 1024
S5_CHUNK = 128
S5_LANE_BLOCKS = 4
S5_IN_BLK = D_MODEL // S5_LANE_BLOCKS
S5_ST_BLK = SSM_GROUPS * SSM_STATE // S5_LANE_BLOCKS

SB_EXIT_LOG = -104.0
NEG_BIG = -1e30
INT_MIN = -(2 ** 31)


def _cparams(*sem):
    return pltpu.CompilerParams(dimension_semantics=sem, vmem_limit_bytes=VMEM_LIMIT_BYTES)


def _dot(a, b):
    return jnp.dot(a, b, preferred_element_type=F32)


def _dot_nt(a, b):
    return lax.dot_general(a, b, (((1,), (1,)), ((), ())), preferred_element_type=F32)


def _layer_norm_rows(y, g, b):
    mu = jnp.mean(y, axis=-1, keepdims=True)
    d = y - mu
    var = jnp.mean(d * d, axis=-1, keepdims=True)
    return d * lax.rsqrt(var + LN_EPS) * g + b


def _mm_kernel(x_ref, w_ref, o_ref):
    o_ref[...] = _dot(x_ref[...].astype(BF16), w_ref[...]).astype(o_ref.dtype)


def _matmul(x, w, *, tm, out_dtype, x_map=None, out_map=None, grid=None, out_shape=None):
    m, k = x.shape
    n = w.shape[1]
    grid = grid or (m // tm,)
    x_map = x_map or (lambda i: (i, 0))
    out_map = out_map or (lambda i: (i, 0))
    out_shape = out_shape or (m, n)
    return pl.pallas_call(
        _mm_kernel,
        grid=grid,
        in_specs=[pl.BlockSpec((tm, k), x_map), pl.BlockSpec((k, n), lambda *a: (0, 0))],
        out_specs=pl.BlockSpec((tm, n), out_map),
        out_shape=jax.ShapeDtypeStruct(out_shape, out_dtype),
        compiler_params=_cparams(*(("parallel",) * len(grid))),
        name="matmul",
    )(x, w)


def _lin_ln_kernel(*refs, n_in, glu):
    xs, ws = refs[:n_in], refs[n_in:2 * n_in]
    res_ref, g_ref, b_ref, o_ref = refs[2 * n_in:]
    acc = _dot(xs[0][...].astype(BF16), ws[0][...])
    for x_ref, w_ref in zip(xs[1:], ws[1:]):
        acc = acc + _dot(x_ref[...].astype(BF16), w_ref[...])
    if glu:
        acc = acc[:, :D_MODEL] * jax.nn.sigmoid(acc[:, D_MODEL:])
    y = DN_ALPHA * res_ref[...] + acc
    o_ref[...] = _layer_norm_rows(y, g_ref[...], b_ref[...])


def _linear_residual_ln(xs, ws, res, g, b, *, tm, glu=False, grid=None, x_maps=None, res_map=None):
    n_rows = res.shape[0]
    grid = grid or (n_rows // tm,)
    x_maps = x_maps or [lambda i: (i, 0)] * len(xs)
    res_map = res_map or (lambda i: (i, 0))
    const = lambda *a: (0, 0)
    in_specs = [pl.BlockSpec((tm, w.shape[0]), m) for w, m in zip(ws, x_maps)]
    in_specs += [pl.BlockSpec(w.shape, const) for w in ws]
    in_specs += [pl.BlockSpec((tm, D_MODEL), res_map), pl.BlockSpec((1, D_MODEL), const),
                 pl.BlockSpec((1, D_MODEL), const)]
    return pl.pallas_call(
        functools.partial(_lin_ln_kernel, n_in=len(xs), glu=glu),
        grid=grid,
        in_specs=in_specs,
        out_specs=pl.BlockSpec((tm, D_MODEL), res_map),
        out_shape=jax.ShapeDtypeStruct((n_rows, D_MODEL), F32),
        compiler_params=_cparams(*(("parallel",) * len(grid))),
        name="linear_residual_ln",
    )(*xs, *ws, res, g.reshape(1, D_MODEL), b.reshape(1, D_MODEL))


_EV_CQ, _EV_KA, _EV_VA, _EV_KI, _EV_QB = 0, 256, 384, 512, 640
_EV_KB = _EV_QB + B_WIDTH
_EV_VB = _EV_KB + B_WIDTH
_EV_COLS = _EV_VB + B_WIDTH


def _rope_tables(seq):
    inv = ROPE_THETA ** (-jnp.arange(ROPE_HALF, dtype=F32) / ROPE_HALF)
    ang = jnp.arange(seq, dtype=F32)[:, None] * inv[None, :]
    cos, sin = jnp.cos(ang), jnp.sin(ang)
    rest = HEAD_DIM - 2 * ROPE_HALF
    zh = jnp.zeros((seq, ROPE_HALF), F32)
    c = jnp.concatenate([cos, cos, jnp.ones((seq, rest), F32)], axis=1)
    s1 = jnp.concatenate([-sin, zh, jnp.zeros((seq, rest), F32)], axis=1)
    s2 = jnp.concatenate([zh, sin, jnp.zeros((seq, rest), F32)], axis=1)
    rep = LANES // HEAD_DIM
    return jnp.tile(c, (1, rep)), jnp.tile(s1, (1, rep)), jnp.tile(s2, (1, rep))


def _even_proj_kernel(x_ref, w_ref, qg_ref, wuq_ref, wuqi_ref, lg_ref, lb_ref, c_ref, s1_ref, s2_ref,
                      qa_ref, qi_ref, ka_ref, va_ref, ki_ref, wi_ref, qb_ref, kb_ref, vb_ref):
    p = _dot(x_ref[...].astype(BF16), w_ref[...])
    c, s1, s2 = c_ref[...], s1_ref[...], s2_ref[...]

    def rope(t):
        return (t * c + pltpu.roll(t, LANES - ROPE_HALF, 1) * s1 + pltpu.roll(t, ROPE_HALF, 1) * s2)

    cq = p[:, _EV_CQ:_EV_CQ + Q_RANK]
    cn = cq * lax.rsqrt(jnp.mean(cq * cq, axis=-1, keepdims=True) + LN_EPS) * qg_ref[...]
    cnb = cn.astype(BF16)
    qa = _dot(cnb, wuq_ref[...])
    qi = _dot(cnb, wuqi_ref[...])
    low = lax.broadcasted_iota(I32, c.shape, 1) < HEAD_DIM
    for j in range(A_WIDTH // LANES):
        sl = slice(j * LANES, (j + 1) * LANES)
        pair = rope(qa[:, sl]) * (HEAD_DIM ** -0.5)
        for e, src in enumerate((pair, pltpu.roll(pair, HEAD_DIM, 1))):
            h = 2 * j + e
            qa_ref[:, h * LANES:(h + 1) * LANES] = jnp.where(low, src, 0.0).astype(BF16)
        qi_ref[:, sl] = (rope(qi[:, sl]) * (IDX_DIM ** -0.5)).astype(BF16)
    kpair = rope(p[:, _EV_KA:_EV_KA + LANES])
    vpair = p[:, _EV_VA:_EV_VA + LANES]
    v_pad = jnp.where(lax.broadcasted_iota(I32, c.shape, 1) == HEAD_DIM, 1.0, 0.0)
    for g, (ks, vs) in enumerate(((kpair, vpair), (pltpu.roll(kpair, HEAD_DIM, 1), pltpu.roll(vpair, HEAD_DIM, 1)))):
        ka_ref[:, g * LANES:(g + 1) * LANES] = jnp.where(low, ks, 0.0).astype(BF16)
        va_ref[:, g * LANES:(g + 1) * LANES] = jnp.where(low, vs, v_pad).astype(BF16)

    t = p[:, _EV_KI:_EV_KI + LANES]
    lane = lax.broadcasted_iota(I32, t.shape, 1)
    is_k = lane < IDX_DIM
    mu = jnp.sum(jnp.where(is_k, t, 0.0), axis=-1, keepdims=True) * (1.0 / IDX_DIM)
    d = jnp.where(is_k, t - mu, 0.0)
    var = jnp.sum(d * d, axis=-1, keepdims=True) * (1.0 / IDX_DIM)
    kin = d * lax.rsqrt(var + LN_EPS) * lg_ref[...] + lb_ref[...]
    ki_ref[...] = rope(kin)[:, :IDX_DIM].astype(BF16)
    wi_ref[...] = t[:, IDX_DIM:IDX_DIM + IDX_HEADS] * (IDX_HEADS ** -0.5)

    qb_ref[...] = (p[:, _EV_QB:_EV_KB] * (HEAD_DIM ** -0.5)).astype(BF16)
    kb_ref[...] = p[:, _EV_KB:_EV_VB].astype(BF16)
    vb_ref[...] = p[:, _EV_VB:_EV_COLS].astype(BF16)


def _even_proj(x2d, bsz, seq, w_in, qnorm_g, w_uq, w_uq_idx, kidx_g, kidx_b):
    n = x2d.shape[0]
    tm = ROW_TILE
    per_seq = seq // tm
    c0 = Q_RANK + 2 * A_KV_HEADS * HEAD_DIM + IDX_DIM + IDX_HEADS
    w_pack = jnp.concatenate(
        [w_in[:, :c0], jnp.zeros((D_MODEL, _EV_QB - c0), w_in.dtype), w_in[:, c0:]], axis=1).astype(BF16)
    pad = LANES - IDX_DIM
    lg = jnp.concatenate([kidx_g, jnp.zeros((pad,), F32)]).reshape(1, LANES)
    lb = jnp.concatenate([kidx_b, jnp.zeros((pad,), F32)]).reshape(1, LANES)
    c, s1, s2 = _rope_tables(seq)
    row = lambda i: (i, 0)
    const = lambda i: (0, 0)
    pos = lambda i: (i % per_seq, 0)
    head_shape = jax.ShapeDtypeStruct((n, B_WIDTH), BF16)
    head_spec = pl.BlockSpec((tm, B_WIDTH), row)
    return pl.pallas_call(
        _even_proj_kernel,
        grid=(n // tm,),
        in_specs=[pl.BlockSpec((tm, D_MODEL), row), pl.BlockSpec((D_MODEL, _EV_COLS), const),
                  pl.BlockSpec((1, Q_RANK), const), pl.BlockSpec((Q_RANK, A_WIDTH), const),
                  pl.BlockSpec((Q_RANK, IDX_HEADS * IDX_DIM), const),
                  pl.BlockSpec((1, LANES), const), pl.BlockSpec((1, LANES), const),
                  pl.BlockSpec((tm, LANES), pos), pl.BlockSpec((tm, LANES), pos), pl.BlockSpec((tm, LANES), pos)],
        out_specs=[pl.BlockSpec((tm, A_HEADS * LANES), row), pl.BlockSpec((tm, IDX_HEADS * IDX_DIM), row),
                   pl.BlockSpec((tm, A_KV_HEADS * LANES), row), pl.BlockSpec((tm, A_KV_HEADS * LANES), row),
                   pl.BlockSpec((tm, IDX_DIM), row), pl.BlockSpec((tm, IDX_HEADS), row),
                   head_spec, head_spec, head_spec],
        out_shape=[jax.ShapeDtypeStruct((n, A_HEADS * LANES), BF16), jax.ShapeDtypeStruct((n, IDX_HEADS * IDX_DIM), BF16),
                   jax.ShapeDtypeStruct((n, A_KV_HEADS * LANES), BF16),
                   jax.ShapeDtypeStruct((n, A_KV_HEADS * LANES), BF16),
                   jax.ShapeDtypeStruct((n, IDX_DIM), BF16), jax.ShapeDtypeStruct((n, IDX_HEADS), F32),
                   head_shape, head_shape, head_shape],
        compiler_params=_cparams("parallel"),
        name="even_proj",
    )(x2d, w_pack, qnorm_g.reshape(1, Q_RANK), w_uq.astype(BF16), w_uq_idx.astype(BF16), lg, lb, c, s1, s2)


def _key_to_float(key):
    bits = key ^ ((key >> 31) & jnp.int32(0x7FFFFFFF))
    return lax.bitcast_convert_type(bits, F32)


def _dsa_kernel(qa_ref, qi_ref, wit_ref, ki_ref, ka_ref, vat_ref, o_ref, sc_scr, *, topk, ts, ta):
    seq = sc_scr.shape[0]
    qb = pl.program_id(1)
    q0 = qb * Q_BLOCK
    nkt = (q0 + Q_BLOCK - 1) // ts + 1
    t_row = q0 + lax.broadcasted_iota(I32, (1, Q_BLOCK), 1)
    key = lax.broadcasted_iota(I32, (ts, Q_BLOCK), 0)
    kf = jnp.float32(topk)

    qi = qi_ref[0]
    qs = jnp.concatenate([qi[:, h * IDX_DIM:(h + 1) * IDX_DIM] for h in range(IDX_HEADS)], axis=0)
    wit = wit_ref[0]

    def score_tile(kt, carry):
        off = pl.multiple_of(kt * ts, ts)
        s_all = _dot_nt(ki_ref[0, pl.ds(off, ts), :], qs)
        acc = jnp.zeros((ts, Q_BLOCK), F32)
        for h in range(IDX_HEADS):
            acc = acc + jnp.maximum(s_all[:, h * Q_BLOCK:(h + 1) * Q_BLOCK], 0.0) * wit[h:h + 1, :]
        sc_scr[pl.ds(off, ts), :] = jnp.where(off + key <= t_row, acc, -jnp.inf)
        return carry

    lax.fori_loop(0, nkt, score_tile, 0)

    def count(pred):
        def body(kt, acc):
            off = pl.multiple_of(kt * ts, ts)
            ind = pred(sc_scr[pl.ds(off, ts), :], off + key)
            return acc + jnp.sum(ind.reshape(ts // DSA_COUNT_ROWS, DSA_COUNT_ROWS, Q_BLOCK), axis=0)
        acc = lax.fori_loop(0, nkt, body, jnp.zeros((DSA_COUNT_ROWS, Q_BLOCK), F32))
        return jnp.sum(acc, axis=0, keepdims=True)

    def bit_step(i, base):
        cand = base + jnp.left_shift(jnp.int32(1), 31 - i)
        cf = _key_to_float(cand)
        cnt = count(lambda sc, idx: jnp.where(sc >= cf, 1.0, 0.0))
        return jnp.where(cnt >= kf, cand, base)

    base = lax.fori_loop(0, 32, bit_step, jnp.full((1, Q_BLOCK), INT_MIN, I32))
    thr = jnp.where(base == INT_MIN, -jnp.inf, _key_to_float(base))

    cnt_ge = count(lambda sc, idx: jnp.where(sc >= thr, 1.0, 0.0))
    tied = jnp.logical_and(cnt_ge > kf, thr > -jnp.inf)
    any_tied = jnp.max(jnp.where(tied, 1.0, 0.0)) > 0.0
    seq_bits = max(1, int(math.ceil(math.log2(seq))))

    def tie_cut():
        cnt_gt = count(lambda sc, idx: jnp.where(sc > thr, 1.0, 0.0))
        need = kf - cnt_gt

        def idx_step(i, pos):
            cand = pos + jnp.left_shift(jnp.int32(1), seq_bits - 1 - i)
            cnt = count(lambda sc, idx: jnp.where(sc == thr, jnp.where(idx < cand, 1.0, 0.0), 0.0))
            return jnp.where(cnt < need, cand, pos)

        return lax.fori_loop(0, seq_bits, idx_step, jnp.zeros((1, Q_BLOCK), I32))

    cut = lax.cond(any_tied, tie_cut, lambda: jnp.full((1, Q_BLOCK), seq, I32))
    cut = jnp.where(tied, cut, seq)

    nkt_a = (q0 + Q_BLOCK - 1) // ta + 1
    key_a = lax.broadcasted_iota(I32, (ta, Q_BLOCK), 0)
    cols = A_REP * Q_BLOCK
    qg = [jnp.concatenate([qa_ref[0, :, (g * A_REP + r) * LANES:(g * A_REP + r + 1) * LANES]
                           for r in range(A_REP)], axis=0) for g in range(A_KV_HEADS)]

    def att_pair(i, carry):
        offs = [pl.multiple_of((2 * i + e) * ta, ta) for e in range(2)]
        logits = [_dot_nt(ka_ref[0, pl.ds(offs[e], ta), g * LANES:(g + 1) * LANES], qg[g])
                  for e in range(2) for g in range(A_KV_HEADS)]
        out = []
        for e in range(2):
            sc = sc_scr[pl.ds(offs[e], ta), :]
            idx = offs[e] + key_a
            keep = jnp.where(sc > thr, 0.0, jnp.where(sc == thr, jnp.where(idx <= cut, 0.0, NEG_BIG), NEG_BIG))
            bias = jnp.where(idx <= t_row, keep, NEG_BIG)
            bias = jnp.concatenate([bias] * A_REP, axis=1)
            for g in range(A_KV_HEADS):
                m, acc = carry[e * A_KV_HEADS + g]
                s = logits[e * A_KV_HEADS + g] + bias
                m_new = jnp.maximum(m, jnp.max(s, axis=0, keepdims=True))
                p = jnp.exp(s - m_new)
                vt = vat_ref[0, g * LANES:(g + 1) * LANES, pl.ds(offs[e], ta)]
                out.append((m_new, jnp.exp(m - m_new) * acc + _dot(vt, p.astype(BF16))))
        return tuple(out)

    init = tuple((jnp.full((1, cols), NEG_BIG, F32), jnp.zeros((LANES, cols), F32))
                 for _ in range(2 * A_KV_HEADS))
    final = lax.fori_loop(0, (nkt_a + 1) // 2, att_pair, init)
    low = lax.broadcasted_iota(I32, (Q_BLOCK, LANES), 1) < HEAD_DIM
    outs = []
    for g in range(A_KV_HEADS):
        (m0, acc0), (m1, acc1) = final[g], final[A_KV_HEADS + g]
        m = jnp.maximum(m0, m1)
        acc = jnp.exp(m0 - m) * acc0 + jnp.exp(m1 - m) * acc1
        og = acc / acc[HEAD_DIM:HEAD_DIM + 1, :]
        outs += [og[:, r * Q_BLOCK:(r + 1) * Q_BLOCK].T for r in range(A_REP)]
    for j in range(A_HEADS // 2):
        pair = jnp.where(low, outs[2 * j], pltpu.roll(outs[2 * j + 1], HEAD_DIM, 1))
        o_ref[0, :, j * LANES:(j + 1) * LANES] = pair.astype(BF16)


def _dsa_attention(qa, qi, wi, ki, ka, va, bsz, seq):
    topk = min(IDX_TOPK, seq // 4)
    ts = min(DSA_KEY_TILE, seq)
    ta = min(DSA_ATT_TILE, seq // 2)
    assert seq % (2 * ta) == 0 and seq % ts == 0, "the attention loop walks the key tiles in pairs"
    blk = lambda b, i: (b, i, 0)
    full = lambda b, i: (b, 0, 0)
    r3 = lambda a: a.reshape(bsz, seq, a.shape[-1])
    wit = jnp.swapaxes(r3(wi), 1, 2)
    vat = jnp.swapaxes(r3(va), 1, 2)
    return pl.pallas_call(
        functools.partial(_dsa_kernel, topk=topk, ts=ts, ta=ta),
        grid=(bsz, seq // Q_BLOCK),
        in_specs=[pl.BlockSpec((1, Q_BLOCK, A_HEADS * LANES), blk),
                  pl.BlockSpec((1, Q_BLOCK, IDX_HEADS * IDX_DIM), blk),
                  pl.BlockSpec((1, IDX_HEADS, Q_BLOCK), lambda b, i: (b, 0, i)),
                  pl.BlockSpec((1, seq, IDX_DIM), full),
                  pl.BlockSpec((1, seq, A_KV_HEADS * LANES), full),
                  pl.BlockSpec((1, A_KV_HEADS * LANES, seq), full)],
        out_specs=pl.BlockSpec((1, Q_BLOCK, A_WIDTH), blk),
        out_shape=jax.ShapeDtypeStruct((bsz, seq, A_WIDTH), BF16),
        scratch_shapes=[pltpu.VMEM((seq, Q_BLOCK), F32)],
        compiler_params=_cparams("parallel", "parallel"),
        name="dsa_attention",
    )(r3(qa), r3(qi), wit, r3(ki), r3(ka), vat).reshape(bsz * seq, A_WIDTH)


def _sb_kernel(q_ref, k_ref, v_ref, u_ref, o_ref, acc_scr, run_scr, *, tk):
    q0 = pl.program_id(1) * Q_BLOCK
    t_col = q0 + lax.broadcasted_iota(I32, (Q_BLOCK, 1), 0)
    lane = lax.broadcasted_iota(I32, (Q_BLOCK, tk), 1)
    low = lax.broadcasted_iota(I32, (Q_BLOCK, LANES), 1) < HEAD_DIM
    upper = u_ref[...]
    nkt = (q0 + Q_BLOCK - 1) // tk + 1
    q = q_ref[0]
    zero = jnp.zeros((Q_BLOCK, LANES), BF16)
    qm = []
    for h in range(B_HEADS):
        pair = q[:, (h // 2) * LANES:(h // 2 + 1) * LANES]
        qm.append(jnp.where(low, pair, zero) if h % 2 == 0 else jnp.where(low, zero, pair))
    acc_scr[...] = jnp.zeros_like(acc_scr)
    run_scr[...] = jnp.zeros_like(run_scr)

    def cond(carry):
        i, worst = carry
        return jnp.logical_and(i < nkt, worst >= SB_EXIT_LOG)

    def body(carry):
        i, _ = carry
        off = pl.multiple_of((nkt - 1 - i) * tk, tk)
        strict = off + lane < t_col
        worst = None
        for p in range(B_HEADS // 2):
            cols = slice(p * LANES, (p + 1) * LANES)
            kp = k_ref[0, pl.ds(off, tk), cols]
            vp = v_ref[0, pl.ds(off, tk), cols]
            outs = []
            for e in range(2):
                h = 2 * p + e
                run = run_scr[h]
                z = _dot_nt(qm[h], kp)
                softplus = jnp.maximum(z, 0.0) + jnp.log(1.0 + jnp.exp(-jnp.abs(z)))
                log_1mb = jnp.where(strict, -softplus, 0.0)
                hi = log_1mb.astype(BF16)
                lo = (log_1mb - hi.astype(F32)).astype(BF16)
                after = _dot(hi, upper) + _dot(lo, upper) + run
                a = jnp.where(strict, jnp.exp(z - softplus + after), 0.0)
                outs.append(_dot(a.astype(BF16), vp))
                run = run + jnp.sum(log_1mb, axis=1, keepdims=True)
                run_scr[h] = run
                worst = run if worst is None else jnp.maximum(worst, run)
            acc_scr[:, cols] += jnp.where(low, outs[0], outs[1])
        return i + 1, jnp.max(worst)

    lax.while_loop(cond, body, (jnp.int32(0), jnp.float32(0.0)))
    o_ref[0] = acc_scr[...].astype(BF16)


def _stick_breaking(qb, kb, vb, bsz, seq):
    tk = min(SB_KEY_TILE, seq)
    r = lax.broadcasted_iota(I32, (tk, tk), 0)
    c = lax.broadcasted_iota(I32, (tk, tk), 1)
    upper = jnp.where(r > c, 1.0, 0.0).astype(BF16)
    blk = lambda b, i: (b, i, 0)
    full = lambda b, i: (b, 0, 0)
    r3 = lambda a: a.reshape(bsz, seq, B_WIDTH)
    return pl.pallas_call(
        functools.partial(_sb_kernel, tk=tk),
        grid=(bsz, seq // Q_BLOCK),
        in_specs=[pl.BlockSpec((1, Q_BLOCK, B_WIDTH), blk), pl.BlockSpec((1, seq, B_WIDTH), full),
                  pl.BlockSpec((1, seq, B_WIDTH), full), pl.BlockSpec((tk, tk), lambda b, i: (0, 0))],
        out_specs=pl.BlockSpec((1, Q_BLOCK, B_WIDTH), blk),
        out_shape=jax.ShapeDtypeStruct((bsz, seq, B_WIDTH), BF16),
        scratch_shapes=[pltpu.VMEM((Q_BLOCK, B_WIDTH), F32), pltpu.VMEM((B_HEADS, Q_BLOCK, 1), F32)],
        compiler_params=_cparams("parallel", "arbitrary"),
        name="stick_breaking",
    )(r3(qb), r3(kb), r3(vb), upper).reshape(bsz * seq, B_WIDTH)


HALF_D = D_MODEL // 2
U32 = jnp.uint32
HIGH16 = 0xFFFF0000


def _pack_bf16_pairs(x):
    def bits(v):
        return lax.bitcast_convert_type(v.astype(BF16).astype(F32), U32)
    word = (bits(x[:, HALF_D:]) & U32(HIGH16)) | (bits(x[:, :HALF_D]) >> 16)
    return lax.bitcast_convert_type(word, I32)


def _unpack_bf16_pairs(word):
    u = lax.bitcast_convert_type(word, U32)
    lo = lax.bitcast_convert_type(u << 16, F32)
    hi = lax.bitcast_convert_type(u & U32(HIGH16), F32)
    return lo.astype(BF16), hi.astype(BF16)


def _xattn_kernel(h_ref, wq_ref, kv_ref, wo_ref, g_ref, b_ref, o_ref, packed_ref):
    h = h_ref[...]
    q = (_dot(h.astype(BF16), wq_ref[...]) * (XA_HEAD_DIM ** -0.5)).astype(BF16)
    kv = kv_ref[0]
    outs = []
    for hd in range(XA_HEADS):
        sl = slice(hd * XA_HEAD_DIM, (hd + 1) * XA_HEAD_DIM)
        s = _dot_nt(q[:, sl], kv[:, sl])
        p = jnp.exp(s - jnp.max(s, axis=1, keepdims=True))
        vh = kv[:, D_MODEL + hd * XA_HEAD_DIM:D_MODEL + (hd + 1) * XA_HEAD_DIM]
        outs.append((_dot(p.astype(BF16), vh) / jnp.sum(p, axis=1, keepdims=True)).astype(BF16))
    y = _dot(jnp.concatenate(outs, axis=1), wo_ref[...])
    out = _layer_norm_rows(DN_ALPHA * h + y, g_ref[...], b_ref[...])
    o_ref[...] = out
    packed_ref[...] = _pack_bf16_pairs(out)


def _cross_attention_block(h2d, mem, bsz, seq, w_q, w_kv, w_o, g, b):
    tm = ROW_TILE
    per_seq = seq // tm
    mem_len = mem.shape[1]
    kv = _matmul(mem.reshape(bsz * mem_len, D_MODEL), w_kv.astype(BF16), tm=mem_len, out_dtype=BF16)
    kv = kv.reshape(bsz, mem_len, 2 * D_MODEL)
    row = lambda i: (i, 0)
    const = lambda i: (0, 0)
    return pl.pallas_call(
        _xattn_kernel,
        grid=(bsz * per_seq,),
        in_specs=[pl.BlockSpec((tm, D_MODEL), row), pl.BlockSpec((D_MODEL, D_MODEL), const),
                  pl.BlockSpec((1, mem_len, 2 * D_MODEL), lambda i: (i // per_seq, 0, 0)),
                  pl.BlockSpec((D_MODEL, D_MODEL), const),
                  pl.BlockSpec((1, D_MODEL), const), pl.BlockSpec((1, D_MODEL), const)],
        out_specs=[pl.BlockSpec((tm, D_MODEL), row), pl.BlockSpec((tm, HALF_D), row)],
        out_shape=[jax.ShapeDtypeStruct(h2d.shape, F32), jax.ShapeDtypeStruct((h2d.shape[0], HALF_D), I32)],
        compiler_params=_cparams("parallel"),
        name="cross_attention",
    )(h2d, w_q.astype(BF16), kv, w_o.astype(BF16), g.reshape(1, D_MODEL), b.reshape(1, D_MODEL))


def _router_kernel(h_ref, w_ref, b_ref, tri_ref, idx_ref, gate_ref, rank_ref, cnt_ref, run_scr):
    @pl.when(pl.program_id(0) == 0)
    def _():
        run_scr[...] = jnp.zeros_like(run_scr)

    h = h_ref[...]
    hh = h.astype(BF16)
    hl = (h - hh.astype(F32)).astype(BF16)
    w = w_ref[...]
    wh = w.astype(BF16)
    wl = (w - wh.astype(F32)).astype(BF16)
    logits = _dot(hh, wh) + _dot(hl, wh) + _dot(hh, wl) + b_ref[...]
    lane = lax.broadcasted_iota(I32, logits.shape, 1).astype(F32)
    vals, sels = [], []
    onehot = jnp.zeros(logits.shape, F32)
    for k in range(TOP_K):
        m = jnp.max(logits, axis=1, keepdims=True)
        sel = jnp.min(jnp.where(logits == m, lane, float(LANES)), axis=1, keepdims=True)
        idx_ref[:, k:k + 1] = sel.astype(I32)
        vals.append(m)
        sels.append(sel)
        onehot = onehot + jnp.where(lane == sel, 1.0, 0.0)
        logits = jnp.where(lane == sel, -jnp.inf, logits)
    es = [jnp.exp(v - vals[0]) for v in vals]
    tot = es[0] + es[1] + es[2] + es[3]
    for k in range(TOP_K):
        gate_ref[:, k:k + 1] = es[k] / tot

    earlier = _dot(tri_ref[...], onehot.astype(BF16)) + run_scr[...]
    for k in range(TOP_K):
        rank = jnp.sum(jnp.where(lane == sels[k], earlier, 0.0), axis=1, keepdims=True)
        rank_ref[:, k:k + 1] = rank.astype(I32)
    run = run_scr[...] + jnp.sum(onehot, axis=0, keepdims=True)
    run_scr[...] = run
    cnt_ref[...] = run


def _router(h2d, w_router, b_router):
    n = h2d.shape[0]
    tm = 2 * ROW_TILE
    pad = LANES - N_EXPERTS
    w = jnp.concatenate([w_router, jnp.zeros((D_MODEL, pad), F32)], axis=1)
    b = jnp.concatenate([b_router, jnp.full((pad,), NEG_BIG, F32)]).reshape(1, LANES)
    r = lax.broadcasted_iota(I32, (tm, tm), 0)
    c = lax.broadcasted_iota(I32, (tm, tm), 1)
    tri = jnp.where(c < r, 1.0, 0.0).astype(BF16)
    row = lambda i: (i, 0)
    const = lambda i: (0, 0)
    return pl.pallas_call(
        _router_kernel,
        grid=(n // tm,),
        in_specs=[pl.BlockSpec((tm, D_MODEL), row), pl.BlockSpec((D_MODEL, LANES), const),
                  pl.BlockSpec((1, LANES), const), pl.BlockSpec((tm, tm), const)],
        out_specs=[pl.BlockSpec((tm, TOP_K), row), pl.BlockSpec((tm, TOP_K), row),
                   pl.BlockSpec((tm, TOP_K), row), pl.BlockSpec((1, LANES), const)],
        out_shape=[jax.ShapeDtypeStruct((n, TOP_K), I32), jax.ShapeDtypeStruct((n, TOP_K), F32),
                   jax.ShapeDtypeStruct((n, TOP_K), I32), jax.ShapeDtypeStruct((1, LANES), F32)],
        scratch_shapes=[pltpu.VMEM((1, LANES), F32)],
        compiler_params=_cparams("arbitrary"),
        name="moe_router",
    )(h2d, w, b, tri)


def _gather_rows(src, idx):
    n_out = idx.shape[0]
    width = src.shape[1]
    win = SC_GATHER_SLOT_BYTES // (width * src.dtype.itemsize)
    mesh = plsc.VectorSubcoreMesh(core_axis_name="core", subcore_axis_name="subcore")
    n_workers = mesh.num_cores * mesh.num_subcores
    per_worker = n_out // n_workers
    steps = per_worker // win
    assert per_worker * n_workers == n_out and steps * win == per_worker and steps % 2 == 0

    @functools.partial(
        pl.kernel, out_type=jax.ShapeDtypeStruct((n_out, width), src.dtype), mesh=mesh,
        scratch_types=[pltpu.VMEM((per_worker,), I32), pltpu.VMEM((2, win, width), src.dtype),
                       pltpu.SemaphoreType.DMA, pltpu.SemaphoreType.DMA])
    def gather_kernel(src_hbm, idx_hbm, dst_hbm, idx_v, rows_v, sem0, sem1):
        worker = lax.axis_index("subcore") * mesh.num_cores + lax.axis_index("core")
        base = worker * per_worker
        sems = (sem0, sem1)
        pltpu.sync_copy(idx_hbm.at[pl.ds(base, per_worker)], idx_v)

        def gather(step, slot):
            return pltpu.make_async_copy(src_hbm.at[idx_v.at[pl.ds(step * win, win)]], rows_v.at[slot], sems[slot])

        gather(0, 0).start()

        @pl.loop(0, steps, step=2)
        def _(s):
            for slot in range(2):
                step = s + slot
                gather(step, slot).wait()

                @pl.when(step + 1 < steps)
                def _():
                    gather(step + 1, 1 - slot).start()

                pltpu.sync_copy(rows_v.at[slot], dst_hbm.at[pl.ds(base + step * win, win)])

    return gather_kernel(src, idx)


def _expert_kernel(blk_exp_ref, n_used_ref, x_ref, wgu_ref, bgu_ref, wd_ref, bd_ref, o_ref, wgu_bf, wd_bf):
    i = pl.program_id(0)

    @pl.when(jnp.logical_or(i == 0, blk_exp_ref[i] != blk_exp_ref[jnp.maximum(i - 1, 0)]))
    def _():
        wgu_bf[...] = wgu_ref[0, 0].astype(BF16)
        wd_bf[...] = wd_ref[0, 0].astype(BF16)

    @pl.when(i < n_used_ref[0])
    def _():
        x_lo, x_hi = _unpack_bf16_pairs(x_ref[...])
        hgu = _dot(x_lo, wgu_bf[:HALF_D, :]) + _dot(x_hi, wgu_bf[HALF_D:, :]) + bgu_ref[0]
        gate = jnp.minimum(hgu[:, :D_EXPERT], SWIGLU_LIMIT)
        up = jnp.clip(hgu[:, D_EXPERT:], -SWIGLU_LIMIT, SWIGLU_LIMIT)
        act = gate * jax.nn.sigmoid(gate * SWIGLU_ALPHA) * (up + 1.0)
        o_ref[...] = _dot(act.astype(BF16), wd_bf[...]) + bd_ref[0]

    @pl.when(i >= n_used_ref[0])
    def _():
        o_ref[...] = jnp.zeros_like(o_ref)


def _expert_mlp(xs, block_exp, n_used, layer, w_gu, b_gu, w_down, b_down):
    n_rows = xs.shape[0]
    bm = MOE_BLOCK_ROWS
    row = lambda i, be, nu: (i, 0)
    exp3 = lambda i, be, nu: (be[i], 0, 0)
    exp4 = lambda i, be, nu: (layer, be[i], 0, 0)
    grid_spec = pltpu.PrefetchScalarGridSpec(
        num_scalar_prefetch=2,
        grid=(n_rows // bm,),
        in_specs=[pl.BlockSpec((bm, HALF_D), row),
                  pl.BlockSpec((1, 1, D_MODEL, 2 * D_EXPERT), exp4), pl.BlockSpec((1, 1, 2 * D_EXPERT), exp3),
                  pl.BlockSpec((1, 1, D_EXPERT, D_MODEL), exp4), pl.BlockSpec((1, 1, D_MODEL), exp3)],
        out_specs=pl.BlockSpec((bm, D_MODEL), row),
        scratch_shapes=[pltpu.VMEM((D_MODEL, 2 * D_EXPERT), BF16), pltpu.VMEM((D_EXPERT, D_MODEL), BF16)],
    )
    return pl.pallas_call(
        _expert_kernel,
        grid_spec=grid_spec,
        out_shape=jax.ShapeDtypeStruct((n_rows, D_MODEL), F32),
        compiler_params=_cparams("arbitrary"),
        name="moe_experts",
    )(block_exp, n_used, xs, w_gu, b_gu.reshape(N_EXPERTS, 1, 2 * D_EXPERT),
      w_down, b_down.reshape(N_EXPERTS, 1, D_MODEL))


def _combine_kernel(y0_ref, y1_ref, y2_ref, y3_ref, gate_ref, res_ref, g_ref, b_ref, o_ref):
    gates = gate_ref[...]
    acc = y0_ref[...] * gates[:, 0:1]
    for k, y_ref in enumerate((y1_ref, y2_ref, y3_ref), start=1):
        acc = acc + y_ref[...] * gates[:, k:k + 1]
    o_ref[...] = _layer_norm_rows(DN_ALPHA * res_ref[...] + acc, g_ref[...], b_ref[...])


def _moe_block(h2d, h_packed, layer, w_router, b_router, w_gu, b_gu, w_down, b_down, g, b):
    n = h2d.shape[0]
    n_slots = n * TOP_K
    bm = MOE_BLOCK_ROWS
    top_idx, gates, rank, totals = _router(h2d, w_router, b_router)

    e_flat = top_idx.reshape(-1)
    order = jnp.argsort(e_flat).astype(I32)
    counts = totals[0, :N_EXPERTS].astype(I32)
    padded = (counts + bm - 1) // bm * bm
    start = jnp.cumsum(counts) - counts
    ends_p = jnp.cumsum(padded)
    pstart = ends_p - padded
    n_rows = n_slots + N_EXPERTS * bm
    n_blocks = n_rows // bm
    r = jnp.arange(n_rows, dtype=I32)
    e_r = jnp.minimum(jnp.searchsorted(ends_p, r, side="right"), N_EXPERTS - 1).astype(I32)
    j = r - pstart[e_r]
    valid = j < counts[e_r]
    slot_of_row = order[jnp.where(valid, start[e_r] + j, 0)]
    rows_tok = jnp.where(valid, slot_of_row // TOP_K, 0).astype(I32)
    slot_pos = pstart[e_flat] + rank.reshape(-1)
    block_exp = e_r[::bm]
    n_used = (ends_p[-1] // bm).astype(I32).reshape(1)

    xs = _gather_rows(h_packed, rows_tok)
    ys = _expert_mlp(xs, block_exp, n_used, layer, w_gu, b_gu, w_down, b_down)
    yk = _gather_rows(ys, slot_pos.reshape(n, TOP_K).T.reshape(-1))

    tm = ROW_TILE
    row = lambda i: (i, 0)
    const = lambda i: (0, 0)
    choice = lambda k: (lambda i: (k * (n // tm) + i, 0))
    return pl.pallas_call(
        _combine_kernel,
        grid=(n // tm,),
        in_specs=[pl.BlockSpec((tm, D_MODEL), choice(k)) for k in range(TOP_K)] + [
                  pl.BlockSpec((tm, TOP_K), row),
                  pl.BlockSpec((tm, D_MODEL), row), pl.BlockSpec((1, D_MODEL), const),
                  pl.BlockSpec((1, D_MODEL), const)],
        out_specs=pl.BlockSpec((tm, D_MODEL), row),
        out_shape=jax.ShapeDtypeStruct((n, D_MODEL), F32),
        compiler_params=_cparams("parallel"),
        name="moe_combine",
    )(yk, yk, yk, yk, gates, h2d, g.reshape(1, D_MODEL), b.reshape(1, D_MODEL))


def _s5_kernel(u_ref, bre_ref, bim_ref, cre_ref, cim_ref, are_ref, aim_ref, d_ref, y_ref,
               bu_re, bu_im, st_re, st_im, h_re, h_im, *, bsz):
    @pl.when(pl.program_id(0) == 0)
    def _():
        h_re[...] = jnp.zeros_like(h_re)
        h_im[...] = jnp.zeros_like(h_im)

    rows = u_ref.shape[0]
    first = lax.broadcasted_iota(I32, (SUBLANES, S5_ST_BLK), 0) < bsz
    for j in range(S5_LANE_BLOCKS):
        cin = slice(j * S5_IN_BLK, (j + 1) * S5_IN_BLK)
        cst = slice(j * S5_ST_BLK, (j + 1) * S5_ST_BLK)
        uj = u_ref[:, cin]
        ujb = uj.astype(BF16)
        bu_re[...] = _dot(ujb, bre_ref[j])
        bu_im[...] = _dot(ujb, bim_ref[j])
        ar = jnp.broadcast_to(are_ref[:, cst], (SUBLANES, S5_ST_BLK))
        ai = jnp.broadcast_to(aim_ref[:, cst], (SUBLANES, S5_ST_BLK))

        def step(i, carry):
            hr, hi = carry
            r0 = pl.multiple_of(i * SUBLANES, SUBLANES)
            vr = bu_re[pl.ds(r0, SUBLANES), :]
            vi = bu_im[pl.ds(r0, SUBLANES), :]
            h1r = ar * hr - ai * hi + vr
            h1i = ar * hi + ai * hr + vi
            h1rs = pltpu.roll(h1r, bsz, 0)
            h1is = pltpu.roll(h1i, bsz, 0)
            h2r = ar * h1rs - ai * h1is + vr
            h2i = ar * h1is + ai * h1rs + vi
            st_re[pl.ds(r0, SUBLANES), :] = jnp.where(first, h1r, h2r)
            st_im[pl.ds(r0, SUBLANES), :] = jnp.where(first, h1i, h2i)
            return pltpu.roll(h2r, bsz, 0), pltpu.roll(h2i, bsz, 0)

        hr, hi = lax.fori_loop(0, rows // SUBLANES, step, (h_re[:, cst], h_im[:, cst]))
        h_re[:, cst] = hr
        h_im[:, cst] = hi
        yj = _dot(st_re[...].astype(BF16), cre_ref[j]) + _dot(st_im[...].astype(BF16), cim_ref[j])
        yj = yj + d_ref[:, cin] * uj
        y_ref[:, cin] = jax.nn.gelu(yj).astype(BF16)


def _s5_block_diag(w, n_in, n_out):
    gpb = SSM_GROUPS // S5_LANE_BLOCKS
    w4 = w.reshape(S5_LANE_BLOCKS, gpb, n_in, n_out)
    eye = jnp.eye(gpb, dtype=w.dtype)
    return jnp.einsum("jgio,gh->jgiho", w4, eye).reshape(S5_LANE_BLOCKS, gpb * n_in, gpb * n_out)


def _s5_mixer_block(h2d, bsz, seq, w_in, log_dt, lam_re, lam_im, b_re, b_im, c_re, c_im, d, w_out, g, b):
    assert 2 * bsz == SUBLANES, "the scan packs two time steps of bsz rows into one 8-row tile"
    tm = ROW_TILE
    per_seq = seq // tm
    u_t = _matmul(h2d, w_in.astype(BF16), tm=tm, out_dtype=F32, grid=(bsz, per_seq),
                  x_map=lambda bb, i: (bb * per_seq + i, 0), out_map=lambda bb, i: (i, bb),
                  out_shape=(seq, bsz * D_MODEL)).reshape(seq * bsz, D_MODEL)

    dt = jnp.exp(log_dt)[:, None]
    mag = jnp.exp(lam_re * dt)
    a_re, a_im = mag * jnp.cos(lam_im * dt), mag * jnp.sin(lam_im * dt)
    den = lam_re * lam_re + lam_im * lam_im
    coef_re = ((a_re - 1.0) * lam_re + a_im * lam_im) / den
    coef_im = (a_im * lam_re - (a_re - 1.0) * lam_im) / den
    bb_re = coef_re[..., None] * b_re - coef_im[..., None] * b_im
    bb_im = coef_re[..., None] * b_im + coef_im[..., None] * b_re
    bre = _s5_block_diag(jnp.swapaxes(bb_re, 1, 2), SSM_GROUP, SSM_STATE).astype(BF16)
    bim = _s5_block_diag(jnp.swapaxes(bb_im, 1, 2), SSM_GROUP, SSM_STATE).astype(BF16)
    cre = _s5_block_diag(jnp.swapaxes(c_re, 1, 2), SSM_STATE, SSM_GROUP).astype(BF16)
    cim = _s5_block_diag(jnp.swapaxes(-c_im, 1, 2), SSM_STATE, SSM_GROUP).astype(BF16)
    n_state = SSM_GROUPS * SSM_STATE

    rows = S5_CHUNK * bsz
    row = lambda c: (c, 0)
    c2 = lambda c: (0, 0)
    c3 = lambda c: (0, 0, 0)
    y_t = pl.pallas_call(
        functools.partial(_s5_kernel, bsz=bsz),
        grid=(seq // S5_CHUNK,),
        in_specs=[pl.BlockSpec((rows, D_MODEL), row),
                  pl.BlockSpec(bre.shape, c3), pl.BlockSpec(bim.shape, c3),
                  pl.BlockSpec(cre.shape, c3), pl.BlockSpec(cim.shape, c3),
                  pl.BlockSpec((1, n_state), c2), pl.BlockSpec((1, n_state), c2), pl.BlockSpec((1, D_MODEL), c2)],
        out_specs=pl.BlockSpec((rows, D_MODEL), row),
        out_shape=jax.ShapeDtypeStruct((seq * bsz, D_MODEL), BF16),
        scratch_shapes=[pltpu.VMEM((rows, S5_ST_BLK), F32)] * 4 + [pltpu.VMEM((SUBLANES, n_state), F32)] * 2,
        compiler_params=_cparams("arbitrary"),
        name="s5_scan",
    )(u_t, bre, bim, cre, cim, a_re.reshape(1, n_state), a_im.reshape(1, n_state), d.reshape(1, D_MODEL))

    y2 = y_t.reshape(seq, bsz * D_MODEL)
    return _linear_residual_ln(
        [y2], [w_out.astype(BF16)], h2d, g, b, tm=tm, glu=True, grid=(bsz, per_seq),
        x_maps=[lambda bb, i: (i, bb)], res_map=lambda bb, i: (bb * per_seq + i, 0))


def _even_mixer_block(h2d, bsz, seq, w_in, qnorm_g, w_uq, w_uq_idx, kidx_g, kidx_b, w_out, g, b):
    qa, qi, ka, va, ki, wi, qb, kb, vb = _even_proj(h2d, bsz, seq, w_in, qnorm_g, w_uq, w_uq_idx, kidx_g, kidx_b)
    o_a = _dsa_attention(qa, qi, wi, ki, ka, va, bsz, seq)
    o_b = _stick_breaking(qb, kb, vb, bsz, seq)
    w_out = w_out.astype(BF16)
    return _linear_residual_ln([o_a, o_b], [w_out[:A_WIDTH], w_out[A_WIDTH:]], h2d, g, b, tm=ROW_TILE)


def kernel(x, mem, ev_w_in, ev_qnorm_g, ev_w_uq, ev_w_uq_idx, ev_kidx_ln_g, ev_kidx_ln_b, ev_w_out, od_w_in, od_log_dt, od_lambda_re, od_lambda_im, od_b_re, od_b_im, od_c_re, od_c_im, od_d, od_w_out, mix_ln_g, mix_ln_b, xa_w_q, xa_w_kv, xa_w_o, xa_ln_g, xa_ln_b, moe_w_router, moe_b_router, moe_w_gu, moe_b_gu, moe_w_down, moe_b_down, ffn_ln_g, ffn_ln_b):
    bsz, seq, _ = x.shape
    h = x.reshape(bsz * seq, D_MODEL)
    for layer in range(DEPTH):
        j = layer // 2
        if layer % 2 == 0:
            h = _even_mixer_block(h, bsz, seq, ev_w_in[j], ev_qnorm_g[j], ev_w_uq[j], ev_w_uq_idx[j],
                                  ev_kidx_ln_g[j], ev_kidx_ln_b[j], ev_w_out[j], mix_ln_g[layer], mix_ln_b[layer])
        else:
            h = _s5_mixer_block(h, bsz, seq, od_w_in[j], od_log_dt[j], od_lambda_re[j], od_lambda_im[j],
                                od_b_re[j], od_b_im[j], od_c_re[j], od_c_im[j], od_d[j], od_w_out[j],
                                mix_ln_g[layer], mix_ln_b[layer])
        h, h_packed = _cross_attention_block(h, mem, bsz, seq, xa_w_q[layer], xa_w_kv[layer], xa_w_o[layer],
                                             xa_ln_g[layer], xa_ln_b[layer])
        h = _moe_block(h, h_packed, layer, moe_w_router[layer], moe_b_router[layer], moe_w_gu, moe_b_gu[layer],
                       moe_w_down, moe_b_down[layer], ffn_ln_g[layer], ffn_ln_b[layer])
    return h.reshape(bsz, seq, D_MODEL)
```

```python
import functools
import math

import jax
import jax.numpy as jnp
from jax import lax
from jax.experimental import pallas as pl
from jax.experimental.pallas import tpu as pltpu
from jax.experimental.pallas import tpu_sc as plsc

F32 = jnp.float32
BF16 = jnp.bfloat16
I32 = jnp.int32

D_MODEL = 1024
DEPTH = 2
HEAD_DIM = 64
A_HEADS = 8
A_KV_HEADS = 2
A_REP = A_HEADS // A_KV_HEADS
Q_RANK = 256
IDX_HEADS = 8
IDX_DIM = 64
IDX_TOPK = 256
B_HEADS = 8
A_WIDTH = A_HEADS * HEAD_DIM
B_WIDTH = B_HEADS * HEAD_DIM
SSM_GROUP = 16
SSM_GROUPS = D_MODEL // SSM_GROUP
SSM_STATE = 64
XA_HEADS = 4
XA_HEAD_DIM = D_MODEL // XA_HEADS
N_EXPERTS = 32
TOP_K = 4
D_EXPERT = D_MODEL
SWIGLU_LIMIT = 7.0
SWIGLU_ALPHA = 1.702
ROPE_THETA = 500000.0
ROPE_HALF = HEAD_DIM // 8
LN_EPS = 1e-5
DN_ALPHA = (2 * DEPTH) ** 0.25

LANES = 128
SUBLANES = 8
VMEM_LIMIT_BYTES = 56 * 1024 * 1024

Q_BLOCK = 256
DSA_KEY_TILE = 512
DSA_ATT_TILE = 256
DSA_COUNT_ROWS = 8 * SUBLANES
SB_KEY_TILE = 256
ROW_TILE = 256
MOE_BLOCK_ROWS = 512
SC_GATHER_SLOT_BYTES = 128 * 1024
S5_CHUNK = 128
S5_LANE_BLOCKS = 4
S5_IN_BLK = D_MODEL // S5_LANE_BLOCKS
S5_ST_BLK = SSM_GROUPS * SSM_STATE // S5_LANE_BLOCKS

SB_EXIT_LOG = -104.0
NEG_BIG = -1e30
INT_MIN = -(2 ** 31)


def _cparams(*sem):
    return pltpu.CompilerParams(dimension_semantics=sem, vmem_limit_bytes=VMEM_LIMIT_BYTES)


def _dot(a, b):
    return jnp.dot(a, b, preferred_element_type=F32)


def _dot_nt(a, b):
    return lax.dot_general(a, b, (((1,), (1,)), ((), ())), preferred_element_type=F32)


def _layer_norm_rows(y, g, b):
    mu = jnp.mean(y, axis=-1, keepdims=True)
    d = y - mu
    var = jnp.mean(d * d, axis=-1, keepdims=True)
    return d * lax.rsqrt(var + LN_EPS) * g + b


def _mm_kernel(x_ref, w_ref, o_ref):
    o_ref[...] = _dot(x_ref[...].astype(BF16), w_ref[...]).astype(o_ref.dtype)


def _matmul(x, w, *, tm, out_dtype, x_map=None, out_map=None, grid=None, out_shape=None):
    m, k = x.shape
    n = w.shape[1]
    grid = grid or (m // tm,)
    x_map = x_map or (lambda i: (i, 0))
    out_map = out_map or (lambda i: (i, 0))
    out_shape = out_shape or (m, n)
    return pl.pallas_call(
        _mm_kernel,
        grid=grid,
        in_specs=[pl.BlockSpec((tm, k), x_map), pl.BlockSpec((k, n), lambda *a: (0, 0))],
        out_specs=pl.BlockSpec((tm, n), out_map),
        out_shape=jax.ShapeDtypeStruct(out_shape, out_dtype),
        compiler_params=_cparams(*(("parallel",) * len(grid))),
        name="matmul",
    )(x, w)


def _lin_ln_kernel(*refs, n_in, glu):
    xs, ws = refs[:n_in], refs[n_in:2 * n_in]
    res_ref, g_ref, b_ref, o_ref = refs[2 * n_in:]
    acc = _dot(xs[0][...].astype(BF16), ws[0][...])
    for x_ref, w_ref in zip(xs[1:], ws[1:]):
        acc = acc + _dot(x_ref[...].astype(BF16), w_ref[...])
    if glu:
        acc = acc[:, :D_MODEL] * jax.nn.sigmoid(acc[:, D_MODEL:])
    y = DN_ALPHA * res_ref[...] + acc
    o_ref[...] = _layer_norm_rows(y, g_ref[...], b_ref[...])


def _linear_residual_ln(xs, ws, res, g, b, *, tm, glu=False, grid=None, x_maps=None, res_map=None):
    n_rows = res.shape[0]
    grid = grid or (n_rows // tm,)
    x_maps = x_maps or [lambda i: (i, 0)] * len(xs)
    res_map = res_map or (lambda i: (i, 0))
    const = lambda *a: (0, 0)
    in_specs = [pl.BlockSpec((tm, w.shape[0]), m) for w, m in zip(ws, x_maps)]
    in_specs += [pl.BlockSpec(w.shape, const) for w in ws]
    in_specs += [pl.BlockSpec((tm, D_MODEL), res_map), pl.BlockSpec((1, D_MODEL), const),
                 pl.BlockSpec((1, D_MODEL), const)]
    return pl.pallas_call(
        functools.partial(_lin_ln_kernel, n_in=len(xs), glu=glu),
        grid=grid,
        in_specs=in_specs,
        out_specs=pl.BlockSpec((tm, D_MODEL), res_map),
        out_shape=jax.ShapeDtypeStruct((n_rows, D_MODEL), F32),
        compiler_params=_cparams(*(("parallel",) * len(grid))),
        name="linear_residual_ln",
    )(*xs, *ws, res, g.reshape(1, D_MODEL), b.reshape(1, D_MODEL))


_EV_CQ, _EV_KA, _EV_VA, _EV_KI, _EV_QB = 0, 256, 384, 512, 640
_EV_KB = _EV_QB + B_WIDTH
_EV_VB = _EV_KB + B_WIDTH
_EV_COLS = _EV_VB + B_WIDTH


def _rope_tables(seq):
    inv = ROPE_THETA ** (-jnp.arange(ROPE_HALF, dtype=F32) / ROPE_HALF)
    ang = jnp.arange(seq, dtype=F32)[:, None] * inv[None, :]
    cos, sin = jnp.cos(ang), jnp.sin(ang)
    rest = HEAD_DIM - 2 * ROPE_HALF
    zh = jnp.zeros((seq, ROPE_HALF), F32)
    c = jnp.concatenate([cos, cos, jnp.ones((seq, rest), F32)], axis=1)
    s1 = jnp.concatenate([-sin, zh, jnp.zeros((seq, rest), F32)], axis=1)
    s2 = jnp.concatenate([zh, sin, jnp.zeros((seq, rest), F32)], axis=1)
    rep = LANES // HEAD_DIM
    return jnp.tile(c, (1, rep)), jnp.tile(s1, (1, rep)), jnp.tile(s2, (1, rep))


def _even_proj_kernel(x_ref, w_ref, qg_ref, wuq_ref, wuqi_ref, lg_ref, lb_ref, c_ref, s1_ref, s2_ref,
                      qa_ref, qi_ref, ka_ref, va_ref, ki_ref, wi_ref, qb_ref, kb_ref, vb_ref):
    p = _dot(x_ref[...].astype(BF16), w_ref[...])
    c, s1, s2 = c_ref[...], s1_ref[...], s2_ref[...]

    def rope(t):
        return (t * c + pltpu.roll(t, LANES - ROPE_HALF, 1) * s1 + pltpu.roll(t, ROPE_HALF, 1) * s2)

    cq = p[:, _EV_CQ:_EV_CQ + Q_RANK]
    cn = cq * lax.rsqrt(jnp.mean(cq * cq, axis=-1, keepdims=True) + LN_EPS) * qg_ref[...]
    cnb = cn.astype(BF16)
    qa = _dot(cnb, wuq_ref[...])
    qi = _dot(cnb, wuqi_ref[...])
    low = lax.broadcasted_iota(I32, c.shape, 1) < HEAD_DIM
    for j in range(A_WIDTH // LANES):
        sl = slice(j * LANES, (j + 1) * LANES)
        pair = rope(qa[:, sl]) * (HEAD_DIM ** -0.5)
        for e, src in enumerate((pair, pltpu.roll(pair, HEAD_DIM, 1))):
            h = 2 * j + e
            qa_ref[:, h * LANES:(h + 1) * LANES] = jnp.where(low, src, 0.0).astype(BF16)
        qi_ref[:, sl] = (rope(qi[:, sl]) * (IDX_DIM ** -0.5)).astype(BF16)
    kpair = rope(p[:, _EV_KA:_EV_KA + LANES])
    vpair = p[:, _EV_VA:_EV_VA + LANES]
    v_pad = jnp.where(lax.broadcasted_iota(I32, c.shape, 1) == HEAD_DIM, 1.0, 0.0)
    for g, (ks, vs) in enumerate(((kpair, vpair), (pltpu.roll(kpair, HEAD_DIM, 1), pltpu.roll(vpair, HEAD_DIM, 1)))):
        ka_ref[:, g * LANES:(g + 1) * LANES] = jnp.where(low, ks, 0.0).astype(BF16)
        va_ref[:, g * LANES:(g + 1) * LANES] = jnp.where(low, vs, v_pad).astype(BF16)

    t = p[:, _EV_KI:_EV_KI + LANES]
    lane = lax.broadcasted_iota(I32, t.shape, 1)
    is_k = lane < IDX_DIM
    mu = jnp.sum(jnp.where(is_k, t, 0.0), axis=-1, keepdims=True) * (1.0 / IDX_DIM)
    d = jnp.where(is_k, t - mu, 0.0)
    var = jnp.sum(d * d, axis=-1, keepdims=True) * (1.0 / IDX_DIM)
    kin = d * lax.rsqrt(var + LN_EPS) * lg_ref[...] + lb_ref[...]
    ki_ref[...] = rope(kin)[:, :IDX_DIM].astype(BF16)
    wi_ref[...] = t[:, IDX_DIM:IDX_DIM + IDX_HEADS] * (IDX_HEADS ** -0.5)

    qb_ref[...] = (p[:, _EV_QB:_EV_KB] * (HEAD_DIM ** -0.5)).astype(BF16)
    kb_ref[...] = p[:, _EV_KB:_EV_VB].astype(BF16)
    vb_ref[...] = p[:, _EV_VB:_EV_COLS].astype(BF16)


def _even_proj(x2d, bsz, seq, w_in, qnorm_g, w_uq, w_uq_idx, kidx_g, kidx_b):
    n = x2d.shape[0]
    tm = ROW_TILE
    per_seq = seq // tm
    c0 = Q_RANK + 2 * A_KV_HEADS * HEAD_DIM + IDX_DIM + IDX_HEADS
    w_pack = jnp.concatenate(
        [w_in[:, :c0], jnp.zeros((D_MODEL, _EV_QB - c0), w_in.dtype), w_in[:, c0:]], axis=1).astype(BF16)
    pad = LANES - IDX_DIM
    lg = jnp.concatenate([kidx_g, jnp.zeros((pad,), F32)]).reshape(1, LANES)
    lb = jnp.concatenate([kidx_b, jnp.zeros((pad,), F32)]).reshape(1, LANES)
    c, s1, s2 = _rope_tables(seq)
    row = lambda i: (i, 0)
    const = lambda i: (0, 0)
    pos = lambda i: (i % per_seq, 0)
    head_shape = jax.ShapeDtypeStruct((n, B_WIDTH), BF16)
    head_spec = pl.BlockSpec((tm, B_WIDTH), row)
    return pl.pallas_call(
        _even_proj_kernel,
        grid=(n // tm,),
        in_specs=[pl.BlockSpec((tm, D_MODEL), row), pl.BlockSpec((D_MODEL, _EV_COLS), const),
                  pl.BlockSpec((1, Q_RANK), const), pl.BlockSpec((Q_RANK, A_WIDTH), const),
                  pl.BlockSpec((Q_RANK, IDX_HEADS * IDX_DIM), const),
                  pl.BlockSpec((1, LANES), const), pl.BlockSpec((1, LANES), const),
                  pl.BlockSpec((tm, LANES), pos), pl.BlockSpec((tm, LANES), pos), pl.BlockSpec((tm, LANES), pos)],
        out_specs=[pl.BlockSpec((tm, A_HEADS * LANES), row), pl.BlockSpec((tm, IDX_HEADS * IDX_DIM), row),
                   pl.BlockSpec((tm, A_KV_HEADS * LANES), row), pl.BlockSpec((tm, A_KV_HEADS * LANES), row),
                   pl.BlockSpec((tm, IDX_DIM), row), pl.BlockSpec((tm, IDX_HEADS), row),
                   head_spec, head_spec, head_spec],
        out_shape=[jax.ShapeDtypeStruct((n, A_HEADS * LANES), BF16), jax.ShapeDtypeStruct((n, IDX_HEADS * IDX_DIM), BF16),
                   jax.ShapeDtypeStruct((n, A_KV_HEADS * LANES), BF16),
                   jax.ShapeDtypeStruct((n, A_KV_HEADS * LANES), BF16),
                   jax.ShapeDtypeStruct((n, IDX_DIM), BF16), jax.ShapeDtypeStruct((n, IDX_HEADS), F32),
                   head_shape, head_shape, head_shape],
        compiler_params=_cparams("parallel"),
        name="even_proj",
    )(x2d, w_pack, qnorm_g.reshape(1, Q_RANK), w_uq.astype(BF16), w_uq_idx.astype(BF16), lg, lb, c, s1, s2)


def _key_to_float(key):
    bits = key ^ ((key >> 31) & jnp.int32(0x7FFFFFFF))
    return lax.bitcast_convert_type(bits, F32)


def _dsa_kernel(qa_ref, qi_ref, wit_ref, ki_ref, ka_ref, vat_ref, o_ref, sc_scr, *, topk, ts, ta):
    seq = sc_scr.shape[0]
    qb = pl.program_id(1)
    q0 = qb * Q_BLOCK
    nkt = (q0 + Q_BLOCK - 1) // ts + 1
    t_row = q0 + lax.broadcasted_iota(I32, (1, Q_BLOCK), 1)
    key = lax.broadcasted_iota(I32, (ts, Q_BLOCK), 0)
    kf = jnp.float32(topk)

    qi = qi_ref[0]
    qs = jnp.concatenate([qi[:, h * IDX_DIM:(h + 1) * IDX_DIM] for h in range(IDX_HEADS)], axis=0)
    wit = wit_ref[0]

    def score_tile(kt, carry):
        off = pl.multiple_of(kt * ts, ts)
        s_all = _dot_nt(ki_ref[0, pl.ds(off, ts), :], qs)
        acc = jnp.zeros((ts, Q_BLOCK), F32)
        for h in range(IDX_HEADS):
            acc = acc + jnp.maximum(s_all[:, h * Q_BLOCK:(h + 1) * Q_BLOCK], 0.0) * wit[h:h + 1, :]
        sc_scr[pl.ds(off, ts), :] = jnp.where(off + key <= t_row, acc, -jnp.inf)
        return carry

    lax.fori_loop(0, nkt, score_tile, 0)

    def count(pred):
        def body(kt, acc):
            off = pl.multiple_of(kt * ts, ts)
            ind = pred(sc_scr[pl.ds(off, ts), :], off + key)
            return acc + jnp.sum(ind.reshape(ts // DSA_COUNT_ROWS, DSA_COUNT_ROWS, Q_BLOCK), axis=0)
        acc = lax.fori_loop(0, nkt, body, jnp.zeros((DSA_COUNT_ROWS, Q_BLOCK), F32))
        return jnp.sum(acc, axis=0, keepdims=True)

    def bit_step(carry):
        i, base, exact = carry
        cand = base + jnp.left_shift(jnp.int32(1), 31 - i)
        cf = _key_to_float(cand)
        cnt = count(lambda sc, idx: jnp.where(sc >= cf, 1.0, 0.0))
        take = cnt >= kf
        return i + 1, jnp.where(take, cand, base), jnp.where(take, jnp.where(cnt == kf, 1.0, 0.0), exact)

    def search_on(carry):
        return jnp.logical_and(carry[0] < 32, jnp.min(carry[2]) < 1.0)

    few_keys = jnp.where(t_row + 1 <= topk, 1.0, 0.0)
    _, base, _ = lax.while_loop(search_on, bit_step,
                                (jnp.int32(0), jnp.full((1, Q_BLOCK), INT_MIN, I32), few_keys))
    thr = jnp.where(base == INT_MIN, -jnp.inf, _key_to_float(base))

    cnt_ge = count(lambda sc, idx: jnp.where(sc >= thr, 1.0, 0.0))
    tied = jnp.logical_and(cnt_ge > kf, thr > -jnp.inf)
    any_tied = jnp.max(jnp.where(tied, 1.0, 0.0)) > 0.0
    seq_bits = max(1, int(math.ceil(math.log2(seq))))

    def tie_cut():
        cnt_gt = count(lambda sc, idx: jnp.where(sc > thr, 1.0, 0.0))
        need = kf - cnt_gt

        def idx_step(i, pos):
            cand = pos + jnp.left_shift(jnp.int32(1), seq_bits - 1 - i)
            cnt = count(lambda sc, idx: jnp.where(sc == thr, jnp.where(idx < cand, 1.0, 0.0), 0.0))
            return jnp.where(cnt < need, cand, pos)

        return lax.fori_loop(0, seq_bits, idx_step, jnp.zeros((1, Q_BLOCK), I32))

    cut = lax.cond(any_tied, tie_cut, lambda: jnp.full((1, Q_BLOCK), seq, I32))
    cut = jnp.where(tied, cut, seq)

    nkt_a = (q0 + Q_BLOCK - 1) // ta + 1
    key_a = lax.broadcasted_iota(I32, (ta, Q_BLOCK), 0)
    cols = A_REP * Q_BLOCK
    qg = [jnp.concatenate([qa_ref[0, :, (g * A_REP + r) * LANES:(g * A_REP + r + 1) * LANES]
                           for r in range(A_REP)], axis=0) for g in range(A_KV_HEADS)]

    def att_pair(i, carry):
        offs = [pl.multiple_of((2 * i + e) * ta, ta) for e in range(2)]
        logits = [_dot_nt(ka_ref[0, pl.ds(offs[e], ta), g * LANES:(g + 1) * LANES], qg[g])
                  for e in range(2) for g in range(A_KV_HEADS)]
        out = []
        for e in range(2):
            sc = sc_scr[pl.ds(offs[e], ta), :]
            idx = offs[e] + key_a
            keep = jnp.where(sc > thr, 0.0, jnp.where(sc == thr, jnp.where(idx <= cut, 0.0, NEG_BIG), NEG_BIG))
            bias = jnp.where(idx <= t_row, keep, NEG_BIG)
            bias = jnp.concatenate([bias] * A_REP, axis=1)
            for g in range(A_KV_HEADS):
                m, acc = carry[e * A_KV_HEADS + g]
                s = logits[e * A_KV_HEADS + g] + bias
                m_new = jnp.maximum(m, jnp.max(s, axis=0, keepdims=True))
                p = jnp.exp(s - m_new)
                vt = vat_ref[0, g * LANES:(g + 1) * LANES, pl.ds(offs[e], ta)]
                out.append((m_new, jnp.exp(m - m_new) * acc + _dot(vt, p.astype(BF16))))
        return tuple(out)

    init = tuple((jnp.full((1, cols), NEG_BIG, F32), jnp.zeros((LANES, cols), F32))
                 for _ in range(2 * A_KV_HEADS))
    final = lax.fori_loop(0, (nkt_a + 1) // 2, att_pair, init)
    low = lax.broadcasted_iota(I32, (Q_BLOCK, LANES), 1) < HEAD_DIM
    outs = []
    for g in range(A_KV_HEADS):
        (m0, acc0), (m1, acc1) = final[g], final[A_KV_HEADS + g]
        m = jnp.maximum(m0, m1)
        acc = jnp.exp(m0 - m) * acc0 + jnp.exp(m1 - m) * acc1
        og = acc / acc[HEAD_DIM:HEAD_DIM + 1, :]
        outs += [og[:, r * Q_BLOCK:(r + 1) * Q_BLOCK].T for r in range(A_REP)]
    for j in range(A_HEADS // 2):
        pair = jnp.where(low, outs[2 * j], pltpu.roll(outs[2 * j + 1], HEAD_DIM, 1))
        o_ref[0, :, j * LANES:(j + 1) * LANES] = pair.astype(BF16)


def _dsa_attention(qa, qi, wi, ki, ka, va, bsz, seq):
    topk = min(IDX_TOPK, seq // 4)
    ts = min(DSA_KEY_TILE, seq)
    ta = min(DSA_ATT_TILE, seq // 2)
    assert seq % (2 * ta) == 0 and seq % ts == 0, "the attention loop walks the key tiles in pairs"
    blk = lambda b, i: (b, i, 0)
    full = lambda b, i: (b, 0, 0)
    r3 = lambda a: a.reshape(bsz, seq, a.shape[-1])
    wit = jnp.swapaxes(r3(wi), 1, 2)
    vat = jnp.swapaxes(r3(va), 1, 2)
    return pl.pallas_call(
        functools.partial(_dsa_kernel, topk=topk, ts=ts, ta=ta),
        grid=(bsz, seq // Q_BLOCK),
        in_specs=[pl.BlockSpec((1, Q_BLOCK, A_HEADS * LANES), blk),
                  pl.BlockSpec((1, Q_BLOCK, IDX_HEADS * IDX_DIM), blk),
                  pl.BlockSpec((1, IDX_HEADS, Q_BLOCK), lambda b, i: (b, 0, i)),
                  pl.BlockSpec((1, seq, IDX_DIM), full),
                  pl.BlockSpec((1, seq, A_KV_HEADS * LANES), full),
                  pl.BlockSpec((1, A_KV_HEADS * LANES, seq), full)],
        out_specs=pl.BlockSpec((1, Q_BLOCK, A_WIDTH), blk),
        out_shape=jax.ShapeDtypeStruct((bsz, seq, A_WIDTH), BF16),
        scratch_shapes=[pltpu.VMEM((seq, Q_BLOCK), F32)],
        compiler_params=_cparams("parallel", "parallel"),
        name="dsa_attention",
    )(r3(qa), r3(qi), wit, r3(ki), r3(ka), vat).reshape(bsz * seq, A_WIDTH)


def _sb_kernel(q_ref, k_ref, v_ref, u_ref, o_ref, acc_scr, run_scr, *, tk):
    q0 = pl.program_id(1) * Q_BLOCK
    t_col = q0 + lax.broadcasted_iota(I32, (Q_BLOCK, 1), 0)
    lane = lax.broadcasted_iota(I32, (Q_BLOCK, tk), 1)
    low = lax.broadcasted_iota(I32, (Q_BLOCK, LANES), 1) < HEAD_DIM
    upper = u_ref[...]
    nkt = (q0 + Q_BLOCK - 1) // tk + 1
    q = q_ref[0]
    zero = jnp.zeros((Q_BLOCK, LANES), BF16)
    qm = []
    for h in range(B_HEADS):
        pair = q[:, (h // 2) * LANES:(h // 2 + 1) * LANES]
        qm.append(jnp.where(low, pair, zero) if h % 2 == 0 else jnp.where(low, zero, pair))
    acc_scr[...] = jnp.zeros_like(acc_scr)
    run_scr[...] = jnp.zeros_like(run_scr)

    def cond(carry):
        i, worst = carry
        return jnp.logical_and(i < nkt, worst >= SB_EXIT_LOG)

    def body(carry):
        i, _ = carry
        off = pl.multiple_of((nkt - 1 - i) * tk, tk)
        strict = off + lane < t_col
        worst = None
        for p in range(B_HEADS // 2):
            cols = slice(p * LANES, (p + 1) * LANES)
            kp = k_ref[0, pl.ds(off, tk), cols]
            vp = v_ref[0, pl.ds(off, tk), cols]
            outs = []
            for e in range(2):
                h = 2 * p + e
                run = run_scr[h]
                z = _dot_nt(qm[h], kp)
                softplus = jnp.maximum(z, 0.0) + jnp.log(1.0 + jnp.exp(-jnp.abs(z)))
                log_1mb = jnp.where(strict, -softplus, 0.0)
                hi = log_1mb.astype(BF16)
                lo = (log_1mb - hi.astype(F32)).astype(BF16)
                after = _dot(hi, upper) + _dot(lo, upper) + run
                a = jnp.where(strict, jnp.exp(z - softplus + after), 0.0)
                outs.append(_dot(a.astype(BF16), vp))
                run = run + jnp.sum(log_1mb, axis=1, keepdims=True)
                run_scr[h] = run
                worst = run if worst is None else jnp.maximum(worst, run)
            acc_scr[:, cols] += jnp.where(low, outs[0], outs[1])
        return i + 1, jnp.max(worst)

    lax.while_loop(cond, body, (jnp.int32(0), jnp.float32(0.0)))
    o_ref[0] = acc_scr[...].astype(BF16)


def _stick_breaking(qb, kb, vb, bsz, seq):
    tk = min(SB_KEY_TILE, seq)
    r = lax.broadcasted_iota(I32, (tk, tk), 0)
    c = lax.broadcasted_iota(I32, (tk, tk), 1)
    upper = jnp.where(r > c, 1.0, 0.0).astype(BF16)
    blk = lambda b, i: (b, i, 0)
    full = lambda b, i: (b, 0, 0)
    r3 = lambda a: a.reshape(bsz, seq, B_WIDTH)
    return pl.pallas_call(
        functools.partial(_sb_kernel, tk=tk),
        grid=(bsz, seq // Q_BLOCK),
        in_specs=[pl.BlockSpec((1, Q_BLOCK, B_WIDTH), blk), pl.BlockSpec((1, seq, B_WIDTH), full),
                  pl.BlockSpec((1, seq, B_WIDTH), full), pl.BlockSpec((tk, tk), lambda b, i: (0, 0))],
        out_specs=pl.BlockSpec((1, Q_BLOCK, B_WIDTH), blk),
        out_shape=jax.ShapeDtypeStruct((bsz, seq, B_WIDTH), BF16),
        scratch_shapes=[pltpu.VMEM((Q_BLOCK, B_WIDTH), F32), pltpu.VMEM((B_HEADS, Q_BLOCK, 1), F32)],
        compiler_params=_cparams("parallel", "arbitrary"),
        name="stick_breaking",
    )(r3(qb), r3(kb), r3(vb), upper).reshape(bsz * seq, B_WIDTH)


HALF_D = D_MODEL // 2
U32 = jnp.uint32
HIGH16 = 0xFFFF0000


def _pack_bf16_pairs(x):
    def bits(v):
        return lax.bitcast_convert_type(v.astype(BF16).astype(F32), U32)
    word = (bits(x[:, HALF_D:]) & U32(HIGH16)) | (bits(x[:, :HALF_D]) >> 16)
    return lax.bitcast_convert_type(word, I32)


def _unpack_bf16_pairs(word):
    u = lax.bitcast_convert_type(word, U32)
    lo = lax.bitcast_convert_type(u << 16, F32)
    hi = lax.bitcast_convert_type(u & U32(HIGH16), F32)
    return lo.astype(BF16), hi.astype(BF16)


def _xattn_kernel(h_ref, wq_ref, kv_ref, wo_ref, g_ref, b_ref, o_ref, packed_ref):
    h = h_ref[...]
    q = (_dot(h.astype(BF16), wq_ref[...]) * (XA_HEAD_DIM ** -0.5)).astype(BF16)
    kv = kv_ref[0]
    outs = []
    for hd in range(XA_HEADS):
        sl = slice(hd * XA_HEAD_DIM, (hd + 1) * XA_HEAD_DIM)
        s = _dot_nt(q[:, sl], kv[:, sl])
        p = jnp.exp(s - jnp.max(s, axis=1, keepdims=True))
        vh = kv[:, D_MODEL + hd * XA_HEAD_DIM:D_MODEL + (hd + 1) * XA_HEAD_DIM]
        outs.append((_dot(p.astype(BF16), vh) / jnp.sum(p, axis=1, keepdims=True)).astype(BF16))
    y = _dot(jnp.concatenate(outs, axis=1), wo_ref[...])
    out = _layer_norm_rows(DN_ALPHA * h + y, g_ref[...], b_ref[...])
    o_ref[...] = out
    packed_ref[...] = _pack_bf16_pairs(out)


def _cross_attention_block(h2d, mem, bsz, seq, w_q, w_kv, w_o, g, b):
    tm = ROW_TILE
    per_seq = seq // tm
    mem_len = mem.shape[1]
    kv = _matmul(mem.reshape(bsz * mem_len, D_MODEL), w_kv.astype(BF16), tm=mem_len, out_dtype=BF16)
    kv = kv.reshape(bsz, mem_len, 2 * D_MODEL)
    row = lambda i: (i, 0)
    const = lambda i: (0, 0)
    return pl.pallas_call(
        _xattn_kernel,
        grid=(bsz * per_seq,),
        in_specs=[pl.BlockSpec((tm, D_MODEL), row), pl.BlockSpec((D_MODEL, D_MODEL), const),
                  pl.BlockSpec((1, mem_len, 2 * D_MODEL), lambda i: (i // per_seq, 0, 0)),
                  pl.BlockSpec((D_MODEL, D_MODEL), const),
                  pl.BlockSpec((1, D_MODEL), const), pl.BlockSpec((1, D_MODEL), const)],
        out_specs=[pl.BlockSpec((tm, D_MODEL), row), pl.BlockSpec((tm, HALF_D), row)],
        out_shape=[jax.ShapeDtypeStruct(h2d.shape, F32), jax.ShapeDtypeStruct((h2d.shape[0], HALF_D), I32)],
        compiler_params=_cparams("parallel"),
        name="cross_attention",
    )(h2d, w_q.astype(BF16), kv, w_o.astype(BF16), g.reshape(1, D_MODEL), b.reshape(1, D_MODEL))


def _router_kernel(h_ref, w_ref, b_ref, tri_ref, idx_ref, gate_ref, rank_ref, cnt_ref, run_scr):
    @pl.when(pl.program_id(0) == 0)
    def _():
        run_scr[...] = jnp.zeros_like(run_scr)

    h = h_ref[...]
    hh = h.astype(BF16)
    hl = (h - hh.astype(F32)).astype(BF16)
    w = w_ref[...]
    wh = w.astype(BF16)
    wl = (w - wh.astype(F32)).astype(BF16)
    logits = _dot(hh, wh) + _dot(hl, wh) + _dot(hh, wl) + b_ref[...]
    lane = lax.broadcasted_iota(I32, logits.shape, 1).astype(F32)
    vals, sels = [], []
    onehot = jnp.zeros(logits.shape, F32)
    for k in range(TOP_K):
        m = jnp.max(logits, axis=1, keepdims=True)
        sel = jnp.min(jnp.where(logits == m, lane, float(LANES)), axis=1, keepdims=True)
        idx_ref[:, k:k + 1] = sel.astype(I32)
        vals.append(m)
        sels.append(sel)
        onehot = onehot + jnp.where(lane == sel, 1.0, 0.0)
        logits = jnp.where(lane == sel, -jnp.inf, logits)
    es = [jnp.exp(v - vals[0]) for v in vals]
    tot = es[0] + es[1] + es[2] + es[3]
    for k in range(TOP_K):
        gate_ref[:, k:k + 1] = es[k] / tot

    earlier = _dot(tri_ref[...], onehot.astype(BF16)) + run_scr[...]
    for k in range(TOP_K):
        rank = jnp.sum(jnp.where(lane == sels[k], earlier, 0.0), axis=1, keepdims=True)
        rank_ref[:, k:k + 1] = rank.astype(I32)
    run = run_scr[...] + jnp.sum(onehot, axis=0, keepdims=True)
    run_scr[...] = run
    cnt_ref[...] = run


def _router(h2d, w_router, b_router):
    n = h2d.shape[0]
    tm = 2 * ROW_TILE
    pad = LANES - N_EXPERTS
    w = jnp.concatenate([w_router, jnp.zeros((D_MODEL, pad), F32)], axis=1)
    b = jnp.concatenate([b_router, jnp.full((pad,), NEG_BIG, F32)]).reshape(1, LANES)
    r = lax.broadcasted_iota(I32, (tm, tm), 0)
    c = lax.broadcasted_iota(I32, (tm, tm), 1)
    tri = jnp.where(c < r, 1.0, 0.0).astype(BF16)
    row = lambda i: (i, 0)
    const = lambda i: (0, 0)
    return pl.pallas_call(
        _router_kernel,
        grid=(n // tm,),
        in_specs=[pl.BlockSpec((tm, D_MODEL), row), pl.BlockSpec((D_MODEL, LANES), const),
                  pl.BlockSpec((1, LANES), const), pl.BlockSpec((tm, tm), const)],
        out_specs=[pl.BlockSpec((tm, TOP_K), row), pl.BlockSpec((tm, TOP_K), row),
                   pl.BlockSpec((tm, TOP_K), row), pl.BlockSpec((1, LANES), const)],
        out_shape=[jax.ShapeDtypeStruct((n, TOP_K), I32), jax.ShapeDtypeStruct((n, TOP_K), F32),
                   jax.ShapeDtypeStruct((n, TOP_K), I32), jax.ShapeDtypeStruct((1, LANES), F32)],
        scratch_shapes=[pltpu.VMEM((1, LANES), F32)],
        compiler_params=_cparams("arbitrary"),
        name="moe_router",
    )(h2d, w, b, tri)


def _gather_rows(src, idx):
    n_out = idx.shape[0]
    width = src.shape[1]
    win = SC_GATHER_SLOT_BYTES // (width * src.dtype.itemsize)
    mesh = plsc.VectorSubcoreMesh(core_axis_name="core", subcore_axis_name="subcore")
    n_workers = mesh.num_cores * mesh.num_subcores
    per_worker = n_out // n_workers
    steps = per_worker // win
    assert per_worker * n_workers == n_out and steps * win == per_worker and steps % 2 == 0

    @functools.partial(
        pl.kernel, out_type=jax.ShapeDtypeStruct((n_out, width), src.dtype), mesh=mesh,
        scratch_types=[pltpu.VMEM((per_worker,), I32), pltpu.VMEM((2, win, width), src.dtype),
                       pltpu.SemaphoreType.DMA, pltpu.SemaphoreType.DMA])
    def gather_kernel(src_hbm, idx_hbm, dst_hbm, idx_v, rows_v, sem0, sem1):
        worker = lax.axis_index("subcore") * mesh.num_cores + lax.axis_index("core")
        base = worker * per_worker
        sems = (sem0, sem1)
        pltpu.sync_copy(idx_hbm.at[pl.ds(base, per_worker)], idx_v)

        def gather(step, slot):
            return pltpu.make_async_copy(src_hbm.at[idx_v.at[pl.ds(step * win, win)]], rows_v.at[slot], sems[slot])

        gather(0, 0).start()

        @pl.loop(0, steps, step=2)
        def _(s):
            for slot in range(2):
                step = s + slot
                gather(step, slot).wait()

                @pl.when(step + 1 < steps)
                def _():
                    gather(step + 1, 1 - slot).start()

                pltpu.sync_copy(rows_v.at[slot], dst_hbm.at[pl.ds(base + step * win, win)])

    return gather_kernel(src, idx)


def _expert_kernel(blk_exp_ref, n_used_ref, x_ref, wgu_ref, bgu_ref, wd_ref, bd_ref, o_ref, wgu_bf, wd_bf):
    i = pl.program_id(0)

    @pl.when(jnp.logical_or(i == 0, blk_exp_ref[i] != blk_exp_ref[jnp.maximum(i - 1, 0)]))
    def _():
        wgu_bf[...] = wgu_ref[0, 0].astype(BF16)
        wd_bf[...] = wd_ref[0, 0].astype(BF16)

    @pl.when(i < n_used_ref[0])
    def _():
        x_lo, x_hi = _unpack_bf16_pairs(x_ref[...])
        hgu = _dot(x_lo, wgu_bf[:HALF_D, :]) + _dot(x_hi, wgu_bf[HALF_D:, :]) + bgu_ref[0]
        gate = jnp.minimum(hgu[:, :D_EXPERT], SWIGLU_LIMIT)
        up = jnp.clip(hgu[:, D_EXPERT:], -SWIGLU_LIMIT, SWIGLU_LIMIT)
        act = gate * jax.nn.sigmoid(gate * SWIGLU_ALPHA) * (up + 1.0)
        o_ref[...] = _dot(act.astype(BF16), wd_bf[...]) + bd_ref[0]

    @pl.when(i >= n_used_ref[0])
    def _():
        o_ref[...] = jnp.zeros_like(o_ref)


def _expert_mlp(xs, block_exp, n_used, layer, w_gu, b_gu, w_down, b_down):
    n_rows = xs.shape[0]
    bm = MOE_BLOCK_ROWS
    row = lambda i, be, nu: (i, 0)
    exp3 = lambda i, be, nu: (be[i], 0, 0)
    exp4 = lambda i, be, nu: (layer, be[i], 0, 0)
    grid_spec = pltpu.PrefetchScalarGridSpec(
        num_scalar_prefetch=2,
        grid=(n_rows // bm,),
        in_specs=[pl.BlockSpec((bm, HALF_D), row),
                  pl.BlockSpec((1, 1, D_MODEL, 2 * D_EXPERT), exp4), pl.BlockSpec((1, 1, 2 * D_EXPERT), exp3),
                  pl.BlockSpec((1, 1, D_EXPERT, D_MODEL), exp4), pl.BlockSpec((1, 1, D_MODEL), exp3)],
        out_specs=pl.BlockSpec((bm, D_MODEL), row),
        scratch_shapes=[pltpu.VMEM((D_MODEL, 2 * D_EXPERT), BF16), pltpu.VMEM((D_EXPERT, D_MODEL), BF16)],
    )
    return pl.pallas_call(
        _expert_kernel,
        grid_spec=grid_spec,
        out_shape=jax.ShapeDtypeStruct((n_rows, D_MODEL), F32),
        compiler_params=_cparams("arbitrary"),
        name="moe_experts",
    )(block_exp, n_used, xs, w_gu, b_gu.reshape(N_EXPERTS, 1, 2 * D_EXPERT),
      w_down, b_down.reshape(N_EXPERTS, 1, D_MODEL))


def _combine_kernel(y0_ref, y1_ref, y2_ref, y3_ref, gate_ref, res_ref, g_ref, b_ref, o_ref):
    gates = gate_ref[...]
    acc = y0_ref[...] * gates[:, 0:1]
    for k, y_ref in enumerate((y1_ref, y2_ref, y3_ref), start=1):
        acc = acc + y_ref[...] * gates[:, k:k + 1]
    o_ref[...] = _layer_norm_rows(DN_ALPHA * res_ref[...] + acc, g_ref[...], b_ref[...])


def _moe_block(h2d, h_packed, layer, w_router, b_router, w_gu, b_gu, w_down, b_down, g, b):
    n = h2d.shape[0]
    n_slots = n * TOP_K
    bm = MOE_BLOCK_ROWS
    top_idx, gates, rank, totals = _router(h2d, w_router, b_router)

    e_flat = top_idx.reshape(-1)
    order = jnp.argsort(e_flat).astype(I32)
    counts = totals[0, :N_EXPERTS].astype(I32)
    padded = (counts + bm - 1) // bm * bm
    start = jnp.cumsum(counts) - counts
    ends_p = jnp.cumsum(padded)
    pstart = ends_p - padded
    n_rows = n_slots + N_EXPERTS * bm
    n_blocks = n_rows // bm
    r = jnp.arange(n_rows, dtype=I32)
    e_r = jnp.minimum(jnp.searchsorted(ends_p, r, side="right"), N_EXPERTS - 1).astype(I32)
    j = r - pstart[e_r]
    valid = j < counts[e_r]
    slot_of_row = order[jnp.where(valid, start[e_r] + j, 0)]
    rows_tok = jnp.where(valid, slot_of_row // TOP_K, r % n).astype(I32)
    slot_pos = pstart[e_flat] + rank.reshape(-1)
    block_exp = e_r[::bm]
    n_used = (ends_p[-1] // bm).astype(I32).reshape(1)

    xs = _gather_rows(h_packed, rows_tok)
    ys = _expert_mlp(xs, block_exp, n_used, layer, w_gu, b_gu, w_down, b_down)
    yk = _gather_rows(ys, slot_pos.reshape(n, TOP_K).T.reshape(-1))

    tm = ROW_TILE
    row = lambda i: (i, 0)
    const = lambda i: (0, 0)
    choice = lambda k: (lambda i: (k * (n // tm) + i, 0))
    return pl.pallas_call(
        _combine_kernel,
        grid=(n // tm,),
        in_specs=[pl.BlockSpec((tm, D_MODEL), choice(k)) for k in range(TOP_K)] + [
                  pl.BlockSpec((tm, TOP_K), row),
                  pl.BlockSpec((tm, D_MODEL), row), pl.BlockSpec((1, D_MODEL), const),
                  pl.BlockSpec((1, D_MODEL), const)],
        out_specs=pl.BlockSpec((tm, D_MODEL), row),
        out_shape=jax.ShapeDtypeStruct((n, D_MODEL), F32),
        compiler_params=_cparams("parallel"),
        name="moe_combine",
    )(yk, yk, yk, yk, gates, h2d, g.reshape(1, D_MODEL), b.reshape(1, D_MODEL))


def _s5_kernel(u_ref, bre_ref, bim_ref, cre_ref, cim_ref, are_ref, aim_ref, d_ref, y_ref,
               bu_re, bu_im, st_re, st_im, h_re, h_im, *, bsz):
    @pl.when(pl.program_id(0) == 0)
    def _():
        h_re[...] = jnp.zeros_like(h_re)
        h_im[...] = jnp.zeros_like(h_im)

    rows = u_ref.shape[0]
    first = lax.broadcasted_iota(I32, (SUBLANES, S5_ST_BLK), 0) < bsz
    for j in range(S5_LANE_BLOCKS):
        cin = slice(j * S5_IN_BLK, (j + 1) * S5_IN_BLK)
        cst = slice(j * S5_ST_BLK, (j + 1) * S5_ST_BLK)
        uj = u_ref[:, cin]
        ujb = uj.astype(BF16)
        bu_re[...] = _dot(ujb, bre_ref[j])
        bu_im[...] = _dot(ujb, bim_ref[j])
        ar = jnp.broadcast_to(are_ref[:, cst], (SUBLANES, S5_ST_BLK))
        ai = jnp.broadcast_to(aim_ref[:, cst], (SUBLANES, S5_ST_BLK))

        def step(i, carry):
            hr, hi = carry
            r0 = pl.multiple_of(i * SUBLANES, SUBLANES)
            vr = bu_re[pl.ds(r0, SUBLANES), :]
            vi = bu_im[pl.ds(r0, SUBLANES), :]
            h1r = ar * hr - ai * hi + vr
            h1i = ar * hi + ai * hr + vi
            h1rs = pltpu.roll(h1r, bsz, 0)
            h1is = pltpu.roll(h1i, bsz, 0)
            h2r = ar * h1rs - ai * h1is + vr
            h2i = ar * h1is + ai * h1rs + vi
            st_re[pl.ds(r0, SUBLANES), :] = jnp.where(first, h1r, h2r)
            st_im[pl.ds(r0, SUBLANES), :] = jnp.where(first, h1i, h2i)
            return pltpu.roll(h2r, bsz, 0), pltpu.roll(h2i, bsz, 0)

        hr, hi = lax.fori_loop(0, rows // SUBLANES, step, (h_re[:, cst], h_im[:, cst]))
        h_re[:, cst] = hr
        h_im[:, cst] = hi
        yj = _dot(st_re[...].astype(BF16), cre_ref[j]) + _dot(st_im[...].astype(BF16), cim_ref[j])
        yj = yj + d_ref[:, cin] * uj
        y_ref[:, cin] = jax.nn.gelu(yj).astype(BF16)


def _s5_block_diag(w, n_in, n_out):
    gpb = SSM_GROUPS // S5_LANE_BLOCKS
    w4 = w.reshape(S5_LANE_BLOCKS, gpb, n_in, n_out)
    eye = jnp.eye(gpb, dtype=w.dtype)
    return jnp.einsum("jgio,gh->jgiho", w4, eye).reshape(S5_LANE_BLOCKS, gpb * n_in, gpb * n_out)


def _s5_mixer_block(h2d, bsz, seq, w_in, log_dt, lam_re, lam_im, b_re, b_im, c_re, c_im, d, w_out, g, b):
    assert 2 * bsz == SUBLANES, "the scan packs two time steps of bsz rows into one 8-row tile"
    tm = ROW_TILE
    per_seq = seq // tm
    u_t = _matmul(h2d, w_in.astype(BF16), tm=tm, out_dtype=F32, grid=(bsz, per_seq),
                  x_map=lambda bb, i: (bb * per_seq + i, 0), out_map=lambda bb, i: (i, bb),
                  out_shape=(seq, bsz * D_MODEL)).reshape(seq * bsz, D_MODEL)

    dt = jnp.exp(log_dt)[:, None]
    mag = jnp.exp(lam_re * dt)
    a_re, a_im = mag * jnp.cos(lam_im * dt), mag * jnp.sin(lam_im * dt)
    den = lam_re * lam_re + lam_im * lam_im
    coef_re = ((a_re - 1.0) * lam_re + a_im * lam_im) / den
    coef_im = (a_im * lam_re - (a_re - 1.0) * lam_im) / den
    bb_re = coef_re[..., None] * b_re - coef_im[..., None] * b_im
    bb_im = coef_re[..., None] * b_im + coef_im[..., None] * b_re
    bre = _s5_block_diag(jnp.swapaxes(bb_re, 1, 2), SSM_GROUP, SSM_STATE).astype(BF16)
    bim = _s5_block_diag(jnp.swapaxes(bb_im, 1, 2), SSM_GROUP, SSM_STATE).astype(BF16)
    cre = _s5_block_diag(jnp.swapaxes(c_re, 1, 2), SSM_STATE, SSM_GROUP).astype(BF16)
    cim = _s5_block_diag(jnp.swapaxes(-c_im, 1, 2), SSM_STATE, SSM_GROUP).astype(BF16)
    n_state = SSM_GROUPS * SSM_STATE

    rows = S5_CHUNK * bsz
    row = lambda c: (c, 0)
    c2 = lambda c: (0, 0)
    c3 = lambda c: (0, 0, 0)
    y_t = pl.pallas_call(
        functools.partial(_s5_kernel, bsz=bsz),
        grid=(seq // S5_CHUNK,),
        in_specs=[pl.BlockSpec((rows, D_MODEL), row),
                  pl.BlockSpec(bre.shape, c3), pl.BlockSpec(bim.shape, c3),
                  pl.BlockSpec(cre.shape, c3), pl.BlockSpec(cim.shape, c3),
                  pl.BlockSpec((1, n_state), c2), pl.BlockSpec((1, n_state), c2), pl.BlockSpec((1, D_MODEL), c2)],
        out_specs=pl.BlockSpec((rows, D_MODEL), row),
        out_shape=jax.ShapeDtypeStruct((seq * bsz, D_MODEL), BF16),
        scratch_shapes=[pltpu.VMEM((rows, S5_ST_BLK), F32)] * 4 + [pltpu.VMEM((SUBLANES, n_state), F32)] * 2,
        compiler_params=_cparams("arbitrary"),
        name="s5_scan",
    )(u_t, bre, bim, cre, cim, a_re.reshape(1, n_state), a_im.reshape(1, n_state), d.reshape(1, D_MODEL))

    y2 = y_t.reshape(seq, bsz * D_MODEL)
    return _linear_residual_ln(
        [y2], [w_out.astype(BF16)], h2d, g, b, tm=tm, glu=True, grid=(bsz, per_seq),
        x_maps=[lambda bb, i: (i, bb)], res_map=lambda bb, i: (bb * per_seq + i, 0))


def _even_mixer_block(h2d, bsz, seq, w_in, qnorm_g, w_uq, w_uq_idx, kidx_g, kidx_b, w_out, g, b):
    qa, qi, ka, va, ki, wi, qb, kb, vb = _even_proj(h2d, bsz, seq, w_in, qnorm_g, w_uq, w_uq_idx, kidx_g, kidx_b)
    o_a = _dsa_attention(qa, qi, wi, ki, ka, va, bsz, seq)
    o_b = _stick_breaking(qb, kb, vb, bsz, seq)
    w_out = w_out.astype(BF16)
    return _linear_residual_ln([o_a, o_b], [w_out[:A_WIDTH], w_out[A_WIDTH:]], h2d, g, b, tm=ROW_TILE)


def kernel(x, mem, ev_w_in, ev_qnorm_g, ev_w_uq, ev_w_uq_idx, ev_kidx_ln_g, ev_kidx_ln_b, ev_w_out, od_w_in, od_log_dt, od_lambda_re, od_lambda_im, od_b_re, od_b_im, od_c_re, od_c_im, od_d, od_w_out, mix_ln_g, mix_ln_b, xa_w_q, xa_w_kv, xa_w_o, xa_ln_g, xa_ln_b, moe_w_router, moe_b_router, moe_w_gu, moe_b_gu, moe_w_down, moe_b_down, ffn_ln_g, ffn_ln_b):
    bsz, seq, _ = x.shape
    h = x.reshape(bsz * seq, D_MODEL)
    for layer in range(DEPTH):
        j = layer // 2
        if layer % 2 == 0:
            h = _even_mixer_block(h, bsz, seq, ev_w_in[j], ev_qnorm_g[j], ev_w_uq[j], ev_w_uq_idx[j],
                                  ev_kidx_ln_g[j], ev_kidx_ln_b[j], ev_w_out[j], mix_ln_g[layer], mix_ln_b[layer])
        else:
            h = _s5_mixer_block(h, bsz, seq, od_w_in[j], od_log_dt[j], od_lambda_re[j], od_lambda_im[j],
                                od_b_re[j], od_b_im[j], od_c_re[j], od_c_im[j], od_d[j], od_w_out[j],
                                mix_ln_g[layer], mix_ln_b[layer])
        h, h_packed = _cross_attention_block(h, mem, bsz, seq, xa_w_q[layer], xa_w_kv[layer], xa_w_o[layer],
                                             xa_ln_g[layer], xa_ln_b[layer])
        h = _moe_block(h, h_packed, layer, moe_w_router[layer], moe_b_router[layer], moe_w_gu, moe_b_gu[layer],
                       moe_w_down, moe_b_down[layer], ffn_ln_g[layer], ffn_ln_b[layer])
    return h.reshape(bsz, seq, D_MODEL)
```

```python
import functools
import math

import jax
import jax.numpy as jnp
from jax import lax
from jax.experimental import pallas as pl
from jax.experimental.pallas import tpu as pltpu
from jax.experimental.pallas import tpu_sc as plsc

F32 = jnp.float32
BF16 = jnp.bfloat16
I32 = jnp.int32

D_MODEL = 1024
DEPTH = 2
HEAD_DIM = 64
A_HEADS = 8
A_KV_HEADS = 2
A_REP = A_HEADS // A_KV_HEADS
Q_RANK = 256
IDX_HEADS = 8
IDX_DIM = 64
IDX_TOPK = 256
B_HEADS = 8
A_WIDTH = A_HEADS * HEAD_DIM
B_WIDTH = B_HEADS * HEAD_DIM
SSM_GROUP = 16
SSM_GROUPS = D_MODEL // SSM_GROUP
SSM_STATE = 64
XA_HEADS = 4
XA_HEAD_DIM = D_MODEL // XA_HEADS
N_EXPERTS = 32
TOP_K = 4
D_EXPERT = D_MODEL
SWIGLU_LIMIT = 7.0
SWIGLU_ALPHA = 1.702
ROPE_THETA = 500000.0
ROPE_HALF = HEAD_DIM // 8
LN_EPS = 1e-5
DN_ALPHA = (2 * DEPTH) ** 0.25

LANES = 128
SUBLANES = 8
VMEM_LIMIT_BYTES = 56 * 1024 * 1024

Q_BLOCK = 256
DSA_KEY_TILE = 512
DSA_ATT_TILE = 256
DSA_COUNT_ROWS = 8 * SUBLANES
SB_KEY_TILE = 256
ROW_TILE = 256
MOE_BLOCK_ROWS = 512
SC_GATHER_SLOT_BYTES = 128 * 1024
S5_CHUNK = 128
S5_LANE_BLOCKS = 4
S5_IN_BLK = D_MODEL // S5_LANE_BLOCKS
S5_ST_BLK = SSM_GROUPS * SSM_STATE // S5_LANE_BLOCKS

SB_EXIT_LOG = -104.0
NEG_BIG = -1e30
INT_MIN = -(2 ** 31)


def _cparams(*sem):
    return pltpu.CompilerParams(dimension_semantics=sem, vmem_limit_bytes=VMEM_LIMIT_BYTES)


def _dot(a, b):
    return jnp.dot(a, b, preferred_element_type=F32)


def _dot_nt(a, b):
    return lax.dot_general(a, b, (((1,), (1,)), ((), ())), preferred_element_type=F32)


def _layer_norm_rows(y, g, b):
    mu = jnp.mean(y, axis=-1, keepdims=True)
    d = y - mu
    var = jnp.mean(d * d, axis=-1, keepdims=True)
    return d * lax.rsqrt(var + LN_EPS) * g + b


def _mm_kernel(x_ref, w_ref, o_ref):
    o_ref[...] = _dot(x_ref[...].astype(BF16), w_ref[...]).astype(o_ref.dtype)


def _matmul(x, w, *, tm, out_dtype, x_map=None, out_map=None, grid=None, out_shape=None):
    m, k = x.shape
    n = w.shape[1]
    grid = grid or (m // tm,)
    x_map = x_map or (lambda i: (i, 0))
    out_map = out_map or (lambda i: (i, 0))
    out_shape = out_shape or (m, n)
    return pl.pallas_call(
        _mm_kernel,
        grid=grid,
        in_specs=[pl.BlockSpec((tm, k), x_map), pl.BlockSpec((k, n), lambda *a: (0, 0))],
        out_specs=pl.BlockSpec((tm, n), out_map),
        out_shape=jax.ShapeDtypeStruct(out_shape, out_dtype),
        compiler_params=_cparams(*(("parallel",) * len(grid))),
        name="matmul",
    )(x, w)


def _lin_ln_kernel(*refs, n_in, glu):
    xs, ws = refs[:n_in], refs[n_in:2 * n_in]
    res_ref, g_ref, b_ref, o_ref = refs[2 * n_in:]
    acc = _dot(xs[0][...].astype(BF16), ws[0][...])
    for x_ref, w_ref in zip(xs[1:], ws[1:]):
        acc = acc + _dot(x_ref[...].astype(BF16), w_ref[...])
    if glu:
        acc = acc[:, :D_MODEL] * jax.nn.sigmoid(acc[:, D_MODEL:])
    y = DN_ALPHA * res_ref[...] + acc
    o_ref[...] = _layer_norm_rows(y, g_ref[...], b_ref[...])


def _linear_residual_ln(xs, ws, res, g, b, *, tm, glu=False, grid=None, x_maps=None, res_map=None):
    n_rows = res.shape[0]
    grid = grid or (n_rows // tm,)
    x_maps = x_maps or [lambda i: (i, 0)] * len(xs)
    res_map = res_map or (lambda i: (i, 0))
    const = lambda *a: (0, 0)
    in_specs = [pl.BlockSpec((tm, w.shape[0]), m) for w, m in zip(ws, x_maps)]
    in_specs += [pl.BlockSpec(w.shape, const) for w in ws]
    in_specs += [pl.BlockSpec((tm, D_MODEL), res_map), pl.BlockSpec((1, D_MODEL), const),
                 pl.BlockSpec((1, D_MODEL), const)]
    return pl.pallas_call(
        functools.partial(_lin_ln_kernel, n_in=len(xs), glu=glu),
        grid=grid,
        in_specs=in_specs,
        out_specs=pl.BlockSpec((tm, D_MODEL), res_map),
        out_shape=jax.ShapeDtypeStruct((n_rows, D_MODEL), F32),
        compiler_params=_cparams(*(("parallel",) * len(grid))),
        name="linear_residual_ln",
    )(*xs, *ws, res, g.reshape(1, D_MODEL), b.reshape(1, D_MODEL))


_EV_CQ, _EV_KA, _EV_VA, _EV_KI, _EV_QB = 0, 256, 384, 512, 640
_EV_KB = _EV_QB + B_WIDTH
_EV_VB = _EV_KB + B_WIDTH
_EV_COLS = _EV_VB + B_WIDTH


def _rope_tables(seq):
    inv = ROPE_THETA ** (-jnp.arange(ROPE_HALF, dtype=F32) / ROPE_HALF)
    ang = jnp.arange(seq, dtype=F32)[:, None] * inv[None, :]
    cos, sin = jnp.cos(ang), jnp.sin(ang)
    rest = HEAD_DIM - 2 * ROPE_HALF
    zh = jnp.zeros((seq, ROPE_HALF), F32)
    c = jnp.concatenate([cos, cos, jnp.ones((seq, rest), F32)], axis=1)
    s1 = jnp.concatenate([-sin, zh, jnp.zeros((seq, rest), F32)], axis=1)
    s2 = jnp.concatenate([zh, sin, jnp.zeros((seq, rest), F32)], axis=1)
    rep = LANES // HEAD_DIM
    return jnp.tile(c, (1, rep)), jnp.tile(s1, (1, rep)), jnp.tile(s2, (1, rep))


def _even_proj_kernel(x_ref, w_ref, qg_ref, wuq_ref, wuqi_ref, lg_ref, lb_ref, c_ref, s1_ref, s2_ref,
                      qa_ref, qi_ref, ka_ref, va_ref, ki_ref, wi_ref, qb_ref, kb_ref, vb_ref):
    p = _dot(x_ref[...].astype(BF16), w_ref[...])
    c, s1, s2 = c_ref[...], s1_ref[...], s2_ref[...]

    def rope(t):
        return (t * c + pltpu.roll(t, LANES - ROPE_HALF, 1) * s1 + pltpu.roll(t, ROPE_HALF, 1) * s2)

    cq = p[:, _EV_CQ:_EV_CQ + Q_RANK]
    cn = cq * lax.rsqrt(jnp.mean(cq * cq, axis=-1, keepdims=True) + LN_EPS) * qg_ref[...]
    cnb = cn.astype(BF16)
    qa = _dot(cnb, wuq_ref[...])
    qi = _dot(cnb, wuqi_ref[...])
    low = lax.broadcasted_iota(I32, c.shape, 1) < HEAD_DIM
    for j in range(A_WIDTH // LANES):
        sl = slice(j * LANES, (j + 1) * LANES)
        pair = rope(qa[:, sl]) * (HEAD_DIM ** -0.5)
        for e, src in enumerate((pair, pltpu.roll(pair, HEAD_DIM, 1))):
            h = 2 * j + e
            qa_ref[:, h * LANES:(h + 1) * LANES] = jnp.where(low, src, 0.0).astype(BF16)
        qi_ref[:, sl] = (rope(qi[:, sl]) * (IDX_DIM ** -0.5)).astype(BF16)
    kpair = rope(p[:, _EV_KA:_EV_KA + LANES])
    vpair = p[:, _EV_VA:_EV_VA + LANES]
    v_pad = jnp.where(lax.broadcasted_iota(I32, c.shape, 1) == HEAD_DIM, 1.0, 0.0)
    for g, (ks, vs) in enumerate(((kpair, vpair), (pltpu.roll(kpair, HEAD_DIM, 1), pltpu.roll(vpair, HEAD_DIM, 1)))):
        ka_ref[:, g * LANES:(g + 1) * LANES] = jnp.where(low, ks, 0.0).astype(BF16)
        va_ref[:, g * LANES:(g + 1) * LANES] = jnp.where(low, vs, v_pad).astype(BF16)

    t = p[:, _EV_KI:_EV_KI + LANES]
    lane = lax.broadcasted_iota(I32, t.shape, 1)
    is_k = lane < IDX_DIM
    mu = jnp.sum(jnp.where(is_k, t, 0.0), axis=-1, keepdims=True) * (1.0 / IDX_DIM)
    d = jnp.where(is_k, t - mu, 0.0)
    var = jnp.sum(d * d, axis=-1, keepdims=True) * (1.0 / IDX_DIM)
    kin = d * lax.rsqrt(var + LN_EPS) * lg_ref[...] + lb_ref[...]
    ki_ref[...] = rope(kin)[:, :IDX_DIM].astype(BF16)
    wi_ref[...] = t[:, IDX_DIM:IDX_DIM + IDX_HEADS] * (IDX_HEADS ** -0.5)

    qb_ref[...] = (p[:, _EV_QB:_EV_KB] * (HEAD_DIM ** -0.5)).astype(BF16)
    kb_ref[...] = p[:, _EV_KB:_EV_VB].astype(BF16)
    vb_ref[...] = p[:, _EV_VB:_EV_COLS].astype(BF16)


def _even_proj(x2d, bsz, seq, w_in, qnorm_g, w_uq, w_uq_idx, kidx_g, kidx_b):
    n = x2d.shape[0]
    tm = ROW_TILE
    per_seq = seq // tm
    c0 = Q_RANK + 2 * A_KV_HEADS * HEAD_DIM + IDX_DIM + IDX_HEADS
    w_pack = jnp.concatenate(
        [w_in[:, :c0], jnp.zeros((D_MODEL, _EV_QB - c0), w_in.dtype), w_in[:, c0:]], axis=1).astype(BF16)
    pad = LANES - IDX_DIM
    lg = jnp.concatenate([kidx_g, jnp.zeros((pad,), F32)]).reshape(1, LANES)
    lb = jnp.concatenate([kidx_b, jnp.zeros((pad,), F32)]).reshape(1, LANES)
    c, s1, s2 = _rope_tables(seq)
    row = lambda i: (i, 0)
    const = lambda i: (0, 0)
    pos = lambda i: (i % per_seq, 0)
    head_shape = jax.ShapeDtypeStruct((n, B_WIDTH), BF16)
    head_spec = pl.BlockSpec((tm, B_WIDTH), row)
    return pl.pallas_call(
        _even_proj_kernel,
        grid=(n // tm,),
        in_specs=[pl.BlockSpec((tm, D_MODEL), row), pl.BlockSpec((D_MODEL, _EV_COLS), const),
                  pl.BlockSpec((1, Q_RANK), const), pl.BlockSpec((Q_RANK, A_WIDTH), const),
                  pl.BlockSpec((Q_RANK, IDX_HEADS * IDX_DIM), const),
                  pl.BlockSpec((1, LANES), const), pl.BlockSpec((1, LANES), const),
                  pl.BlockSpec((tm, LANES), pos), pl.BlockSpec((tm, LANES), pos), pl.BlockSpec((tm, LANES), pos)],
        out_specs=[pl.BlockSpec((tm, A_HEADS * LANES), row), pl.BlockSpec((tm, IDX_HEADS * IDX_DIM), row),
                   pl.BlockSpec((tm, A_KV_HEADS * LANES), row), pl.BlockSpec((tm, A_KV_HEADS * LANES), row),
                   pl.BlockSpec((tm, IDX_DIM), row), pl.BlockSpec((tm, IDX_HEADS), row),
                   head_spec, head_spec, head_spec],
        out_shape=[jax.ShapeDtypeStruct((n, A_HEADS * LANES), BF16), jax.ShapeDtypeStruct((n, IDX_HEADS * IDX_DIM), BF16),
                   jax.ShapeDtypeStruct((n, A_KV_HEADS * LANES), BF16),
                   jax.ShapeDtypeStruct((n, A_KV_HEADS * LANES), BF16),
                   jax.ShapeDtypeStruct((n, IDX_DIM), BF16), jax.ShapeDtypeStruct((n, IDX_HEADS), F32),
                   head_shape, head_shape, head_shape],
        compiler_params=_cparams("parallel"),
        name="even_proj",
    )(x2d, w_pack, qnorm_g.reshape(1, Q_RANK), w_uq.astype(BF16), w_uq_idx.astype(BF16), lg, lb, c, s1, s2)


def _key_to_float(key):
    bits = key ^ ((key >> 31) & jnp.int32(0x7FFFFFFF))
    return lax.bitcast_convert_type(bits, F32)


def _dsa_kernel(qa_ref, qi_ref, wit_ref, ki_ref, ka_ref, vat_ref, o_ref, sc_scr, *, topk, ts, ta):
    seq = sc_scr.shape[0]
    qb = pl.program_id(1)
    q0 = qb * Q_BLOCK
    nkt = (q0 + Q_BLOCK - 1) // ts + 1
    t_row = q0 + lax.broadcasted_iota(I32, (1, Q_BLOCK), 1)
    key = lax.broadcasted_iota(I32, (ts, Q_BLOCK), 0)
    kf = jnp.float32(topk)

    qi = qi_ref[0]
    qs = jnp.concatenate([qi[:, h * IDX_DIM:(h + 1) * IDX_DIM] for h in range(IDX_HEADS)], axis=0)
    wit = wit_ref[0]

    def score_tile(kt, carry):
        off = pl.multiple_of(kt * ts, ts)
        s_all = _dot_nt(ki_ref[0, pl.ds(off, ts), :], qs)
        acc = jnp.zeros((ts, Q_BLOCK), F32)
        for h in range(IDX_HEADS):
            acc = acc + jnp.maximum(s_all[:, h * Q_BLOCK:(h + 1) * Q_BLOCK], 0.0) * wit[h:h + 1, :]
        sc_scr[pl.ds(off, ts), :] = jnp.where(off + key <= t_row, acc, -jnp.inf)
        return carry

    lax.fori_loop(0, nkt, score_tile, 0)

    def count(pred):
        def body(kt, acc):
            off = pl.multiple_of(kt * ts, ts)
            ind = pred(sc_scr[pl.ds(off, ts), :], off + key)
            return acc + jnp.sum(ind.reshape(ts // DSA_COUNT_ROWS, DSA_COUNT_ROWS, Q_BLOCK), axis=0)
        acc = lax.fori_loop(0, nkt, body, jnp.zeros((DSA_COUNT_ROWS, Q_BLOCK), F32))
        return jnp.sum(acc, axis=0, keepdims=True)

    def bit_step(i, base):
        cand = base + jnp.left_shift(jnp.int32(1), 31 - i)
        cf = _key_to_float(cand)
        cnt = count(lambda sc, idx: jnp.where(sc >= cf, 1.0, 0.0))
        return jnp.where(cnt >= kf, cand, base)

    base = lax.fori_loop(0, 32, bit_step, jnp.full((1, Q_BLOCK), INT_MIN, I32))
    thr = jnp.where(base == INT_MIN, -jnp.inf, _key_to_float(base))

    cnt_ge = count(lambda sc, idx: jnp.where(sc >= thr, 1.0, 0.0))
    tied = jnp.logical_and(cnt_ge > kf, thr > -jnp.inf)
    any_tied = jnp.max(jnp.where(tied, 1.0, 0.0)) > 0.0
    seq_bits = max(1, int(math.ceil(math.log2(seq))))

    def tie_cut():
        cnt_gt = count(lambda sc, idx: jnp.where(sc > thr, 1.0, 0.0))
        need = kf - cnt_gt

        def idx_step(i, pos):
            cand = pos + jnp.left_shift(jnp.int32(1), seq_bits - 1 - i)
            cnt = count(lambda sc, idx: jnp.where(sc == thr, jnp.where(idx < cand, 1.0, 0.0), 0.0))
            return jnp.where(cnt < need, cand, pos)

        return lax.fori_loop(0, seq_bits, idx_step, jnp.zeros((1, Q_BLOCK), I32))

    cut = lax.cond(any_tied, tie_cut, lambda: jnp.full((1, Q_BLOCK), seq, I32))
    cut = jnp.where(tied, cut, seq)

    nkt_a = (q0 + Q_BLOCK - 1) // ta + 1
    key_a = lax.broadcasted_iota(I32, (ta, Q_BLOCK), 0)
    cols = A_REP * Q_BLOCK
    qg = [jnp.concatenate([qa_ref[0, :, (g * A_REP + r) * LANES:(g * A_REP + r + 1) * LANES]
                           for r in range(A_REP)], axis=0) for g in range(A_KV_HEADS)]

    def att_pair(i, carry):
        offs = [pl.multiple_of((2 * i + e) * ta, ta) for e in range(2)]
        logits = [_dot_nt(ka_ref[0, pl.ds(offs[e], ta), g * LANES:(g + 1) * LANES], qg[g])
                  for e in range(2) for g in range(A_KV_HEADS)]
        out = []
        for e in range(2):
            sc = sc_scr[pl.ds(offs[e], ta), :]
            idx = offs[e] + key_a
            keep = jnp.where(sc > thr, 0.0, jnp.where(sc == thr, jnp.where(idx <= cut, 0.0, NEG_BIG), NEG_BIG))
            bias = jnp.where(idx <= t_row, keep, NEG_BIG)
            bias = jnp.concatenate([bias] * A_REP, axis=1)
            for g in range(A_KV_HEADS):
                m, acc = carry[e * A_KV_HEADS + g]
                s = logits[e * A_KV_HEADS + g] + bias
                m_new = jnp.maximum(m, jnp.max(s, axis=0, keepdims=True))
                p = jnp.exp(s - m_new)
                vt = vat_ref[0, g * LANES:(g + 1) * LANES, pl.ds(offs[e], ta)]
                out.append((m_new, jnp.exp(m - m_new) * acc + _dot(vt, p.astype(BF16))))
        return tuple(out)

    init = tuple((jnp.full((1, cols), NEG_BIG, F32), jnp.zeros((LANES, cols), F32))
                 for _ in range(2 * A_KV_HEADS))
    final = lax.fori_loop(0, (nkt_a + 1) // 2, att_pair, init)
    low = lax.broadcasted_iota(I32, (Q_BLOCK, LANES), 1) < HEAD_DIM
    outs = []
    for g in range(A_KV_HEADS):
        (m0, acc0), (m1, acc1) = final[g], final[A_KV_HEADS + g]
        m = jnp.maximum(m0, m1)
        acc = jnp.exp(m0 - m) * acc0 + jnp.exp(m1 - m) * acc1
        og = acc / acc[HEAD_DIM:HEAD_DIM + 1, :]
        outs += [og[:, r * Q_BLOCK:(r + 1) * Q_BLOCK].T for r in range(A_REP)]
    for j in range(A_HEADS // 2):
        pair = jnp.where(low, outs[2 * j], pltpu.roll(outs[2 * j + 1], HEAD_DIM, 1))
        o_ref[0, :, j * LANES:(j + 1) * LANES] = pair.astype(BF16)


def _dsa_attention(qa, qi, wi, ki, ka, va, bsz, seq):
    topk = min(IDX_TOPK, seq // 4)
    ts = min(DSA_KEY_TILE, seq)
    ta = min(DSA_ATT_TILE, seq // 2)
    assert seq % (2 * ta) == 0 and seq % ts == 0, "the attention loop walks the key tiles in pairs"
    blk = lambda b, i: (b, i, 0)
    full = lambda b, i: (b, 0, 0)
    r3 = lambda a: a.reshape(bsz, seq, a.shape[-1])
    wit = jnp.swapaxes(r3(wi), 1, 2)
    vat = jnp.swapaxes(r3(va), 1, 2)
    return pl.pallas_call(
        functools.partial(_dsa_kernel, topk=topk, ts=ts, ta=ta),
        grid=(bsz, seq // Q_BLOCK),
        in_specs=[pl.BlockSpec((1, Q_BLOCK, A_HEADS * LANES), blk),
                  pl.BlockSpec((1, Q_BLOCK, IDX_HEADS * IDX_DIM), blk),
                  pl.BlockSpec((1, IDX_HEADS, Q_BLOCK), lambda b, i: (b, 0, i)),
                  pl.BlockSpec((1, seq, IDX_DIM), full),
                  pl.BlockSpec((1, seq, A_KV_HEADS * LANES), full),
                  pl.BlockSpec((1, A_KV_HEADS * LANES, seq), full)],
        out_specs=pl.BlockSpec((1, Q_BLOCK, A_WIDTH), blk),
        out_shape=jax.ShapeDtypeStruct((bsz, seq, A_WIDTH), BF16),
        scratch_shapes=[pltpu.VMEM((seq, Q_BLOCK), F32)],
        compiler_params=_cparams("parallel", "parallel"),
        name="dsa_attention",
    )(r3(qa), r3(qi), wit, r3(ki), r3(ka), vat).reshape(bsz * seq, A_WIDTH)


def _sb_kernel(q_ref, k_ref, v_ref, u_ref, o_ref, acc_scr, run_scr, *, tk):
    q0 = pl.program_id(1) * Q_BLOCK
    t_col = q0 + lax.broadcasted_iota(I32, (Q_BLOCK, 1), 0)
    lane = lax.broadcasted_iota(I32, (Q_BLOCK, tk), 1)
    low = lax.broadcasted_iota(I32, (Q_BLOCK, LANES), 1) < HEAD_DIM
    upper = u_ref[...]
    nkt = (q0 + Q_BLOCK - 1) // tk + 1
    q = q_ref[0]
    zero = jnp.zeros((Q_BLOCK, LANES), BF16)
    qm = []
    for h in range(B_HEADS):
        pair = q[:, (h // 2) * LANES:(h // 2 + 1) * LANES]
        qm.append(jnp.where(low, pair, zero) if h % 2 == 0 else jnp.where(low, zero, pair))
    acc_scr[...] = jnp.zeros_like(acc_scr)
    run_scr[...] = jnp.zeros_like(run_scr)

    def cond(carry):
        i, worst = carry
        return jnp.logical_and(i < nkt, worst >= SB_EXIT_LOG)

    def body(carry):
        i, _ = carry
        off = pl.multiple_of((nkt - 1 - i) * tk, tk)
        strict = off + lane < t_col
        worst = None
        for p in range(B_HEADS // 2):
            cols = slice(p * LANES, (p + 1) * LANES)
            kp = k_ref[0, pl.ds(off, tk), cols]
            vp = v_ref[0, pl.ds(off, tk), cols]
            outs = []
            for e in range(2):
                h = 2 * p + e
                run = run_scr[h]
                z = _dot_nt(qm[h], kp)
                softplus = jnp.maximum(z, 0.0) + jnp.log(1.0 + jnp.exp(-jnp.abs(z)))
                log_1mb = jnp.where(strict, -softplus, 0.0)
                hi = log_1mb.astype(BF16)
                lo = (log_1mb - hi.astype(F32)).astype(BF16)
                after = _dot(hi, upper) + _dot(lo, upper) + run
                a = jnp.where(strict, jnp.exp(z - softplus + after), 0.0)
                outs.append(_dot(a.astype(BF16), vp))
                run = run + jnp.sum(log_1mb, axis=1, keepdims=True)
                run_scr[h] = run
                worst = run if worst is None else jnp.maximum(worst, run)
            acc_scr[:, cols] += jnp.where(low, outs[0], outs[1])
        return i + 1, jnp.max(worst)

    lax.while_loop(cond, body, (jnp.int32(0), jnp.float32(0.0)))
    o_ref[0] = acc_scr[...].astype(BF16)


def _stick_breaking(qb, kb, vb, bsz, seq):
    tk = min(SB_KEY_TILE, seq)
    r = lax.broadcasted_iota(I32, (tk, tk), 0)
    c = lax.broadcasted_iota(I32, (tk, tk), 1)
    upper = jnp.where(r > c, 1.0, 0.0).astype(BF16)
    blk = lambda b, i: (b, i, 0)
    full = lambda b, i: (b, 0, 0)
    r3 = lambda a: a.reshape(bsz, seq, B_WIDTH)
    return pl.pallas_call(
        functools.partial(_sb_kernel, tk=tk),
        grid=(bsz, seq // Q_BLOCK),
        in_specs=[pl.BlockSpec((1, Q_BLOCK, B_WIDTH), blk), pl.BlockSpec((1, seq, B_WIDTH), full),
                  pl.BlockSpec((1, seq, B_WIDTH), full), pl.BlockSpec((tk, tk), lambda b, i: (0, 0))],
        out_specs=pl.BlockSpec((1, Q_BLOCK, B_WIDTH), blk),
        out_shape=jax.ShapeDtypeStruct((bsz, seq, B_WIDTH), BF16),
        scratch_shapes=[pltpu.VMEM((Q_BLOCK, B_WIDTH), F32), pltpu.VMEM((B_HEADS, Q_BLOCK, 1), F32)],
        compiler_params=_cparams("parallel", "arbitrary"),
        name="stick_breaking",
    )(r3(qb), r3(kb), r3(vb), upper).reshape(bsz * seq, B_WIDTH)


HALF_D = D_MODEL // 2
U32 = jnp.uint32
HIGH16 = 0xFFFF0000


def _pack_bf16_pairs(x):
    def bits(v):
        return lax.bitcast_convert_type(v.astype(BF16).astype(F32), U32)
    word = (bits(x[:, HALF_D:]) & U32(HIGH16)) | (bits(x[:, :HALF_D]) >> 16)
    return lax.bitcast_convert_type(word, I32)


def _unpack_bf16_pairs(word):
    u = lax.bitcast_convert_type(word, U32)
    return lax.bitcast_convert_type(u << 16, F32), lax.bitcast_convert_type(u & U32(HIGH16), F32)


def _xattn_kernel(h_ref, wq_ref, kv_ref, wo_ref, g_ref, b_ref, o_ref, packed_ref):
    h = h_ref[...]
    q = (_dot(h.astype(BF16), wq_ref[...]) * (XA_HEAD_DIM ** -0.5)).astype(BF16)
    kv = kv_ref[0]
    outs = []
    for hd in range(XA_HEADS):
        sl = slice(hd * XA_HEAD_DIM, (hd + 1) * XA_HEAD_DIM)
        s = _dot_nt(q[:, sl], kv[:, sl])
        p = jnp.exp(s - jnp.max(s, axis=1, keepdims=True))
        vh = kv[:, D_MODEL + hd * XA_HEAD_DIM:D_MODEL + (hd + 1) * XA_HEAD_DIM]
        outs.append((_dot(p.astype(BF16), vh) / jnp.sum(p, axis=1, keepdims=True)).astype(BF16))
    y = _dot(jnp.concatenate(outs, axis=1), wo_ref[...])
    out = _layer_norm_rows(DN_ALPHA * h + y, g_ref[...], b_ref[...])
    o_ref[...] = out
    packed_ref[...] = _pack_bf16_pairs(out)


def _cross_attention_block(h2d, mem, bsz, seq, w_q, w_kv, w_o, g, b):
    tm = ROW_TILE
    per_seq = seq // tm
    mem_len = mem.shape[1]
    kv = _matmul(mem.reshape(bsz * mem_len, D_MODEL), w_kv.astype(BF16), tm=mem_len, out_dtype=BF16)
    kv = kv.reshape(bsz, mem_len, 2 * D_MODEL)
    row = lambda i: (i, 0)
    const = lambda i: (0, 0)
    return pl.pallas_call(
        _xattn_kernel,
        grid=(bsz * per_seq,),
        in_specs=[pl.BlockSpec((tm, D_MODEL), row), pl.BlockSpec((D_MODEL, D_MODEL), const),
                  pl.BlockSpec((1, mem_len, 2 * D_MODEL), lambda i: (i // per_seq, 0, 0)),
                  pl.BlockSpec((D_MODEL, D_MODEL), const),
                  pl.BlockSpec((1, D_MODEL), const), pl.BlockSpec((1, D_MODEL), const)],
        out_specs=[pl.BlockSpec((tm, D_MODEL), row), pl.BlockSpec((tm, HALF_D), row)],
        out_shape=[jax.ShapeDtypeStruct(h2d.shape, F32), jax.ShapeDtypeStruct((h2d.shape[0], HALF_D), I32)],
        compiler_params=_cparams("parallel"),
        name="cross_attention",
    )(h2d, w_q.astype(BF16), kv, w_o.astype(BF16), g.reshape(1, D_MODEL), b.reshape(1, D_MODEL))


def _router_kernel(h_ref, w_ref, b_ref, tri_ref, idx_ref, gate_ref, rank_ref, cnt_ref, run_scr):
    @pl.when(pl.program_id(0) == 0)
    def _():
        run_scr[...] = jnp.zeros_like(run_scr)

    h = h_ref[...]
    hh = h.astype(BF16)
    hl = (h - hh.astype(F32)).astype(BF16)
    w = w_ref[...]
    wh = w.astype(BF16)
    wl = (w - wh.astype(F32)).astype(BF16)
    logits = _dot(hh, wh) + _dot(hl, wh) + _dot(hh, wl) + b_ref[...]
    lane = lax.broadcasted_iota(I32, logits.shape, 1).astype(F32)
    vals, sels = [], []
    onehot = jnp.zeros(logits.shape, F32)
    for k in range(TOP_K):
        m = jnp.max(logits, axis=1, keepdims=True)
        sel = jnp.min(jnp.where(logits == m, lane, float(LANES)), axis=1, keepdims=True)
        idx_ref[:, k:k + 1] = sel.astype(I32)
        vals.append(m)
        sels.append(sel)
        onehot = onehot + jnp.where(lane == sel, 1.0, 0.0)
        logits = jnp.where(lane == sel, -jnp.inf, logits)
    es = [jnp.exp(v - vals[0]) for v in vals]
    tot = es[0] + es[1] + es[2] + es[3]
    for k in range(TOP_K):
        gate_ref[:, k:k + 1] = es[k] / tot

    earlier = _dot(tri_ref[...], onehot.astype(BF16)) + run_scr[...]
    for k in range(TOP_K):
        rank = jnp.sum(jnp.where(lane == sels[k], earlier, 0.0), axis=1, keepdims=True)
        rank_ref[:, k:k + 1] = rank.astype(I32)
    run = run_scr[...] + jnp.sum(onehot, axis=0, keepdims=True)
    run_scr[...] = run
    cnt_ref[...] = run


def _router(h2d, w_router, b_router):
    n = h2d.shape[0]
    tm = 2 * ROW_TILE
    pad = LANES - N_EXPERTS
    w = jnp.concatenate([w_router, jnp.zeros((D_MODEL, pad), F32)], axis=1)
    b = jnp.concatenate([b_router, jnp.full((pad,), NEG_BIG, F32)]).reshape(1, LANES)
    r = lax.broadcasted_iota(I32, (tm, tm), 0)
    c = lax.broadcasted_iota(I32, (tm, tm), 1)
    tri = jnp.where(c < r, 1.0, 0.0).astype(BF16)
    row = lambda i: (i, 0)
    const = lambda i: (0, 0)
    return pl.pallas_call(
        _router_kernel,
        grid=(n // tm,),
        in_specs=[pl.BlockSpec((tm, D_MODEL), row), pl.BlockSpec((D_MODEL, LANES), const),
                  pl.BlockSpec((1, LANES), const), pl.BlockSpec((tm, tm), const)],
        out_specs=[pl.BlockSpec((tm, TOP_K), row), pl.BlockSpec((tm, TOP_K), row),
                   pl.BlockSpec((tm, TOP_K), row), pl.BlockSpec((1, LANES), const)],
        out_shape=[jax.ShapeDtypeStruct((n, TOP_K), I32), jax.ShapeDtypeStruct((n, TOP_K), F32),
                   jax.ShapeDtypeStruct((n, TOP_K), I32), jax.ShapeDtypeStruct((1, LANES), F32)],
        scratch_shapes=[pltpu.VMEM((1, LANES), F32)],
        compiler_params=_cparams("arbitrary"),
        name="moe_router",
    )(h2d, w, b, tri)


def _gather_rows(src, idx):
    n_out = idx.shape[0]
    width = src.shape[1]
    win = SC_GATHER_SLOT_BYTES // (width * src.dtype.itemsize)
    mesh = plsc.VectorSubcoreMesh(core_axis_name="core", subcore_axis_name="subcore")
    n_workers = mesh.num_cores * mesh.num_subcores
    per_worker = n_out // n_workers
    steps = per_worker // win
    assert per_worker * n_workers == n_out and steps * win == per_worker and steps % 2 == 0

    @functools.partial(
        pl.kernel, out_type=jax.ShapeDtypeStruct((n_out, width), src.dtype), mesh=mesh,
        scratch_types=[pltpu.VMEM((per_worker,), I32), pltpu.VMEM((2, win, width), src.dtype),
                       pltpu.SemaphoreType.DMA, pltpu.SemaphoreType.DMA])
    def gather_kernel(src_hbm, idx_hbm, dst_hbm, idx_v, rows_v, sem0, sem1):
        worker = lax.axis_index("subcore") * mesh.num_cores + lax.axis_index("core")
        base = worker * per_worker
        sems = (sem0, sem1)
        pltpu.sync_copy(idx_hbm.at[pl.ds(base, per_worker)], idx_v)

        def gather(step, slot):
            return pltpu.make_async_copy(src_hbm.at[idx_v.at[pl.ds(step * win, win)]], rows_v.at[slot], sems[slot])

        gather(0, 0).start()

        @pl.loop(0, steps, step=2)
        def _(s):
            for slot in range(2):
                step = s + slot
                gather(step, slot).wait()

                @pl.when(step + 1 < steps)
                def _():
                    gather(step + 1, 1 - slot).start()

                pltpu.sync_copy(rows_v.at[slot], dst_hbm.at[pl.ds(base + step * win, win)])

    return gather_kernel(src, idx)


def _expert_kernel(blk_exp_ref, n_used_ref, x_ref, wgu_ref, bgu_ref, wd_ref, bd_ref, o_ref, wgu_bf, wd_bf):
    i = pl.program_id(0)

    @pl.when(jnp.logical_or(i == 0, blk_exp_ref[i] != blk_exp_ref[jnp.maximum(i - 1, 0)]))
    def _():
        wgu_bf[...] = wgu_ref[0, 0].astype(BF16)
        wd_bf[...] = wd_ref[0, 0].astype(BF16)

    @pl.when(i < n_used_ref[0])
    def _():
        x_lo, x_hi = _unpack_bf16_pairs(x_ref[...])
        hgu = (_dot(x_lo.astype(BF16), wgu_bf[:HALF_D, :]) + _dot(x_hi.astype(BF16), wgu_bf[HALF_D:, :])
               + bgu_ref[0])
        gate = jnp.minimum(hgu[:, :D_EXPERT], SWIGLU_LIMIT)
        up = jnp.clip(hgu[:, D_EXPERT:], -SWIGLU_LIMIT, SWIGLU_LIMIT)
        act = gate * jax.nn.sigmoid(gate * SWIGLU_ALPHA) * (up + 1.0)
        o_ref[...] = _pack_bf16_pairs(_dot(act.astype(BF16), wd_bf[...]) + bd_ref[0])

    @pl.when(i >= n_used_ref[0])
    def _():
        o_ref[...] = jnp.zeros_like(o_ref)


def _expert_mlp(xs, block_exp, n_used, layer, w_gu, b_gu, w_down, b_down):
    n_rows = xs.shape[0]
    bm = MOE_BLOCK_ROWS
    row = lambda i, be, nu: (i, 0)
    exp3 = lambda i, be, nu: (be[i], 0, 0)
    exp4 = lambda i, be, nu: (layer, be[i], 0, 0)
    grid_spec = pltpu.PrefetchScalarGridSpec(
        num_scalar_prefetch=2,
        grid=(n_rows // bm,),
        in_specs=[pl.BlockSpec((bm, HALF_D), row),
                  pl.BlockSpec((1, 1, D_MODEL, 2 * D_EXPERT), exp4), pl.BlockSpec((1, 1, 2 * D_EXPERT), exp3),
                  pl.BlockSpec((1, 1, D_EXPERT, D_MODEL), exp4), pl.BlockSpec((1, 1, D_MODEL), exp3)],
        out_specs=pl.BlockSpec((bm, HALF_D), row),
        scratch_shapes=[pltpu.VMEM((D_MODEL, 2 * D_EXPERT), BF16), pltpu.VMEM((D_EXPERT, D_MODEL), BF16)],
    )
    return pl.pallas_call(
        _expert_kernel,
        grid_spec=grid_spec,
        out_shape=jax.ShapeDtypeStruct((n_rows, HALF_D), I32),
        compiler_params=_cparams("arbitrary"),
        name="moe_experts",
    )(block_exp, n_used, xs, w_gu, b_gu.reshape(N_EXPERTS, 1, 2 * D_EXPERT),
      w_down, b_down.reshape(N_EXPERTS, 1, D_MODEL))


def _combine_kernel(y0_ref, y1_ref, y2_ref, y3_ref, gate_ref, res_ref, g_ref, b_ref, o_ref):
    gates = gate_ref[...]
    acc_lo, acc_hi = None, None
    for k, y_ref in enumerate((y0_ref, y1_ref, y2_ref, y3_ref)):
        lo, hi = _unpack_bf16_pairs(y_ref[...])
        gk = gates[:, k:k + 1]
        acc_lo = lo * gk if acc_lo is None else acc_lo + lo * gk
        acc_hi = hi * gk if acc_hi is None else acc_hi + hi * gk
    acc = jnp.concatenate([acc_lo, acc_hi], axis=1)
    o_ref[...] = _layer_norm_rows(DN_ALPHA * res_ref[...] + acc, g_ref[...], b_ref[...])


def _moe_block(h2d, h_packed, layer, w_router, b_router, w_gu, b_gu, w_down, b_down, g, b):
    n = h2d.shape[0]
    n_slots = n * TOP_K
    bm = MOE_BLOCK_ROWS
    top_idx, gates, rank, totals = _router(h2d, w_router, b_router)

    e_flat = top_idx.reshape(-1)
    order = jnp.argsort(e_flat).astype(I32)
    counts = totals[0, :N_EXPERTS].astype(I32)
    padded = (counts + bm - 1) // bm * bm
    start = jnp.cumsum(counts) - counts
    ends_p = jnp.cumsum(padded)
    pstart = ends_p - padded
    n_rows = n_slots + N_EXPERTS * bm
    n_blocks = n_rows // bm
    r = jnp.arange(n_rows, dtype=I32)
    e_r = jnp.minimum(jnp.searchsorted(ends_p, r, side="right"), N_EXPERTS - 1).astype(I32)
    j = r - pstart[e_r]
    valid = j < counts[e_r]
    slot_of_row = order[jnp.where(valid, start[e_r] + j, 0)]
    rows_tok = jnp.where(valid, slot_of_row // TOP_K, r % n).astype(I32)
    slot_pos = pstart[e_flat] + rank.reshape(-1)
    block_exp = e_r[::bm]
    n_used = (ends_p[-1] // bm).astype(I32).reshape(1)

    xs = _gather_rows(h_packed, rows_tok)
    ys = _expert_mlp(xs, block_exp, n_used, layer, w_gu, b_gu, w_down, b_down)
    yk = _gather_rows(ys, slot_pos.reshape(n, TOP_K).T.reshape(-1))

    tm = ROW_TILE
    row = lambda i: (i, 0)
    const = lambda i: (0, 0)
    choice = lambda k: (lambda i: (k * (n // tm) + i, 0))
    return pl.pallas_call(
        _combine_kernel,
        grid=(n // tm,),
        in_specs=[pl.BlockSpec((tm, HALF_D), choice(k)) for k in range(TOP_K)] + [
                  pl.BlockSpec((tm, TOP_K), row),
                  pl.BlockSpec((tm, D_MODEL), row), pl.BlockSpec((1, D_MODEL), const),
                  pl.BlockSpec((1, D_MODEL), const)],
        out_specs=pl.BlockSpec((tm, D_MODEL), row),
        out_shape=jax.ShapeDtypeStruct((n, D_MODEL), F32),
        compiler_params=_cparams("parallel"),
        name="moe_combine",
    )(yk, yk, yk, yk, gates, h2d, g.reshape(1, D_MODEL), b.reshape(1, D_MODEL))


def _s5_kernel(u_ref, bre_ref, bim_ref, cre_ref, cim_ref, are_ref, aim_ref, d_ref, y_ref,
               bu_re, bu_im, st_re, st_im, h_re, h_im, *, bsz):
    @pl.when(pl.program_id(0) == 0)
    def _():
        h_re[...] = jnp.zeros_like(h_re)
        h_im[...] = jnp.zeros_like(h_im)

    rows = u_ref.shape[0]
    first = lax.broadcasted_iota(I32, (SUBLANES, S5_ST_BLK), 0) < bsz
    for j in range(S5_LANE_BLOCKS):
        cin = slice(j * S5_IN_BLK, (j + 1) * S5_IN_BLK)
        cst = slice(j * S5_ST_BLK, (j + 1) * S5_ST_BLK)
        uj = u_ref[:, cin]
        ujb = uj.astype(BF16)
        bu_re[...] = _dot(ujb, bre_ref[j])
        bu_im[...] = _dot(ujb, bim_ref[j])
        ar = jnp.broadcast_to(are_ref[:, cst], (SUBLANES, S5_ST_BLK))
        ai = jnp.broadcast_to(aim_ref[:, cst], (SUBLANES, S5_ST_BLK))

        def step(i, carry):
            hr, hi = carry
            r0 = pl.multiple_of(i * SUBLANES, SUBLANES)
            vr = bu_re[pl.ds(r0, SUBLANES), :]
            vi = bu_im[pl.ds(r0, SUBLANES), :]
            h1r = ar * hr - ai * hi + vr
            h1i = ar * hi + ai * hr + vi
            h1rs = pltpu.roll(h1r, bsz, 0)
            h1is = pltpu.roll(h1i, bsz, 0)
            h2r = ar * h1rs - ai * h1is + vr
            h2i = ar * h1is + ai * h1rs + vi
            st_re[pl.ds(r0, SUBLANES), :] = jnp.where(first, h1r, h2r)
            st_im[pl.ds(r0, SUBLANES), :] = jnp.where(first, h1i, h2i)
            return pltpu.roll(h2r, bsz, 0), pltpu.roll(h2i, bsz, 0)

        hr, hi = lax.fori_loop(0, rows // SUBLANES, step, (h_re[:, cst], h_im[:, cst]))
        h_re[:, cst] = hr
        h_im[:, cst] = hi
        yj = _dot(st_re[...].astype(BF16), cre_ref[j]) + _dot(st_im[...].astype(BF16), cim_ref[j])
        yj = yj + d_ref[:, cin] * uj
        y_ref[:, cin] = jax.nn.gelu(yj).astype(BF16)


def _s5_block_diag(w, n_in, n_out):
    gpb = SSM_GROUPS // S5_LANE_BLOCKS
    w4 = w.reshape(S5_LANE_BLOCKS, gpb, n_in, n_out)
    eye = jnp.eye(gpb, dtype=w.dtype)
    return jnp.einsum("jgio,gh->jgiho", w4, eye).reshape(S5_LANE_BLOCKS, gpb * n_in, gpb * n_out)


def _s5_mixer_block(h2d, bsz, seq, w_in, log_dt, lam_re, lam_im, b_re, b_im, c_re, c_im, d, w_out, g, b):
    assert 2 * bsz == SUBLANES, "the scan packs two time steps of bsz rows into one 8-row tile"
    tm = ROW_TILE
    per_seq = seq // tm
    u_t = _matmul(h2d, w_in.astype(BF16), tm=tm, out_dtype=F32, grid=(bsz, per_seq),
                  x_map=lambda bb, i: (bb * per_seq + i, 0), out_map=lambda bb, i: (i, bb),
                  out_shape=(seq, bsz * D_MODEL)).reshape(seq * bsz, D_MODEL)

    dt = jnp.exp(log_dt)[:, None]
    mag = jnp.exp(lam_re * dt)
    a_re, a_im = mag * jnp.cos(lam_im * dt), mag * jnp.sin(lam_im * dt)
    den = lam_re * lam_re + lam_im * lam_im
    coef_re = ((a_re - 1.0) * lam_re + a_im * lam_im) / den
    coef_im = (a_im * lam_re - (a_re - 1.0) * lam_im) / den
    bb_re = coef_re[..., None] * b_re - coef_im[..., None] * b_im
    bb_im = coef_re[..., None] * b_im + coef_im[..., None] * b_re
    bre = _s5_block_diag(jnp.swapaxes(bb_re, 1, 2), SSM_GROUP, SSM_STATE).astype(BF16)
    bim = _s5_block_diag(jnp.swapaxes(bb_im, 1, 2), SSM_GROUP, SSM_STATE).astype(BF16)
    cre = _s5_block_diag(jnp.swapaxes(c_re, 1, 2), SSM_STATE, SSM_GROUP).astype(BF16)
    cim = _s5_block_diag(jnp.swapaxes(-c_im, 1, 2), SSM_STATE, SSM_GROUP).astype(BF16)
    n_state = SSM_GROUPS * SSM_STATE

    rows = S5_CHUNK * bsz
    row = lambda c: (c, 0)
    c2 = lambda c: (0, 0)
    c3 = lambda c: (0, 0, 0)
    y_t = pl.pallas_call(
        functools.partial(_s5_kernel, bsz=bsz),
        grid=(seq // S5_CHUNK,),
        in_specs=[pl.BlockSpec((rows, D_MODEL), row),
                  pl.BlockSpec(bre.shape, c3), pl.BlockSpec(bim.shape, c3),
                  pl.BlockSpec(cre.shape, c3), pl.BlockSpec(cim.shape, c3),
                  pl.BlockSpec((1, n_state), c2), pl.BlockSpec((1, n_state), c2), pl.BlockSpec((1, D_MODEL), c2)],
        out_specs=pl.BlockSpec((rows, D_MODEL), row),
        out_shape=jax.ShapeDtypeStruct((seq * bsz, D_MODEL), BF16),
        scratch_shapes=[pltpu.VMEM((rows, S5_ST_BLK), F32)] * 4 + [pltpu.VMEM((SUBLANES, n_state), F32)] * 2,
        compiler_params=_cparams("arbitrary"),
        name="s5_scan",
    )(u_t, bre, bim, cre, cim, a_re.reshape(1, n_state), a_im.reshape(1, n_state), d.reshape(1, D_MODEL))

    y2 = y_t.reshape(seq, bsz * D_MODEL)
    return _linear_residual_ln(
        [y2], [w_out.astype(BF16)], h2d, g, b, tm=tm, glu=True, grid=(bsz, per_seq),
        x_maps=[lambda bb, i: (i, bb)], res_map=lambda bb, i: (bb * per_seq + i, 0))


def _even_mixer_block(h2d, bsz, seq, w_in, qnorm_g, w_uq, w_uq_idx, kidx_g, kidx_b, w_out, g, b):
    qa, qi, ka, va, ki, wi, qb, kb, vb = _even_proj(h2d, bsz, seq, w_in, qnorm_g, w_uq, w_uq_idx, kidx_g, kidx_b)
    o_a = _dsa_attention(qa, qi, wi, ki, ka, va, bsz, seq)
    o_b = _stick_breaking(qb, kb, vb, bsz, seq)
    w_out = w_out.astype(BF16)
    return _linear_residual_ln([o_a, o_b], [w_out[:A_WIDTH], w_out[A_WIDTH:]], h2d, g, b, tm=ROW_TILE)


def kernel(x, mem, ev_w_in, ev_qnorm_g, ev_w_uq, ev_w_uq_idx, ev_kidx_ln_g, ev_kidx_ln_b, ev_w_out, od_w_in, od_log_dt, od_lambda_re, od_lambda_im, od_b_re, od_b_im, od_c_re, od_c_im, od_d, od_w_out, mix_ln_g, mix_ln_b, xa_w_q, xa_w_kv, xa_w_o, xa_ln_g, xa_ln_b, moe_w_router, moe_b_router, moe_w_gu, moe_b_gu, moe_w_down, moe_b_down, ffn_ln_g, ffn_ln_b):
    bsz, seq, _ = x.shape
    h = x.reshape(bsz * seq, D_MODEL)
    for layer in range(DEPTH):
        j = layer // 2
        if layer % 2 == 0:
            h = _even_mixer_block(h, bsz, seq, ev_w_in[j], ev_qnorm_g[j], ev_w_uq[j], ev_w_uq_idx[j],
                                  ev_kidx_ln_g[j], ev_kidx_ln_b[j], ev_w_out[j], mix_ln_g[layer], mix_ln_b[layer])
        else:
            h = _s5_mixer_block(h, bsz, seq, od_w_in[j], od_log_dt[j], od_lambda_re[j], od_lambda_im[j],
                                od_b_re[j], od_b_im[j], od_c_re[j], od_c_im[j], od_d[j], od_w_out[j],
                                mix_ln_g[layer], mix_ln_b[layer])
        h, h_packed = _cross_attention_block(h, mem, bsz, seq, xa_w_q[layer], xa_w_kv[layer], xa_w_o[layer],
                                             xa_ln_g[layer], xa_ln_b[layer])
        h = _moe_block(h, h_packed, layer, moe_w_router[layer], moe_b_router[layer], moe_w_gu, moe_b_gu[layer],
                       moe_w_down, moe_b_down[layer], ffn_ln_g[layer], ffn_ln_b[layer])
    return h.reshape(bsz, seq, D_MODEL)
```

```python
import functools
import math

import jax
import jax.numpy as jnp
from jax import lax
from jax.experimental import pallas as pl
from jax.experimental.pallas import tpu as pltpu
from jax.experimental.pallas import tpu_sc as plsc

F32 = jnp.float32
BF16 = jnp.bfloat16
I32 = jnp.int32

D_MODEL = 1024
DEPTH = 2
HEAD_DIM = 64
A_HEADS = 8
A_KV_HEADS = 2
A_REP = A_HEADS // A_KV_HEADS
Q_RANK = 256
IDX_HEADS = 8
IDX_DIM = 64
IDX_TOPK = 256
B_HEADS = 8
A_WIDTH = A_HEADS * HEAD_DIM
B_WIDTH = B_HEADS * HEAD_DIM
SSM_GROUP = 16
SSM_GROUPS = D_MODEL // SSM_GROUP
SSM_STATE = 64
XA_HEADS = 4
XA_HEAD_DIM = D_MODEL // XA_HEADS
N_EXPERTS = 32
TOP_K = 4
D_EXPERT = D_MODEL
SWIGLU_LIMIT = 7.0
SWIGLU_ALPHA = 1.702
ROPE_THETA = 500000.0
ROPE_HALF = HEAD_DIM // 8
LN_EPS = 1e-5
DN_ALPHA = (2 * DEPTH) ** 0.25

LANES = 128
SUBLANES = 8
VMEM_LIMIT_BYTES = 56 * 1024 * 1024

Q_BLOCK = 256
DSA_KEY_TILE = 512
DSA_ATT_TILE = 256
DSA_COUNT_ROWS = 8 * SUBLANES
SB_KEY_TILE = 256
ROW_TILE = 256
MOE_BLOCK_ROWS = 512
SC_GATHER_SLOT_BYTES = 128 * 1024
S5_CHUNK = 128
S5_LANE_BLOCKS = 4
S5_IN_BLK = D_MODEL // S5_LANE_BLOCKS
S5_ST_BLK = SSM_GROUPS * SSM_STATE // S5_LANE_BLOCKS

SB_EXIT_LOG = -104.0
NEG_BIG = -1e30
INT_MIN = -(2 ** 31)


def _cparams(*sem):
    return pltpu.CompilerParams(dimension_semantics=sem, vmem_limit_bytes=VMEM_LIMIT_BYTES)


def _dot(a, b):
    return jnp.dot(a, b, preferred_element_type=F32)


def _dot_nt(a, b):
    return lax.dot_general(a, b, (((1,), (1,)), ((), ())), preferred_element_type=F32)


def _layer_norm_rows(y, g, b):
    mu = jnp.mean(y, axis=-1, keepdims=True)
    d = y - mu
    var = jnp.mean(d * d, axis=-1, keepdims=True)
    return d * lax.rsqrt(var + LN_EPS) * g + b


def _mm_kernel(x_ref, w_ref, o_ref):
    o_ref[...] = _dot(x_ref[...].astype(BF16), w_ref[...]).astype(o_ref.dtype)


def _matmul(x, w, *, tm, out_dtype, x_map=None, out_map=None, grid=None, out_shape=None):
    m, k = x.shape
    n = w.shape[1]
    grid = grid or (m // tm,)
    x_map = x_map or (lambda i: (i, 0))
    out_map = out_map or (lambda i: (i, 0))
    out_shape = out_shape or (m, n)
    return pl.pallas_call(
        _mm_kernel,
        grid=grid,
        in_specs=[pl.BlockSpec((tm, k), x_map), pl.BlockSpec((k, n), lambda *a: (0, 0))],
        out_specs=pl.BlockSpec((tm, n), out_map),
        out_shape=jax.ShapeDtypeStruct(out_shape, out_dtype),
        compiler_params=_cparams(*(("parallel",) * len(grid))),
        name="matmul",
    )(x, w)


def _lin_ln_kernel(*refs, n_in, glu):
    xs, ws = refs[:n_in], refs[n_in:2 * n_in]
    res_ref, g_ref, b_ref, o_ref = refs[2 * n_in:]
    acc = _dot(xs[0][...].astype(BF16), ws[0][...])
    for x_ref, w_ref in zip(xs[1:], ws[1:]):
        acc = acc + _dot(x_ref[...].astype(BF16), w_ref[...])
    if glu:
        acc = acc[:, :D_MODEL] * jax.nn.sigmoid(acc[:, D_MODEL:])
    y = DN_ALPHA * res_ref[...] + acc
    o_ref[...] = _layer_norm_rows(y, g_ref[...], b_ref[...])


def _linear_residual_ln(xs, ws, res, g, b, *, tm, glu=False, grid=None, x_maps=None, res_map=None):
    n_rows = res.shape[0]
    grid = grid or (n_rows // tm,)
    x_maps = x_maps or [lambda i: (i, 0)] * len(xs)
    res_map = res_map or (lambda i: (i, 0))
    const = lambda *a: (0, 0)
    in_specs = [pl.BlockSpec((tm, w.shape[0]), m) for w, m in zip(ws, x_maps)]
    in_specs += [pl.BlockSpec(w.shape, const) for w in ws]
    in_specs += [pl.BlockSpec((tm, D_MODEL), res_map), pl.BlockSpec((1, D_MODEL), const),
                 pl.BlockSpec((1, D_MODEL), const)]
    return pl.pallas_call(
        functools.partial(_lin_ln_kernel, n_in=len(xs), glu=glu),
        grid=grid,
        in_specs=in_specs,
        out_specs=pl.BlockSpec((tm, D_MODEL), res_map),
        out_shape=jax.ShapeDtypeStruct((n_rows, D_MODEL), F32),
        compiler_params=_cparams(*(("parallel",) * len(grid))),
        name="linear_residual_ln",
    )(*xs, *ws, res, g.reshape(1, D_MODEL), b.reshape(1, D_MODEL))


_EV_CQ, _EV_KA, _EV_VA, _EV_KI, _EV_QB = 0, 256, 384, 512, 640
_EV_KB = _EV_QB + B_WIDTH
_EV_VB = _EV_KB + B_WIDTH
_EV_COLS = _EV_VB + B_WIDTH


def _rope_tables(seq):
    inv = ROPE_THETA ** (-jnp.arange(ROPE_HALF, dtype=F32) / ROPE_HALF)
    ang = jnp.arange(seq, dtype=F32)[:, None] * inv[None, :]
    cos, sin = jnp.cos(ang), jnp.sin(ang)
    rest = HEAD_DIM - 2 * ROPE_HALF
    zh = jnp.zeros((seq, ROPE_HALF), F32)
    c = jnp.concatenate([cos, cos, jnp.ones((seq, rest), F32)], axis=1)
    s1 = jnp.concatenate([-sin, zh, jnp.zeros((seq, rest), F32)], axis=1)
    s2 = jnp.concatenate([zh, sin, jnp.zeros((seq, rest), F32)], axis=1)
    rep = LANES // HEAD_DIM
    return jnp.tile(c, (1, rep)), jnp.tile(s1, (1, rep)), jnp.tile(s2, (1, rep))


def _even_proj_kernel(x_ref, w_ref, qg_ref, wuq_ref, wuqi_ref, lg_ref, lb_ref, c_ref, s1_ref, s2_ref,
                      qa_ref, qi_ref, ka_ref, va_ref, ki_ref, wi_ref, qb_ref, kb_ref, vb_ref):
    p = _dot(x_ref[...].astype(BF16), w_ref[...])
    c, s1, s2 = c_ref[...], s1_ref[...], s2_ref[...]

    def rope(t):
        return (t * c + pltpu.roll(t, LANES - ROPE_HALF, 1) * s1 + pltpu.roll(t, ROPE_HALF, 1) * s2)

    cq = p[:, _EV_CQ:_EV_CQ + Q_RANK]
    cn = cq * lax.rsqrt(jnp.mean(cq * cq, axis=-1, keepdims=True) + LN_EPS) * qg_ref[...]
    cnb = cn.astype(BF16)
    qa = _dot(cnb, wuq_ref[...])
    qi = _dot(cnb, wuqi_ref[...])
    low = lax.broadcasted_iota(I32, c.shape, 1) < HEAD_DIM
    for j in range(A_WIDTH // LANES):
        sl = slice(j * LANES, (j + 1) * LANES)
        pair = rope(qa[:, sl]) * (HEAD_DIM ** -0.5)
        for e, src in enumerate((pair, pltpu.roll(pair, HEAD_DIM, 1))):
            h = 2 * j + e
            qa_ref[:, h * LANES:(h + 1) * LANES] = jnp.where(low, src, 0.0).astype(BF16)
        qi_ref[:, sl] = (rope(qi[:, sl]) * (IDX_DIM ** -0.5)).astype(BF16)
    kpair = rope(p[:, _EV_KA:_EV_KA + LANES])
    vpair = p[:, _EV_VA:_EV_VA + LANES]
    v_pad = jnp.where(lax.broadcasted_iota(I32, c.shape, 1) == HEAD_DIM, 1.0, 0.0)
    for g, (ks, vs) in enumerate(((kpair, vpair), (pltpu.roll(kpair, HEAD_DIM, 1), pltpu.roll(vpair, HEAD_DIM, 1)))):
        ka_ref[:, g * LANES:(g + 1) * LANES] = jnp.where(low, ks, 0.0).astype(BF16)
        va_ref[:, g * LANES:(g + 1) * LANES] = jnp.where(low, vs, v_pad).astype(BF16)

    t = p[:, _EV_KI:_EV_KI + LANES]
    lane = lax.broadcasted_iota(I32, t.shape, 1)
    is_k = lane < IDX_DIM
    mu = jnp.sum(jnp.where(is_k, t, 0.0), axis=-1, keepdims=True) * (1.0 / IDX_DIM)
    d = jnp.where(is_k, t - mu, 0.0)
    var = jnp.sum(d * d, axis=-1, keepdims=True) * (1.0 / IDX_DIM)
    kin = d * lax.rsqrt(var + LN_EPS) * lg_ref[...] + lb_ref[...]
    ki_ref[...] = rope(kin)[:, :IDX_DIM].astype(BF16)
    wi_ref[...] = t[:, IDX_DIM:IDX_DIM + IDX_HEADS] * (IDX_HEADS ** -0.5)

    qb_ref[...] = (p[:, _EV_QB:_EV_KB] * (HEAD_DIM ** -0.5)).astype(BF16)
    kb_ref[...] = p[:, _EV_KB:_EV_VB].astype(BF16)
    vb_ref[...] = p[:, _EV_VB:_EV_COLS].astype(BF16)


def _even_proj(x2d, bsz, seq, w_in, qnorm_g, w_uq, w_uq_idx, kidx_g, kidx_b):
    n = x2d.shape[0]
    tm = ROW_TILE
    per_seq = seq // tm
    c0 = Q_RANK + 2 * A_KV_HEADS * HEAD_DIM + IDX_DIM + IDX_HEADS
    w_pack = jnp.concatenate(
        [w_in[:, :c0], jnp.zeros((D_MODEL, _EV_QB - c0), w_in.dtype), w_in[:, c0:]], axis=1).astype(BF16)
    pad = LANES - IDX_DIM
    lg = jnp.concatenate([kidx_g, jnp.zeros((pad,), F32)]).reshape(1, LANES)
    lb = jnp.concatenate([kidx_b, jnp.zeros((pad,), F32)]).reshape(1, LANES)
    c, s1, s2 = _rope_tables(seq)
    row = lambda i: (i, 0)
    const = lambda i: (0, 0)
    pos = lambda i: (i % per_seq, 0)
    head_shape = jax.ShapeDtypeStruct((n, B_WIDTH), BF16)
    head_spec = pl.BlockSpec((tm, B_WIDTH), row)
    return pl.pallas_call(
        _even_proj_kernel,
        grid=(n // tm,),
        in_specs=[pl.BlockSpec((tm, D_MODEL), row), pl.BlockSpec((D_MODEL, _EV_COLS), const),
                  pl.BlockSpec((1, Q_RANK), const), pl.BlockSpec((Q_RANK, A_WIDTH), const),
                  pl.BlockSpec((Q_RANK, IDX_HEADS * IDX_DIM), const),
                  pl.BlockSpec((1, LANES), const), pl.BlockSpec((1, LANES), const),
                  pl.BlockSpec((tm, LANES), pos), pl.BlockSpec((tm, LANES), pos), pl.BlockSpec((tm, LANES), pos)],
        out_specs=[pl.BlockSpec((tm, A_HEADS * LANES), row), pl.BlockSpec((tm, IDX_HEADS * IDX_DIM), row),
                   pl.BlockSpec((tm, A_KV_HEADS * LANES), row), pl.BlockSpec((tm, A_KV_HEADS * LANES), row),
                   pl.BlockSpec((tm, IDX_DIM), row), pl.BlockSpec((tm, IDX_HEADS), row),
                   head_spec, head_spec, head_spec],
        out_shape=[jax.ShapeDtypeStruct((n, A_HEADS * LANES), BF16), jax.ShapeDtypeStruct((n, IDX_HEADS * IDX_DIM), BF16),
                   jax.ShapeDtypeStruct((n, A_KV_HEADS * LANES), BF16),
                   jax.ShapeDtypeStruct((n, A_KV_HEADS * LANES), BF16),
                   jax.ShapeDtypeStruct((n, IDX_DIM), BF16), jax.ShapeDtypeStruct((n, IDX_HEADS), F32),
                   head_shape, head_shape, head_shape],
        compiler_params=_cparams("parallel"),
        name="even_proj",
    )(x2d, w_pack, qnorm_g.reshape(1, Q_RANK), w_uq.astype(BF16), w_uq_idx.astype(BF16), lg, lb, c, s1, s2)


def _key_to_float(key):
    bits = key ^ ((key >> 31) & jnp.int32(0x7FFFFFFF))
    return lax.bitcast_convert_type(bits, F32)


def _high_half(x):
    bits = lax.bitcast_convert_type(x, jnp.uint32) & jnp.uint32(0xFFFF0000)
    return lax.bitcast_convert_type(bits, F32).astype(BF16)


def _dsa_kernel(qa_ref, qi_ref, wit_ref, ki_ref, ka_ref, vat_ref, o_ref, sc_scr, hi_scr, *, topk, ts, ta):
    seq = sc_scr.shape[0]
    qb = pl.program_id(1)
    q0 = qb * Q_BLOCK
    nkt = (q0 + Q_BLOCK - 1) // ts + 1
    t_row = q0 + lax.broadcasted_iota(I32, (1, Q_BLOCK), 1)
    key = lax.broadcasted_iota(I32, (ts, Q_BLOCK), 0)
    kf = jnp.float32(topk)

    qi = qi_ref[0]
    qs = jnp.concatenate([qi[:, h * IDX_DIM:(h + 1) * IDX_DIM] for h in range(IDX_HEADS)], axis=0)
    wit = wit_ref[0]

    def score_tile(kt, carry):
        off = pl.multiple_of(kt * ts, ts)
        s_all = _dot_nt(ki_ref[0, pl.ds(off, ts), :], qs)
        acc = jnp.zeros((ts, Q_BLOCK), F32)
        for h in range(IDX_HEADS):
            acc = acc + jnp.maximum(s_all[:, h * Q_BLOCK:(h + 1) * Q_BLOCK], 0.0) * wit[h:h + 1, :]
        val = jnp.where(off + key <= t_row, acc, -jnp.inf)
        sc_scr[pl.ds(off, ts), :] = val
        hi_scr[pl.ds(off, ts), :] = _high_half(val)
        return carry

    lax.fori_loop(0, nkt, score_tile, 0)

    def count(pred):
        def body(kt, acc):
            off = pl.multiple_of(kt * ts, ts)
            ind = pred(sc_scr[pl.ds(off, ts), :], off + key)
            return acc + jnp.sum(ind.reshape(ts // DSA_COUNT_ROWS, DSA_COUNT_ROWS, Q_BLOCK), axis=0)
        acc = lax.fori_loop(0, nkt, body, jnp.zeros((DSA_COUNT_ROWS, Q_BLOCK), F32))
        return jnp.sum(acc, axis=0, keepdims=True)

    one_h, zero_h = jnp.ones((), BF16), jnp.zeros((), BF16)

    def count_high(c_hi):
        def body(kt, acc):
            off = pl.multiple_of(kt * ts, ts)
            ind = jnp.where(hi_scr[pl.ds(off, ts), :] >= c_hi, one_h, zero_h)
            part = ind[:DSA_COUNT_ROWS]
            for j in range(1, ts // DSA_COUNT_ROWS):
                part = part + ind[j * DSA_COUNT_ROWS:(j + 1) * DSA_COUNT_ROWS]
            return acc + part.astype(F32)
        acc = lax.fori_loop(0, nkt, body, jnp.zeros((DSA_COUNT_ROWS, Q_BLOCK), F32))
        return jnp.sum(acc, axis=0, keepdims=True)

    def high_step(i, base):
        cand = base + jnp.left_shift(jnp.int32(1), 31 - i)
        cnt = count_high(_high_half(_key_to_float(cand)))
        return jnp.where(cnt >= kf, cand, base)

    def bit_step(i, base):
        cand = base + jnp.left_shift(jnp.int32(1), 31 - i)
        cf = _key_to_float(cand)
        cnt = count(lambda sc, idx: jnp.where(sc >= cf, 1.0, 0.0))
        return jnp.where(cnt >= kf, cand, base)

    base = lax.fori_loop(0, 16, high_step, jnp.full((1, Q_BLOCK), INT_MIN, I32))
    base = lax.fori_loop(16, 32, bit_step, base)
    thr = jnp.where(base == INT_MIN, -jnp.inf, _key_to_float(base))

    cnt_ge = count(lambda sc, idx: jnp.where(sc >= thr, 1.0, 0.0))
    tied = jnp.logical_and(cnt_ge > kf, thr > -jnp.inf)
    any_tied = jnp.max(jnp.where(tied, 1.0, 0.0)) > 0.0
    seq_bits = max(1, int(math.ceil(math.log2(seq))))

    def tie_cut():
        cnt_gt = count(lambda sc, idx: jnp.where(sc > thr, 1.0, 0.0))
        need = kf - cnt_gt

        def idx_step(i, pos):
            cand = pos + jnp.left_shift(jnp.int32(1), seq_bits - 1 - i)
            cnt = count(lambda sc, idx: jnp.where(sc == thr, jnp.where(idx < cand, 1.0, 0.0), 0.0))
            return jnp.where(cnt < need, cand, pos)

        return lax.fori_loop(0, seq_bits, idx_step, jnp.zeros((1, Q_BLOCK), I32))

    cut = lax.cond(any_tied, tie_cut, lambda: jnp.full((1, Q_BLOCK), seq, I32))
    cut = jnp.where(tied, cut, seq)

    nkt_a = (q0 + Q_BLOCK - 1) // ta + 1
    key_a = lax.broadcasted_iota(I32, (ta, Q_BLOCK), 0)
    cols = A_REP * Q_BLOCK
    qg = [jnp.concatenate([qa_ref[0, :, (g * A_REP + r) * LANES:(g * A_REP + r + 1) * LANES]
                           for r in range(A_REP)], axis=0) for g in range(A_KV_HEADS)]

    def att_pair(i, carry):
        offs = [pl.multiple_of((2 * i + e) * ta, ta) for e in range(2)]
        logits = [_dot_nt(ka_ref[0, pl.ds(offs[e], ta), g * LANES:(g + 1) * LANES], qg[g])
                  for e in range(2) for g in range(A_KV_HEADS)]
        out = []
        for e in range(2):
            sc = sc_scr[pl.ds(offs[e], ta), :]
            idx = offs[e] + key_a
            keep = jnp.where(sc > thr, 0.0, jnp.where(sc == thr, jnp.where(idx <= cut, 0.0, NEG_BIG), NEG_BIG))
            bias = jnp.where(idx <= t_row, keep, NEG_BIG)
            bias = jnp.concatenate([bias] * A_REP, axis=1)
            for g in range(A_KV_HEADS):
                m, acc = carry[e * A_KV_HEADS + g]
                s = logits[e * A_KV_HEADS + g] + bias
                m_new = jnp.maximum(m, jnp.max(s, axis=0, keepdims=True))
                p = jnp.exp(s - m_new)
                vt = vat_ref[0, g * LANES:(g + 1) * LANES, pl.ds(offs[e], ta)]
                out.append((m_new, jnp.exp(m - m_new) * acc + _dot(vt, p.astype(BF16))))
        return tuple(out)

    init = tuple((jnp.full((1, cols), NEG_BIG, F32), jnp.zeros((LANES, cols), F32))
                 for _ in range(2 * A_KV_HEADS))
    final = lax.fori_loop(0, (nkt_a + 1) // 2, att_pair, init)
    low = lax.broadcasted_iota(I32, (Q_BLOCK, LANES), 1) < HEAD_DIM
    outs = []
    for g in range(A_KV_HEADS):
        (m0, acc0), (m1, acc1) = final[g], final[A_KV_HEADS + g]
        m = jnp.maximum(m0, m1)
        acc = jnp.exp(m0 - m) * acc0 + jnp.exp(m1 - m) * acc1
        og = acc / acc[HEAD_DIM:HEAD_DIM + 1, :]
        outs += [og[:, r * Q_BLOCK:(r + 1) * Q_BLOCK].T for r in range(A_REP)]
    for j in range(A_HEADS // 2):
        pair = jnp.where(low, outs[2 * j], pltpu.roll(outs[2 * j + 1], HEAD_DIM, 1))
        o_ref[0, :, j * LANES:(j + 1) * LANES] = pair.astype(BF16)


def _dsa_attention(qa, qi, wi, ki, ka, va, bsz, seq):
    topk = min(IDX_TOPK, seq // 4)
    ts = min(DSA_KEY_TILE, seq)
    ta = min(DSA_ATT_TILE, seq // 2)
    assert seq % (2 * ta) == 0 and seq % ts == 0, "the attention loop walks the key tiles in pairs"
    blk = lambda b, i: (b, i, 0)
    full = lambda b, i: (b, 0, 0)
    r3 = lambda a: a.reshape(bsz, seq, a.shape[-1])
    wit = jnp.swapaxes(r3(wi), 1, 2)
    vat = jnp.swapaxes(r3(va), 1, 2)
    return pl.pallas_call(
        functools.partial(_dsa_kernel, topk=topk, ts=ts, ta=ta),
        grid=(bsz, seq // Q_BLOCK),
        in_specs=[pl.BlockSpec((1, Q_BLOCK, A_HEADS * LANES), blk),
                  pl.BlockSpec((1, Q_BLOCK, IDX_HEADS * IDX_DIM), blk),
                  pl.BlockSpec((1, IDX_HEADS, Q_BLOCK), lambda b, i: (b, 0, i)),
                  pl.BlockSpec((1, seq, IDX_DIM), full),
                  pl.BlockSpec((1, seq, A_KV_HEADS * LANES), full),
                  pl.BlockSpec((1, A_KV_HEADS * LANES, seq), full)],
        out_specs=pl.BlockSpec((1, Q_BLOCK, A_WIDTH), blk),
        out_shape=jax.ShapeDtypeStruct((bsz, seq, A_WIDTH), BF16),
        scratch_shapes=[pltpu.VMEM((seq, Q_BLOCK), F32), pltpu.VMEM((seq, Q_BLOCK), BF16)],
        compiler_params=_cparams("parallel", "parallel"),
        name="dsa_attention",
    )(r3(qa), r3(qi), wit, r3(ki), r3(ka), vat).reshape(bsz * seq, A_WIDTH)


def _sb_kernel(q_ref, k_ref, v_ref, u_ref, o_ref, acc_scr, run_scr, *, tk):
    q0 = pl.program_id(1) * Q_BLOCK
    t_col = q0 + lax.broadcasted_iota(I32, (Q_BLOCK, 1), 0)
    lane = lax.broadcasted_iota(I32, (Q_BLOCK, tk), 1)
    low = lax.broadcasted_iota(I32, (Q_BLOCK, LANES), 1) < HEAD_DIM
    upper = u_ref[...]
    nkt = (q0 + Q_BLOCK - 1) // tk + 1
    q = q_ref[0]
    zero = jnp.zeros((Q_BLOCK, LANES), BF16)
    qm = []
    for h in range(B_HEADS):
        pair = q[:, (h // 2) * LANES:(h // 2 + 1) * LANES]
        qm.append(jnp.where(low, pair, zero) if h % 2 == 0 else jnp.where(low, zero, pair))
    acc_scr[...] = jnp.zeros_like(acc_scr)
    run_scr[...] = jnp.zeros_like(run_scr)

    def cond(carry):
        i, worst = carry
        return jnp.logical_and(i < nkt, worst >= SB_EXIT_LOG)

    def body(carry):
        i, _ = carry
        off = pl.multiple_of((nkt - 1 - i) * tk, tk)
        strict = off + lane < t_col
        worst = None
        for p in range(B_HEADS // 2):
            cols = slice(p * LANES, (p + 1) * LANES)
            kp = k_ref[0, pl.ds(off, tk), cols]
            vp = v_ref[0, pl.ds(off, tk), cols]
            outs = []
            for e in range(2):
                h = 2 * p + e
                run = run_scr[h]
                z = _dot_nt(qm[h], kp)
                softplus = jnp.maximum(z, 0.0) + jnp.log(1.0 + jnp.exp(-jnp.abs(z)))
                log_1mb = jnp.where(strict, -softplus, 0.0)
                hi = log_1mb.astype(BF16)
                lo = (log_1mb - hi.astype(F32)).astype(BF16)
                after = _dot(hi, upper) + _dot(lo, upper) + run
                a = jnp.where(strict, jnp.exp(z - softplus + after), 0.0)
                outs.append(_dot(a.astype(BF16), vp))
                run = run + jnp.sum(log_1mb, axis=1, keepdims=True)
                run_scr[h] = run
                worst = run if worst is None else jnp.maximum(worst, run)
            acc_scr[:, cols] += jnp.where(low, outs[0], outs[1])
        return i + 1, jnp.max(worst)

    lax.while_loop(cond, body, (jnp.int32(0), jnp.float32(0.0)))
    o_ref[0] = acc_scr[...].astype(BF16)


def _stick_breaking(qb, kb, vb, bsz, seq):
    tk = min(SB_KEY_TILE, seq)
    r = lax.broadcasted_iota(I32, (tk, tk), 0)
    c = lax.broadcasted_iota(I32, (tk, tk), 1)
    upper = jnp.where(r > c, 1.0, 0.0).astype(BF16)
    blk = lambda b, i: (b, i, 0)
    full = lambda b, i: (b, 0, 0)
    r3 = lambda a: a.reshape(bsz, seq, B_WIDTH)
    return pl.pallas_call(
        functools.partial(_sb_kernel, tk=tk),
        grid=(bsz, seq // Q_BLOCK),
        in_specs=[pl.BlockSpec((1, Q_BLOCK, B_WIDTH), blk), pl.BlockSpec((1, seq, B_WIDTH), full),
                  pl.BlockSpec((1, seq, B_WIDTH), full), pl.BlockSpec((tk, tk), lambda b, i: (0, 0))],
        out_specs=pl.BlockSpec((1, Q_BLOCK, B_WIDTH), blk),
        out_shape=jax.ShapeDtypeStruct((bsz, seq, B_WIDTH), BF16),
        scratch_shapes=[pltpu.VMEM((Q_BLOCK, B_WIDTH), F32), pltpu.VMEM((B_HEADS, Q_BLOCK, 1), F32)],
        compiler_params=_cparams("parallel", "arbitrary"),
        name="stick_breaking",
    )(r3(qb), r3(kb), r3(vb), upper).reshape(bsz * seq, B_WIDTH)


HALF_D = D_MODEL // 2
U32 = jnp.uint32
HIGH16 = 0xFFFF0000


def _pack_bf16_pairs(x):
    def bits(v):
        return lax.bitcast_convert_type(v.astype(BF16).astype(F32), U32)
    word = (bits(x[:, HALF_D:]) & U32(HIGH16)) | (bits(x[:, :HALF_D]) >> 16)
    return lax.bitcast_convert_type(word, I32)


def _unpack_bf16_pairs(word):
    u = lax.bitcast_convert_type(word, U32)
    return lax.bitcast_convert_type(u << 16, F32), lax.bitcast_convert_type(u & U32(HIGH16), F32)


def _xattn_kernel(h_ref, wq_ref, kv_ref, wo_ref, g_ref, b_ref, o_ref, packed_ref):
    h = h_ref[...]
    q = (_dot(h.astype(BF16), wq_ref[...]) * (XA_HEAD_DIM ** -0.5)).astype(BF16)
    kv = kv_ref[0]
    outs = []
    for hd in range(XA_HEADS):
        sl = slice(hd * XA_HEAD_DIM, (hd + 1) * XA_HEAD_DIM)
        s = _dot_nt(q[:, sl], kv[:, sl])
        p = jnp.exp(s - jnp.max(s, axis=1, keepdims=True))
        vh = kv[:, D_MODEL + hd * XA_HEAD_DIM:D_MODEL + (hd + 1) * XA_HEAD_DIM]
        outs.append((_dot(p.astype(BF16), vh) / jnp.sum(p, axis=1, keepdims=True)).astype(BF16))
    y = _dot(jnp.concatenate(outs, axis=1), wo_ref[...])
    out = _layer_norm_rows(DN_ALPHA * h + y, g_ref[...], b_ref[...])
    o_ref[...] = out
    packed_ref[...] = _pack_bf16_pairs(out)


def _cross_attention_block(h2d, mem, bsz, seq, w_q, w_kv, w_o, g, b):
    tm = ROW_TILE
    per_seq = seq // tm
    mem_len = mem.shape[1]
    kv = _matmul(mem.reshape(bsz * mem_len, D_MODEL), w_kv.astype(BF16), tm=mem_len, out_dtype=BF16)
    kv = kv.reshape(bsz, mem_len, 2 * D_MODEL)
    row = lambda i: (i, 0)
    const = lambda i: (0, 0)
    return pl.pallas_call(
        _xattn_kernel,
        grid=(bsz * per_seq,),
        in_specs=[pl.BlockSpec((tm, D_MODEL), row), pl.BlockSpec((D_MODEL, D_MODEL), const),
                  pl.BlockSpec((1, mem_len, 2 * D_MODEL), lambda i: (i // per_seq, 0, 0)),
                  pl.BlockSpec((D_MODEL, D_MODEL), const),
                  pl.BlockSpec((1, D_MODEL), const), pl.BlockSpec((1, D_MODEL), const)],
        out_specs=[pl.BlockSpec((tm, D_MODEL), row), pl.BlockSpec((tm, HALF_D), row)],
        out_shape=[jax.ShapeDtypeStruct(h2d.shape, F32), jax.ShapeDtypeStruct((h2d.shape[0], HALF_D), I32)],
        compiler_params=_cparams("parallel"),
        name="cross_attention",
    )(h2d, w_q.astype(BF16), kv, w_o.astype(BF16), g.reshape(1, D_MODEL), b.reshape(1, D_MODEL))


def _router_kernel(h_ref, w_ref, b_ref, tri_ref, idx_ref, gate_ref, rank_ref, cnt_ref, run_scr):
    @pl.when(pl.program_id(0) == 0)
    def _():
        run_scr[...] = jnp.zeros_like(run_scr)

    h = h_ref[...]
    hh = h.astype(BF16)
    hl = (h - hh.astype(F32)).astype(BF16)
    w = w_ref[...]
    wh = w.astype(BF16)
    wl = (w - wh.astype(F32)).astype(BF16)
    logits = _dot(hh, wh) + _dot(hl, wh) + _dot(hh, wl) + b_ref[...]
    lane = lax.broadcasted_iota(I32, logits.shape, 1).astype(F32)
    vals, sels = [], []
    onehot = jnp.zeros(logits.shape, F32)
    for k in range(TOP_K):
        m = jnp.max(logits, axis=1, keepdims=True)
        sel = jnp.min(jnp.where(logits == m, lane, float(LANES)), axis=1, keepdims=True)
        idx_ref[:, k:k + 1] = sel.astype(I32)
        vals.append(m)
        sels.append(sel)
        onehot = onehot + jnp.where(lane == sel, 1.0, 0.0)
        logits = jnp.where(lane == sel, -jnp.inf, logits)
    es = [jnp.exp(v - vals[0]) for v in vals]
    tot = es[0] + es[1] + es[2] + es[3]
    for k in range(TOP_K):
        gate_ref[:, k:k + 1] = es[k] / tot

    earlier = _dot(tri_ref[...], onehot.astype(BF16)) + run_scr[...]
    for k in range(TOP_K):
        rank = jnp.sum(jnp.where(lane == sels[k], earlier, 0.0), axis=1, keepdims=True)
        rank_ref[:, k:k + 1] = rank.astype(I32)
    run = run_scr[...] + jnp.sum(onehot, axis=0, keepdims=True)
    run_scr[...] = run
    cnt_ref[...] = run


def _router(h2d, w_router, b_router):
    n = h2d.shape[0]
    tm = 2 * ROW_TILE
    pad = LANES - N_EXPERTS
    w = jnp.concatenate([w_router, jnp.zeros((D_MODEL, pad), F32)], axis=1)
    b = jnp.concatenate([b_router, jnp.full((pad,), NEG_BIG, F32)]).reshape(1, LANES)
    r = lax.broadcasted_iota(I32, (tm, tm), 0)
    c = lax.broadcasted_iota(I32, (tm, tm), 1)
    tri = jnp.where(c < r, 1.0, 0.0).astype(BF16)
    row = lambda i: (i, 0)
    const = lambda i: (0, 0)
    return pl.pallas_call(
        _router_kernel,
        grid=(n // tm,),
        in_specs=[pl.BlockSpec((tm, D_MODEL), row), pl.BlockSpec((D_MODEL, LANES), const),
                  pl.BlockSpec((1, LANES), const), pl.BlockSpec((tm, tm), const)],
        out_specs=[pl.BlockSpec((tm, TOP_K), row), pl.BlockSpec((tm, TOP_K), row),
                   pl.BlockSpec((tm, TOP_K), row), pl.BlockSpec((1, LANES), const)],
        out_shape=[jax.ShapeDtypeStruct((n, TOP_K), I32), jax.ShapeDtypeStruct((n, TOP_K), F32),
                   jax.ShapeDtypeStruct((n, TOP_K), I32), jax.ShapeDtypeStruct((1, LANES), F32)],
        scratch_shapes=[pltpu.VMEM((1, LANES), F32)],
        compiler_params=_cparams("arbitrary"),
        name="moe_router",
    )(h2d, w, b, tri)


def _gather_rows(src, idx):
    n_out = idx.shape[0]
    width = src.shape[1]
    win = SC_GATHER_SLOT_BYTES // (width * src.dtype.itemsize)
    mesh = plsc.VectorSubcoreMesh(core_axis_name="core", subcore_axis_name="subcore")
    n_workers = mesh.num_cores * mesh.num_subcores
    per_worker = n_out // n_workers
    steps = per_worker // win
    assert per_worker * n_workers == n_out and steps * win == per_worker and steps % 2 == 0

    @functools.partial(
        pl.kernel, out_type=jax.ShapeDtypeStruct((n_out, width), src.dtype), mesh=mesh,
        scratch_types=[pltpu.VMEM((per_worker,), I32), pltpu.VMEM((2, win, width), src.dtype),
                       pltpu.SemaphoreType.DMA, pltpu.SemaphoreType.DMA])
    def gather_kernel(src_hbm, idx_hbm, dst_hbm, idx_v, rows_v, sem0, sem1):
        worker = lax.axis_index("subcore") * mesh.num_cores + lax.axis_index("core")
        base = worker * per_worker
        sems = (sem0, sem1)
        pltpu.sync_copy(idx_hbm.at[pl.ds(base, per_worker)], idx_v)

        def gather(step, slot):
            return pltpu.make_async_copy(src_hbm.at[idx_v.at[pl.ds(step * win, win)]], rows_v.at[slot], sems[slot])

        gather(0, 0).start()

        @pl.loop(0, steps, step=2)
        def _(s):
            for slot in range(2):
                step = s + slot
                gather(step, slot).wait()

                @pl.when(step + 1 < steps)
                def _():
                    gather(step + 1, 1 - slot).start()

                pltpu.sync_copy(rows_v.at[slot], dst_hbm.at[pl.ds(base + step * win, win)])

    return gather_kernel(src, idx)


def _expert_kernel(blk_exp_ref, n_used_ref, x_ref, wgu_ref, bgu_ref, wd_ref, bd_ref, o_ref, wgu_bf, wd_bf):
    i = pl.program_id(0)

    @pl.when(jnp.logical_or(i == 0, blk_exp_ref[i] != blk_exp_ref[jnp.maximum(i - 1, 0)]))
    def _():
        wgu_bf[...] = wgu_ref[0, 0].astype(BF16)
        wd_bf[...] = wd_ref[0, 0].astype(BF16)

    @pl.when(i < n_used_ref[0])
    def _():
        x_lo, x_hi = _unpack_bf16_pairs(x_ref[...])
        hgu = (_dot(x_lo.astype(BF16), wgu_bf[:HALF_D, :]) + _dot(x_hi.astype(BF16), wgu_bf[HALF_D:, :])
               + bgu_ref[0])
        gate = jnp.minimum(hgu[:, :D_EXPERT], SWIGLU_LIMIT)
        up = jnp.clip(hgu[:, D_EXPERT:], -SWIGLU_LIMIT, SWIGLU_LIMIT)
        act = gate * jax.nn.sigmoid(gate * SWIGLU_ALPHA) * (up + 1.0)
        o_ref[...] = _pack_bf16_pairs(_dot(act.astype(BF16), wd_bf[...]) + bd_ref[0])

    @pl.when(i >= n_used_ref[0])
    def _():
        o_ref[...] = jnp.zeros_like(o_ref)


def _expert_mlp(xs, block_exp, n_used, layer, w_gu, b_gu, w_down, b_down):
    n_rows = xs.shape[0]
    bm = MOE_BLOCK_ROWS
    row = lambda i, be, nu: (i, 0)
    exp3 = lambda i, be, nu: (be[i], 0, 0)
    exp4 = lambda i, be, nu: (layer, be[i], 0, 0)
    grid_spec = pltpu.PrefetchScalarGridSpec(
        num_scalar_prefetch=2,
        grid=(n_rows // bm,),
        in_specs=[pl.BlockSpec((bm, HALF_D), row),
                  pl.BlockSpec((1, 1, D_MODEL, 2 * D_EXPERT), exp4), pl.BlockSpec((1, 1, 2 * D_EXPERT), exp3),
                  pl.BlockSpec((1, 1, D_EXPERT, D_MODEL), exp4), pl.BlockSpec((1, 1, D_MODEL), exp3)],
        out_specs=pl.BlockSpec((bm, HALF_D), row),
        scratch_shapes=[pltpu.VMEM((D_MODEL, 2 * D_EXPERT), BF16), pltpu.VMEM((D_EXPERT, D_MODEL), BF16)],
    )
    return pl.pallas_call(
        _expert_kernel,
        grid_spec=grid_spec,
        out_shape=jax.ShapeDtypeStruct((n_rows, HALF_D), I32),
        compiler_params=_cparams("arbitrary"),
        name="moe_experts",
    )(block_exp, n_used, xs, w_gu, b_gu.reshape(N_EXPERTS, 1, 2 * D_EXPERT),
      w_down, b_down.reshape(N_EXPERTS, 1, D_MODEL))


def _combine_kernel(y0_ref, y1_ref, y2_ref, y3_ref, gate_ref, res_ref, g_ref, b_ref, o_ref):
    gates = gate_ref[...]
    acc_lo, acc_hi = None, None
    for k, y_ref in enumerate((y0_ref, y1_ref, y2_ref, y3_ref)):
        lo, hi = _unpack_bf16_pairs(y_ref[...])
        gk = gates[:, k:k + 1]
        acc_lo = lo * gk if acc_lo is None else acc_lo + lo * gk
        acc_hi = hi * gk if acc_hi is None else acc_hi + hi * gk
    acc = jnp.concatenate([acc_lo, acc_hi], axis=1)
    o_ref[...] = _layer_norm_rows(DN_ALPHA * res_ref[...] + acc, g_ref[...], b_ref[...])


def _moe_block(h2d, h_packed, layer, w_router, b_router, w_gu, b_gu, w_down, b_down, g, b):
    n = h2d.shape[0]
    n_slots = n * TOP_K
    bm = MOE_BLOCK_ROWS
    top_idx, gates, rank, totals = _router(h2d, w_router, b_router)

    e_flat = top_idx.reshape(-1)
    order = jnp.argsort(e_flat).astype(I32)
    counts = totals[0, :N_EXPERTS].astype(I32)
    padded = (counts + bm - 1) // bm * bm
    start = jnp.cumsum(counts) - counts
    ends_p = jnp.cumsum(padded)
    pstart = ends_p - padded
    n_rows = n_slots + N_EXPERTS * bm
    n_blocks = n_rows // bm
    r = jnp.arange(n_rows, dtype=I32)
    e_r = jnp.minimum(jnp.searchsorted(ends_p, r, side="right"), N_EXPERTS - 1).astype(I32)
    j = r - pstart[e_r]
    valid = j < counts[e_r]
    slot_of_row = order[jnp.where(valid, start[e_r] + j, 0)]
    rows_tok = jnp.where(valid, slot_of_row // TOP_K, r % n).astype(I32)
    slot_pos = pstart[e_flat] + rank.reshape(-1)
    block_exp = e_r[::bm]
    n_used = (ends_p[-1] // bm).astype(I32).reshape(1)

    xs = _gather_rows(h_packed, rows_tok)
    ys = _expert_mlp(xs, block_exp, n_used, layer, w_gu, b_gu, w_down, b_down)
    yk = _gather_rows(ys, slot_pos.reshape(n, TOP_K).T.reshape(-1))

    tm = ROW_TILE
    row = lambda i: (i, 0)
    const = lambda i: (0, 0)
    choice = lambda k: (lambda i: (k * (n // tm) + i, 0))
    return pl.pallas_call(
        _combine_kernel,
        grid=(n // tm,),
        in_specs=[pl.BlockSpec((tm, HALF_D), choice(k)) for k in range(TOP_K)] + [
                  pl.BlockSpec((tm, TOP_K), row),
                  pl.BlockSpec((tm, D_MODEL), row), pl.BlockSpec((1, D_MODEL), const),
                  pl.BlockSpec((1, D_MODEL), const)],
        out_specs=pl.BlockSpec((tm, D_MODEL), row),
        out_shape=jax.ShapeDtypeStruct((n, D_MODEL), F32),
        compiler_params=_cparams("parallel"),
        name="moe_combine",
    )(yk, yk, yk, yk, gates, h2d, g.reshape(1, D_MODEL), b.reshape(1, D_MODEL))


def _s5_kernel(u_ref, bre_ref, bim_ref, cre_ref, cim_ref, are_ref, aim_ref, d_ref, y_ref,
               bu_re, bu_im, st_re, st_im, h_re, h_im, *, bsz):
    @pl.when(pl.program_id(0) == 0)
    def _():
        h_re[...] = jnp.zeros_like(h_re)
        h_im[...] = jnp.zeros_like(h_im)

    rows = u_ref.shape[0]
    first = lax.broadcasted_iota(I32, (SUBLANES, S5_ST_BLK), 0) < bsz
    for j in range(S5_LANE_BLOCKS):
        cin = slice(j * S5_IN_BLK, (j + 1) * S5_IN_BLK)
        cst = slice(j * S5_ST_BLK, (j + 1) * S5_ST_BLK)
        uj = u_ref[:, cin]
        ujb = uj.astype(BF16)
        bu_re[...] = _dot(ujb, bre_ref[j])
        bu_im[...] = _dot(ujb, bim_ref[j])
        ar = jnp.broadcast_to(are_ref[:, cst], (SUBLANES, S5_ST_BLK))
        ai = jnp.broadcast_to(aim_ref[:, cst], (SUBLANES, S5_ST_BLK))

        def step(i, carry):
            hr, hi = carry
            r0 = pl.multiple_of(i * SUBLANES, SUBLANES)
            vr = bu_re[pl.ds(r0, SUBLANES), :]
            vi = bu_im[pl.ds(r0, SUBLANES), :]
            h1r = ar * hr - ai * hi + vr
            h1i = ar * hi + ai * hr + vi
            h1rs = pltpu.roll(h1r, bsz, 0)
            h1is = pltpu.roll(h1i, bsz, 0)
            h2r = ar * h1rs - ai * h1is + vr
            h2i = ar * h1is + ai * h1rs + vi
            st_re[pl.ds(r0, SUBLANES), :] = jnp.where(first, h1r, h2r)
            st_im[pl.ds(r0, SUBLANES), :] = jnp.where(first, h1i, h2i)
            return pltpu.roll(h2r, bsz, 0), pltpu.roll(h2i, bsz, 0)

        hr, hi = lax.fori_loop(0, rows // SUBLANES, step, (h_re[:, cst], h_im[:, cst]))
        h_re[:, cst] = hr
        h_im[:, cst] = hi
        yj = _dot(st_re[...].astype(BF16), cre_ref[j]) + _dot(st_im[...].astype(BF16), cim_ref[j])
        yj = yj + d_ref[:, cin] * uj
        y_ref[:, cin] = jax.nn.gelu(yj).astype(BF16)


def _s5_block_diag(w, n_in, n_out):
    gpb = SSM_GROUPS // S5_LANE_BLOCKS
    w4 = w.reshape(S5_LANE_BLOCKS, gpb, n_in, n_out)
    eye = jnp.eye(gpb, dtype=w.dtype)
    return jnp.einsum("jgio,gh->jgiho", w4, eye).reshape(S5_LANE_BLOCKS, gpb * n_in, gpb * n_out)


def _s5_mixer_block(h2d, bsz, seq, w_in, log_dt, lam_re, lam_im, b_re, b_im, c_re, c_im, d, w_out, g, b):
    assert 2 * bsz == SUBLANES, "the scan packs two time steps of bsz rows into one 8-row tile"
    tm = ROW_TILE
    per_seq = seq // tm
    u_t = _matmul(h2d, w_in.astype(BF16), tm=tm, out_dtype=F32, grid=(bsz, per_seq),
                  x_map=lambda bb, i: (bb * per_seq + i, 0), out_map=lambda bb, i: (i, bb),
                  out_shape=(seq, bsz * D_MODEL)).reshape(seq * bsz, D_MODEL)

    dt = jnp.exp(log_dt)[:, None]
    mag = jnp.exp(lam_re * dt)
    a_re, a_im = mag * jnp.cos(lam_im * dt), mag * jnp.sin(lam_im * dt)
    den = lam_re * lam_re + lam_im * lam_im
    coef_re = ((a_re - 1.0) * lam_re + a_im * lam_im) / den
    coef_im = (a_im * lam_re - (a_re - 1.0) * lam_im) / den
    bb_re = coef_re[..., None] * b_re - coef_im[..., None] * b_im
    bb_im = coef_re[..., None] * b_im + coef_im[..., None] * b_re
    bre = _s5_block_diag(jnp.swapaxes(bb_re, 1, 2), SSM_GROUP, SSM_STATE).astype(BF16)
    bim = _s5_block_diag(jnp.swapaxes(bb_im, 1, 2), SSM_GROUP, SSM_STATE).astype(BF16)
    cre = _s5_block_diag(jnp.swapaxes(c_re, 1, 2), SSM_STATE, SSM_GROUP).astype(BF16)
    cim = _s5_block_diag(jnp.swapaxes(-c_im, 1, 2), SSM_STATE, SSM_GROUP).astype(BF16)
    n_state = SSM_GROUPS * SSM_STATE

    rows = S5_CHUNK * bsz
    row = lambda c: (c, 0)
    c2 = lambda c: (0, 0)
    c3 = lambda c: (0, 0, 0)
    y_t = pl.pallas_call(
        functools.partial(_s5_kernel, bsz=bsz),
        grid=(seq // S5_CHUNK,),
        in_specs=[pl.BlockSpec((rows, D_MODEL), row),
                  pl.BlockSpec(bre.shape, c3), pl.BlockSpec(bim.shape, c3),
                  pl.BlockSpec(cre.shape, c3), pl.BlockSpec(cim.shape, c3),
                  pl.BlockSpec((1, n_state), c2), pl.BlockSpec((1, n_state), c2), pl.BlockSpec((1, D_MODEL), c2)],
        out_specs=pl.BlockSpec((rows, D_MODEL), row),
        out_shape=jax.ShapeDtypeStruct((seq * bsz, D_MODEL), BF16),
        scratch_shapes=[pltpu.VMEM((rows, S5_ST_BLK), F32)] * 4 + [pltpu.VMEM((SUBLANES, n_state), F32)] * 2,
        compiler_params=_cparams("arbitrary"),
        name="s5_scan",
    )(u_t, bre, bim, cre, cim, a_re.reshape(1, n_state), a_im.reshape(1, n_state), d.reshape(1, D_MODEL))

    y2 = y_t.reshape(seq, bsz * D_MODEL)
    return _linear_residual_ln(
        [y2], [w_out.astype(BF16)], h2d, g, b, tm=tm, glu=True, grid=(bsz, per_seq),
        x_maps=[lambda bb, i: (i, bb)], res_map=lambda bb, i: (bb * per_seq + i, 0))


def _even_mixer_block(h2d, bsz, seq, w_in, qnorm_g, w_uq, w_uq_idx, kidx_g, kidx_b, w_out, g, b):
    qa, qi, ka, va, ki, wi, qb, kb, vb = _even_proj(h2d, bsz, seq, w_in, qnorm_g, w_uq, w_uq_idx, kidx_g, kidx_b)
    o_a = _dsa_attention(qa, qi, wi, ki, ka, va, bsz, seq)
    o_b = _stick_breaking(qb, kb, vb, bsz, seq)
    w_out = w_out.astype(BF16)
    return _linear_residual_ln([o_a, o_b], [w_out[:A_WIDTH], w_out[A_WIDTH:]], h2d, g, b, tm=ROW_TILE)


def kernel(x, mem, ev_w_in, ev_qnorm_g, ev_w_uq, ev_w_uq_idx, ev_kidx_ln_g, ev_kidx_ln_b, ev_w_out, od_w_in, od_log_dt, od_lambda_re, od_lambda_im, od_b_re, od_b_im, od_c_re, od_c_im, od_d, od_w_out, mix_ln_g, mix_ln_b, xa_w_q, xa_w_kv, xa_w_o, xa_ln_g, xa_ln_b, moe_w_router, moe_b_router, moe_w_gu, moe_b_gu, moe_w_down, moe_b_down, ffn_ln_g, ffn_ln_b):
    bsz, seq, _ = x.shape
    h = x.reshape(bsz * seq, D_MODEL)
    for layer in range(DEPTH):
        j = layer // 2
        if layer % 2 == 0:
            h = _even_mixer_block(h, bsz, seq, ev_w_in[j], ev_qnorm_g[j], ev_w_uq[j], ev_w_uq_idx[j],
                                  ev_kidx_ln_g[j], ev_kidx_ln_b[j], ev_w_out[j], mix_ln_g[layer], mix_ln_b[layer])
        else:
            h = _s5_mixer_block(h, bsz, seq, od_w_in[j], od_log_dt[j], od_lambda_re[j], od_lambda_im[j],
                                od_b_re[j], od_b_im[j], od_c_re[j], od_c_im[j], od_d[j], od_w_out[j],
                                mix_ln_g[layer], mix_ln_b[layer])
        h, h_packed = _cross_attention_block(h, mem, bsz, seq, xa_w_q[layer], xa_w_kv[layer], xa_w_o[layer],
                                             xa_ln_g[layer], xa_ln_b[layer])
        h = _moe_block(h, h_packed, layer, moe_w_router[layer], moe_b_router[layer], moe_w_gu, moe_b_gu[layer],
                       moe_w_down, moe_b_down[layer], ffn_ln_g[layer], ffn_ln_b[layer])
    return h.reshape(bsz, seq, D_MODEL)
```

```python
import functools
import math

import jax
import jax.numpy as jnp
from jax import lax
from jax.experimental import pallas as pl
from jax.experimental.pallas import tpu as pltpu
from jax.experimental.pallas import tpu_sc as plsc

F32 = jnp.float32
BF16 = jnp.bfloat16
I32 = jnp.int32

D_MODEL = 1024
DEPTH = 2
HEAD_DIM = 64
A_HEADS = 8
A_KV_HEADS = 2
A_REP = A_HEADS // A_KV_HEADS
Q_RANK = 256
IDX_HEADS = 8
IDX_DIM = 64
IDX_TOPK = 256
B_HEADS = 8
A_WIDTH = A_HEADS * HEAD_DIM
B_WIDTH = B_HEADS * HEAD_DIM
SSM_GROUP = 16
SSM_GROUPS = D_MODEL // SSM_GROUP
SSM_STATE = 64
XA_HEADS = 4
XA_HEAD_DIM = D_MODEL // XA_HEADS
N_EXPERTS = 32
TOP_K = 4
D_EXPERT = D_MODEL
SWIGLU_LIMIT = 7.0
SWIGLU_ALPHA = 1.702
ROPE_THETA = 500000.0
ROPE_HALF = HEAD_DIM // 8
LN_EPS = 1e-5
DN_ALPHA = (2 * DEPTH) ** 0.25

LANES = 128
SUBLANES = 8
VMEM_LIMIT_BYTES = 56 * 1024 * 1024

Q_BLOCK = 256
DSA_KEY_TILE = 512
DSA_ATT_TILE = 512
DSA_COUNT_ROWS = 8 * SUBLANES
SB_KEY_TILE = 256
ROW_TILE = 256
MOE_BLOCK_ROWS = 512
SC_GATHER_SLOT_BYTES = 128 * 1024
S5_CHUNK = 128
S5_LANE_BLOCKS = 4
S5_IN_BLK = D_MODEL // S5_LANE_BLOCKS
S5_ST_BLK = SSM_GROUPS * SSM_STATE // S5_LANE_BLOCKS

SB_EXIT_LOG = -104.0
NEG_BIG = -1e30
INT_MIN = -(2 ** 31)


def _cparams(*sem):
    return pltpu.CompilerParams(dimension_semantics=sem, vmem_limit_bytes=VMEM_LIMIT_BYTES)


def _dot(a, b):
    return jnp.dot(a, b, preferred_element_type=F32)


def _dot_nt(a, b):
    return lax.dot_general(a, b, (((1,), (1,)), ((), ())), preferred_element_type=F32)


def _layer_norm_rows(y, g, b):
    mu = jnp.mean(y, axis=-1, keepdims=True)
    d = y - mu
    var = jnp.mean(d * d, axis=-1, keepdims=True)
    return d * lax.rsqrt(var + LN_EPS) * g + b


def _mm_kernel(x_ref, w_ref, o_ref):
    o_ref[...] = _dot(x_ref[...].astype(BF16), w_ref[...]).astype(o_ref.dtype)


def _matmul(x, w, *, tm, out_dtype, x_map=None, out_map=None, grid=None, out_shape=None):
    m, k = x.shape
    n = w.shape[1]
    grid = grid or (m // tm,)
    x_map = x_map or (lambda i: (i, 0))
    out_map = out_map or (lambda i: (i, 0))
    out_shape = out_shape or (m, n)
    return pl.pallas_call(
        _mm_kernel,
        grid=grid,
        in_specs=[pl.BlockSpec((tm, k), x_map), pl.BlockSpec((k, n), lambda *a: (0, 0))],
        out_specs=pl.BlockSpec((tm, n), out_map),
        out_shape=jax.ShapeDtypeStruct(out_shape, out_dtype),
        compiler_params=_cparams(*(("parallel",) * len(grid))),
        name="matmul",
    )(x, w)


def _lin_ln_kernel(*refs, n_in, glu):
    xs, ws = refs[:n_in], refs[n_in:2 * n_in]
    res_ref, g_ref, b_ref, o_ref = refs[2 * n_in:]
    acc = _dot(xs[0][...].astype(BF16), ws[0][...])
    for x_ref, w_ref in zip(xs[1:], ws[1:]):
        acc = acc + _dot(x_ref[...].astype(BF16), w_ref[...])
    if glu:
        acc = acc[:, :D_MODEL] * jax.nn.sigmoid(acc[:, D_MODEL:])
    y = DN_ALPHA * res_ref[...] + acc
    o_ref[...] = _layer_norm_rows(y, g_ref[...], b_ref[...])


def _linear_residual_ln(xs, ws, res, g, b, *, tm, glu=False, grid=None, x_maps=None, res_map=None):
    n_rows = res.shape[0]
    grid = grid or (n_rows // tm,)
    x_maps = x_maps or [lambda i: (i, 0)] * len(xs)
    res_map = res_map or (lambda i: (i, 0))
    const = lambda *a: (0, 0)
    in_specs = [pl.BlockSpec((tm, w.shape[0]), m) for w, m in zip(ws, x_maps)]
    in_specs += [pl.BlockSpec(w.shape, const) for w in ws]
    in_specs += [pl.BlockSpec((tm, D_MODEL), res_map), pl.BlockSpec((1, D_MODEL), const),
                 pl.BlockSpec((1, D_MODEL), const)]
    return pl.pallas_call(
        functools.partial(_lin_ln_kernel, n_in=len(xs), glu=glu),
        grid=grid,
        in_specs=in_specs,
        out_specs=pl.BlockSpec((tm, D_MODEL), res_map),
        out_shape=jax.ShapeDtypeStruct((n_rows, D_MODEL), F32),
        compiler_params=_cparams(*(("parallel",) * len(grid))),
        name="linear_residual_ln",
    )(*xs, *ws, res, g.reshape(1, D_MODEL), b.reshape(1, D_MODEL))


_EV_CQ, _EV_KA, _EV_VA, _EV_KI, _EV_QB = 0, 256, 384, 512, 640
_EV_KB = _EV_QB + B_WIDTH
_EV_VB = _EV_KB + B_WIDTH
_EV_COLS = _EV_VB + B_WIDTH


def _rope_tables(seq):
    inv = ROPE_THETA ** (-jnp.arange(ROPE_HALF, dtype=F32) / ROPE_HALF)
    ang = jnp.arange(seq, dtype=F32)[:, None] * inv[None, :]
    cos, sin = jnp.cos(ang), jnp.sin(ang)
    rest = HEAD_DIM - 2 * ROPE_HALF
    zh = jnp.zeros((seq, ROPE_HALF), F32)
    c = jnp.concatenate([cos, cos, jnp.ones((seq, rest), F32)], axis=1)
    s1 = jnp.concatenate([-sin, zh, jnp.zeros((seq, rest), F32)], axis=1)
    s2 = jnp.concatenate([zh, sin, jnp.zeros((seq, rest), F32)], axis=1)
    rep = LANES // HEAD_DIM
    return jnp.tile(c, (1, rep)), jnp.tile(s1, (1, rep)), jnp.tile(s2, (1, rep))


def _even_proj_kernel(x_ref, w_ref, qg_ref, wuq_ref, wuqi_ref, lg_ref, lb_ref, c_ref, s1_ref, s2_ref,
                      qa_ref, qi_ref, ka_ref, va_ref, ki_ref, wi_ref, qb_ref, kb_ref, vb_ref):
    p = _dot(x_ref[...].astype(BF16), w_ref[...])
    c, s1, s2 = c_ref[...], s1_ref[...], s2_ref[...]

    def rope(t):
        return (t * c + pltpu.roll(t, LANES - ROPE_HALF, 1) * s1 + pltpu.roll(t, ROPE_HALF, 1) * s2)

    cq = p[:, _EV_CQ:_EV_CQ + Q_RANK]
    cn = cq * lax.rsqrt(jnp.mean(cq * cq, axis=-1, keepdims=True) + LN_EPS) * qg_ref[...]
    cnb = cn.astype(BF16)
    qa = _dot(cnb, wuq_ref[...])
    qi = _dot(cnb, wuqi_ref[...])
    low = lax.broadcasted_iota(I32, c.shape, 1) < HEAD_DIM
    for j in range(A_WIDTH // LANES):
        sl = slice(j * LANES, (j + 1) * LANES)
        pair = rope(qa[:, sl]) * (HEAD_DIM ** -0.5)
        for e, src in enumerate((pair, pltpu.roll(pair, HEAD_DIM, 1))):
            h = 2 * j + e
            qa_ref[:, h * LANES:(h + 1) * LANES] = jnp.where(low, src, 0.0).astype(BF16)
        qi_ref[:, sl] = (rope(qi[:, sl]) * (IDX_DIM ** -0.5)).astype(BF16)
    kpair = rope(p[:, _EV_KA:_EV_KA + LANES])
    vpair = p[:, _EV_VA:_EV_VA + LANES]
    v_pad = jnp.where(lax.broadcasted_iota(I32, c.shape, 1) == HEAD_DIM, 1.0, 0.0)
    for g, (ks, vs) in enumerate(((kpair, vpair), (pltpu.roll(kpair, HEAD_DIM, 1), pltpu.roll(vpair, HEAD_DIM, 1)))):
        ka_ref[:, g * LANES:(g + 1) * LANES] = jnp.where(low, ks, 0.0).astype(BF16)
        va_ref[:, g * LANES:(g + 1) * LANES] = jnp.where(low, vs, v_pad).astype(BF16)

    t = p[:, _EV_KI:_EV_KI + LANES]
    lane = lax.broadcasted_iota(I32, t.shape, 1)
    is_k = lane < IDX_DIM
    mu = jnp.sum(jnp.where(is_k, t, 0.0), axis=-1, keepdims=True) * (1.0 / IDX_DIM)
    d = jnp.where(is_k, t - mu, 0.0)
    var = jnp.sum(d * d, axis=-1, keepdims=True) * (1.0 / IDX_DIM)
    kin = d * lax.rsqrt(var + LN_EPS) * lg_ref[...] + lb_ref[...]
    ki_ref[...] = rope(kin)[:, :IDX_DIM].astype(BF16)
    wi_ref[...] = t[:, IDX_DIM:IDX_DIM + IDX_HEADS] * (IDX_HEADS ** -0.5)

    qb_ref[...] = (p[:, _EV_QB:_EV_KB] * (HEAD_DIM ** -0.5)).astype(BF16)
    kb_ref[...] = p[:, _EV_KB:_EV_VB].astype(BF16)
    vb_ref[...] = p[:, _EV_VB:_EV_COLS].astype(BF16)


def _even_proj(x2d, bsz, seq, w_in, qnorm_g, w_uq, w_uq_idx, kidx_g, kidx_b):
    n = x2d.shape[0]
    tm = ROW_TILE
    per_seq = seq // tm
    c0 = Q_RANK + 2 * A_KV_HEADS * HEAD_DIM + IDX_DIM + IDX_HEADS
    w_pack = jnp.concatenate(
        [w_in[:, :c0], jnp.zeros((D_MODEL, _EV_QB - c0), w_in.dtype), w_in[:, c0:]], axis=1).astype(BF16)
    pad = LANES - IDX_DIM
    lg = jnp.concatenate([kidx_g, jnp.zeros((pad,), F32)]).reshape(1, LANES)
    lb = jnp.concatenate([kidx_b, jnp.zeros((pad,), F32)]).reshape(1, LANES)
    c, s1, s2 = _rope_tables(seq)
    row = lambda i: (i, 0)
    const = lambda i: (0, 0)
    pos = lambda i: (i % per_seq, 0)
    head_shape = jax.ShapeDtypeStruct((n, B_WIDTH), BF16)
    head_spec = pl.BlockSpec((tm, B_WIDTH), row)
    return pl.pallas_call(
        _even_proj_kernel,
        grid=(n // tm,),
        in_specs=[pl.BlockSpec((tm, D_MODEL), row), pl.BlockSpec((D_MODEL, _EV_COLS), const),
                  pl.BlockSpec((1, Q_RANK), const), pl.BlockSpec((Q_RANK, A_WIDTH), const),
                  pl.BlockSpec((Q_RANK, IDX_HEADS * IDX_DIM), const),
                  pl.BlockSpec((1, LANES), const), pl.BlockSpec((1, LANES), const),
                  pl.BlockSpec((tm, LANES), pos), pl.BlockSpec((tm, LANES), pos), pl.BlockSpec((tm, LANES), pos)],
        out_specs=[pl.BlockSpec((tm, A_HEADS * LANES), row), pl.BlockSpec((tm, IDX_HEADS * IDX_DIM), row),
                   pl.BlockSpec((tm, A_KV_HEADS * LANES), row), pl.BlockSpec((tm, A_KV_HEADS * LANES), row),
                   pl.BlockSpec((tm, IDX_DIM), row), pl.BlockSpec((tm, IDX_HEADS), row),
                   head_spec, head_spec, head_spec],
        out_shape=[jax.ShapeDtypeStruct((n, A_HEADS * LANES), BF16), jax.ShapeDtypeStruct((n, IDX_HEADS * IDX_DIM), BF16),
                   jax.ShapeDtypeStruct((n, A_KV_HEADS * LANES), BF16),
                   jax.ShapeDtypeStruct((n, A_KV_HEADS * LANES), BF16),
                   jax.ShapeDtypeStruct((n, IDX_DIM), BF16), jax.ShapeDtypeStruct((n, IDX_HEADS), F32),
                   head_shape, head_shape, head_shape],
        compiler_params=_cparams("parallel"),
        name="even_proj",
    )(x2d, w_pack, qnorm_g.reshape(1, Q_RANK), w_uq.astype(BF16), w_uq_idx.astype(BF16), lg, lb, c, s1, s2)


def _key_to_float(key):
    bits = key ^ ((key >> 31) & jnp.int32(0x7FFFFFFF))
    return lax.bitcast_convert_type(bits, F32)


def _high_half(x):
    bits = lax.bitcast_convert_type(x, jnp.uint32) & jnp.uint32(0xFFFF0000)
    return lax.bitcast_convert_type(bits, F32).astype(BF16)


def _dsa_kernel(qa_ref, qi_ref, wit_ref, ki_ref, ka_ref, vat_ref, o_ref, sc_scr, hi_scr, *, topk, ts, ta):
    seq = sc_scr.shape[0]
    qb = pl.program_id(1)
    q0 = qb * Q_BLOCK
    nkt = (q0 + Q_BLOCK - 1) // ts + 1
    t_row = q0 + lax.broadcasted_iota(I32, (1, Q_BLOCK), 1)
    key = lax.broadcasted_iota(I32, (ts, Q_BLOCK), 0)
    kf = jnp.float32(topk)

    qi = qi_ref[0]
    qs = jnp.concatenate([qi[:, h * IDX_DIM:(h + 1) * IDX_DIM] for h in range(IDX_HEADS)], axis=0)
    wit = wit_ref[0]

    def score_tile(kt, carry):
        off = pl.multiple_of(kt * ts, ts)
        s_all = _dot_nt(ki_ref[0, pl.ds(off, ts), :], qs)
        acc = jnp.zeros((ts, Q_BLOCK), F32)
        for h in range(IDX_HEADS):
            acc = acc + jnp.maximum(s_all[:, h * Q_BLOCK:(h + 1) * Q_BLOCK], 0.0) * wit[h:h + 1, :]
        val = jnp.where(off + key <= t_row, acc, -jnp.inf)
        sc_scr[pl.ds(off, ts), :] = val
        hi_scr[pl.ds(off, ts), :] = _high_half(val)
        return carry

    lax.fori_loop(0, nkt, score_tile, 0)

    def count(pred):
        def body(kt, acc):
            off = pl.multiple_of(kt * ts, ts)
            ind = pred(sc_scr[pl.ds(off, ts), :], off + key)
            return acc + jnp.sum(ind.reshape(ts // DSA_COUNT_ROWS, DSA_COUNT_ROWS, Q_BLOCK), axis=0)
        acc = lax.fori_loop(0, nkt, body, jnp.zeros((DSA_COUNT_ROWS, Q_BLOCK), F32))
        return jnp.sum(acc, axis=0, keepdims=True)

    one_h, zero_h = jnp.ones((), BF16), jnp.zeros((), BF16)

    def count_high(c_hi):
        def body(kt, acc):
            off = pl.multiple_of(kt * ts, ts)
            ind = jnp.where(hi_scr[pl.ds(off, ts), :] >= c_hi, one_h, zero_h)
            part = ind[:DSA_COUNT_ROWS]
            for j in range(1, ts // DSA_COUNT_ROWS):
                part = part + ind[j * DSA_COUNT_ROWS:(j + 1) * DSA_COUNT_ROWS]
            return acc + part.astype(F32)
        acc = lax.fori_loop(0, nkt, body, jnp.zeros((DSA_COUNT_ROWS, Q_BLOCK), F32))
        return jnp.sum(acc, axis=0, keepdims=True)

    def high_step(i, base):
        cand = base + jnp.left_shift(jnp.int32(1), 31 - i)
        cnt = count_high(_high_half(_key_to_float(cand)))
        return jnp.where(cnt >= kf, cand, base)

    def bit_step(i, base):
        cand = base + jnp.left_shift(jnp.int32(1), 31 - i)
        cf = _key_to_float(cand)
        cnt = count(lambda sc, idx: jnp.where(sc >= cf, 1.0, 0.0))
        return jnp.where(cnt >= kf, cand, base)

    base = lax.fori_loop(0, 16, high_step, jnp.full((1, Q_BLOCK), INT_MIN, I32))
    base = lax.fori_loop(16, 32, bit_step, base)
    thr = jnp.where(base == INT_MIN, -jnp.inf, _key_to_float(base))

    cnt_ge = count(lambda sc, idx: jnp.where(sc >= thr, 1.0, 0.0))
    tied = jnp.logical_and(cnt_ge > kf, thr > -jnp.inf)
    any_tied = jnp.max(jnp.where(tied, 1.0, 0.0)) > 0.0
    seq_bits = max(1, int(math.ceil(math.log2(seq))))

    def tie_cut():
        cnt_gt = count(lambda sc, idx: jnp.where(sc > thr, 1.0, 0.0))
        need = kf - cnt_gt

        def idx_step(i, pos):
            cand = pos + jnp.left_shift(jnp.int32(1), seq_bits - 1 - i)
            cnt = count(lambda sc, idx: jnp.where(sc == thr, jnp.where(idx < cand, 1.0, 0.0), 0.0))
            return jnp.where(cnt < need, cand, pos)

        return lax.fori_loop(0, seq_bits, idx_step, jnp.zeros((1, Q_BLOCK), I32))

    cut = lax.cond(any_tied, tie_cut, lambda: jnp.full((1, Q_BLOCK), seq, I32))
    cut = jnp.where(tied, cut, seq)

    nkt_a = (q0 + Q_BLOCK - 1) // ta + 1
    key_a = lax.broadcasted_iota(I32, (ta, Q_BLOCK), 0)
    cols = A_REP * Q_BLOCK
    qg = [jnp.concatenate([qa_ref[0, :, (g * A_REP + r) * LANES:(g * A_REP + r + 1) * LANES]
                           for r in range(A_REP)], axis=0) for g in range(A_KV_HEADS)]

    def att_pair(i, carry):
        offs = [pl.multiple_of((2 * i + e) * ta, ta) for e in range(2)]
        logits = [_dot_nt(ka_ref[0, pl.ds(offs[e], ta), g * LANES:(g + 1) * LANES], qg[g])
                  for e in range(2) for g in range(A_KV_HEADS)]
        out = []
        for e in range(2):
            sc = sc_scr[pl.ds(offs[e], ta), :]
            idx = offs[e] + key_a
            keep = jnp.where(sc > thr, 0.0, jnp.where(sc == thr, jnp.where(idx <= cut, 0.0, NEG_BIG), NEG_BIG))
            bias = jnp.where(idx <= t_row, keep, NEG_BIG)
            bias = jnp.concatenate([bias] * A_REP, axis=1)
            for g in range(A_KV_HEADS):
                m, acc = carry[e * A_KV_HEADS + g]
                s = logits[e * A_KV_HEADS + g] + bias
                m_new = jnp.maximum(m, jnp.max(s, axis=0, keepdims=True))
                p = jnp.exp(s - m_new)
                vt = vat_ref[0, g * LANES:(g + 1) * LANES, pl.ds(offs[e], ta)]
                out.append((m_new, jnp.exp(m - m_new) * acc + _dot(vt, p.astype(BF16))))
        return tuple(out)

    init = tuple((jnp.full((1, cols), NEG_BIG, F32), jnp.zeros((LANES, cols), F32))
                 for _ in range(2 * A_KV_HEADS))
    final = lax.fori_loop(0, (nkt_a + 1) // 2, att_pair, init)
    low = lax.broadcasted_iota(I32, (Q_BLOCK, LANES), 1) < HEAD_DIM
    outs = []
    for g in range(A_KV_HEADS):
        (m0, acc0), (m1, acc1) = final[g], final[A_KV_HEADS + g]
        m = jnp.maximum(m0, m1)
        acc = jnp.exp(m0 - m) * acc0 + jnp.exp(m1 - m) * acc1
        og = acc / acc[HEAD_DIM:HEAD_DIM + 1, :]
        outs += [og[:, r * Q_BLOCK:(r + 1) * Q_BLOCK].T for r in range(A_REP)]
    for j in range(A_HEADS // 2):
        pair = jnp.where(low, outs[2 * j], pltpu.roll(outs[2 * j + 1], HEAD_DIM, 1))
        o_ref[0, :, j * LANES:(j + 1) * LANES] = pair.astype(BF16)


def _dsa_attention(qa, qi, wi, ki, ka, va, bsz, seq):
    topk = min(IDX_TOPK, seq // 4)
    ts = min(DSA_KEY_TILE, seq)
    ta = min(DSA_ATT_TILE, seq // 2)
    assert seq % (2 * ta) == 0 and seq % ts == 0, "the attention loop walks the key tiles in pairs"
    blk = lambda b, i: (b, i, 0)
    full = lambda b, i: (b, 0, 0)
    r3 = lambda a: a.reshape(bsz, seq, a.shape[-1])
    wit = jnp.swapaxes(r3(wi), 1, 2)
    vat = jnp.swapaxes(r3(va), 1, 2)
    return pl.pallas_call(
        functools.partial(_dsa_kernel, topk=topk, ts=ts, ta=ta),
        grid=(bsz, seq // Q_BLOCK),
        in_specs=[pl.BlockSpec((1, Q_BLOCK, A_HEADS * LANES), blk),
                  pl.BlockSpec((1, Q_BLOCK, IDX_HEADS * IDX_DIM), blk),
                  pl.BlockSpec((1, IDX_HEADS, Q_BLOCK), lambda b, i: (b, 0, i)),
                  pl.BlockSpec((1, seq, IDX_DIM), full),
                  pl.BlockSpec((1, seq, A_KV_HEADS * LANES), full),
                  pl.BlockSpec((1, A_KV_HEADS * LANES, seq), full)],
        out_specs=pl.BlockSpec((1, Q_BLOCK, A_WIDTH), blk),
        out_shape=jax.ShapeDtypeStruct((bsz, seq, A_WIDTH), BF16),
        scratch_shapes=[pltpu.VMEM((seq, Q_BLOCK), F32), pltpu.VMEM((seq, Q_BLOCK), BF16)],
        compiler_params=_cparams("parallel", "parallel"),
        name="dsa_attention",
    )(r3(qa), r3(qi), wit, r3(ki), r3(ka), vat).reshape(bsz * seq, A_WIDTH)


def _sb_kernel(q_ref, k_ref, v_ref, u_ref, o_ref, acc_scr, run_scr, *, tk):
    q0 = pl.program_id(1) * Q_BLOCK
    t_col = q0 + lax.broadcasted_iota(I32, (Q_BLOCK, 1), 0)
    lane = lax.broadcasted_iota(I32, (Q_BLOCK, tk), 1)
    low = lax.broadcasted_iota(I32, (Q_BLOCK, LANES), 1) < HEAD_DIM
    upper = u_ref[...]
    nkt = (q0 + Q_BLOCK - 1) // tk + 1
    q = q_ref[0]
    zero = jnp.zeros((Q_BLOCK, LANES), BF16)
    qm = []
    for h in range(B_HEADS):
        pair = q[:, (h // 2) * LANES:(h // 2 + 1) * LANES]
        qm.append(jnp.where(low, pair, zero) if h % 2 == 0 else jnp.where(low, zero, pair))
    acc_scr[...] = jnp.zeros_like(acc_scr)
    run_scr[...] = jnp.zeros_like(run_scr)

    def cond(carry):
        i, worst = carry
        return jnp.logical_and(i < nkt, worst >= SB_EXIT_LOG)

    def body(carry):
        i, _ = carry
        off = pl.multiple_of((nkt - 1 - i) * tk, tk)
        strict = off + lane < t_col
        worst = None
        for p in range(B_HEADS // 2):
            cols = slice(p * LANES, (p + 1) * LANES)
            kp = k_ref[0, pl.ds(off, tk), cols]
            vp = v_ref[0, pl.ds(off, tk), cols]
            outs = []
            for e in range(2):
                h = 2 * p + e
                run = run_scr[h]
                z = _dot_nt(qm[h], kp)
                softplus = jnp.maximum(z, 0.0) + jnp.log(1.0 + jnp.exp(-jnp.abs(z)))
                log_1mb = jnp.where(strict, -softplus, 0.0)
                hi = log_1mb.astype(BF16)
                lo = (log_1mb - hi.astype(F32)).astype(BF16)
                after = _dot(hi, upper) + _dot(lo, upper) + run
                a = jnp.where(strict, jnp.exp(z - softplus + after), 0.0)
                outs.append(_dot(a.astype(BF16), vp))
                run = run + jnp.sum(log_1mb, axis=1, keepdims=True)
                run_scr[h] = run
                worst = run if worst is None else jnp.maximum(worst, run)
            acc_scr[:, cols] += jnp.where(low, outs[0], outs[1])
        return i + 1, jnp.max(worst)

    lax.while_loop(cond, body, (jnp.int32(0), jnp.float32(0.0)))
    o_ref[0] = acc_scr[...].astype(BF16)


def _stick_breaking(qb, kb, vb, bsz, seq):
    tk = min(SB_KEY_TILE, seq)
    r = lax.broadcasted_iota(I32, (tk, tk), 0)
    c = lax.broadcasted_iota(I32, (tk, tk), 1)
    upper = jnp.where(r > c, 1.0, 0.0).astype(BF16)
    blk = lambda b, i: (b, i, 0)
    full = lambda b, i: (b, 0, 0)
    r3 = lambda a: a.reshape(bsz, seq, B_WIDTH)
    return pl.pallas_call(
        functools.partial(_sb_kernel, tk=tk),
        grid=(bsz, seq // Q_BLOCK),
        in_specs=[pl.BlockSpec((1, Q_BLOCK, B_WIDTH), blk), pl.BlockSpec((1, seq, B_WIDTH), full),
                  pl.BlockSpec((1, seq, B_WIDTH), full), pl.BlockSpec((tk, tk), lambda b, i: (0, 0))],
        out_specs=pl.BlockSpec((1, Q_BLOCK, B_WIDTH), blk),
        out_shape=jax.ShapeDtypeStruct((bsz, seq, B_WIDTH), BF16),
        scratch_shapes=[pltpu.VMEM((Q_BLOCK, B_WIDTH), F32), pltpu.VMEM((B_HEADS, Q_BLOCK, 1), F32)],
        compiler_params=_cparams("parallel", "arbitrary"),
        name="stick_breaking",
    )(r3(qb), r3(kb), r3(vb), upper).reshape(bsz * seq, B_WIDTH)


HALF_D = D_MODEL // 2
U32 = jnp.uint32
HIGH16 = 0xFFFF0000


def _pack_bf16_pairs(x):
    def bits(v):
        return lax.bitcast_convert_type(v.astype(BF16).astype(F32), U32)
    word = (bits(x[:, HALF_D:]) & U32(HIGH16)) | (bits(x[:, :HALF_D]) >> 16)
    return lax.bitcast_convert_type(word, I32)


def _unpack_bf16_pairs(word):
    u = lax.bitcast_convert_type(word, U32)
    return lax.bitcast_convert_type(u << 16, F32), lax.bitcast_convert_type(u & U32(HIGH16), F32)


def _xattn_kernel(h_ref, wq_ref, kv_ref, wo_ref, g_ref, b_ref, o_ref, packed_ref):
    h = h_ref[...]
    q = (_dot(h.astype(BF16), wq_ref[...]) * (XA_HEAD_DIM ** -0.5)).astype(BF16)
    kv = kv_ref[0]
    outs = []
    for hd in range(XA_HEADS):
        sl = slice(hd * XA_HEAD_DIM, (hd + 1) * XA_HEAD_DIM)
        s = _dot_nt(q[:, sl], kv[:, sl])
        p = jnp.exp(s - jnp.max(s, axis=1, keepdims=True))
        vh = kv[:, D_MODEL + hd * XA_HEAD_DIM:D_MODEL + (hd + 1) * XA_HEAD_DIM]
        outs.append((_dot(p.astype(BF16), vh) / jnp.sum(p, axis=1, keepdims=True)).astype(BF16))
    y = _dot(jnp.concatenate(outs, axis=1), wo_ref[...])
    out = _layer_norm_rows(DN_ALPHA * h + y, g_ref[...], b_ref[...])
    o_ref[...] = out
    packed_ref[...] = _pack_bf16_pairs(out)


def _cross_attention_block(h2d, mem, bsz, seq, w_q, w_kv, w_o, g, b):
    tm = ROW_TILE
    per_seq = seq // tm
    mem_len = mem.shape[1]
    kv = _matmul(mem.reshape(bsz * mem_len, D_MODEL), w_kv.astype(BF16), tm=mem_len, out_dtype=BF16)
    kv = kv.reshape(bsz, mem_len, 2 * D_MODEL)
    row = lambda i: (i, 0)
    const = lambda i: (0, 0)
    return pl.pallas_call(
        _xattn_kernel,
        grid=(bsz * per_seq,),
        in_specs=[pl.BlockSpec((tm, D_MODEL), row), pl.BlockSpec((D_MODEL, D_MODEL), const),
                  pl.BlockSpec((1, mem_len, 2 * D_MODEL), lambda i: (i // per_seq, 0, 0)),
                  pl.BlockSpec((D_MODEL, D_MODEL), const),
                  pl.BlockSpec((1, D_MODEL), const), pl.BlockSpec((1, D_MODEL), const)],
        out_specs=[pl.BlockSpec((tm, D_MODEL), row), pl.BlockSpec((tm, HALF_D), row)],
        out_shape=[jax.ShapeDtypeStruct(h2d.shape, F32), jax.ShapeDtypeStruct((h2d.shape[0], HALF_D), I32)],
        compiler_params=_cparams("parallel"),
        name="cross_attention",
    )(h2d, w_q.astype(BF16), kv, w_o.astype(BF16), g.reshape(1, D_MODEL), b.reshape(1, D_MODEL))


def _router_kernel(h_ref, w_ref, b_ref, tri_ref, idx_ref, gate_ref, rank_ref, cnt_ref, run_scr):
    @pl.when(pl.program_id(0) == 0)
    def _():
        run_scr[...] = jnp.zeros_like(run_scr)

    h = h_ref[...]
    hh = h.astype(BF16)
    hl = (h - hh.astype(F32)).astype(BF16)
    w = w_ref[...]
    wh = w.astype(BF16)
    wl = (w - wh.astype(F32)).astype(BF16)
    logits = _dot(hh, wh) + _dot(hl, wh) + _dot(hh, wl) + b_ref[...]
    lane = lax.broadcasted_iota(I32, logits.shape, 1).astype(F32)
    vals, sels = [], []
    onehot = jnp.zeros(logits.shape, F32)
    for k in range(TOP_K):
        m = jnp.max(logits, axis=1, keepdims=True)
        sel = jnp.min(jnp.where(logits == m, lane, float(LANES)), axis=1, keepdims=True)
        idx_ref[:, k:k + 1] = sel.astype(I32)
        vals.append(m)
        sels.append(sel)
        onehot = onehot + jnp.where(lane == sel, 1.0, 0.0)
        logits = jnp.where(lane == sel, -jnp.inf, logits)
    es = [jnp.exp(v - vals[0]) for v in vals]
    tot = es[0] + es[1] + es[2] + es[3]
    for k in range(TOP_K):
        gate_ref[:, k:k + 1] = es[k] / tot

    earlier = _dot(tri_ref[...], onehot.astype(BF16)) + run_scr[...]
    for k in range(TOP_K):
        rank = jnp.sum(jnp.where(lane == sels[k], earlier, 0.0), axis=1, keepdims=True)
        rank_ref[:, k:k + 1] = rank.astype(I32)
    run = run_scr[...] + jnp.sum(onehot, axis=0, keepdims=True)
    run_scr[...] = run
    cnt_ref[...] = run


def _router(h2d, w_router, b_router):
    n = h2d.shape[0]
    tm = 2 * ROW_TILE
    pad = LANES - N_EXPERTS
    w = jnp.concatenate([w_router, jnp.zeros((D_MODEL, pad), F32)], axis=1)
    b = jnp.concatenate([b_router, jnp.full((pad,), NEG_BIG, F32)]).reshape(1, LANES)
    r = lax.broadcasted_iota(I32, (tm, tm), 0)
    c = lax.broadcasted_iota(I32, (tm, tm), 1)
    tri = jnp.where(c < r, 1.0, 0.0).astype(BF16)
    row = lambda i: (i, 0)
    const = lambda i: (0, 0)
    return pl.pallas_call(
        _router_kernel,
        grid=(n // tm,),
        in_specs=[pl.BlockSpec((tm, D_MODEL), row), pl.BlockSpec((D_MODEL, LANES), const),
                  pl.BlockSpec((1, LANES), const), pl.BlockSpec((tm, tm), const)],
        out_specs=[pl.BlockSpec((tm, TOP_K), row), pl.BlockSpec((tm, TOP_K), row),
                   pl.BlockSpec((tm, TOP_K), row), pl.BlockSpec((1, LANES), const)],
        out_shape=[jax.ShapeDtypeStruct((n, TOP_K), I32), jax.ShapeDtypeStruct((n, TOP_K), F32),
                   jax.ShapeDtypeStruct((n, TOP_K), I32), jax.ShapeDtypeStruct((1, LANES), F32)],
        scratch_shapes=[pltpu.VMEM((1, LANES), F32)],
        compiler_params=_cparams("arbitrary"),
        name="moe_router",
    )(h2d, w, b, tri)


def _gather_rows(src, idx):
    n_out = idx.shape[0]
    width = src.shape[1]
    win = SC_GATHER_SLOT_BYTES // (width * src.dtype.itemsize)
    mesh = plsc.VectorSubcoreMesh(core_axis_name="core", subcore_axis_name="subcore")
    n_workers = mesh.num_cores * mesh.num_subcores
    per_worker = n_out // n_workers
    steps = per_worker // win
    assert per_worker * n_workers == n_out and steps * win == per_worker and steps % 2 == 0

    @functools.partial(
        pl.kernel, out_type=jax.ShapeDtypeStruct((n_out, width), src.dtype), mesh=mesh,
        scratch_types=[pltpu.VMEM((per_worker,), I32), pltpu.VMEM((2, win, width), src.dtype),
                       pltpu.SemaphoreType.DMA, pltpu.SemaphoreType.DMA])
    def gather_kernel(src_hbm, idx_hbm, dst_hbm, idx_v, rows_v, sem0, sem1):
        worker = lax.axis_index("subcore") * mesh.num_cores + lax.axis_index("core")
        base = worker * per_worker
        sems = (sem0, sem1)
        pltpu.sync_copy(idx_hbm.at[pl.ds(base, per_worker)], idx_v)

        def gather(step, slot):
            return pltpu.make_async_copy(src_hbm.at[idx_v.at[pl.ds(step * win, win)]], rows_v.at[slot], sems[slot])

        gather(0, 0).start()

        @pl.loop(0, steps, step=2)
        def _(s):
            for slot in range(2):
                step = s + slot
                gather(step, slot).wait()

                @pl.when(step + 1 < steps)
                def _():
                    gather(step + 1, 1 - slot).start()

                pltpu.sync_copy(rows_v.at[slot], dst_hbm.at[pl.ds(base + step * win, win)])

    return gather_kernel(src, idx)


def _expert_kernel(blk_exp_ref, n_used_ref, x_ref, wgu_ref, bgu_ref, wd_ref, bd_ref, o_ref, wgu_bf, wd_bf):
    i = pl.program_id(0)

    @pl.when(jnp.logical_or(i == 0, blk_exp_ref[i] != blk_exp_ref[jnp.maximum(i - 1, 0)]))
    def _():
        wgu_bf[...] = wgu_ref[0, 0].astype(BF16)
        wd_bf[...] = wd_ref[0, 0].astype(BF16)

    @pl.when(i < n_used_ref[0])
    def _():
        x_lo, x_hi = _unpack_bf16_pairs(x_ref[...])
        hgu = (_dot(x_lo.astype(BF16), wgu_bf[:HALF_D, :]) + _dot(x_hi.astype(BF16), wgu_bf[HALF_D:, :])
               + bgu_ref[0])
        gate = jnp.minimum(hgu[:, :D_EXPERT], SWIGLU_LIMIT)
        up = jnp.clip(hgu[:, D_EXPERT:], -SWIGLU_LIMIT, SWIGLU_LIMIT)
        act = gate * jax.nn.sigmoid(gate * SWIGLU_ALPHA) * (up + 1.0)
        o_ref[...] = _pack_bf16_pairs(_dot(act.astype(BF16), wd_bf[...]) + bd_ref[0])

    @pl.when(i >= n_used_ref[0])
    def _():
        o_ref[...] = jnp.zeros_like(o_ref)


def _expert_mlp(xs, block_exp, n_used, layer, w_gu, b_gu, w_down, b_down):
    n_rows = xs.shape[0]
    bm = MOE_BLOCK_ROWS
    row = lambda i, be, nu: (i, 0)
    exp3 = lambda i, be, nu: (be[i], 0, 0)
    exp4 = lambda i, be, nu: (layer, be[i], 0, 0)
    grid_spec = pltpu.PrefetchScalarGridSpec(
        num_scalar_prefetch=2,
        grid=(n_rows // bm,),
        in_specs=[pl.BlockSpec((bm, HALF_D), row),
                  pl.BlockSpec((1, 1, D_MODEL, 2 * D_EXPERT), exp4), pl.BlockSpec((1, 1, 2 * D_EXPERT), exp3),
                  pl.BlockSpec((1, 1, D_EXPERT, D_MODEL), exp4), pl.BlockSpec((1, 1, D_MODEL), exp3)],
        out_specs=pl.BlockSpec((bm, HALF_D), row),
        scratch_shapes=[pltpu.VMEM((D_MODEL, 2 * D_EXPERT), BF16), pltpu.VMEM((D_EXPERT, D_MODEL), BF16)],
    )
    return pl.pallas_call(
        _expert_kernel,
        grid_spec=grid_spec,
        out_shape=jax.ShapeDtypeStruct((n_rows, HALF_D), I32),
        compiler_params=_cparams("arbitrary"),
        name="moe_experts",
    )(block_exp, n_used, xs, w_gu, b_gu.reshape(N_EXPERTS, 1, 2 * D_EXPERT),
      w_down, b_down.reshape(N_EXPERTS, 1, D_MODEL))


def _combine_kernel(y0_ref, y1_ref, y2_ref, y3_ref, gate_ref, res_ref, g_ref, b_ref, o_ref):
    gates = gate_ref[...]
    acc_lo, acc_hi = None, None
    for k, y_ref in enumerate((y0_ref, y1_ref, y2_ref, y3_ref)):
        lo, hi = _unpack_bf16_pairs(y_ref[...])
        gk = gates[:, k:k + 1]
        acc_lo = lo * gk if acc_lo is None else acc_lo + lo * gk
        acc_hi = hi * gk if acc_hi is None else acc_hi + hi * gk
    acc = jnp.concatenate([acc_lo, acc_hi], axis=1)
    o_ref[...] = _layer_norm_rows(DN_ALPHA * res_ref[...] + acc, g_ref[...], b_ref[...])


def _moe_block(h2d, h_packed, layer, w_router, b_router, w_gu, b_gu, w_down, b_down, g, b):
    n = h2d.shape[0]
    n_slots = n * TOP_K
    bm = MOE_BLOCK_ROWS
    top_idx, gates, rank, totals = _router(h2d, w_router, b_router)

    e_flat = top_idx.reshape(-1)
    order = jnp.argsort(e_flat).astype(I32)
    counts = totals[0, :N_EXPERTS].astype(I32)
    padded = (counts + bm - 1) // bm * bm
    start = jnp.cumsum(counts) - counts
    ends_p = jnp.cumsum(padded)
    pstart = ends_p - padded
    n_rows = n_slots + N_EXPERTS * bm
    n_blocks = n_rows // bm
    r = jnp.arange(n_rows, dtype=I32)
    e_r = jnp.minimum(jnp.searchsorted(ends_p, r, side="right"), N_EXPERTS - 1).astype(I32)
    j = r - pstart[e_r]
    valid = j < counts[e_r]
    slot_of_row = order[jnp.where(valid, start[e_r] + j, 0)]
    rows_tok = jnp.where(valid, slot_of_row // TOP_K, r % n).astype(I32)
    slot_pos = pstart[e_flat] + rank.reshape(-1)
    block_exp = e_r[::bm]
    n_used = (ends_p[-1] // bm).astype(I32).reshape(1)

    xs = _gather_rows(h_packed, rows_tok)
    ys = _expert_mlp(xs, block_exp, n_used, layer, w_gu, b_gu, w_down, b_down)
    yk = _gather_rows(ys, slot_pos.reshape(n, TOP_K).T.reshape(-1))

    tm = ROW_TILE
    row = lambda i: (i, 0)
    const = lambda i: (0, 0)
    choice = lambda k: (lambda i: (k * (n // tm) + i, 0))
    return pl.pallas_call(
        _combine_kernel,
        grid=(n // tm,),
        in_specs=[pl.BlockSpec((tm, HALF_D), choice(k)) for k in range(TOP_K)] + [
                  pl.BlockSpec((tm, TOP_K), row),
                  pl.BlockSpec((tm, D_MODEL), row), pl.BlockSpec((1, D_MODEL), const),
                  pl.BlockSpec((1, D_MODEL), const)],
        out_specs=pl.BlockSpec((tm, D_MODEL), row),
        out_shape=jax.ShapeDtypeStruct((n, D_MODEL), F32),
        compiler_params=_cparams("parallel"),
        name="moe_combine",
    )(yk, yk, yk, yk, gates, h2d, g.reshape(1, D_MODEL), b.reshape(1, D_MODEL))


def _s5_kernel(u_ref, bre_ref, bim_ref, cre_ref, cim_ref, are_ref, aim_ref, d_ref, y_ref,
               bu_re, bu_im, st_re, st_im, h_re, h_im, *, bsz):
    @pl.when(pl.program_id(0) == 0)
    def _():
        h_re[...] = jnp.zeros_like(h_re)
        h_im[...] = jnp.zeros_like(h_im)

    rows = u_ref.shape[0]
    first = lax.broadcasted_iota(I32, (SUBLANES, S5_ST_BLK), 0) < bsz
    for j in range(S5_LANE_BLOCKS):
        cin = slice(j * S5_IN_BLK, (j + 1) * S5_IN_BLK)
        cst = slice(j * S5_ST_BLK, (j + 1) * S5_ST_BLK)
        uj = u_ref[:, cin]
        ujb = uj.astype(BF16)
        bu_re[...] = _dot(ujb, bre_ref[j])
        bu_im[...] = _dot(ujb, bim_ref[j])
        ar = jnp.broadcast_to(are_ref[:, cst], (SUBLANES, S5_ST_BLK))
        ai = jnp.broadcast_to(aim_ref[:, cst], (SUBLANES, S5_ST_BLK))

        def step(i, carry):
            hr, hi = carry
            r0 = pl.multiple_of(i * SUBLANES, SUBLANES)
            vr = bu_re[pl.ds(r0, SUBLANES), :]
            vi = bu_im[pl.ds(r0, SUBLANES), :]
            h1r = ar * hr - ai * hi + vr
            h1i = ar * hi + ai * hr + vi
            h1rs = pltpu.roll(h1r, bsz, 0)
            h1is = pltpu.roll(h1i, bsz, 0)
            h2r = ar * h1rs - ai * h1is + vr
            h2i = ar * h1is + ai * h1rs + vi
            st_re[pl.ds(r0, SUBLANES), :] = jnp.where(first, h1r, h2r)
            st_im[pl.ds(r0, SUBLANES), :] = jnp.where(first, h1i, h2i)
            return pltpu.roll(h2r, bsz, 0), pltpu.roll(h2i, bsz, 0)

        hr, hi = lax.fori_loop(0, rows // SUBLANES, step, (h_re[:, cst], h_im[:, cst]))
        h_re[:, cst] = hr
        h_im[:, cst] = hi
        yj = _dot(st_re[...].astype(BF16), cre_ref[j]) + _dot(st_im[...].astype(BF16), cim_ref[j])
        yj = yj + d_ref[:, cin] * uj
        y_ref[:, cin] = jax.nn.gelu(yj).astype(BF16)


def _s5_block_diag(w, n_in, n_out):
    gpb = SSM_GROUPS // S5_LANE_BLOCKS
    w4 = w.reshape(S5_LANE_BLOCKS, gpb, n_in, n_out)
    eye = jnp.eye(gpb, dtype=w.dtype)
    return jnp.einsum("jgio,gh->jgiho", w4, eye).reshape(S5_LANE_BLOCKS, gpb * n_in, gpb * n_out)


def _s5_mixer_block(h2d, bsz, seq, w_in, log_dt, lam_re, lam_im, b_re, b_im, c_re, c_im, d, w_out, g, b):
    assert 2 * bsz == SUBLANES, "the scan packs two time steps of bsz rows into one 8-row tile"
    tm = ROW_TILE
    per_seq = seq // tm
    u_t = _matmul(h2d, w_in.astype(BF16), tm=tm, out_dtype=F32, grid=(bsz, per_seq),
                  x_map=lambda bb, i: (bb * per_seq + i, 0), out_map=lambda bb, i: (i, bb),
                  out_shape=(seq, bsz * D_MODEL)).reshape(seq * bsz, D_MODEL)

    dt = jnp.exp(log_dt)[:, None]
    mag = jnp.exp(lam_re * dt)
    a_re, a_im = mag * jnp.cos(lam_im * dt), mag * jnp.sin(lam_im * dt)
    den = lam_re * lam_re + lam_im * lam_im
    coef_re = ((a_re - 1.0) * lam_re + a_im * lam_im) / den
    coef_im = (a_im * lam_re - (a_re - 1.0) * lam_im) / den
    bb_re = coef_re[..., None] * b_re - coef_im[..., None] * b_im
    bb_im = coef_re[..., None] * b_im + coef_im[..., None] * b_re
    bre = _s5_block_diag(jnp.swapaxes(bb_re, 1, 2), SSM_GROUP, SSM_STATE).astype(BF16)
    bim = _s5_block_diag(jnp.swapaxes(bb_im, 1, 2), SSM_GROUP, SSM_STATE).astype(BF16)
    cre = _s5_block_diag(jnp.swapaxes(c_re, 1, 2), SSM_STATE, SSM_GROUP).astype(BF16)
    cim = _s5_block_diag(jnp.swapaxes(-c_im, 1, 2), SSM_STATE, SSM_GROUP).astype(BF16)
    n_state = SSM_GROUPS * SSM_STATE

    rows = S5_CHUNK * bsz
    row = lambda c: (c, 0)
    c2 = lambda c: (0, 0)
    c3 = lambda c: (0, 0, 0)
    y_t = pl.pallas_call(
        functools.partial(_s5_kernel, bsz=bsz),
        grid=(seq // S5_CHUNK,),
        in_specs=[pl.BlockSpec((rows, D_MODEL), row),
                  pl.BlockSpec(bre.shape, c3), pl.BlockSpec(bim.shape, c3),
                  pl.BlockSpec(cre.shape, c3), pl.BlockSpec(cim.shape, c3),
                  pl.BlockSpec((1, n_state), c2), pl.BlockSpec((1, n_state), c2), pl.BlockSpec((1, D_MODEL), c2)],
        out_specs=pl.BlockSpec((rows, D_MODEL), row),
        out_shape=jax.ShapeDtypeStruct((seq * bsz, D_MODEL), BF16),
        scratch_shapes=[pltpu.VMEM((rows, S5_ST_BLK), F32)] * 4 + [pltpu.VMEM((SUBLANES, n_state), F32)] * 2,
        compiler_params=_cparams("arbitrary"),
        name="s5_scan",
    )(u_t, bre, bim, cre, cim, a_re.reshape(1, n_state), a_im.reshape(1, n_state), d.reshape(1, D_MODEL))

    y2 = y_t.reshape(seq, bsz * D_MODEL)
    return _linear_residual_ln(
        [y2], [w_out.astype(BF16)], h2d, g, b, tm=tm, glu=True, grid=(bsz, per_seq),
        x_maps=[lambda bb, i: (i, bb)], res_map=lambda bb, i: (bb * per_seq + i, 0))


def _even_mixer_block(h2d, bsz, seq, w_in, qnorm_g, w_uq, w_uq_idx, kidx_g, kidx_b, w_out, g, b):
    qa, qi, ka, va, ki, wi, qb, kb, vb = _even_proj(h2d, bsz, seq, w_in, qnorm_g, w_uq, w_uq_idx, kidx_g, kidx_b)
    o_a = _dsa_attention(qa, qi, wi, ki, ka, va, bsz, seq)
    o_b = _stick_breaking(qb, kb, vb, bsz, seq)
    w_out = w_out.astype(BF16)
    return _linear_residual_ln([o_a, o_b], [w_out[:A_WIDTH], w_out[A_WIDTH:]], h2d, g, b, tm=ROW_TILE)


def kernel(x, mem, ev_w_in, ev_qnorm_g, ev_w_uq, ev_w_uq_idx, ev_kidx_ln_g, ev_kidx_ln_b, ev_w_out, od_w_in, od_log_dt, od_lambda_re, od_lambda_im, od_b_re, od_b_im, od_c_re, od_c_im, od_d, od_w_out, mix_ln_g, mix_ln_b, xa_w_q, xa_w_kv, xa_w_o, xa_ln_g, xa_ln_b, moe_w_router, moe_b_router, moe_w_gu, moe_b_gu, moe_w_down, moe_b_down, ffn_ln_g, ffn_ln_b):
    bsz, seq, _ = x.shape
    h = x.reshape(bsz * seq, D_MODEL)
    for layer in range(DEPTH):
        j = layer // 2
        if layer % 2 == 0:
            h = _even_mixer_block(h, bsz, seq, ev_w_in[j], ev_qnorm_g[j], ev_w_uq[j], ev_w_uq_idx[j],
                                  ev_kidx_ln_g[j], ev_kidx_ln_b[j], ev_w_out[j], mix_ln_g[layer], mix_ln_b[layer])
        else:
            h = _s5_mixer_block(h, bsz, seq, od_w_in[j], od_log_dt[j], od_lambda_re[j], od_lambda_im[j],
                                od_b_re[j], od_b_im[j], od_c_re[j], od_c_im[j], od_d[j], od_w_out[j],
                                mix_ln_g[layer], mix_ln_b[layer])
        h, h_packed = _cross_attention_block(h, mem, bsz, seq, xa_w_q[layer], xa_w_kv[layer], xa_w_o[layer],
                                             xa_ln_g[layer], xa_ln_b[layer])
        h = _moe_block(h, h_packed, layer, moe_w_router[layer], moe_b_router[layer], moe_w_gu, moe_b_gu[layer],
                       moe_w_down, moe_b_down[layer], ffn_ln_g[layer], ffn_ln_b[layer])
    return h.reshape(bsz, seq, D_MODEL)
```

```python
import functools
import math

import jax
import jax.numpy as jnp
from jax import lax
from jax.experimental import pallas as pl
from jax.experimental.pallas import tpu as pltpu
from jax.experimental.pallas import tpu_sc as plsc

F32 = jnp.float32
BF16 = jnp.bfloat16
I32 = jnp.int32

D_MODEL = 1024
DEPTH = 2
HEAD_DIM = 64
A_HEADS = 8
A_KV_HEADS = 2
A_REP = A_HEADS // A_KV_HEADS
Q_RANK = 256
IDX_HEADS = 8
IDX_DIM = 64
IDX_TOPK = 256
B_HEADS = 8
A_WIDTH = A_HEADS * HEAD_DIM
B_WIDTH = B_HEADS * HEAD_DIM
SSM_GROUP = 16
SSM_GROUPS = D_MODEL // SSM_GROUP
SSM_STATE = 64
XA_HEADS = 4
XA_HEAD_DIM = D_MODEL // XA_HEADS
N_EXPERTS = 32
TOP_K = 4
D_EXPERT = D_MODEL
SWIGLU_LIMIT = 7.0
SWIGLU_ALPHA = 1.702
ROPE_THETA = 500000.0
ROPE_HALF = HEAD_DIM // 8
LN_EPS = 1e-5
DN_ALPHA = (2 * DEPTH) ** 0.25

LANES = 128
SUBLANES = 8
VMEM_LIMIT_BYTES = 56 * 1024 * 1024

Q_BLOCK = 256
DSA_KEY_TILE = 512
DSA_ATT_TILE = 512
DSA_COUNT_ROWS = 8 * SUBLANES
SB_KEY_TILE = 256
ROW_TILE = 512
MOE_BLOCK_ROWS = 512
SC_GATHER_SLOT_BYTES = 128 * 1024
S5_CHUNK = 128
S5_LANE_BLOCKS = 4
S5_IN_BLK = D_MODEL // S5_LANE_BLOCKS
S5_ST_BLK = SSM_GROUPS * SSM_STATE // S5_LANE_BLOCKS

SB_EXIT_LOG = -104.0
NEG_BIG = -1e30
INT_MIN = -(2 ** 31)


def _cparams(*sem):
    return pltpu.CompilerParams(dimension_semantics=sem, vmem_limit_bytes=VMEM_LIMIT_BYTES)


def _dot(a, b):
    return jnp.dot(a, b, preferred_element_type=F32)


def _dot_nt(a, b):
    return lax.dot_general(a, b, (((1,), (1,)), ((), ())), preferred_element_type=F32)


def _layer_norm_rows(y, g, b):
    mu = jnp.mean(y, axis=-1, keepdims=True)
    d = y - mu
    var = jnp.mean(d * d, axis=-1, keepdims=True)
    return d * lax.rsqrt(var + LN_EPS) * g + b


def _mm_kernel(x_ref, w_ref, o_ref):
    o_ref[...] = _dot(x_ref[...].astype(BF16), w_ref[...]).astype(o_ref.dtype)


def _matmul(x, w, *, tm, out_dtype, x_map=None, out_map=None, grid=None, out_shape=None):
    m, k = x.shape
    n = w.shape[1]
    grid = grid or (m // tm,)
    x_map = x_map or (lambda i: (i, 0))
    out_map = out_map or (lambda i: (i, 0))
    out_shape = out_shape or (m, n)
    return pl.pallas_call(
        _mm_kernel,
        grid=grid,
        in_specs=[pl.BlockSpec((tm, k), x_map), pl.BlockSpec((k, n), lambda *a: (0, 0))],
        out_specs=pl.BlockSpec((tm, n), out_map),
        out_shape=jax.ShapeDtypeStruct(out_shape, out_dtype),
        compiler_params=_cparams(*(("parallel",) * len(grid))),
        name="matmul",
    )(x, w)


def _lin_ln_kernel(*refs, n_in, glu):
    xs, ws = refs[:n_in], refs[n_in:2 * n_in]
    res_ref, g_ref, b_ref, o_ref = refs[2 * n_in:]
    acc = _dot(xs[0][...].astype(BF16), ws[0][...])
    for x_ref, w_ref in zip(xs[1:], ws[1:]):
        acc = acc + _dot(x_ref[...].astype(BF16), w_ref[...])
    if glu:
        acc = acc[:, :D_MODEL] * jax.nn.sigmoid(acc[:, D_MODEL:])
    y = DN_ALPHA * res_ref[...] + acc
    o_ref[...] = _layer_norm_rows(y, g_ref[...], b_ref[...])


def _linear_residual_ln(xs, ws, res, g, b, *, tm, glu=False, grid=None, x_maps=None, res_map=None):
    n_rows = res.shape[0]
    grid = grid or (n_rows // tm,)
    x_maps = x_maps or [lambda i: (i, 0)] * len(xs)
    res_map = res_map or (lambda i: (i, 0))
    const = lambda *a: (0, 0)
    in_specs = [pl.BlockSpec((tm, w.shape[0]), m) for w, m in zip(ws, x_maps)]
    in_specs += [pl.BlockSpec(w.shape, const) for w in ws]
    in_specs += [pl.BlockSpec((tm, D_MODEL), res_map), pl.BlockSpec((1, D_MODEL), const),
                 pl.BlockSpec((1, D_MODEL), const)]
    return pl.pallas_call(
        functools.partial(_lin_ln_kernel, n_in=len(xs), glu=glu),
        grid=grid,
        in_specs=in_specs,
        out_specs=pl.BlockSpec((tm, D_MODEL), res_map),
        out_shape=jax.ShapeDtypeStruct((n_rows, D_MODEL), F32),
        compiler_params=_cparams(*(("parallel",) * len(grid))),
        name="linear_residual_ln",
    )(*xs, *ws, res, g.reshape(1, D_MODEL), b.reshape(1, D_MODEL))


_EV_CQ, _EV_KA, _EV_VA, _EV_KI, _EV_QB = 0, 256, 384, 512, 640
_EV_KB = _EV_QB + B_WIDTH
_EV_VB = _EV_KB + B_WIDTH
_EV_COLS = _EV_VB + B_WIDTH


def _rope_tables(seq):
    inv = ROPE_THETA ** (-jnp.arange(ROPE_HALF, dtype=F32) / ROPE_HALF)
    ang = jnp.arange(seq, dtype=F32)[:, None] * inv[None, :]
    cos, sin = jnp.cos(ang), jnp.sin(ang)
    rest = HEAD_DIM - 2 * ROPE_HALF
    zh = jnp.zeros((seq, ROPE_HALF), F32)
    c = jnp.concatenate([cos, cos, jnp.ones((seq, rest), F32)], axis=1)
    s1 = jnp.concatenate([-sin, zh, jnp.zeros((seq, rest), F32)], axis=1)
    s2 = jnp.concatenate([zh, sin, jnp.zeros((seq, rest), F32)], axis=1)
    rep = LANES // HEAD_DIM
    return jnp.tile(c, (1, rep)), jnp.tile(s1, (1, rep)), jnp.tile(s2, (1, rep))


def _even_proj_kernel(x_ref, w_ref, qg_ref, wuq_ref, wuqi_ref, lg_ref, lb_ref, c_ref, s1_ref, s2_ref,
                      qa_ref, qi_ref, ka_ref, va_ref, ki_ref, wi_ref, qb_ref, kb_ref, vb_ref):
    p = _dot(x_ref[...].astype(BF16), w_ref[...])
    c, s1, s2 = c_ref[...], s1_ref[...], s2_ref[...]

    def rope(t):
        return (t * c + pltpu.roll(t, LANES - ROPE_HALF, 1) * s1 + pltpu.roll(t, ROPE_HALF, 1) * s2)

    cq = p[:, _EV_CQ:_EV_CQ + Q_RANK]
    cn = cq * lax.rsqrt(jnp.mean(cq * cq, axis=-1, keepdims=True) + LN_EPS) * qg_ref[...]
    cnb = cn.astype(BF16)
    qa = _dot(cnb, wuq_ref[...])
    qi = _dot(cnb, wuqi_ref[...])
    low = lax.broadcasted_iota(I32, c.shape, 1) < HEAD_DIM
    for j in range(A_WIDTH // LANES):
        sl = slice(j * LANES, (j + 1) * LANES)
        pair = rope(qa[:, sl]) * (HEAD_DIM ** -0.5)
        for e, src in enumerate((pair, pltpu.roll(pair, HEAD_DIM, 1))):
            h = 2 * j + e
            qa_ref[:, h * LANES:(h + 1) * LANES] = jnp.where(low, src, 0.0).astype(BF16)
        qi_ref[:, sl] = (rope(qi[:, sl]) * (IDX_DIM ** -0.5)).astype(BF16)
    kpair = rope(p[:, _EV_KA:_EV_KA + LANES])
    vpair = p[:, _EV_VA:_EV_VA + LANES]
    v_pad = jnp.where(lax.broadcasted_iota(I32, c.shape, 1) == HEAD_DIM, 1.0, 0.0)
    for g, (ks, vs) in enumerate(((kpair, vpair), (pltpu.roll(kpair, HEAD_DIM, 1), pltpu.roll(vpair, HEAD_DIM, 1)))):
        ka_ref[:, g * LANES:(g + 1) * LANES] = jnp.where(low, ks, 0.0).astype(BF16)
        va_ref[:, g * LANES:(g + 1) * LANES] = jnp.where(low, vs, v_pad).astype(BF16)

    t = p[:, _EV_KI:_EV_KI + LANES]
    lane = lax.broadcasted_iota(I32, t.shape, 1)
    is_k = lane < IDX_DIM
    mu = jnp.sum(jnp.where(is_k, t, 0.0), axis=-1, keepdims=True) * (1.0 / IDX_DIM)
    d = jnp.where(is_k, t - mu, 0.0)
    var = jnp.sum(d * d, axis=-1, keepdims=True) * (1.0 / IDX_DIM)
    kin = d * lax.rsqrt(var + LN_EPS) * lg_ref[...] + lb_ref[...]
    ki_ref[...] = rope(kin)[:, :IDX_DIM].astype(BF16)
    wi_ref[...] = t[:, IDX_DIM:IDX_DIM + IDX_HEADS] * (IDX_HEADS ** -0.5)

    qb_ref[...] = (p[:, _EV_QB:_EV_KB] * (HEAD_DIM ** -0.5)).astype(BF16)
    kb_ref[...] = p[:, _EV_KB:_EV_VB].astype(BF16)
    vb_ref[...] = p[:, _EV_VB:_EV_COLS].astype(BF16)


def _even_proj(x2d, bsz, seq, w_in, qnorm_g, w_uq, w_uq_idx, kidx_g, kidx_b):
    n = x2d.shape[0]
    tm = ROW_TILE
    per_seq = seq // tm
    c0 = Q_RANK + 2 * A_KV_HEADS * HEAD_DIM + IDX_DIM + IDX_HEADS
    w_pack = jnp.concatenate(
        [w_in[:, :c0], jnp.zeros((D_MODEL, _EV_QB - c0), w_in.dtype), w_in[:, c0:]], axis=1).astype(BF16)
    pad = LANES - IDX_DIM
    lg = jnp.concatenate([kidx_g, jnp.zeros((pad,), F32)]).reshape(1, LANES)
    lb = jnp.concatenate([kidx_b, jnp.zeros((pad,), F32)]).reshape(1, LANES)
    c, s1, s2 = _rope_tables(seq)
    row = lambda i: (i, 0)
    const = lambda i: (0, 0)
    pos = lambda i: (i % per_seq, 0)
    head_shape = jax.ShapeDtypeStruct((n, B_WIDTH), BF16)
    head_spec = pl.BlockSpec((tm, B_WIDTH), row)
    return pl.pallas_call(
        _even_proj_kernel,
        grid=(n // tm,),
        in_specs=[pl.BlockSpec((tm, D_MODEL), row), pl.BlockSpec((D_MODEL, _EV_COLS), const),
                  pl.BlockSpec((1, Q_RANK), const), pl.BlockSpec((Q_RANK, A_WIDTH), const),
                  pl.BlockSpec((Q_RANK, IDX_HEADS * IDX_DIM), const),
                  pl.BlockSpec((1, LANES), const), pl.BlockSpec((1, LANES), const),
                  pl.BlockSpec((tm, LANES), pos), pl.BlockSpec((tm, LANES), pos), pl.BlockSpec((tm, LANES), pos)],
        out_specs=[pl.BlockSpec((tm, A_HEADS * LANES), row), pl.BlockSpec((tm, IDX_HEADS * IDX_DIM), row),
                   pl.BlockSpec((tm, A_KV_HEADS * LANES), row), pl.BlockSpec((tm, A_KV_HEADS * LANES), row),
                   pl.BlockSpec((tm, IDX_DIM), row), pl.BlockSpec((tm, IDX_HEADS), row),
                   head_spec, head_spec, head_spec],
        out_shape=[jax.ShapeDtypeStruct((n, A_HEADS * LANES), BF16), jax.ShapeDtypeStruct((n, IDX_HEADS * IDX_DIM), BF16),
                   jax.ShapeDtypeStruct((n, A_KV_HEADS * LANES), BF16),
                   jax.ShapeDtypeStruct((n, A_KV_HEADS * LANES), BF16),
                   jax.ShapeDtypeStruct((n, IDX_DIM), BF16), jax.ShapeDtypeStruct((n, IDX_HEADS), F32),
                   head_shape, head_shape, head_shape],
        compiler_params=_cparams("parallel"),
        name="even_proj",
    )(x2d, w_pack, qnorm_g.reshape(1, Q_RANK), w_uq.astype(BF16), w_uq_idx.astype(BF16), lg, lb, c, s1, s2)


def _key_to_float(key):
    bits = key ^ ((key >> 31) & jnp.int32(0x7FFFFFFF))
    return lax.bitcast_convert_type(bits, F32)


def _high_half(x):
    bits = lax.bitcast_convert_type(x, jnp.uint32) & jnp.uint32(0xFFFF0000)
    return lax.bitcast_convert_type(bits, F32).astype(BF16)


def _dsa_kernel(qa_ref, qi_ref, wit_ref, ki_ref, ka_ref, vat_ref, o_ref, sc_scr, hi_scr, *, topk, ts, ta):
    seq = sc_scr.shape[0]
    qb = pl.program_id(1)
    q0 = qb * Q_BLOCK
    nkt = (q0 + Q_BLOCK - 1) // ts + 1
    t_row = q0 + lax.broadcasted_iota(I32, (1, Q_BLOCK), 1)
    key = lax.broadcasted_iota(I32, (ts, Q_BLOCK), 0)
    kf = jnp.float32(topk)

    qi = qi_ref[0]
    qs = jnp.concatenate([qi[:, h * IDX_DIM:(h + 1) * IDX_DIM] for h in range(IDX_HEADS)], axis=0)
    wit = wit_ref[0]

    def score_tile(kt, carry):
        off = pl.multiple_of(kt * ts, ts)
        s_all = _dot_nt(ki_ref[0, pl.ds(off, ts), :], qs)
        acc = jnp.zeros((ts, Q_BLOCK), F32)
        for h in range(IDX_HEADS):
            acc = acc + jnp.maximum(s_all[:, h * Q_BLOCK:(h + 1) * Q_BLOCK], 0.0) * wit[h:h + 1, :]
        val = jnp.where(off + key <= t_row, acc, -jnp.inf)
        sc_scr[pl.ds(off, ts), :] = val
        hi_scr[pl.ds(off, ts), :] = _high_half(val)
        return carry

    lax.fori_loop(0, nkt, score_tile, 0)

    def count(pred):
        def body(kt, acc):
            off = pl.multiple_of(kt * ts, ts)
            ind = pred(sc_scr[pl.ds(off, ts), :], off + key)
            return acc + jnp.sum(ind.reshape(ts // DSA_COUNT_ROWS, DSA_COUNT_ROWS, Q_BLOCK), axis=0)
        acc = lax.fori_loop(0, nkt, body, jnp.zeros((DSA_COUNT_ROWS, Q_BLOCK), F32))
        return jnp.sum(acc, axis=0, keepdims=True)

    one_h, zero_h = jnp.ones((), BF16), jnp.zeros((), BF16)

    def count_high(c_hi):
        def body(kt, acc):
            off = pl.multiple_of(kt * ts, ts)
            ind = jnp.where(hi_scr[pl.ds(off, ts), :] >= c_hi, one_h, zero_h)
            part = ind[:DSA_COUNT_ROWS]
            for j in range(1, ts // DSA_COUNT_ROWS):
                part = part + ind[j * DSA_COUNT_ROWS:(j + 1) * DSA_COUNT_ROWS]
            return acc + part.astype(F32)
        acc = lax.fori_loop(0, nkt, body, jnp.zeros((DSA_COUNT_ROWS, Q_BLOCK), F32))
        return jnp.sum(acc, axis=0, keepdims=True)

    def high_step(i, base):
        cand = base + jnp.left_shift(jnp.int32(1), 31 - i)
        cnt = count_high(_high_half(_key_to_float(cand)))
        return jnp.where(cnt >= kf, cand, base)

    def bit_step(i, base):
        cand = base + jnp.left_shift(jnp.int32(1), 31 - i)
        cf = _key_to_float(cand)
        cnt = count(lambda sc, idx: jnp.where(sc >= cf, 1.0, 0.0))
        return jnp.where(cnt >= kf, cand, base)

    base = lax.fori_loop(0, 16, high_step, jnp.full((1, Q_BLOCK), INT_MIN, I32))
    base = lax.fori_loop(16, 32, bit_step, base)
    thr = jnp.where(base == INT_MIN, -jnp.inf, _key_to_float(base))

    cnt_ge = count(lambda sc, idx: jnp.where(sc >= thr, 1.0, 0.0))
    tied = jnp.logical_and(cnt_ge > kf, thr > -jnp.inf)
    any_tied = jnp.max(jnp.where(tied, 1.0, 0.0)) > 0.0
    seq_bits = max(1, int(math.ceil(math.log2(seq))))

    def tie_cut():
        cnt_gt = count(lambda sc, idx: jnp.where(sc > thr, 1.0, 0.0))
        need = kf - cnt_gt

        def idx_step(i, pos):
            cand = pos + jnp.left_shift(jnp.int32(1), seq_bits - 1 - i)
            cnt = count(lambda sc, idx: jnp.where(sc == thr, jnp.where(idx < cand, 1.0, 0.0), 0.0))
            return jnp.where(cnt < need, cand, pos)

        return lax.fori_loop(0, seq_bits, idx_step, jnp.zeros((1, Q_BLOCK), I32))

    cut = lax.cond(any_tied, tie_cut, lambda: jnp.full((1, Q_BLOCK), seq, I32))
    cut = jnp.where(tied, cut, seq)

    nkt_a = (q0 + Q_BLOCK - 1) // ta + 1
    key_a = lax.broadcasted_iota(I32, (ta, Q_BLOCK), 0)
    cols = A_REP * Q_BLOCK
    qg = [jnp.concatenate([qa_ref[0, :, (g * A_REP + r) * LANES:(g * A_REP + r + 1) * LANES]
                           for r in range(A_REP)], axis=0) for g in range(A_KV_HEADS)]

    def att_pair(i, carry):
        offs = [pl.multiple_of((2 * i + e) * ta, ta) for e in range(2)]
        logits = [_dot_nt(ka_ref[0, pl.ds(offs[e], ta), g * LANES:(g + 1) * LANES], qg[g])
                  for e in range(2) for g in range(A_KV_HEADS)]
        out = []
        for e in range(2):
            sc = sc_scr[pl.ds(offs[e], ta), :]
            idx = offs[e] + key_a
            keep = jnp.where(sc > thr, 0.0, jnp.where(sc == thr, jnp.where(idx <= cut, 0.0, NEG_BIG), NEG_BIG))
            bias = jnp.where(idx <= t_row, keep, NEG_BIG)
            bias = jnp.concatenate([bias] * A_REP, axis=1)
            for g in range(A_KV_HEADS):
                m, acc = carry[e * A_KV_HEADS + g]
                s = logits[e * A_KV_HEADS + g] + bias
                m_new = jnp.maximum(m, jnp.max(s, axis=0, keepdims=True))
                p = jnp.exp(s - m_new)
                vt = vat_ref[0, g * LANES:(g + 1) * LANES, pl.ds(offs[e], ta)]
                out.append((m_new, jnp.exp(m - m_new) * acc + _dot(vt, p.astype(BF16))))
        return tuple(out)

    init = tuple((jnp.full((1, cols), NEG_BIG, F32), jnp.zeros((LANES, cols), F32))
                 for _ in range(2 * A_KV_HEADS))
    final = lax.fori_loop(0, (nkt_a + 1) // 2, att_pair, init)
    low = lax.broadcasted_iota(I32, (Q_BLOCK, LANES), 1) < HEAD_DIM
    outs = []
    for g in range(A_KV_HEADS):
        (m0, acc0), (m1, acc1) = final[g], final[A_KV_HEADS + g]
        m = jnp.maximum(m0, m1)
        acc = jnp.exp(m0 - m) * acc0 + jnp.exp(m1 - m) * acc1
        og = acc / acc[HEAD_DIM:HEAD_DIM + 1, :]
        outs += [og[:, r * Q_BLOCK:(r + 1) * Q_BLOCK].T for r in range(A_REP)]
    for j in range(A_HEADS // 2):
        pair = jnp.where(low, outs[2 * j], pltpu.roll(outs[2 * j + 1], HEAD_DIM, 1))
        o_ref[0, :, j * LANES:(j + 1) * LANES] = pair.astype(BF16)


def _dsa_attention(qa, qi, wi, ki, ka, va, bsz, seq):
    topk = min(IDX_TOPK, seq // 4)
    ts = min(DSA_KEY_TILE, seq)
    ta = min(DSA_ATT_TILE, seq // 2)
    assert seq % (2 * ta) == 0 and seq % ts == 0, "the attention loop walks the key tiles in pairs"
    blk = lambda b, i: (b, i, 0)
    full = lambda b, i: (b, 0, 0)
    r3 = lambda a: a.reshape(bsz, seq, a.shape[-1])
    wit = jnp.swapaxes(r3(wi), 1, 2)
    vat = jnp.swapaxes(r3(va), 1, 2)
    return pl.pallas_call(
        functools.partial(_dsa_kernel, topk=topk, ts=ts, ta=ta),
        grid=(bsz, seq // Q_BLOCK),
        in_specs=[pl.BlockSpec((1, Q_BLOCK, A_HEADS * LANES), blk),
                  pl.BlockSpec((1, Q_BLOCK, IDX_HEADS * IDX_DIM), blk),
                  pl.BlockSpec((1, IDX_HEADS, Q_BLOCK), lambda b, i: (b, 0, i)),
                  pl.BlockSpec((1, seq, IDX_DIM), full),
                  pl.BlockSpec((1, seq, A_KV_HEADS * LANES), full),
                  pl.BlockSpec((1, A_KV_HEADS * LANES, seq), full)],
        out_specs=pl.BlockSpec((1, Q_BLOCK, A_WIDTH), blk),
        out_shape=jax.ShapeDtypeStruct((bsz, seq, A_WIDTH), BF16),
        scratch_shapes=[pltpu.VMEM((seq, Q_BLOCK), F32), pltpu.VMEM((seq, Q_BLOCK), BF16)],
        compiler_params=_cparams("parallel", "parallel"),
        name="dsa_attention",
    )(r3(qa), r3(qi), wit, r3(ki), r3(ka), vat).reshape(bsz * seq, A_WIDTH)


def _sb_kernel(q_ref, k_ref, v_ref, u_ref, o_ref, acc_scr, run_scr, *, tk):
    q0 = pl.program_id(1) * Q_BLOCK
    t_col = q0 + lax.broadcasted_iota(I32, (Q_BLOCK, 1), 0)
    lane = lax.broadcasted_iota(I32, (Q_BLOCK, tk), 1)
    low = lax.broadcasted_iota(I32, (Q_BLOCK, LANES), 1) < HEAD_DIM
    upper = u_ref[...]
    nkt = (q0 + Q_BLOCK - 1) // tk + 1
    q = q_ref[0]
    zero = jnp.zeros((Q_BLOCK, LANES), BF16)
    qm = []
    for p in range(B_HEADS // 2):
        pair = q[:, p * LANES:(p + 1) * LANES]
        qm.append(jnp.concatenate([jnp.where(low, pair, zero), jnp.where(low, zero, pair)], axis=0))
    acc_scr[...] = jnp.zeros_like(acc_scr)
    run_scr[...] = jnp.zeros_like(run_scr)

    def cond(carry):
        i, worst = carry
        return jnp.logical_and(i < nkt, worst >= SB_EXIT_LOG)

    def body(carry):
        i, _ = carry
        off = pl.multiple_of((nkt - 1 - i) * tk, tk)
        strict = off + lane < t_col
        strict = jnp.concatenate([strict, strict], axis=0)
        worst = None
        for p in range(B_HEADS // 2):
            cols = slice(p * LANES, (p + 1) * LANES)
            kp = k_ref[0, pl.ds(off, tk), cols]
            vp = v_ref[0, pl.ds(off, tk), cols]
            run = run_scr[p]
            z = _dot_nt(qm[p], kp)
            softplus = jnp.maximum(z, 0.0) + jnp.log(1.0 + jnp.exp(-jnp.abs(z)))
            log_1mb = jnp.where(strict, -softplus, 0.0)
            hi = log_1mb.astype(BF16)
            lo = (log_1mb - hi.astype(F32)).astype(BF16)
            after = _dot(hi, upper) + _dot(lo, upper) + run
            a = jnp.where(strict, jnp.exp(z - softplus + after), 0.0)
            out = _dot(a.astype(BF16), vp)
            run = run + jnp.sum(log_1mb, axis=1, keepdims=True)
            run_scr[p] = run
            worst = run if worst is None else jnp.maximum(worst, run)
            acc_scr[:, cols] += jnp.where(low, out[:Q_BLOCK], out[Q_BLOCK:])
        return i + 1, jnp.max(worst)

    lax.while_loop(cond, body, (jnp.int32(0), jnp.float32(0.0)))
    o_ref[0] = acc_scr[...].astype(BF16)


def _stick_breaking(qb, kb, vb, bsz, seq):
    tk = min(SB_KEY_TILE, seq)
    r = lax.broadcasted_iota(I32, (tk, tk), 0)
    c = lax.broadcasted_iota(I32, (tk, tk), 1)
    upper = jnp.where(r > c, 1.0, 0.0).astype(BF16)
    blk = lambda b, i: (b, i, 0)
    full = lambda b, i: (b, 0, 0)
    r3 = lambda a: a.reshape(bsz, seq, B_WIDTH)
    return pl.pallas_call(
        functools.partial(_sb_kernel, tk=tk),
        grid=(bsz, seq // Q_BLOCK),
        in_specs=[pl.BlockSpec((1, Q_BLOCK, B_WIDTH), blk), pl.BlockSpec((1, seq, B_WIDTH), full),
                  pl.BlockSpec((1, seq, B_WIDTH), full), pl.BlockSpec((tk, tk), lambda b, i: (0, 0))],
        out_specs=pl.BlockSpec((1, Q_BLOCK, B_WIDTH), blk),
        out_shape=jax.ShapeDtypeStruct((bsz, seq, B_WIDTH), BF16),
        scratch_shapes=[pltpu.VMEM((Q_BLOCK, B_WIDTH), F32), pltpu.VMEM((B_HEADS // 2, 2 * Q_BLOCK, 1), F32)],
        compiler_params=_cparams("parallel", "arbitrary"),
        name="stick_breaking",
    )(r3(qb), r3(kb), r3(vb), upper).reshape(bsz * seq, B_WIDTH)


HALF_D = D_MODEL // 2
U32 = jnp.uint32
HIGH16 = 0xFFFF0000


def _pack_bf16_pairs(x):
    def bits(v):
        return lax.bitcast_convert_type(v.astype(BF16).astype(F32), U32)
    word = (bits(x[:, HALF_D:]) & U32(HIGH16)) | (bits(x[:, :HALF_D]) >> 16)
    return lax.bitcast_convert_type(word, I32)


def _unpack_bf16_pairs(word):
    u = lax.bitcast_convert_type(word, U32)
    return lax.bitcast_convert_type(u << 16, F32), lax.bitcast_convert_type(u & U32(HIGH16), F32)


def _xattn_kernel(h_ref, wq_ref, kv_ref, wo_ref, g_ref, b_ref, o_ref, packed_ref):
    h = h_ref[...]
    q = (_dot(h.astype(BF16), wq_ref[...]) * (XA_HEAD_DIM ** -0.5)).astype(BF16)
    kv = kv_ref[0]
    outs = []
    for hd in range(XA_HEADS):
        sl = slice(hd * XA_HEAD_DIM, (hd + 1) * XA_HEAD_DIM)
        s = _dot_nt(q[:, sl], kv[:, sl])
        p = jnp.exp(s - jnp.max(s, axis=1, keepdims=True))
        vh = kv[:, D_MODEL + hd * XA_HEAD_DIM:D_MODEL + (hd + 1) * XA_HEAD_DIM]
        outs.append((_dot(p.astype(BF16), vh) / jnp.sum(p, axis=1, keepdims=True)).astype(BF16))
    y = _dot(jnp.concatenate(outs, axis=1), wo_ref[...])
    out = _layer_norm_rows(DN_ALPHA * h + y, g_ref[...], b_ref[...])
    o_ref[...] = out
    packed_ref[...] = _pack_bf16_pairs(out)


def _cross_attention_block(h2d, mem, bsz, seq, w_q, w_kv, w_o, g, b):
    tm = ROW_TILE
    per_seq = seq // tm
    mem_len = mem.shape[1]
    kv = _matmul(mem.reshape(bsz * mem_len, D_MODEL), w_kv.astype(BF16), tm=mem_len, out_dtype=BF16)
    kv = kv.reshape(bsz, mem_len, 2 * D_MODEL)
    row = lambda i: (i, 0)
    const = lambda i: (0, 0)
    return pl.pallas_call(
        _xattn_kernel,
        grid=(bsz * per_seq,),
        in_specs=[pl.BlockSpec((tm, D_MODEL), row), pl.BlockSpec((D_MODEL, D_MODEL), const),
                  pl.BlockSpec((1, mem_len, 2 * D_MODEL), lambda i: (i // per_seq, 0, 0)),
                  pl.BlockSpec((D_MODEL, D_MODEL), const),
                  pl.BlockSpec((1, D_MODEL), const), pl.BlockSpec((1, D_MODEL), const)],
        out_specs=[pl.BlockSpec((tm, D_MODEL), row), pl.BlockSpec((tm, HALF_D), row)],
        out_shape=[jax.ShapeDtypeStruct(h2d.shape, F32), jax.ShapeDtypeStruct((h2d.shape[0], HALF_D), I32)],
        compiler_params=_cparams("parallel"),
        name="cross_attention",
    )(h2d, w_q.astype(BF16), kv, w_o.astype(BF16), g.reshape(1, D_MODEL), b.reshape(1, D_MODEL))


def _router_kernel(h_ref, w_ref, b_ref, tri_ref, idx_ref, gate_ref, rank_ref, cnt_ref, run_scr):
    @pl.when(pl.program_id(0) == 0)
    def _():
        run_scr[...] = jnp.zeros_like(run_scr)

    h = h_ref[...]
    hh = h.astype(BF16)
    hl = (h - hh.astype(F32)).astype(BF16)
    w = w_ref[...]
    wh = w.astype(BF16)
    wl = (w - wh.astype(F32)).astype(BF16)
    logits = _dot(hh, wh) + _dot(hl, wh) + _dot(hh, wl) + b_ref[...]
    lane = lax.broadcasted_iota(I32, logits.shape, 1).astype(F32)
    vals, sels = [], []
    onehot = jnp.zeros(logits.shape, F32)
    for k in range(TOP_K):
        m = jnp.max(logits, axis=1, keepdims=True)
        sel = jnp.min(jnp.where(logits == m, lane, float(LANES)), axis=1, keepdims=True)
        idx_ref[:, k:k + 1] = sel.astype(I32)
        vals.append(m)
        sels.append(sel)
        onehot = onehot + jnp.where(lane == sel, 1.0, 0.0)
        logits = jnp.where(lane == sel, -jnp.inf, logits)
    es = [jnp.exp(v - vals[0]) for v in vals]
    tot = es[0] + es[1] + es[2] + es[3]
    for k in range(TOP_K):
        gate_ref[:, k:k + 1] = es[k] / tot

    earlier = _dot(tri_ref[...], onehot.astype(BF16)) + run_scr[...]
    for k in range(TOP_K):
        rank = jnp.sum(jnp.where(lane == sels[k], earlier, 0.0), axis=1, keepdims=True)
        rank_ref[:, k:k + 1] = rank.astype(I32)
    run = run_scr[...] + jnp.sum(onehot, axis=0, keepdims=True)
    run_scr[...] = run
    cnt_ref[...] = run


def _router(h2d, w_router, b_router):
    n = h2d.shape[0]
    tm = 2 * ROW_TILE
    pad = LANES - N_EXPERTS
    w = jnp.concatenate([w_router, jnp.zeros((D_MODEL, pad), F32)], axis=1)
    b = jnp.concatenate([b_router, jnp.full((pad,), NEG_BIG, F32)]).reshape(1, LANES)
    r = lax.broadcasted_iota(I32, (tm, tm), 0)
    c = lax.broadcasted_iota(I32, (tm, tm), 1)
    tri = jnp.where(c < r, 1.0, 0.0).astype(BF16)
    row = lambda i: (i, 0)
    const = lambda i: (0, 0)
    return pl.pallas_call(
        _router_kernel,
        grid=(n // tm,),
        in_specs=[pl.BlockSpec((tm, D_MODEL), row), pl.BlockSpec((D_MODEL, LANES), const),
                  pl.BlockSpec((1, LANES), const), pl.BlockSpec((tm, tm), const)],
        out_specs=[pl.BlockSpec((tm, TOP_K), row), pl.BlockSpec((tm, TOP_K), row),
                   pl.BlockSpec((tm, TOP_K), row), pl.BlockSpec((1, LANES), const)],
        out_shape=[jax.ShapeDtypeStruct((n, TOP_K), I32), jax.ShapeDtypeStruct((n, TOP_K), F32),
                   jax.ShapeDtypeStruct((n, TOP_K), I32), jax.ShapeDtypeStruct((1, LANES), F32)],
        scratch_shapes=[pltpu.VMEM((1, LANES), F32)],
        compiler_params=_cparams("arbitrary"),
        name="moe_router",
    )(h2d, w, b, tri)


def _gather_rows(src, idx):
    n_out = idx.shape[0]
    width = src.shape[1]
    win = SC_GATHER_SLOT_BYTES // (width * src.dtype.itemsize)
    mesh = plsc.VectorSubcoreMesh(core_axis_name="core", subcore_axis_name="subcore")
    n_workers = mesh.num_cores * mesh.num_subcores
    per_worker = n_out // n_workers
    steps = per_worker // win
    assert per_worker * n_workers == n_out and steps * win == per_worker and steps % 2 == 0

    @functools.partial(
        pl.kernel, out_type=jax.ShapeDtypeStruct((n_out, width), src.dtype), mesh=mesh,
        scratch_types=[pltpu.VMEM((per_worker,), I32), pltpu.VMEM((2, win, width), src.dtype),
                       pltpu.SemaphoreType.DMA, pltpu.SemaphoreType.DMA])
    def gather_kernel(src_hbm, idx_hbm, dst_hbm, idx_v, rows_v, sem0, sem1):
        worker = lax.axis_index("subcore") * mesh.num_cores + lax.axis_index("core")
        base = worker * per_worker
        sems = (sem0, sem1)
        pltpu.sync_copy(idx_hbm.at[pl.ds(base, per_worker)], idx_v)

        def gather(step, slot):
            return pltpu.make_async_copy(src_hbm.at[idx_v.at[pl.ds(step * win, win)]], rows_v.at[slot], sems[slot])

        gather(0, 0).start()

        @pl.loop(0, steps, step=2)
        def _(s):
            for slot in range(2):
                step = s + slot
                gather(step, slot).wait()

                @pl.when(step + 1 < steps)
                def _():
                    gather(step + 1, 1 - slot).start()

                pltpu.sync_copy(rows_v.at[slot], dst_hbm.at[pl.ds(base + step * win, win)])

    return gather_kernel(src, idx)


def _expert_kernel(blk_exp_ref, n_used_ref, x_ref, wgu_ref, bgu_ref, wd_ref, bd_ref, o_ref, wgu_bf, wd_bf):
    i = pl.program_id(0)

    @pl.when(jnp.logical_or(i == 0, blk_exp_ref[i] != blk_exp_ref[jnp.maximum(i - 1, 0)]))
    def _():
        wgu_bf[...] = wgu_ref[0, 0].astype(BF16)
        wd_bf[...] = wd_ref[0, 0].astype(BF16)

    @pl.when(i < n_used_ref[0])
    def _():
        x_lo, x_hi = _unpack_bf16_pairs(x_ref[...])
        hgu = (_dot(x_lo.astype(BF16), wgu_bf[:HALF_D, :]) + _dot(x_hi.astype(BF16), wgu_bf[HALF_D:, :])
               + bgu_ref[0])
        gate = jnp.minimum(hgu[:, :D_EXPERT], SWIGLU_LIMIT)
        up = jnp.clip(hgu[:, D_EXPERT:], -SWIGLU_LIMIT, SWIGLU_LIMIT)
        act = gate * jax.nn.sigmoid(gate * SWIGLU_ALPHA) * (up + 1.0)
        o_ref[...] = _pack_bf16_pairs(_dot(act.astype(BF16), wd_bf[...]) + bd_ref[0])

    @pl.when(i >= n_used_ref[0])
    def _():
        o_ref[...] = jnp.zeros_like(o_ref)


def _expert_mlp(xs, block_exp, n_used, layer, w_gu, b_gu, w_down, b_down):
    n_rows = xs.shape[0]
    bm = MOE_BLOCK_ROWS
    row = lambda i, be, nu: (i, 0)
    exp3 = lambda i, be, nu: (be[i], 0, 0)
    exp4 = lambda i, be, nu: (layer, be[i], 0, 0)
    grid_spec = pltpu.PrefetchScalarGridSpec(
        num_scalar_prefetch=2,
        grid=(n_rows // bm,),
        in_specs=[pl.BlockSpec((bm, HALF_D), row),
                  pl.BlockSpec((1, 1, D_MODEL, 2 * D_EXPERT), exp4), pl.BlockSpec((1, 1, 2 * D_EXPERT), exp3),
                  pl.BlockSpec((1, 1, D_EXPERT, D_MODEL), exp4), pl.BlockSpec((1, 1, D_MODEL), exp3)],
        out_specs=pl.BlockSpec((bm, HALF_D), row),
        scratch_shapes=[pltpu.VMEM((D_MODEL, 2 * D_EXPERT), BF16), pltpu.VMEM((D_EXPERT, D_MODEL), BF16)],
    )
    return pl.pallas_call(
        _expert_kernel,
        grid_spec=grid_spec,
        out_shape=jax.ShapeDtypeStruct((n_rows, HALF_D), I32),
        compiler_params=_cparams("arbitrary"),
        name="moe_experts",
    )(block_exp, n_used, xs, w_gu, b_gu.reshape(N_EXPERTS, 1, 2 * D_EXPERT),
      w_down, b_down.reshape(N_EXPERTS, 1, D_MODEL))


def _combine_kernel(y0_ref, y1_ref, y2_ref, y3_ref, gate_ref, res_ref, g_ref, b_ref, o_ref):
    gates = gate_ref[...]
    acc_lo, acc_hi = None, None
    for k, y_ref in enumerate((y0_ref, y1_ref, y2_ref, y3_ref)):
        lo, hi = _unpack_bf16_pairs(y_ref[...])
        gk = gates[:, k:k + 1]
        acc_lo = lo * gk if acc_lo is None else acc_lo + lo * gk
        acc_hi = hi * gk if acc_hi is None else acc_hi + hi * gk
    acc = jnp.concatenate([acc_lo, acc_hi], axis=1)
    o_ref[...] = _layer_norm_rows(DN_ALPHA * res_ref[...] + acc, g_ref[...], b_ref[...])


def _moe_block(h2d, h_packed, layer, w_router, b_router, w_gu, b_gu, w_down, b_down, g, b):
    n = h2d.shape[0]
    n_slots = n * TOP_K
    bm = MOE_BLOCK_ROWS
    top_idx, gates, rank, totals = _router(h2d, w_router, b_router)

    e_flat = top_idx.reshape(-1)
    order = jnp.argsort(e_flat).astype(I32)
    counts = totals[0, :N_EXPERTS].astype(I32)
    padded = (counts + bm - 1) // bm * bm
    start = jnp.cumsum(counts) - counts
    ends_p = jnp.cumsum(padded)
    pstart = ends_p - padded
    n_rows = n_slots + N_EXPERTS * bm
    n_blocks = n_rows // bm
    r = jnp.arange(n_rows, dtype=I32)
    e_r = jnp.minimum(jnp.searchsorted(ends_p, r, side="right"), N_EXPERTS - 1).astype(I32)
    j = r - pstart[e_r]
    valid = j < counts[e_r]
    slot_of_row = order[jnp.where(valid, start[e_r] + j, 0)]
    rows_tok = jnp.where(valid, slot_of_row // TOP_K, r % n).astype(I32)
    slot_pos = pstart[e_flat] + rank.reshape(-1)
    block_exp = e_r[::bm]
    n_used = (ends_p[-1] // bm).astype(I32).reshape(1)

    xs = _gather_rows(h_packed, rows_tok)
    ys = _expert_mlp(xs, block_exp, n_used, layer, w_gu, b_gu, w_down, b_down)
    yk = _gather_rows(ys, slot_pos.reshape(n, TOP_K).T.reshape(-1))

    tm = ROW_TILE
    row = lambda i: (i, 0)
    const = lambda i: (0, 0)
    choice = lambda k: (lambda i: (k * (n // tm) + i, 0))
    return pl.pallas_call(
        _combine_kernel,
        grid=(n // tm,),
        in_specs=[pl.BlockSpec((tm, HALF_D), choice(k)) for k in range(TOP_K)] + [
                  pl.BlockSpec((tm, TOP_K), row),
                  pl.BlockSpec((tm, D_MODEL), row), pl.BlockSpec((1, D_MODEL), const),
                  pl.BlockSpec((1, D_MODEL), const)],
        out_specs=pl.BlockSpec((tm, D_MODEL), row),
        out_shape=jax.ShapeDtypeStruct((n, D_MODEL), F32),
        compiler_params=_cparams("parallel"),
        name="moe_combine",
    )(yk, yk, yk, yk, gates, h2d, g.reshape(1, D_MODEL), b.reshape(1, D_MODEL))


def _s5_kernel(u_ref, bre_ref, bim_ref, cre_ref, cim_ref, are_ref, aim_ref, d_ref, y_ref,
               bu_re, bu_im, st_re, st_im, h_re, h_im, *, bsz):
    @pl.when(pl.program_id(0) == 0)
    def _():
        h_re[...] = jnp.zeros_like(h_re)
        h_im[...] = jnp.zeros_like(h_im)

    rows = u_ref.shape[0]
    first = lax.broadcasted_iota(I32, (SUBLANES, S5_ST_BLK), 0) < bsz
    for j in range(S5_LANE_BLOCKS):
        cin = slice(j * S5_IN_BLK, (j + 1) * S5_IN_BLK)
        cst = slice(j * S5_ST_BLK, (j + 1) * S5_ST_BLK)
        uj = u_ref[:, cin]
        ujb = uj.astype(BF16)
        bu_re[...] = _dot(ujb, bre_ref[j])
        bu_im[...] = _dot(ujb, bim_ref[j])
        ar = jnp.broadcast_to(are_ref[:, cst], (SUBLANES, S5_ST_BLK))
        ai = jnp.broadcast_to(aim_ref[:, cst], (SUBLANES, S5_ST_BLK))

        def step(i, carry):
            hr, hi = carry
            r0 = pl.multiple_of(i * SUBLANES, SUBLANES)
            vr = bu_re[pl.ds(r0, SUBLANES), :]
            vi = bu_im[pl.ds(r0, SUBLANES), :]
            h1r = ar * hr - ai * hi + vr
            h1i = ar * hi + ai * hr + vi
            h1rs = pltpu.roll(h1r, bsz, 0)
            h1is = pltpu.roll(h1i, bsz, 0)
            h2r = ar * h1rs - ai * h1is + vr
            h2i = ar * h1is + ai * h1rs + vi
            st_re[pl.ds(r0, SUBLANES), :] = jnp.where(first, h1r, h2r)
            st_im[pl.ds(r0, SUBLANES), :] = jnp.where(first, h1i, h2i)
            return pltpu.roll(h2r, bsz, 0), pltpu.roll(h2i, bsz, 0)

        hr, hi = lax.fori_loop(0, rows // SUBLANES, step, (h_re[:, cst], h_im[:, cst]))
        h_re[:, cst] = hr
        h_im[:, cst] = hi
        yj = _dot(st_re[...].astype(BF16), cre_ref[j]) + _dot(st_im[...].astype(BF16), cim_ref[j])
        yj = yj + d_ref[:, cin] * uj
        y_ref[:, cin] = jax.nn.gelu(yj).astype(BF16)


def _s5_block_diag(w, n_in, n_out):
    gpb = SSM_GROUPS // S5_LANE_BLOCKS
    w4 = w.reshape(S5_LANE_BLOCKS, gpb, n_in, n_out)
    eye = jnp.eye(gpb, dtype=w.dtype)
    return jnp.einsum("jgio,gh->jgiho", w4, eye).reshape(S5_LANE_BLOCKS, gpb * n_in, gpb * n_out)


def _s5_mixer_block(h2d, bsz, seq, w_in, log_dt, lam_re, lam_im, b_re, b_im, c_re, c_im, d, w_out, g, b):
    assert 2 * bsz == SUBLANES, "the scan packs two time steps of bsz rows into one 8-row tile"
    tm = ROW_TILE
    per_seq = seq // tm
    u_t = _matmul(h2d, w_in.astype(BF16), tm=tm, out_dtype=F32, grid=(bsz, per_seq),
                  x_map=lambda bb, i: (bb * per_seq + i, 0), out_map=lambda bb, i: (i, bb),
                  out_shape=(seq, bsz * D_MODEL)).reshape(seq * bsz, D_MODEL)

    dt = jnp.exp(log_dt)[:, None]
    mag = jnp.exp(lam_re * dt)
    a_re, a_im = mag * jnp.cos(lam_im * dt), mag * jnp.sin(lam_im * dt)
    den = lam_re * lam_re + lam_im * lam_im
    coef_re = ((a_re - 1.0) * lam_re + a_im * lam_im) / den
    coef_im = (a_im * lam_re - (a_re - 1.0) * lam_im) / den
    bb_re = coef_re[..., None] * b_re - coef_im[..., None] * b_im
    bb_im = coef_re[..., None] * b_im + coef_im[..., None] * b_re
    bre = _s5_block_diag(jnp.swapaxes(bb_re, 1, 2), SSM_GROUP, SSM_STATE).astype(BF16)
    bim = _s5_block_diag(jnp.swapaxes(bb_im, 1, 2), SSM_GROUP, SSM_STATE).astype(BF16)
    cre = _s5_block_diag(jnp.swapaxes(c_re, 1, 2), SSM_STATE, SSM_GROUP).astype(BF16)
    cim = _s5_block_diag(jnp.swapaxes(-c_im, 1, 2), SSM_STATE, SSM_GROUP).astype(BF16)
    n_state = SSM_GROUPS * SSM_STATE

    rows = S5_CHUNK * bsz
    row = lambda c: (c, 0)
    c2 = lambda c: (0, 0)
    c3 = lambda c: (0, 0, 0)
    y_t = pl.pallas_call(
        functools.partial(_s5_kernel, bsz=bsz),
        grid=(seq // S5_CHUNK,),
        in_specs=[pl.BlockSpec((rows, D_MODEL), row),
                  pl.BlockSpec(bre.shape, c3), pl.BlockSpec(bim.shape, c3),
                  pl.BlockSpec(cre.shape, c3), pl.BlockSpec(cim.shape, c3),
                  pl.BlockSpec((1, n_state), c2), pl.BlockSpec((1, n_state), c2), pl.BlockSpec((1, D_MODEL), c2)],
        out_specs=pl.BlockSpec((rows, D_MODEL), row),
        out_shape=jax.ShapeDtypeStruct((seq * bsz, D_MODEL), BF16),
        scratch_shapes=[pltpu.VMEM((rows, S5_ST_BLK), F32)] * 4 + [pltpu.VMEM((SUBLANES, n_state), F32)] * 2,
        compiler_params=_cparams("arbitrary"),
        name="s5_scan",
    )(u_t, bre, bim, cre, cim, a_re.reshape(1, n_state), a_im.reshape(1, n_state), d.reshape(1, D_MODEL))

    y2 = y_t.reshape(seq, bsz * D_MODEL)
    return _linear_residual_ln(
        [y2], [w_out.astype(BF16)], h2d, g, b, tm=tm, glu=True, grid=(bsz, per_seq),
        x_maps=[lambda bb, i: (i, bb)], res_map=lambda bb, i: (bb * per_seq + i, 0))


def _even_mixer_block(h2d, bsz, seq, w_in, qnorm_g, w_uq, w_uq_idx, kidx_g, kidx_b, w_out, g, b):
    qa, qi, ka, va, ki, wi, qb, kb, vb = _even_proj(h2d, bsz, seq, w_in, qnorm_g, w_uq, w_uq_idx, kidx_g, kidx_b)
    o_a = _dsa_attention(qa, qi, wi, ki, ka, va, bsz, seq)
    o_b = _stick_breaking(qb, kb, vb, bsz, seq)
    w_out = w_out.astype(BF16)
    return _linear_residual_ln([o_a, o_b], [w_out[:A_WIDTH], w_out[A_WIDTH:]], h2d, g, b, tm=ROW_TILE)


def kernel(x, mem, ev_w_in, ev_qnorm_g, ev_w_uq, ev_w_uq_idx, ev_kidx_ln_g, ev_kidx_ln_b, ev_w_out, od_w_in, od_log_dt, od_lambda_re, od_lambda_im, od_b_re, od_b_im, od_c_re, od_c_im, od_d, od_w_out, mix_ln_g, mix_ln_b, xa_w_q, xa_w_kv, xa_w_o, xa_ln_g, xa_ln_b, moe_w_router, moe_b_router, moe_w_gu, moe_b_gu, moe_w_down, moe_b_down, ffn_ln_g, ffn_ln_b):
    bsz, seq, _ = x.shape
    h = x.reshape(bsz * seq, D_MODEL)
    for layer in range(DEPTH):
        j = layer // 2
        if layer % 2 == 0:
            h = _even_mixer_block(h, bsz, seq, ev_w_in[j], ev_qnorm_g[j], ev_w_uq[j], ev_w_uq_idx[j],
                                  ev_kidx_ln_g[j], ev_kidx_ln_b[j], ev_w_out[j], mix_ln_g[layer], mix_ln_b[layer])
        else:
            h = _s5_mixer_block(h, bsz, seq, od_w_in[j], od_log_dt[j], od_lambda_re[j], od_lambda_im[j],
                                od_b_re[j], od_b_im[j], od_c_re[j], od_c_im[j], od_d[j], od_w_out[j],
                                mix_ln_g[layer], mix_ln_b[layer])
        h, h_packed = _cross_attention_block(h, mem, bsz, seq, xa_w_q[layer], xa_w_kv[layer], xa_w_o[layer],
                                             xa_ln_g[layer], xa_ln_b[layer])
        h = _moe_block(h, h_packed, layer, moe_w_router[layer], moe_b_router[layer], moe_w_gu, moe_b_gu[layer],
                       moe_w_down, moe_b_down[layer], ffn_ln_g[layer], ffn_ln_b[layer])
    return h.reshape(bsz, seq, D_MODEL)
```

```python
import functools
import math

import jax
import jax.numpy as jnp
from jax import lax
from jax.experimental import pallas as pl
from jax.experimental.pallas import tpu as pltpu
from jax.experimental.pallas import tpu_sc as plsc

F32 = jnp.float32
BF16 = jnp.bfloat16
I32 = jnp.int32

D_MODEL = 1024
DEPTH = 2
HEAD_DIM = 64
A_HEADS = 8
A_KV_HEADS = 2
A_REP = A_HEADS // A_KV_HEADS
Q_RANK = 256
IDX_HEADS = 8
IDX_DIM = 64
IDX_TOPK = 256
B_HEADS = 8
A_WIDTH = A_HEADS * HEAD_DIM
B_WIDTH = B_HEADS * HEAD_DIM
SSM_GROUP = 16
SSM_GROUPS = D_MODEL // SSM_GROUP
SSM_STATE = 64
XA_HEADS = 4
XA_HEAD_DIM = D_MODEL // XA_HEADS
N_EXPERTS = 32
TOP_K = 4
D_EXPERT = D_MODEL
SWIGLU_LIMIT = 7.0
SWIGLU_ALPHA = 1.702
ROPE_THETA = 500000.0
ROPE_HALF = HEAD_DIM // 8
LN_EPS = 1e-5
DN_ALPHA = (2 * DEPTH) ** 0.25

LANES = 128
SUBLANES = 8
VMEM_LIMIT_BYTES = 56 * 1024 * 1024

Q_BLOCK = 256
DSA_KEY_TILE = 512
DSA_ATT_TILE = 1024
DSA_ATT_PAR = 1
DSA_COUNT_ROWS = 8 * SUBLANES
SB_KEY_TILE = 256
ROW_TILE = 512
MOE_BLOCK_ROWS = 512
SC_GATHER_SLOT_BYTES = 128 * 1024
S5_CHUNK = 256
S5_LANE_BLOCKS = 4
S5_IN_BLK = D_MODEL // S5_LANE_BLOCKS
S5_ST_BLK = SSM_GROUPS * SSM_STATE // S5_LANE_BLOCKS

SB_EXIT_LOG = -104.0
NEG_BIG = -1e30
INT_MIN = -(2 ** 31)


def _cparams(*sem):
    return pltpu.CompilerParams(dimension_semantics=sem, vmem_limit_bytes=VMEM_LIMIT_BYTES)


def _dot(a, b):
    return jnp.dot(a, b, preferred_element_type=F32)


def _dot_nt(a, b):
    return lax.dot_general(a, b, (((1,), (1,)), ((), ())), preferred_element_type=F32)


def _layer_norm_rows(y, g, b):
    mu = jnp.mean(y, axis=-1, keepdims=True)
    d = y - mu
    var = jnp.mean(d * d, axis=-1, keepdims=True)
    return d * lax.rsqrt(var + LN_EPS) * g + b


def _mm_kernel(x_ref, w_ref, o_ref):
    o_ref[...] = _dot(x_ref[...].astype(BF16), w_ref[...]).astype(o_ref.dtype)


def _matmul(x, w, *, tm, out_dtype, x_map=None, out_map=None, grid=None, out_shape=None):
    m, k = x.shape
    n = w.shape[1]
    grid = grid or (m // tm,)
    x_map = x_map or (lambda i: (i, 0))
    out_map = out_map or (lambda i: (i, 0))
    out_shape = out_shape or (m, n)
    return pl.pallas_call(
        _mm_kernel,
        grid=grid,
        in_specs=[pl.BlockSpec((tm, k), x_map), pl.BlockSpec((k, n), lambda *a: (0, 0))],
        out_specs=pl.BlockSpec((tm, n), out_map),
        out_shape=jax.ShapeDtypeStruct(out_shape, out_dtype),
        compiler_params=_cparams(*(("parallel",) * len(grid))),
        name="matmul",
    )(x, w)


def _lin_ln_kernel(*refs, n_in, glu):
    xs, ws = refs[:n_in], refs[n_in:2 * n_in]
    res_ref, g_ref, b_ref, o_ref = refs[2 * n_in:]
    acc = _dot(xs[0][...].astype(BF16), ws[0][...])
    for x_ref, w_ref in zip(xs[1:], ws[1:]):
        acc = acc + _dot(x_ref[...].astype(BF16), w_ref[...])
    if glu:
        acc = acc[:, :D_MODEL] * jax.nn.sigmoid(acc[:, D_MODEL:])
    y = DN_ALPHA * res_ref[...] + acc
    o_ref[...] = _layer_norm_rows(y, g_ref[...], b_ref[...])


def _linear_residual_ln(xs, ws, res, g, b, *, tm, glu=False, grid=None, x_maps=None, res_map=None):
    n_rows = res.shape[0]
    grid = grid or (n_rows // tm,)
    x_maps = x_maps or [lambda i: (i, 0)] * len(xs)
    res_map = res_map or (lambda i: (i, 0))
    const = lambda *a: (0, 0)
    in_specs = [pl.BlockSpec((tm, w.shape[0]), m) for w, m in zip(ws, x_maps)]
    in_specs += [pl.BlockSpec(w.shape, const) for w in ws]
    in_specs += [pl.BlockSpec((tm, D_MODEL), res_map), pl.BlockSpec((1, D_MODEL), const),
                 pl.BlockSpec((1, D_MODEL), const)]
    return pl.pallas_call(
        functools.partial(_lin_ln_kernel, n_in=len(xs), glu=glu),
        grid=grid,
        in_specs=in_specs,
        out_specs=pl.BlockSpec((tm, D_MODEL), res_map),
        out_shape=jax.ShapeDtypeStruct((n_rows, D_MODEL), F32),
        compiler_params=_cparams(*(("parallel",) * len(grid))),
        name="linear_residual_ln",
    )(*xs, *ws, res, g.reshape(1, D_MODEL), b.reshape(1, D_MODEL))


_EV_CQ, _EV_KA, _EV_VA, _EV_KI, _EV_QB = 0, 256, 384, 512, 640
_EV_KB = _EV_QB + B_WIDTH
_EV_VB = _EV_KB + B_WIDTH
_EV_COLS = _EV_VB + B_WIDTH


def _rope_tables(seq):
    inv = ROPE_THETA ** (-jnp.arange(ROPE_HALF, dtype=F32) / ROPE_HALF)
    ang = jnp.arange(seq, dtype=F32)[:, None] * inv[None, :]
    cos, sin = jnp.cos(ang), jnp.sin(ang)
    rest = HEAD_DIM - 2 * ROPE_HALF
    zh = jnp.zeros((seq, ROPE_HALF), F32)
    c = jnp.concatenate([cos, cos, jnp.ones((seq, rest), F32)], axis=1)
    s1 = jnp.concatenate([-sin, zh, jnp.zeros((seq, rest), F32)], axis=1)
    s2 = jnp.concatenate([zh, sin, jnp.zeros((seq, rest), F32)], axis=1)
    rep = LANES // HEAD_DIM
    return jnp.tile(c, (1, rep)), jnp.tile(s1, (1, rep)), jnp.tile(s2, (1, rep))


def _even_proj_kernel(x_ref, w_ref, qg_ref, wuq_ref, wuqi_ref, lg_ref, lb_ref, c_ref, s1_ref, s2_ref,
                      qa_ref, qi_ref, ka_ref, va_ref, ki_ref, wi_ref, qb_ref, kb_ref, vb_ref):
    p = _dot(x_ref[...].astype(BF16), w_ref[...])
    c, s1, s2 = c_ref[...], s1_ref[...], s2_ref[...]

    def rope(t):
        return (t * c + pltpu.roll(t, LANES - ROPE_HALF, 1) * s1 + pltpu.roll(t, ROPE_HALF, 1) * s2)

    cq = p[:, _EV_CQ:_EV_CQ + Q_RANK]
    cn = cq * lax.rsqrt(jnp.mean(cq * cq, axis=-1, keepdims=True) + LN_EPS) * qg_ref[...]
    cnb = cn.astype(BF16)
    qa = _dot(cnb, wuq_ref[...])
    qi = _dot(cnb, wuqi_ref[...])
    low = lax.broadcasted_iota(I32, c.shape, 1) < HEAD_DIM
    for j in range(A_WIDTH // LANES):
        sl = slice(j * LANES, (j + 1) * LANES)
        pair = rope(qa[:, sl]) * (HEAD_DIM ** -0.5)
        for e, src in enumerate((pair, pltpu.roll(pair, HEAD_DIM, 1))):
            h = 2 * j + e
            qa_ref[:, h * LANES:(h + 1) * LANES] = jnp.where(low, src, 0.0).astype(BF16)
        qi_ref[:, sl] = (rope(qi[:, sl]) * (IDX_DIM ** -0.5)).astype(BF16)
    kpair = rope(p[:, _EV_KA:_EV_KA + LANES])
    vpair = p[:, _EV_VA:_EV_VA + LANES]
    v_pad = jnp.where(lax.broadcasted_iota(I32, c.shape, 1) == HEAD_DIM, 1.0, 0.0)
    for g, (ks, vs) in enumerate(((kpair, vpair), (pltpu.roll(kpair, HEAD_DIM, 1), pltpu.roll(vpair, HEAD_DIM, 1)))):
        ka_ref[:, g * LANES:(g + 1) * LANES] = jnp.where(low, ks, 0.0).astype(BF16)
        va_ref[:, g * LANES:(g + 1) * LANES] = jnp.where(low, vs, v_pad).astype(BF16)

    t = p[:, _EV_KI:_EV_KI + LANES]
    lane = lax.broadcasted_iota(I32, t.shape, 1)
    is_k = lane < IDX_DIM
    mu = jnp.sum(jnp.where(is_k, t, 0.0), axis=-1, keepdims=True) * (1.0 / IDX_DIM)
    d = jnp.where(is_k, t - mu, 0.0)
    var = jnp.sum(d * d, axis=-1, keepdims=True) * (1.0 / IDX_DIM)
    kin = d * lax.rsqrt(var + LN_EPS) * lg_ref[...] + lb_ref[...]
    ki_ref[...] = rope(kin)[:, :IDX_DIM].astype(BF16)
    wi_ref[...] = t[:, IDX_DIM:IDX_DIM + IDX_HEADS] * (IDX_HEADS ** -0.5)

    qb_ref[...] = (p[:, _EV_QB:_EV_KB] * (HEAD_DIM ** -0.5)).astype(BF16)
    kb_ref[...] = p[:, _EV_KB:_EV_VB].astype(BF16)
    vb_ref[...] = p[:, _EV_VB:_EV_COLS].astype(BF16)


def _even_proj(x2d, bsz, seq, w_in, qnorm_g, w_uq, w_uq_idx, kidx_g, kidx_b):
    n = x2d.shape[0]
    tm = ROW_TILE
    per_seq = seq // tm
    c0 = Q_RANK + 2 * A_KV_HEADS * HEAD_DIM + IDX_DIM + IDX_HEADS
    w_pack = jnp.concatenate(
        [w_in[:, :c0], jnp.zeros((D_MODEL, _EV_QB - c0), w_in.dtype), w_in[:, c0:]], axis=1).astype(BF16)
    pad = LANES - IDX_DIM
    lg = jnp.concatenate([kidx_g, jnp.zeros((pad,), F32)]).reshape(1, LANES)
    lb = jnp.concatenate([kidx_b, jnp.zeros((pad,), F32)]).reshape(1, LANES)
    c, s1, s2 = _rope_tables(seq)
    row = lambda i: (i, 0)
    const = lambda i: (0, 0)
    pos = lambda i: (i % per_seq, 0)
    head_shape = jax.ShapeDtypeStruct((n, B_WIDTH), BF16)
    head_spec = pl.BlockSpec((tm, B_WIDTH), row)
    return pl.pallas_call(
        _even_proj_kernel,
        grid=(n // tm,),
        in_specs=[pl.BlockSpec((tm, D_MODEL), row), pl.BlockSpec((D_MODEL, _EV_COLS), const),
                  pl.BlockSpec((1, Q_RANK), const), pl.BlockSpec((Q_RANK, A_WIDTH), const),
                  pl.BlockSpec((Q_RANK, IDX_HEADS * IDX_DIM), const),
                  pl.BlockSpec((1, LANES), const), pl.BlockSpec((1, LANES), const),
                  pl.BlockSpec((tm, LANES), pos), pl.BlockSpec((tm, LANES), pos), pl.BlockSpec((tm, LANES), pos)],
        out_specs=[pl.BlockSpec((tm, A_HEADS * LANES), row), pl.BlockSpec((tm, IDX_HEADS * IDX_DIM), row),
                   pl.BlockSpec((tm, A_KV_HEADS * LANES), row), pl.BlockSpec((tm, A_KV_HEADS * LANES), row),
                   pl.BlockSpec((tm, IDX_DIM), row), pl.BlockSpec((tm, IDX_HEADS), row),
                   head_spec, head_spec, head_spec],
        out_shape=[jax.ShapeDtypeStruct((n, A_HEADS * LANES), BF16), jax.ShapeDtypeStruct((n, IDX_HEADS * IDX_DIM), BF16),
                   jax.ShapeDtypeStruct((n, A_KV_HEADS * LANES), BF16),
                   jax.ShapeDtypeStruct((n, A_KV_HEADS * LANES), BF16),
                   jax.ShapeDtypeStruct((n, IDX_DIM), BF16), jax.ShapeDtypeStruct((n, IDX_HEADS), F32),
                   head_shape, head_shape, head_shape],
        compiler_params=_cparams("parallel"),
        name="even_proj",
    )(x2d, w_pack, qnorm_g.reshape(1, Q_RANK), w_uq.astype(BF16), w_uq_idx.astype(BF16), lg, lb, c, s1, s2)


def _key_to_float(key):
    bits = key ^ ((key >> 31) & jnp.int32(0x7FFFFFFF))
    return lax.bitcast_convert_type(bits, F32)


def _high_half(x):
    bits = lax.bitcast_convert_type(x, jnp.uint32) & jnp.uint32(0xFFFF0000)
    return lax.bitcast_convert_type(bits, F32).astype(BF16)


def _dsa_kernel(qa_ref, qi_ref, wit_ref, ki_ref, ka_ref, vat_ref, o_ref, sc_scr, hi_scr, *, topk, ts, ta):
    seq = sc_scr.shape[0]
    qb = pl.program_id(1)
    q0 = qb * Q_BLOCK
    nkt = (q0 + Q_BLOCK - 1) // ts + 1
    t_row = q0 + lax.broadcasted_iota(I32, (1, Q_BLOCK), 1)
    key = lax.broadcasted_iota(I32, (ts, Q_BLOCK), 0)
    kf = jnp.float32(topk)

    qi = qi_ref[0]
    qs = jnp.concatenate([qi[:, h * IDX_DIM:(h + 1) * IDX_DIM] for h in range(IDX_HEADS)], axis=0)
    wit = wit_ref[0]

    def score_tile(kt, carry):
        off = pl.multiple_of(kt * ts, ts)
        s_all = _dot_nt(ki_ref[0, pl.ds(off, ts), :], qs)
        acc = jnp.zeros((ts, Q_BLOCK), F32)
        for h in range(IDX_HEADS):
            acc = acc + jnp.maximum(s_all[:, h * Q_BLOCK:(h + 1) * Q_BLOCK], 0.0) * wit[h:h + 1, :]
        val = jnp.where(off + key <= t_row, acc, -jnp.inf)
        sc_scr[pl.ds(off, ts), :] = val
        hi_scr[pl.ds(off, ts), :] = _high_half(val)
        return carry

    lax.fori_loop(0, nkt, score_tile, 0)

    def count(pred):
        def body(kt, acc):
            off = pl.multiple_of(kt * ts, ts)
            ind = pred(sc_scr[pl.ds(off, ts), :], off + key)
            return acc + jnp.sum(ind.reshape(ts // DSA_COUNT_ROWS, DSA_COUNT_ROWS, Q_BLOCK), axis=0)
        acc = lax.fori_loop(0, nkt, body, jnp.zeros((DSA_COUNT_ROWS, Q_BLOCK), F32))
        return jnp.sum(acc, axis=0, keepdims=True)

    one_h, zero_h = jnp.ones((), BF16), jnp.zeros((), BF16)

    def count_high(c_hi):
        def body(kt, acc):
            off = pl.multiple_of(kt * ts, ts)
            ind = jnp.where(hi_scr[pl.ds(off, ts), :] >= c_hi, one_h, zero_h)
            part = ind[:DSA_COUNT_ROWS]
            for j in range(1, ts // DSA_COUNT_ROWS):
                part = part + ind[j * DSA_COUNT_ROWS:(j + 1) * DSA_COUNT_ROWS]
            return acc + part.astype(F32)
        acc = lax.fori_loop(0, nkt, body, jnp.zeros((DSA_COUNT_ROWS, Q_BLOCK), F32))
        return jnp.sum(acc, axis=0, keepdims=True)

    def high_step(i, base):
        cand = base + jnp.left_shift(jnp.int32(1), 31 - i)
        cnt = count_high(_high_half(_key_to_float(cand)))
        return jnp.where(cnt >= kf, cand, base)

    def bit_step(i, base):
        cand = base + jnp.left_shift(jnp.int32(1), 31 - i)
        cf = _key_to_float(cand)
        cnt = count(lambda sc, idx: jnp.where(sc >= cf, 1.0, 0.0))
        return jnp.where(cnt >= kf, cand, base)

    base = lax.fori_loop(0, 16, high_step, jnp.full((1, Q_BLOCK), INT_MIN, I32))
    base = lax.fori_loop(16, 32, bit_step, base)
    thr = jnp.where(base == INT_MIN, -jnp.inf, _key_to_float(base))

    cnt_ge = count(lambda sc, idx: jnp.where(sc >= thr, 1.0, 0.0))
    tied = jnp.logical_and(cnt_ge > kf, thr > -jnp.inf)
    any_tied = jnp.max(jnp.where(tied, 1.0, 0.0)) > 0.0
    seq_bits = max(1, int(math.ceil(math.log2(seq))))

    def tie_cut():
        cnt_gt = count(lambda sc, idx: jnp.where(sc > thr, 1.0, 0.0))
        need = kf - cnt_gt

        def idx_step(i, pos):
            cand = pos + jnp.left_shift(jnp.int32(1), seq_bits - 1 - i)
            cnt = count(lambda sc, idx: jnp.where(sc == thr, jnp.where(idx < cand, 1.0, 0.0), 0.0))
            return jnp.where(cnt < need, cand, pos)

        return lax.fori_loop(0, seq_bits, idx_step, jnp.zeros((1, Q_BLOCK), I32))

    cut = lax.cond(any_tied, tie_cut, lambda: jnp.full((1, Q_BLOCK), seq, I32))
    cut = jnp.where(tied, cut, seq)

    nkt_a = (q0 + Q_BLOCK - 1) // ta + 1
    key_a = lax.broadcasted_iota(I32, (ta, Q_BLOCK), 0)
    cols = A_REP * Q_BLOCK
    qg = [jnp.concatenate([qa_ref[0, :, (g * A_REP + r) * LANES:(g * A_REP + r + 1) * LANES]
                           for r in range(A_REP)], axis=0) for g in range(A_KV_HEADS)]

    def att_step(i, carry):
        offs = [pl.multiple_of((DSA_ATT_PAR * i + e) * ta, ta) for e in range(DSA_ATT_PAR)]
        logits = [_dot_nt(ka_ref[0, pl.ds(offs[e], ta), g * LANES:(g + 1) * LANES], qg[g])
                  for e in range(DSA_ATT_PAR) for g in range(A_KV_HEADS)]
        out = []
        for e in range(DSA_ATT_PAR):
            sc = sc_scr[pl.ds(offs[e], ta), :]
            idx = offs[e] + key_a
            keep = jnp.where(sc > thr, 0.0, jnp.where(sc == thr, jnp.where(idx <= cut, 0.0, NEG_BIG), NEG_BIG))
            bias = jnp.where(idx <= t_row, keep, NEG_BIG)
            bias = jnp.concatenate([bias] * A_REP, axis=1)
            for g in range(A_KV_HEADS):
                m, acc = carry[e * A_KV_HEADS + g]
                s = logits[e * A_KV_HEADS + g] + bias
                m_new = jnp.maximum(m, jnp.max(s, axis=0, keepdims=True))
                p = jnp.exp(s - m_new)
                vt = vat_ref[0, g * LANES:(g + 1) * LANES, pl.ds(offs[e], ta)]
                out.append((m_new, jnp.exp(m - m_new) * acc + _dot(vt, p.astype(BF16))))
        return tuple(out)

    init = tuple((jnp.full((1, cols), NEG_BIG, F32), jnp.zeros((LANES, cols), F32))
                 for _ in range(DSA_ATT_PAR * A_KV_HEADS))
    final = lax.fori_loop(0, (nkt_a + DSA_ATT_PAR - 1) // DSA_ATT_PAR, att_step, init)
    low = lax.broadcasted_iota(I32, (Q_BLOCK, LANES), 1) < HEAD_DIM
    outs = []
    for g in range(A_KV_HEADS):
        m, acc = final[g]
        for e in range(1, DSA_ATT_PAR):
            m1, acc1 = final[e * A_KV_HEADS + g]
            m_all = jnp.maximum(m, m1)
            m, acc = m_all, jnp.exp(m - m_all) * acc + jnp.exp(m1 - m_all) * acc1
        og = acc / acc[HEAD_DIM:HEAD_DIM + 1, :]
        outs += [og[:, r * Q_BLOCK:(r + 1) * Q_BLOCK].T for r in range(A_REP)]
    for j in range(A_HEADS // 2):
        pair = jnp.where(low, outs[2 * j], pltpu.roll(outs[2 * j + 1], HEAD_DIM, 1))
        o_ref[0, :, j * LANES:(j + 1) * LANES] = pair.astype(BF16)


def _dsa_attention(qa, qi, wi, ki, ka, va, bsz, seq):
    topk = min(IDX_TOPK, seq // 4)
    ts = min(DSA_KEY_TILE, seq)
    ta = min(DSA_ATT_TILE, seq // DSA_ATT_PAR)
    assert seq % (DSA_ATT_PAR * ta) == 0 and seq % ts == 0, "the attention loop walks whole groups of key tiles"
    blk = lambda b, i: (b, i, 0)
    full = lambda b, i: (b, 0, 0)
    r3 = lambda a: a.reshape(bsz, seq, a.shape[-1])
    wit = jnp.swapaxes(r3(wi), 1, 2)
    vat = jnp.swapaxes(r3(va), 1, 2)
    return pl.pallas_call(
        functools.partial(_dsa_kernel, topk=topk, ts=ts, ta=ta),
        grid=(bsz, seq // Q_BLOCK),
        in_specs=[pl.BlockSpec((1, Q_BLOCK, A_HEADS * LANES), blk),
                  pl.BlockSpec((1, Q_BLOCK, IDX_HEADS * IDX_DIM), blk),
                  pl.BlockSpec((1, IDX_HEADS, Q_BLOCK), lambda b, i: (b, 0, i)),
                  pl.BlockSpec((1, seq, IDX_DIM), full),
                  pl.BlockSpec((1, seq, A_KV_HEADS * LANES), full),
                  pl.BlockSpec((1, A_KV_HEADS * LANES, seq), full)],
        out_specs=pl.BlockSpec((1, Q_BLOCK, A_WIDTH), blk),
        out_shape=jax.ShapeDtypeStruct((bsz, seq, A_WIDTH), BF16),
        scratch_shapes=[pltpu.VMEM((seq, Q_BLOCK), F32), pltpu.VMEM((seq, Q_BLOCK), BF16)],
        compiler_params=_cparams("parallel", "parallel"),
        name="dsa_attention",
    )(r3(qa), r3(qi), wit, r3(ki), r3(ka), vat).reshape(bsz * seq, A_WIDTH)


def _sb_kernel(q_ref, k_ref, v_ref, u_ref, o_ref, acc_scr, run_scr, *, tk):
    q0 = pl.program_id(1) * Q_BLOCK
    t_col = q0 + lax.broadcasted_iota(I32, (Q_BLOCK, 1), 0)
    lane = lax.broadcasted_iota(I32, (Q_BLOCK, tk), 1)
    low = lax.broadcasted_iota(I32, (Q_BLOCK, LANES), 1) < HEAD_DIM
    upper = u_ref[...]
    nkt = (q0 + Q_BLOCK - 1) // tk + 1
    q = q_ref[0]
    zero = jnp.zeros((Q_BLOCK, LANES), BF16)
    qm = []
    for p in range(B_HEADS // 2):
        pair = q[:, p * LANES:(p + 1) * LANES]
        qm.append(jnp.concatenate([jnp.where(low, pair, zero), jnp.where(low, zero, pair)], axis=0))
    acc_scr[...] = jnp.zeros_like(acc_scr)
    run_scr[...] = jnp.zeros_like(run_scr)

    def cond(carry):
        i, worst = carry
        return jnp.logical_and(i < nkt, worst >= SB_EXIT_LOG)

    def body(carry):
        i, _ = carry
        off = pl.multiple_of((nkt - 1 - i) * tk, tk)
        strict = off + lane < t_col
        strict = jnp.concatenate([strict, strict], axis=0)
        worst = None
        for p in range(B_HEADS // 2):
            cols = slice(p * LANES, (p + 1) * LANES)
            kp = k_ref[0, pl.ds(off, tk), cols]
            vp = v_ref[0, pl.ds(off, tk), cols]
            run = run_scr[p]
            z = _dot_nt(qm[p], kp)
            softplus = jnp.maximum(z, 0.0) + jnp.log(1.0 + jnp.exp(-jnp.abs(z)))
            log_1mb = jnp.where(strict, -softplus, 0.0)
            hi = log_1mb.astype(BF16)
            lo = (log_1mb - hi.astype(F32)).astype(BF16)
            after = _dot(hi, upper) + _dot(lo, upper) + run
            a = jnp.where(strict, jnp.exp(z - softplus + after), 0.0)
            out = _dot(a.astype(BF16), vp)
            run = run + jnp.sum(log_1mb, axis=1, keepdims=True)
            run_scr[p] = run
            worst = run if worst is None else jnp.maximum(worst, run)
            acc_scr[:, cols] += jnp.where(low, out[:Q_BLOCK], out[Q_BLOCK:])
        return i + 1, jnp.max(worst)

    lax.while_loop(cond, body, (jnp.int32(0), jnp.float32(0.0)))
    o_ref[0] = acc_scr[...].astype(BF16)


def _stick_breaking(qb, kb, vb, bsz, seq):
    tk = min(SB_KEY_TILE, seq)
    r = lax.broadcasted_iota(I32, (tk, tk), 0)
    c = lax.broadcasted_iota(I32, (tk, tk), 1)
    upper = jnp.where(r > c, 1.0, 0.0).astype(BF16)
    blk = lambda b, i: (b, i, 0)
    full = lambda b, i: (b, 0, 0)
    r3 = lambda a: a.reshape(bsz, seq, B_WIDTH)
    return pl.pallas_call(
        functools.partial(_sb_kernel, tk=tk),
        grid=(bsz, seq // Q_BLOCK),
        in_specs=[pl.BlockSpec((1, Q_BLOCK, B_WIDTH), blk), pl.BlockSpec((1, seq, B_WIDTH), full),
                  pl.BlockSpec((1, seq, B_WIDTH), full), pl.BlockSpec((tk, tk), lambda b, i: (0, 0))],
        out_specs=pl.BlockSpec((1, Q_BLOCK, B_WIDTH), blk),
        out_shape=jax.ShapeDtypeStruct((bsz, seq, B_WIDTH), BF16),
        scratch_shapes=[pltpu.VMEM((Q_BLOCK, B_WIDTH), F32), pltpu.VMEM((B_HEADS // 2, 2 * Q_BLOCK, 1), F32)],
        compiler_params=_cparams("parallel", "arbitrary"),
        name="stick_breaking",
    )(r3(qb), r3(kb), r3(vb), upper).reshape(bsz * seq, B_WIDTH)


HALF_D = D_MODEL // 2
U32 = jnp.uint32
HIGH16 = 0xFFFF0000


def _pack_bf16_pairs(x):
    def bits(v):
        return lax.bitcast_convert_type(v.astype(BF16).astype(F32), U32)
    word = (bits(x[:, HALF_D:]) & U32(HIGH16)) | (bits(x[:, :HALF_D]) >> 16)
    return lax.bitcast_convert_type(word, I32)


def _unpack_bf16_pairs(word):
    u = lax.bitcast_convert_type(word, U32)
    return lax.bitcast_convert_type(u << 16, F32), lax.bitcast_convert_type(u & U32(HIGH16), F32)


def _xattn_kernel(h_ref, wq_ref, kv_ref, wo_ref, g_ref, b_ref, o_ref, packed_ref):
    h = h_ref[...]
    q = (_dot(h.astype(BF16), wq_ref[...]) * (XA_HEAD_DIM ** -0.5)).astype(BF16)
    kv = kv_ref[0]
    outs = []
    for hd in range(XA_HEADS):
        sl = slice(hd * XA_HEAD_DIM, (hd + 1) * XA_HEAD_DIM)
        s = _dot_nt(q[:, sl], kv[:, sl])
        p = jnp.exp(s - jnp.max(s, axis=1, keepdims=True))
        vh = kv[:, D_MODEL + hd * XA_HEAD_DIM:D_MODEL + (hd + 1) * XA_HEAD_DIM]
        outs.append((_dot(p.astype(BF16), vh) / jnp.sum(p, axis=1, keepdims=True)).astype(BF16))
    y = _dot(jnp.concatenate(outs, axis=1), wo_ref[...])
    out = _layer_norm_rows(DN_ALPHA * h + y, g_ref[...], b_ref[...])
    o_ref[...] = out
    packed_ref[...] = _pack_bf16_pairs(out)


def _cross_attention_block(h2d, mem, bsz, seq, w_q, w_kv, w_o, g, b):
    tm = ROW_TILE
    per_seq = seq // tm
    mem_len = mem.shape[1]
    kv = _matmul(mem.reshape(bsz * mem_len, D_MODEL), w_kv.astype(BF16), tm=mem_len, out_dtype=BF16)
    kv = kv.reshape(bsz, mem_len, 2 * D_MODEL)
    row = lambda i: (i, 0)
    const = lambda i: (0, 0)
    return pl.pallas_call(
        _xattn_kernel,
        grid=(bsz * per_seq,),
        in_specs=[pl.BlockSpec((tm, D_MODEL), row), pl.BlockSpec((D_MODEL, D_MODEL), const),
                  pl.BlockSpec((1, mem_len, 2 * D_MODEL), lambda i: (i // per_seq, 0, 0)),
                  pl.BlockSpec((D_MODEL, D_MODEL), const),
                  pl.BlockSpec((1, D_MODEL), const), pl.BlockSpec((1, D_MODEL), const)],
        out_specs=[pl.BlockSpec((tm, D_MODEL), row), pl.BlockSpec((tm, HALF_D), row)],
        out_shape=[jax.ShapeDtypeStruct(h2d.shape, F32), jax.ShapeDtypeStruct((h2d.shape[0], HALF_D), I32)],
        compiler_params=_cparams("parallel"),
        name="cross_attention",
    )(h2d, w_q.astype(BF16), kv, w_o.astype(BF16), g.reshape(1, D_MODEL), b.reshape(1, D_MODEL))


def _router_kernel(h_ref, w_ref, b_ref, tri_ref, idx_ref, gate_ref, rank_ref, cnt_ref, run_scr):
    @pl.when(pl.program_id(0) == 0)
    def _():
        run_scr[...] = jnp.zeros_like(run_scr)

    h = h_ref[...]
    hh = h.astype(BF16)
    hl = (h - hh.astype(F32)).astype(BF16)
    w = w_ref[...]
    wh = w.astype(BF16)
    wl = (w - wh.astype(F32)).astype(BF16)
    logits = _dot(hh, wh) + _dot(hl, wh) + _dot(hh, wl) + b_ref[...]
    lane = lax.broadcasted_iota(I32, logits.shape, 1).astype(F32)
    vals, sels = [], []
    onehot = jnp.zeros(logits.shape, F32)
    for k in range(TOP_K):
        m = jnp.max(logits, axis=1, keepdims=True)
        sel = jnp.min(jnp.where(logits == m, lane, float(LANES)), axis=1, keepdims=True)
        idx_ref[:, k:k + 1] = sel.astype(I32)
        vals.append(m)
        sels.append(sel)
        onehot = onehot + jnp.where(lane == sel, 1.0, 0.0)
        logits = jnp.where(lane == sel, -jnp.inf, logits)
    es = [jnp.exp(v - vals[0]) for v in vals]
    tot = es[0] + es[1] + es[2] + es[3]
    for k in range(TOP_K):
        gate_ref[:, k:k + 1] = es[k] / tot

    earlier = _dot(tri_ref[...], onehot.astype(BF16)) + run_scr[...]
    for k in range(TOP_K):
        rank = jnp.sum(jnp.where(lane == sels[k], earlier, 0.0), axis=1, keepdims=True)
        rank_ref[:, k:k + 1] = rank.astype(I32)
    run = run_scr[...] + jnp.sum(onehot, axis=0, keepdims=True)
    run_scr[...] = run
    cnt_ref[...] = run


def _router(h2d, w_router, b_router):
    n = h2d.shape[0]
    tm = 2 * ROW_TILE
    pad = LANES - N_EXPERTS
    w = jnp.concatenate([w_router, jnp.zeros((D_MODEL, pad), F32)], axis=1)
    b = jnp.concatenate([b_router, jnp.full((pad,), NEG_BIG, F32)]).reshape(1, LANES)
    r = lax.broadcasted_iota(I32, (tm, tm), 0)
    c = lax.broadcasted_iota(I32, (tm, tm), 1)
    tri = jnp.where(c < r, 1.0, 0.0).astype(BF16)
    row = lambda i: (i, 0)
    const = lambda i: (0, 0)
    return pl.pallas_call(
        _router_kernel,
        grid=(n // tm,),
        in_specs=[pl.BlockSpec((tm, D_MODEL), row), pl.BlockSpec((D_MODEL, LANES), const),
                  pl.BlockSpec((1, LANES), const), pl.BlockSpec((tm, tm), const)],
        out_specs=[pl.BlockSpec((tm, TOP_K), row), pl.BlockSpec((tm, TOP_K), row),
                   pl.BlockSpec((tm, TOP_K), row), pl.BlockSpec((1, LANES), const)],
        out_shape=[jax.ShapeDtypeStruct((n, TOP_K), I32), jax.ShapeDtypeStruct((n, TOP_K), F32),
                   jax.ShapeDtypeStruct((n, TOP_K), I32), jax.ShapeDtypeStruct((1, LANES), F32)],
        scratch_shapes=[pltpu.VMEM((1, LANES), F32)],
        compiler_params=_cparams("arbitrary"),
        name="moe_router",
    )(h2d, w, b, tri)


def _gather_rows(src, idx):
    n_out = idx.shape[0]
    width = src.shape[1]
    win = SC_GATHER_SLOT_BYTES // (width * src.dtype.itemsize)
    mesh = plsc.VectorSubcoreMesh(core_axis_name="core", subcore_axis_name="subcore")
    n_workers = mesh.num_cores * mesh.num_subcores
    per_worker = n_out // n_workers
    steps = per_worker // win
    assert per_worker * n_workers == n_out and steps * win == per_worker and steps % 2 == 0

    @functools.partial(
        pl.kernel, out_type=jax.ShapeDtypeStruct((n_out, width), src.dtype), mesh=mesh,
        scratch_types=[pltpu.VMEM((per_worker,), I32), pltpu.VMEM((2, win, width), src.dtype),
                       pltpu.SemaphoreType.DMA, pltpu.SemaphoreType.DMA])
    def gather_kernel(src_hbm, idx_hbm, dst_hbm, idx_v, rows_v, sem0, sem1):
        worker = lax.axis_index("subcore") * mesh.num_cores + lax.axis_index("core")
        base = worker * per_worker
        sems = (sem0, sem1)
        pltpu.sync_copy(idx_hbm.at[pl.ds(base, per_worker)], idx_v)

        def gather(step, slot):
            return pltpu.make_async_copy(src_hbm.at[idx_v.at[pl.ds(step * win, win)]], rows_v.at[slot], sems[slot])

        gather(0, 0).start()

        @pl.loop(0, steps, step=2)
        def _(s):
            for slot in range(2):
                step = s + slot
                gather(step, slot).wait()

                @pl.when(step + 1 < steps)
                def _():
                    gather(step + 1, 1 - slot).start()

                pltpu.sync_copy(rows_v.at[slot], dst_hbm.at[pl.ds(base + step * win, win)])

    return gather_kernel(src, idx)


def _expert_kernel(blk_exp_ref, n_used_ref, x_ref, wgu_ref, bgu_ref, wd_ref, bd_ref, o_ref, wgu_bf, wd_bf):
    i = pl.program_id(0)

    @pl.when(jnp.logical_or(i == 0, blk_exp_ref[i] != blk_exp_ref[jnp.maximum(i - 1, 0)]))
    def _():
        wgu_bf[...] = wgu_ref[0, 0].astype(BF16)
        wd_bf[...] = wd_ref[0, 0].astype(BF16)

    @pl.when(i < n_used_ref[0])
    def _():
        x_lo, x_hi = _unpack_bf16_pairs(x_ref[...])
        hgu = (_dot(x_lo.astype(BF16), wgu_bf[:HALF_D, :]) + _dot(x_hi.astype(BF16), wgu_bf[HALF_D:, :])
               + bgu_ref[0])
        gate = jnp.minimum(hgu[:, :D_EXPERT], SWIGLU_LIMIT)
        up = jnp.clip(hgu[:, D_EXPERT:], -SWIGLU_LIMIT, SWIGLU_LIMIT)
        act = gate * jax.nn.sigmoid(gate * SWIGLU_ALPHA) * (up + 1.0)
        o_ref[...] = _pack_bf16_pairs(_dot(act.astype(BF16), wd_bf[...]) + bd_ref[0])

    @pl.when(i >= n_used_ref[0])
    def _():
        o_ref[...] = jnp.zeros_like(o_ref)


def _expert_mlp(xs, block_exp, n_used, layer, w_gu, b_gu, w_down, b_down):
    n_rows = xs.shape[0]
    bm = MOE_BLOCK_ROWS
    row = lambda i, be, nu: (i, 0)
    exp3 = lambda i, be, nu: (be[i], 0, 0)
    exp4 = lambda i, be, nu: (layer, be[i], 0, 0)
    grid_spec = pltpu.PrefetchScalarGridSpec(
        num_scalar_prefetch=2,
        grid=(n_rows // bm,),
        in_specs=[pl.BlockSpec((bm, HALF_D), row),
                  pl.BlockSpec((1, 1, D_MODEL, 2 * D_EXPERT), exp4), pl.BlockSpec((1, 1, 2 * D_EXPERT), exp3),
                  pl.BlockSpec((1, 1, D_EXPERT, D_MODEL), exp4), pl.BlockSpec((1, 1, D_MODEL), exp3)],
        out_specs=pl.BlockSpec((bm, HALF_D), row),
        scratch_shapes=[pltpu.VMEM((D_MODEL, 2 * D_EXPERT), BF16), pltpu.VMEM((D_EXPERT, D_MODEL), BF16)],
    )
    return pl.pallas_call(
        _expert_kernel,
        grid_spec=grid_spec,
        out_shape=jax.ShapeDtypeStruct((n_rows, HALF_D), I32),
        compiler_params=_cparams("arbitrary"),
        name="moe_experts",
    )(block_exp, n_used, xs, w_gu, b_gu.reshape(N_EXPERTS, 1, 2 * D_EXPERT),
      w_down, b_down.reshape(N_EXPERTS, 1, D_MODEL))


def _combine_kernel(y0_ref, y1_ref, y2_ref, y3_ref, gate_ref, res_ref, g_ref, b_ref, o_ref):
    gates = gate_ref[...]
    acc_lo, acc_hi = None, None
    for k, y_ref in enumerate((y0_ref, y1_ref, y2_ref, y3_ref)):
        lo, hi = _unpack_bf16_pairs(y_ref[...])
        gk = gates[:, k:k + 1]
        acc_lo = lo * gk if acc_lo is None else acc_lo + lo * gk
        acc_hi = hi * gk if acc_hi is None else acc_hi + hi * gk
    acc = jnp.concatenate([acc_lo, acc_hi], axis=1)
    o_ref[...] = _layer_norm_rows(DN_ALPHA * res_ref[...] + acc, g_ref[...], b_ref[...])


def _moe_block(h2d, h_packed, layer, w_router, b_router, w_gu, b_gu, w_down, b_down, g, b):
    n = h2d.shape[0]
    n_slots = n * TOP_K
    bm = MOE_BLOCK_ROWS
    top_idx, gates, rank, totals = _router(h2d, w_router, b_router)

    e_flat = top_idx.reshape(-1)
    order = jnp.argsort(e_flat).astype(I32)
    counts = totals[0, :N_EXPERTS].astype(I32)
    padded = (counts + bm - 1) // bm * bm
    start = jnp.cumsum(counts) - counts
    ends_p = jnp.cumsum(padded)
    pstart = ends_p - padded
    n_rows = n_slots + N_EXPERTS * bm
    n_blocks = n_rows // bm
    r = jnp.arange(n_rows, dtype=I32)
    e_r = jnp.minimum(jnp.searchsorted(ends_p, r, side="right"), N_EXPERTS - 1).astype(I32)
    j = r - pstart[e_r]
    valid = j < counts[e_r]
    slot_of_row = order[jnp.where(valid, start[e_r] + j, 0)]
    rows_tok = jnp.where(valid, slot_of_row // TOP_K, r % n).astype(I32)
    slot_pos = pstart[e_flat] + rank.reshape(-1)
    block_exp = e_r[::bm]
    n_used = (ends_p[-1] // bm).astype(I32).reshape(1)

    xs = _gather_rows(h_packed, rows_tok)
    ys = _expert_mlp(xs, block_exp, n_used, layer, w_gu, b_gu, w_down, b_down)
    yk = _gather_rows(ys, slot_pos.reshape(n, TOP_K).T.reshape(-1))

    tm = ROW_TILE
    row = lambda i: (i, 0)
    const = lambda i: (0, 0)
    choice = lambda k: (lambda i: (k * (n // tm) + i, 0))
    return pl.pallas_call(
        _combine_kernel,
        grid=(n // tm,),
        in_specs=[pl.BlockSpec((tm, HALF_D), choice(k)) for k in range(TOP_K)] + [
                  pl.BlockSpec((tm, TOP_K), row),
                  pl.BlockSpec((tm, D_MODEL), row), pl.BlockSpec((1, D_MODEL), const),
                  pl.BlockSpec((1, D_MODEL), const)],
        out_specs=pl.BlockSpec((tm, D_MODEL), row),
        out_shape=jax.ShapeDtypeStruct((n, D_MODEL), F32),
        compiler_params=_cparams("parallel"),
        name="moe_combine",
    )(yk, yk, yk, yk, gates, h2d, g.reshape(1, D_MODEL), b.reshape(1, D_MODEL))


def _s5_kernel(u_ref, bre_ref, bim_ref, cre_ref, cim_ref, are_ref, aim_ref, d_ref, y_ref,
               bu_re, bu_im, st_re, st_im, h_re, h_im, *, bsz):
    @pl.when(pl.program_id(0) == 0)
    def _():
        h_re[...] = jnp.zeros_like(h_re)
        h_im[...] = jnp.zeros_like(h_im)

    rows = u_ref.shape[0]
    first = lax.broadcasted_iota(I32, (SUBLANES, S5_ST_BLK), 0) < bsz
    for j in range(S5_LANE_BLOCKS):
        cin = slice(j * S5_IN_BLK, (j + 1) * S5_IN_BLK)
        cst = slice(j * S5_ST_BLK, (j + 1) * S5_ST_BLK)
        uj = u_ref[:, cin]
        ujb = uj.astype(BF16)
        bu_re[...] = _dot(ujb, bre_ref[j])
        bu_im[...] = _dot(ujb, bim_ref[j])
        ar = jnp.broadcast_to(are_ref[:, cst], (SUBLANES, S5_ST_BLK))
        ai = jnp.broadcast_to(aim_ref[:, cst], (SUBLANES, S5_ST_BLK))

        def step(i, carry):
            hr, hi = carry
            r0 = pl.multiple_of(i * SUBLANES, SUBLANES)
            vr = bu_re[pl.ds(r0, SUBLANES), :]
            vi = bu_im[pl.ds(r0, SUBLANES), :]
            h1r = ar * hr - ai * hi + vr
            h1i = ar * hi + ai * hr + vi
            h1rs = pltpu.roll(h1r, bsz, 0)
            h1is = pltpu.roll(h1i, bsz, 0)
            h2r = ar * h1rs - ai * h1is + vr
            h2i = ar * h1is + ai * h1rs + vi
            st_re[pl.ds(r0, SUBLANES), :] = jnp.where(first, h1r, h2r)
            st_im[pl.ds(r0, SUBLANES), :] = jnp.where(first, h1i, h2i)
            return pltpu.roll(h2r, bsz, 0), pltpu.roll(h2i, bsz, 0)

        hr, hi = lax.fori_loop(0, rows // SUBLANES, step, (h_re[:, cst], h_im[:, cst]))
        h_re[:, cst] = hr
        h_im[:, cst] = hi
        yj = _dot(st_re[...].astype(BF16), cre_ref[j]) + _dot(st_im[...].astype(BF16), cim_ref[j])
        yj = yj + d_ref[:, cin] * uj
        y_ref[:, cin] = jax.nn.gelu(yj).astype(BF16)


def _s5_block_diag(w, n_in, n_out):
    gpb = SSM_GROUPS // S5_LANE_BLOCKS
    w4 = w.reshape(S5_LANE_BLOCKS, gpb, n_in, n_out)
    eye = jnp.eye(gpb, dtype=w.dtype)
    return jnp.einsum("jgio,gh->jgiho", w4, eye).reshape(S5_LANE_BLOCKS, gpb * n_in, gpb * n_out)


def _s5_mixer_block(h2d, bsz, seq, w_in, log_dt, lam_re, lam_im, b_re, b_im, c_re, c_im, d, w_out, g, b):
    assert 2 * bsz == SUBLANES, "the scan packs two time steps of bsz rows into one 8-row tile"
    tm = ROW_TILE
    per_seq = seq // tm
    u_t = _matmul(h2d, w_in.astype(BF16), tm=tm, out_dtype=F32, grid=(bsz, per_seq),
                  x_map=lambda bb, i: (bb * per_seq + i, 0), out_map=lambda bb, i: (i, bb),
                  out_shape=(seq, bsz * D_MODEL)).reshape(seq * bsz, D_MODEL)

    dt = jnp.exp(log_dt)[:, None]
    mag = jnp.exp(lam_re * dt)
    a_re, a_im = mag * jnp.cos(lam_im * dt), mag * jnp.sin(lam_im * dt)
    den = lam_re * lam_re + lam_im * lam_im
    coef_re = ((a_re - 1.0) * lam_re + a_im * lam_im) / den
    coef_im = (a_im * lam_re - (a_re - 1.0) * lam_im) / den
    bb_re = coef_re[..., None] * b_re - coef_im[..., None] * b_im
    bb_im = coef_re[..., None] * b_im + coef_im[..., None] * b_re
    bre = _s5_block_diag(jnp.swapaxes(bb_re, 1, 2), SSM_GROUP, SSM_STATE).astype(BF16)
    bim = _s5_block_diag(jnp.swapaxes(bb_im, 1, 2), SSM_GROUP, SSM_STATE).astype(BF16)
    cre = _s5_block_diag(jnp.swapaxes(c_re, 1, 2), SSM_STATE, SSM_GROUP).astype(BF16)
    cim = _s5_block_diag(jnp.swapaxes(-c_im, 1, 2), SSM_STATE, SSM_GROUP).astype(BF16)
    n_state = SSM_GROUPS * SSM_STATE

    rows = S5_CHUNK * bsz
    row = lambda c: (c, 0)
    c2 = lambda c: (0, 0)
    c3 = lambda c: (0, 0, 0)
    y_t = pl.pallas_call(
        functools.partial(_s5_kernel, bsz=bsz),
        grid=(seq // S5_CHUNK,),
        in_specs=[pl.BlockSpec((rows, D_MODEL), row),
                  pl.BlockSpec(bre.shape, c3), pl.BlockSpec(bim.shape, c3),
                  pl.BlockSpec(cre.shape, c3), pl.BlockSpec(cim.shape, c3),
                  pl.BlockSpec((1, n_state), c2), pl.BlockSpec((1, n_state), c2), pl.BlockSpec((1, D_MODEL), c2)],
        out_specs=pl.BlockSpec((rows, D_MODEL), row),
        out_shape=jax.ShapeDtypeStruct((seq * bsz, D_MODEL), BF16),
        scratch_shapes=[pltpu.VMEM((rows, S5_ST_BLK), F32)] * 4 + [pltpu.VMEM((SUBLANES, n_state), F32)] * 2,
        compiler_params=_cparams("arbitrary"),
        name="s5_scan",
    )(u_t, bre, bim, cre, cim, a_re.reshape(1, n_state), a_im.reshape(1, n_state), d.reshape(1, D_MODEL))

    y2 = y_t.reshape(seq, bsz * D_MODEL)
    return _linear_residual_ln(
        [y2], [w_out.astype(BF16)], h2d, g, b, tm=tm, glu=True, grid=(bsz, per_seq),
        x_maps=[lambda bb, i: (i, bb)], res_map=lambda bb, i: (bb * per_seq + i, 0))


def _even_mixer_block(h2d, bsz, seq, w_in, qnorm_g, w_uq, w_uq_idx, kidx_g, kidx_b, w_out, g, b):
    qa, qi, ka, va, ki, wi, qb, kb, vb = _even_proj(h2d, bsz, seq, w_in, qnorm_g, w_uq, w_uq_idx, kidx_g, kidx_b)
    o_a = _dsa_attention(qa, qi, wi, ki, ka, va, bsz, seq)
    o_b = _stick_breaking(qb, kb, vb, bsz, seq)
    w_out = w_out.astype(BF16)
    return _linear_residual_ln([o_a, o_b], [w_out[:A_WIDTH], w_out[A_WIDTH:]], h2d, g, b, tm=ROW_TILE)


def kernel(x, mem, ev_w_in, ev_qnorm_g, ev_w_uq, ev_w_uq_idx, ev_kidx_ln_g, ev_kidx_ln_b, ev_w_out, od_w_in, od_log_dt, od_lambda_re, od_lambda_im, od_b_re, od_b_im, od_c_re, od_c_im, od_d, od_w_out, mix_ln_g, mix_ln_b, xa_w_q, xa_w_kv, xa_w_o, xa_ln_g, xa_ln_b, moe_w_router, moe_b_router, moe_w_gu, moe_b_gu, moe_w_down, moe_b_down, ffn_ln_g, ffn_ln_b):
    bsz, seq, _ = x.shape
    h = x.reshape(bsz * seq, D_MODEL)
    for layer in range(DEPTH):
        j = layer // 2
        if layer % 2 == 0:
            h = _even_mixer_block(h, bsz, seq, ev_w_in[j], ev_qnorm_g[j], ev_w_uq[j], ev_w_uq_idx[j],
                                  ev_kidx_ln_g[j], ev_kidx_ln_b[j], ev_w_out[j], mix_ln_g[layer], mix_ln_b[layer])
        else:
            h = _s5_mixer_block(h, bsz, seq, od_w_in[j], od_log_dt[j], od_lambda_re[j], od_lambda_im[j],
                                od_b_re[j], od_b_im[j], od_c_re[j], od_c_im[j], od_d[j], od_w_out[j],
                                mix_ln_g[layer], mix_ln_b[layer])
        h, h_packed = _cross_attention_block(h, mem, bsz, seq, xa_w_q[layer], xa_w_kv[layer], xa_w_o[layer],
                                             xa_ln_g[layer], xa_ln_b[layer])
        h = _moe_block(h, h_packed, layer, moe_w_router[layer], moe_b_router[layer], moe_w_gu, moe_b_gu[layer],
                       moe_w_down, moe_b_down[layer], ffn_ln_g[layer], ffn_ln_b[layer])
    return h.reshape(bsz, seq, D_MODEL)
```

```python
import functools
import math

import jax
import jax.numpy as jnp
from jax import lax
from jax.experimental import pallas as pl
from jax.experimental.pallas import tpu as pltpu
from jax.experimental.pallas import tpu_sc as plsc

F32 = jnp.float32
BF16 = jnp.bfloat16
I32 = jnp.int32

D_MODEL = 1024
DEPTH = 2
HEAD_DIM = 64
A_HEADS = 8
A_KV_HEADS = 2
A_REP = A_HEADS // A_KV_HEADS
Q_RANK = 256
IDX_HEADS = 8
IDX_DIM = 64
IDX_TOPK = 256
B_HEADS = 8
A_WIDTH = A_HEADS * HEAD_DIM
B_WIDTH = B_HEADS * HEAD_DIM
SSM_GROUP = 16
SSM_GROUPS = D_MODEL // SSM_GROUP
SSM_STATE = 64
XA_HEADS = 4
XA_HEAD_DIM = D_MODEL // XA_HEADS
N_EXPERTS = 32
TOP_K = 4
D_EXPERT = D_MODEL
SWIGLU_LIMIT = 7.0
SWIGLU_ALPHA = 1.702
ROPE_THETA = 500000.0
ROPE_HALF = HEAD_DIM // 8
LN_EPS = 1e-5
DN_ALPHA = (2 * DEPTH) ** 0.25

LANES = 128
SUBLANES = 8
VMEM_LIMIT_BYTES = 56 * 1024 * 1024

Q_BLOCK = 256
DSA_KEY_TILE = 512
DSA_ATT_TILE = 1024
DSA_ATT_PAR = 1
DSA_COUNT_ROWS = 8 * SUBLANES
SB_KEY_TILE = 256
ROW_TILE = 512
MOE_BLOCK_ROWS = 512
SC_GATHER_SLOT_BYTES = 128 * 1024
S5_CHUNK = 256
S5_LANE_BLOCKS = 4
S5_IN_BLK = D_MODEL // S5_LANE_BLOCKS
S5_ST_BLK = SSM_GROUPS * SSM_STATE // S5_LANE_BLOCKS

SB_EXIT_LOG = -104.0
NEG_BIG = -1e30
INT_MIN = -(2 ** 31)


def _cparams(*sem):
    return pltpu.CompilerParams(dimension_semantics=sem, vmem_limit_bytes=VMEM_LIMIT_BYTES)


def _dot(a, b):
    return jnp.dot(a, b, preferred_element_type=F32)


def _dot_nt(a, b):
    return lax.dot_general(a, b, (((1,), (1,)), ((), ())), preferred_element_type=F32)


def _layer_norm_rows(y, g, b):
    mu = jnp.mean(y, axis=-1, keepdims=True)
    d = y - mu
    var = jnp.mean(d * d, axis=-1, keepdims=True)
    return d * lax.rsqrt(var + LN_EPS) * g + b


def _mm_kernel(x_ref, w_ref, o_ref):
    o_ref[...] = _dot(x_ref[...].astype(BF16), w_ref[...]).astype(o_ref.dtype)


def _matmul(x, w, *, tm, out_dtype, x_map=None, out_map=None, grid=None, out_shape=None):
    m, k = x.shape
    n = w.shape[1]
    grid = grid or (m // tm,)
    x_map = x_map or (lambda i: (i, 0))
    out_map = out_map or (lambda i: (i, 0))
    out_shape = out_shape or (m, n)
    return pl.pallas_call(
        _mm_kernel,
        grid=grid,
        in_specs=[pl.BlockSpec((tm, k), x_map), pl.BlockSpec((k, n), lambda *a: (0, 0))],
        out_specs=pl.BlockSpec((tm, n), out_map),
        out_shape=jax.ShapeDtypeStruct(out_shape, out_dtype),
        compiler_params=_cparams(*(("parallel",) * len(grid))),
        name="matmul",
    )(x, w)


def _lin_ln_kernel(*refs, n_in, glu):
    xs, ws = refs[:n_in], refs[n_in:2 * n_in]
    res_ref, g_ref, b_ref, o_ref = refs[2 * n_in:]
    acc = _dot(xs[0][...].astype(BF16), ws[0][...])
    for x_ref, w_ref in zip(xs[1:], ws[1:]):
        acc = acc + _dot(x_ref[...].astype(BF16), w_ref[...])
    if glu:
        acc = acc[:, :D_MODEL] * jax.nn.sigmoid(acc[:, D_MODEL:])
    y = DN_ALPHA * res_ref[...] + acc
    o_ref[...] = _layer_norm_rows(y, g_ref[...], b_ref[...])


def _linear_residual_ln(xs, ws, res, g, b, *, tm, glu=False, grid=None, x_maps=None, res_map=None):
    n_rows = res.shape[0]
    grid = grid or (n_rows // tm,)
    x_maps = x_maps or [lambda i: (i, 0)] * len(xs)
    res_map = res_map or (lambda i: (i, 0))
    const = lambda *a: (0, 0)
    in_specs = [pl.BlockSpec((tm, w.shape[0]), m) for w, m in zip(ws, x_maps)]
    in_specs += [pl.BlockSpec(w.shape, const) for w in ws]
    in_specs += [pl.BlockSpec((tm, D_MODEL), res_map), pl.BlockSpec((1, D_MODEL), const),
                 pl.BlockSpec((1, D_MODEL), const)]
    return pl.pallas_call(
        functools.partial(_lin_ln_kernel, n_in=len(xs), glu=glu),
        grid=grid,
        in_specs=in_specs,
        out_specs=pl.BlockSpec((tm, D_MODEL), res_map),
        out_shape=jax.ShapeDtypeStruct((n_rows, D_MODEL), F32),
        compiler_params=_cparams(*(("parallel",) * len(grid))),
        name="linear_residual_ln",
    )(*xs, *ws, res, g.reshape(1, D_MODEL), b.reshape(1, D_MODEL))


_EV_CQ, _EV_KA, _EV_VA, _EV_KI, _EV_QB = 0, 256, 384, 512, 640
_EV_KB = _EV_QB + B_WIDTH
_EV_VB = _EV_KB + B_WIDTH
_EV_COLS = _EV_VB + B_WIDTH


def _rope_tables(seq):
    inv = ROPE_THETA ** (-jnp.arange(ROPE_HALF, dtype=F32) / ROPE_HALF)
    ang = jnp.arange(seq, dtype=F32)[:, None] * inv[None, :]
    cos, sin = jnp.cos(ang), jnp.sin(ang)
    rest = HEAD_DIM - 2 * ROPE_HALF
    zh = jnp.zeros((seq, ROPE_HALF), F32)
    c = jnp.concatenate([cos, cos, jnp.ones((seq, rest), F32)], axis=1)
    s1 = jnp.concatenate([-sin, zh, jnp.zeros((seq, rest), F32)], axis=1)
    s2 = jnp.concatenate([zh, sin, jnp.zeros((seq, rest), F32)], axis=1)
    rep = LANES // HEAD_DIM
    return jnp.tile(c, (1, rep)), jnp.tile(s1, (1, rep)), jnp.tile(s2, (1, rep))


def _even_proj_kernel(x_ref, w_ref, qg_ref, wuq_ref, wuqi_ref, lg_ref, lb_ref, c_ref, s1_ref, s2_ref,
                      qa_ref, qi_ref, ka_ref, va_ref, ki_ref, wi_ref, qb_ref, kb_ref, vb_ref):
    p = _dot(x_ref[...].astype(BF16), w_ref[...])
    c, s1, s2 = c_ref[...], s1_ref[...], s2_ref[...]

    def rope(t):
        return (t * c + pltpu.roll(t, LANES - ROPE_HALF, 1) * s1 + pltpu.roll(t, ROPE_HALF, 1) * s2)

    cq = p[:, _EV_CQ:_EV_CQ + Q_RANK]
    cn = cq * lax.rsqrt(jnp.mean(cq * cq, axis=-1, keepdims=True) + LN_EPS) * qg_ref[...]
    cnb = cn.astype(BF16)
    qa = _dot(cnb, wuq_ref[...])
    qi = _dot(cnb, wuqi_ref[...])
    low = lax.broadcasted_iota(I32, c.shape, 1) < HEAD_DIM
    for j in range(A_WIDTH // LANES):
        sl = slice(j * LANES, (j + 1) * LANES)
        pair = rope(qa[:, sl]) * (HEAD_DIM ** -0.5)
        for e, src in enumerate((pair, pltpu.roll(pair, HEAD_DIM, 1))):
            h = 2 * j + e
            qa_ref[:, h * LANES:(h + 1) * LANES] = jnp.where(low, src, 0.0).astype(BF16)
        qi_ref[:, sl] = (rope(qi[:, sl]) * (IDX_DIM ** -0.5)).astype(BF16)
    kpair = rope(p[:, _EV_KA:_EV_KA + LANES])
    vpair = p[:, _EV_VA:_EV_VA + LANES]
    v_pad = jnp.where(lax.broadcasted_iota(I32, c.shape, 1) == HEAD_DIM, 1.0, 0.0)
    for g, (ks, vs) in enumerate(((kpair, vpair), (pltpu.roll(kpair, HEAD_DIM, 1), pltpu.roll(vpair, HEAD_DIM, 1)))):
        ka_ref[:, g * LANES:(g + 1) * LANES] = jnp.where(low, ks, 0.0).astype(BF16)
        va_ref[:, g * LANES:(g + 1) * LANES] = jnp.where(low, vs, v_pad).astype(BF16)

    t = p[:, _EV_KI:_EV_KI + LANES]
    lane = lax.broadcasted_iota(I32, t.shape, 1)
    is_k = lane < IDX_DIM
    mu = jnp.sum(jnp.where(is_k, t, 0.0), axis=-1, keepdims=True) * (1.0 / IDX_DIM)
    d = jnp.where(is_k, t - mu, 0.0)
    var = jnp.sum(d * d, axis=-1, keepdims=True) * (1.0 / IDX_DIM)
    kin = d * lax.rsqrt(var + LN_EPS) * lg_ref[...] + lb_ref[...]
    ki_ref[...] = rope(kin)[:, :IDX_DIM].astype(BF16)
    wi_ref[...] = t[:, IDX_DIM:IDX_DIM + IDX_HEADS] * (IDX_HEADS ** -0.5)

    qb_ref[...] = (p[:, _EV_QB:_EV_KB] * (HEAD_DIM ** -0.5)).astype(BF16)
    kb_ref[...] = p[:, _EV_KB:_EV_VB].astype(BF16)
    vb_ref[...] = p[:, _EV_VB:_EV_COLS].astype(BF16)


def _even_proj(x2d, bsz, seq, w_in, qnorm_g, w_uq, w_uq_idx, kidx_g, kidx_b):
    n = x2d.shape[0]
    tm = ROW_TILE
    per_seq = seq // tm
    c0 = Q_RANK + 2 * A_KV_HEADS * HEAD_DIM + IDX_DIM + IDX_HEADS
    w_pack = jnp.concatenate(
        [w_in[:, :c0], jnp.zeros((D_MODEL, _EV_QB - c0), w_in.dtype), w_in[:, c0:]], axis=1).astype(BF16)
    pad = LANES - IDX_DIM
    lg = jnp.concatenate([kidx_g, jnp.zeros((pad,), F32)]).reshape(1, LANES)
    lb = jnp.concatenate([kidx_b, jnp.zeros((pad,), F32)]).reshape(1, LANES)
    c, s1, s2 = _rope_tables(seq)
    row = lambda i: (i, 0)
    const = lambda i: (0, 0)
    pos = lambda i: (i % per_seq, 0)
    head_shape = jax.ShapeDtypeStruct((n, B_WIDTH), BF16)
    head_spec = pl.BlockSpec((tm, B_WIDTH), row)
    return pl.pallas_call(
        _even_proj_kernel,
        grid=(n // tm,),
        in_specs=[pl.BlockSpec((tm, D_MODEL), row), pl.BlockSpec((D_MODEL, _EV_COLS), const),
                  pl.BlockSpec((1, Q_RANK), const), pl.BlockSpec((Q_RANK, A_WIDTH), const),
                  pl.BlockSpec((Q_RANK, IDX_HEADS * IDX_DIM), const),
                  pl.BlockSpec((1, LANES), const), pl.BlockSpec((1, LANES), const),
                  pl.BlockSpec((tm, LANES), pos), pl.BlockSpec((tm, LANES), pos), pl.BlockSpec((tm, LANES), pos)],
        out_specs=[pl.BlockSpec((tm, A_HEADS * LANES), row), pl.BlockSpec((tm, IDX_HEADS * IDX_DIM), row),
                   pl.BlockSpec((tm, A_KV_HEADS * LANES), row), pl.BlockSpec((tm, A_KV_HEADS * LANES), row),
                   pl.BlockSpec((tm, IDX_DIM), row), pl.BlockSpec((tm, IDX_HEADS), row),
                   head_spec, head_spec, head_spec],
        out_shape=[jax.ShapeDtypeStruct((n, A_HEADS * LANES), BF16), jax.ShapeDtypeStruct((n, IDX_HEADS * IDX_DIM), BF16),
                   jax.ShapeDtypeStruct((n, A_KV_HEADS * LANES), BF16),
                   jax.ShapeDtypeStruct((n, A_KV_HEADS * LANES), BF16),
                   jax.ShapeDtypeStruct((n, IDX_DIM), BF16), jax.ShapeDtypeStruct((n, IDX_HEADS), F32),
                   head_shape, head_shape, head_shape],
        compiler_params=_cparams("parallel"),
        name="even_proj",
    )(x2d, w_pack, qnorm_g.reshape(1, Q_RANK), w_uq.astype(BF16), w_uq_idx.astype(BF16), lg, lb, c, s1, s2)


def _key_to_float(key):
    bits = key ^ ((key >> 31) & jnp.int32(0x7FFFFFFF))
    return lax.bitcast_convert_type(bits, F32)


def _high_half(x):
    bits = lax.bitcast_convert_type(x, jnp.uint32) & jnp.uint32(0xFFFF0000)
    return lax.bitcast_convert_type(bits, F32).astype(BF16)


def _dsa_kernel(qa_ref, qi_ref, wit_ref, ki_ref, ka_ref, vat_ref, o_ref, sc_scr, hi_scr, *, topk, ts, ta):
    seq = sc_scr.shape[0]
    qb = pl.program_id(1)
    q0 = qb * Q_BLOCK
    nkt = (q0 + Q_BLOCK - 1) // ts + 1
    t_row = q0 + lax.broadcasted_iota(I32, (1, Q_BLOCK), 1)
    key = lax.broadcasted_iota(I32, (ts, Q_BLOCK), 0)
    kf = jnp.float32(topk)

    qi = qi_ref[0]
    qs = jnp.concatenate([qi[:, h * IDX_DIM:(h + 1) * IDX_DIM] for h in range(IDX_HEADS)], axis=0)
    wit = wit_ref[0]

    def score_tile(kt, carry):
        off = pl.multiple_of(kt * ts, ts)
        s_all = _dot_nt(ki_ref[0, pl.ds(off, ts), :], qs)
        acc = jnp.zeros((ts, Q_BLOCK), F32)
        for h in range(IDX_HEADS):
            acc = acc + jnp.maximum(s_all[:, h * Q_BLOCK:(h + 1) * Q_BLOCK], 0.0) * wit[h:h + 1, :]
        val = jnp.where(off + key <= t_row, acc, -jnp.inf)
        sc_scr[pl.ds(off, ts), :] = val
        hi_scr[pl.ds(off, ts), :] = _high_half(val)
        return carry

    lax.fori_loop(0, nkt, score_tile, 0)

    def count(pred):
        def body(kt, acc):
            off = pl.multiple_of(kt * ts, ts)
            ind = pred(sc_scr[pl.ds(off, ts), :], off + key)
            return acc + jnp.sum(ind.reshape(ts // DSA_COUNT_ROWS, DSA_COUNT_ROWS, Q_BLOCK), axis=0)
        acc = lax.fori_loop(0, nkt, body, jnp.zeros((DSA_COUNT_ROWS, Q_BLOCK), F32))
        return jnp.sum(acc, axis=0, keepdims=True)

    one_h, zero_h = jnp.ones((), BF16), jnp.zeros((), BF16)

    def count_high(c_hi):
        def body(kt, acc):
            off = pl.multiple_of(kt * ts, ts)
            ind = jnp.where(hi_scr[pl.ds(off, ts), :] >= c_hi, one_h, zero_h)
            part = ind[:DSA_COUNT_ROWS]
            for j in range(1, ts // DSA_COUNT_ROWS):
                part = part + ind[j * DSA_COUNT_ROWS:(j + 1) * DSA_COUNT_ROWS]
            return acc + part.astype(F32)
        acc = lax.fori_loop(0, nkt, body, jnp.zeros((DSA_COUNT_ROWS, Q_BLOCK), F32))
        return jnp.sum(acc, axis=0, keepdims=True)

    def high_step(i, base):
        cand = base + jnp.left_shift(jnp.int32(1), 31 - i)
        cnt = count_high(_high_half(_key_to_float(cand)))
        return jnp.where(cnt >= kf, cand, base)

    def bit_step(i, base):
        cand = base + jnp.left_shift(jnp.int32(1), 31 - i)
        cf = _key_to_float(cand)
        cnt = count(lambda sc, idx: jnp.where(sc >= cf, 1.0, 0.0))
        return jnp.where(cnt >= kf, cand, base)

    base = lax.fori_loop(0, 16, high_step, jnp.full((1, Q_BLOCK), INT_MIN, I32))
    base = lax.fori_loop(16, 32, bit_step, base)
    thr = jnp.where(base == INT_MIN, -jnp.inf, _key_to_float(base))

    cnt_ge = count(lambda sc, idx: jnp.where(sc >= thr, 1.0, 0.0))
    tied = jnp.logical_and(cnt_ge > kf, thr > -jnp.inf)
    any_tied = jnp.max(jnp.where(tied, 1.0, 0.0)) > 0.0
    seq_bits = max(1, int(math.ceil(math.log2(seq))))

    def tie_cut():
        cnt_gt = count(lambda sc, idx: jnp.where(sc > thr, 1.0, 0.0))
        need = kf - cnt_gt

        def idx_step(i, pos):
            cand = pos + jnp.left_shift(jnp.int32(1), seq_bits - 1 - i)
            cnt = count(lambda sc, idx: jnp.where(sc == thr, jnp.where(idx < cand, 1.0, 0.0), 0.0))
            return jnp.where(cnt < need, cand, pos)

        return lax.fori_loop(0, seq_bits, idx_step, jnp.zeros((1, Q_BLOCK), I32))

    cut = lax.cond(any_tied, tie_cut, lambda: jnp.full((1, Q_BLOCK), seq, I32))
    cut = jnp.where(tied, cut, seq)

    nkt_a = (q0 + Q_BLOCK - 1) // ta + 1
    key_a = lax.broadcasted_iota(I32, (ta, Q_BLOCK), 0)
    cols = A_REP * Q_BLOCK
    qg = [jnp.concatenate([qa_ref[0, :, (g * A_REP + r) * LANES:(g * A_REP + r + 1) * LANES]
                           for r in range(A_REP)], axis=0) for g in range(A_KV_HEADS)]

    def att_step(i, carry):
        offs = [pl.multiple_of((DSA_ATT_PAR * i + e) * ta, ta) for e in range(DSA_ATT_PAR)]
        logits = [_dot_nt(ka_ref[0, pl.ds(offs[e], ta), g * LANES:(g + 1) * LANES], qg[g])
                  for e in range(DSA_ATT_PAR) for g in range(A_KV_HEADS)]
        out = []
        for e in range(DSA_ATT_PAR):
            sc = sc_scr[pl.ds(offs[e], ta), :]
            idx = offs[e] + key_a
            keep = jnp.where(sc > thr, 0.0, jnp.where(sc == thr, jnp.where(idx <= cut, 0.0, NEG_BIG), NEG_BIG))
            bias = jnp.where(idx <= t_row, keep, NEG_BIG)
            bias = jnp.concatenate([bias] * A_REP, axis=1)
            for g in range(A_KV_HEADS):
                m, acc = carry[e * A_KV_HEADS + g]
                s = logits[e * A_KV_HEADS + g] + bias
                m_new = jnp.maximum(m, jnp.max(s, axis=0, keepdims=True))
                p = jnp.exp(s - m_new)
                vt = vat_ref[0, g * LANES:(g + 1) * LANES, pl.ds(offs[e], ta)]
                out.append((m_new, jnp.exp(m - m_new) * acc + _dot(vt, p.astype(BF16))))
        return tuple(out)

    init = tuple((jnp.full((1, cols), NEG_BIG, F32), jnp.zeros((LANES, cols), F32))
                 for _ in range(DSA_ATT_PAR * A_KV_HEADS))
    final = lax.fori_loop(0, (nkt_a + DSA_ATT_PAR - 1) // DSA_ATT_PAR, att_step, init)
    low = lax.broadcasted_iota(I32, (Q_BLOCK, LANES), 1) < HEAD_DIM
    outs = []
    for g in range(A_KV_HEADS):
        m, acc = final[g]
        for e in range(1, DSA_ATT_PAR):
            m1, acc1 = final[e * A_KV_HEADS + g]
            m_all = jnp.maximum(m, m1)
            m, acc = m_all, jnp.exp(m - m_all) * acc + jnp.exp(m1 - m_all) * acc1
        og = acc / acc[HEAD_DIM:HEAD_DIM + 1, :]
        outs += [og[:, r * Q_BLOCK:(r + 1) * Q_BLOCK].T for r in range(A_REP)]
    for j in range(A_HEADS // 2):
        pair = jnp.where(low, outs[2 * j], pltpu.roll(outs[2 * j + 1], HEAD_DIM, 1))
        o_ref[0, :, j * LANES:(j + 1) * LANES] = pair.astype(BF16)


def _dsa_attention(qa, qi, wi, ki, ka, va, bsz, seq):
    topk = min(IDX_TOPK, seq // 4)
    ts = min(DSA_KEY_TILE, seq)
    ta = min(DSA_ATT_TILE, seq // DSA_ATT_PAR)
    assert seq % (DSA_ATT_PAR * ta) == 0 and seq % ts == 0, "the attention loop walks whole groups of key tiles"
    blk = lambda b, i: (b, i, 0)
    full = lambda b, i: (b, 0, 0)
    r3 = lambda a: a.reshape(bsz, seq, a.shape[-1])
    wit = jnp.swapaxes(r3(wi), 1, 2)
    vat = jnp.swapaxes(r3(va), 1, 2)
    return pl.pallas_call(
        functools.partial(_dsa_kernel, topk=topk, ts=ts, ta=ta),
        grid=(bsz, seq // Q_BLOCK),
        in_specs=[pl.BlockSpec((1, Q_BLOCK, A_HEADS * LANES), blk),
                  pl.BlockSpec((1, Q_BLOCK, IDX_HEADS * IDX_DIM), blk),
                  pl.BlockSpec((1, IDX_HEADS, Q_BLOCK), lambda b, i: (b, 0, i)),
                  pl.BlockSpec((1, seq, IDX_DIM), full),
                  pl.BlockSpec((1, seq, A_KV_HEADS * LANES), full),
                  pl.BlockSpec((1, A_KV_HEADS * LANES, seq), full)],
        out_specs=pl.BlockSpec((1, Q_BLOCK, A_WIDTH), blk),
        out_shape=jax.ShapeDtypeStruct((bsz, seq, A_WIDTH), BF16),
        scratch_shapes=[pltpu.VMEM((seq, Q_BLOCK), F32), pltpu.VMEM((seq, Q_BLOCK), BF16)],
        compiler_params=_cparams("parallel", "parallel"),
        name="dsa_attention",
    )(r3(qa), r3(qi), wit, r3(ki), r3(ka), vat).reshape(bsz * seq, A_WIDTH)


def _sb_kernel(q_ref, k_ref, v_ref, u_ref, o_ref, acc_scr, run_scr, *, tk):
    q0 = pl.program_id(1) * Q_BLOCK
    t_col = q0 + lax.broadcasted_iota(I32, (Q_BLOCK, 1), 0)
    lane = lax.broadcasted_iota(I32, (Q_BLOCK, tk), 1)
    low = lax.broadcasted_iota(I32, (Q_BLOCK, LANES), 1) < HEAD_DIM
    upper = u_ref[...]
    nkt = (q0 + Q_BLOCK - 1) // tk + 1
    q = q_ref[0]
    zero = jnp.zeros((Q_BLOCK, LANES), BF16)
    qm = []
    for p in range(B_HEADS // 2):
        pair = q[:, p * LANES:(p + 1) * LANES]
        qm.append(jnp.concatenate([jnp.where(low, pair, zero), jnp.where(low, zero, pair)], axis=0))
    acc_scr[...] = jnp.zeros_like(acc_scr)
    run_scr[...] = jnp.zeros_like(run_scr)

    def cond(carry):
        i, worst = carry
        return jnp.logical_and(i < nkt, worst >= SB_EXIT_LOG)

    def body(carry):
        i, _ = carry
        off = pl.multiple_of((nkt - 1 - i) * tk, tk)
        strict = off + lane < t_col
        strict = jnp.concatenate([strict, strict], axis=0)
        worst = None
        for p in range(B_HEADS // 2):
            cols = slice(p * LANES, (p + 1) * LANES)
            kp = k_ref[0, pl.ds(off, tk), cols]
            vp = v_ref[0, pl.ds(off, tk), cols]
            run = run_scr[p]
            z = _dot_nt(qm[p], kp)
            softplus = jnp.maximum(z, 0.0) + jnp.log(1.0 + jnp.exp(-jnp.abs(z)))
            log_1mb = jnp.where(strict, -softplus, 0.0)
            hi = log_1mb.astype(BF16)
            lo = (log_1mb - hi.astype(F32)).astype(BF16)
            after = _dot(hi, upper) + _dot(lo, upper) + run
            a = jnp.where(strict, jnp.exp(z - softplus + after), 0.0)
            out = _dot(a.astype(BF16), vp)
            run = run + jnp.sum(log_1mb, axis=1, keepdims=True)
            run_scr[p] = run
            worst = run if worst is None else jnp.maximum(worst, run)
            acc_scr[:, cols] += jnp.where(low, out[:Q_BLOCK], out[Q_BLOCK:])
        return i + 1, jnp.max(worst)

    lax.while_loop(cond, body, (jnp.int32(0), jnp.float32(0.0)))
    o_ref[0] = acc_scr[...].astype(BF16)


def _stick_breaking(qb, kb, vb, bsz, seq):
    tk = min(SB_KEY_TILE, seq)
    r = lax.broadcasted_iota(I32, (tk, tk), 0)
    c = lax.broadcasted_iota(I32, (tk, tk), 1)
    upper = jnp.where(r > c, 1.0, 0.0).astype(BF16)
    blk = lambda b, i: (b, i, 0)
    full = lambda b, i: (b, 0, 0)
    r3 = lambda a: a.reshape(bsz, seq, B_WIDTH)
    return pl.pallas_call(
        functools.partial(_sb_kernel, tk=tk),
        grid=(bsz, seq // Q_BLOCK),
        in_specs=[pl.BlockSpec((1, Q_BLOCK, B_WIDTH), blk), pl.BlockSpec((1, seq, B_WIDTH), full),
                  pl.BlockSpec((1, seq, B_WIDTH), full), pl.BlockSpec((tk, tk), lambda b, i: (0, 0))],
        out_specs=pl.BlockSpec((1, Q_BLOCK, B_WIDTH), blk),
        out_shape=jax.ShapeDtypeStruct((bsz, seq, B_WIDTH), BF16),
        scratch_shapes=[pltpu.VMEM((Q_BLOCK, B_WIDTH), F32), pltpu.VMEM((B_HEADS // 2, 2 * Q_BLOCK, 1), F32)],
        compiler_params=_cparams("parallel", "arbitrary"),
        name="stick_breaking",
    )(r3(qb), r3(kb), r3(vb), upper).reshape(bsz * seq, B_WIDTH)


HALF_D = D_MODEL // 2
U32 = jnp.uint32
HIGH16 = 0xFFFF0000


def _pack_bf16_pairs(x):
    def bits(v):
        return lax.bitcast_convert_type(v.astype(BF16).astype(F32), U32)
    word = (bits(x[:, HALF_D:]) & U32(HIGH16)) | (bits(x[:, :HALF_D]) >> 16)
    return lax.bitcast_convert_type(word, I32)


def _unpack_bf16_pairs(word):
    u = lax.bitcast_convert_type(word, U32)
    return lax.bitcast_convert_type(u << 16, F32), lax.bitcast_convert_type(u & U32(HIGH16), F32)


def _xattn_kernel(h_ref, wq_ref, kv_ref, wo_ref, g_ref, b_ref, o_ref, packed_ref):
    h = h_ref[...]
    q = (_dot(h.astype(BF16), wq_ref[...]) * (XA_HEAD_DIM ** -0.5)).astype(BF16)
    kv = kv_ref[0]
    outs = []
    for hd in range(XA_HEADS):
        sl = slice(hd * XA_HEAD_DIM, (hd + 1) * XA_HEAD_DIM)
        s = _dot_nt(q[:, sl], kv[:, sl])
        p = jnp.exp(s - jnp.max(s, axis=1, keepdims=True))
        vh = kv[:, D_MODEL + hd * XA_HEAD_DIM:D_MODEL + (hd + 1) * XA_HEAD_DIM]
        outs.append((_dot(p.astype(BF16), vh) / jnp.sum(p, axis=1, keepdims=True)).astype(BF16))
    y = _dot(jnp.concatenate(outs, axis=1), wo_ref[...])
    out = _layer_norm_rows(DN_ALPHA * h + y, g_ref[...], b_ref[...])
    o_ref[...] = out
    packed_ref[...] = _pack_bf16_pairs(out)


def _cross_attention_block(h2d, mem, bsz, seq, w_q, w_kv, w_o, g, b):
    tm = ROW_TILE
    per_seq = seq // tm
    mem_len = mem.shape[1]
    kv = _matmul(mem.reshape(bsz * mem_len, D_MODEL), w_kv.astype(BF16), tm=mem_len, out_dtype=BF16)
    kv = kv.reshape(bsz, mem_len, 2 * D_MODEL)
    row = lambda i: (i, 0)
    const = lambda i: (0, 0)
    return pl.pallas_call(
        _xattn_kernel,
        grid=(bsz * per_seq,),
        in_specs=[pl.BlockSpec((tm, D_MODEL), row), pl.BlockSpec((D_MODEL, D_MODEL), const),
                  pl.BlockSpec((1, mem_len, 2 * D_MODEL), lambda i: (i // per_seq, 0, 0)),
                  pl.BlockSpec((D_MODEL, D_MODEL), const),
                  pl.BlockSpec((1, D_MODEL), const), pl.BlockSpec((1, D_MODEL), const)],
        out_specs=[pl.BlockSpec((tm, D_MODEL), row), pl.BlockSpec((tm, HALF_D), row)],
        out_shape=[jax.ShapeDtypeStruct(h2d.shape, F32), jax.ShapeDtypeStruct((h2d.shape[0], HALF_D), I32)],
        compiler_params=_cparams("parallel"),
        name="cross_attention",
    )(h2d, w_q.astype(BF16), kv, w_o.astype(BF16), g.reshape(1, D_MODEL), b.reshape(1, D_MODEL))


def _router_kernel(h_ref, w_ref, b_ref, tri_ref, idx_ref, gate_ref, rank_ref, cnt_ref, run_scr):
    @pl.when(pl.program_id(0) == 0)
    def _():
        run_scr[...] = jnp.zeros_like(run_scr)

    h = h_ref[...]
    hh = h.astype(BF16)
    hl = (h - hh.astype(F32)).astype(BF16)
    w = w_ref[...]
    wh = w.astype(BF16)
    wl = (w - wh.astype(F32)).astype(BF16)
    logits = _dot(hh, wh) + _dot(hl, wh) + _dot(hh, wl) + b_ref[...]
    lane = lax.broadcasted_iota(I32, logits.shape, 1).astype(F32)
    vals, sels = [], []
    onehot = jnp.zeros(logits.shape, F32)
    for k in range(TOP_K):
        m = jnp.max(logits, axis=1, keepdims=True)
        sel = jnp.min(jnp.where(logits == m, lane, float(LANES)), axis=1, keepdims=True)
        idx_ref[:, k:k + 1] = sel.astype(I32)
        vals.append(m)
        sels.append(sel)
        onehot = onehot + jnp.where(lane == sel, 1.0, 0.0)
        logits = jnp.where(lane == sel, -jnp.inf, logits)
    es = [jnp.exp(v - vals[0]) for v in vals]
    tot = es[0] + es[1] + es[2] + es[3]
    for k in range(TOP_K):
        gate_ref[:, k:k + 1] = es[k] / tot

    earlier = _dot(tri_ref[...], onehot.astype(BF16)) + run_scr[...]
    for k in range(TOP_K):
        rank = jnp.sum(jnp.where(lane == sels[k], earlier, 0.0), axis=1, keepdims=True)
        rank_ref[:, k:k + 1] = rank.astype(I32)
    run = run_scr[...] + jnp.sum(onehot, axis=0, keepdims=True)
    run_scr[...] = run
    cnt_ref[...] = run


def _router(h2d, w_router, b_router):
    n = h2d.shape[0]
    tm = 2 * ROW_TILE
    pad = LANES - N_EXPERTS
    w = jnp.concatenate([w_router, jnp.zeros((D_MODEL, pad), F32)], axis=1)
    b = jnp.concatenate([b_router, jnp.full((pad,), NEG_BIG, F32)]).reshape(1, LANES)
    r = lax.broadcasted_iota(I32, (tm, tm), 0)
    c = lax.broadcasted_iota(I32, (tm, tm), 1)
    tri = jnp.where(c < r, 1.0, 0.0).astype(BF16)
    row = lambda i: (i, 0)
    const = lambda i: (0, 0)
    return pl.pallas_call(
        _router_kernel,
        grid=(n // tm,),
        in_specs=[pl.BlockSpec((tm, D_MODEL), row), pl.BlockSpec((D_MODEL, LANES), const),
                  pl.BlockSpec((1, LANES), const), pl.BlockSpec((tm, tm), const)],
        out_specs=[pl.BlockSpec((tm, TOP_K), row), pl.BlockSpec((tm, TOP_K), row),
                   pl.BlockSpec((tm, TOP_K), row), pl.BlockSpec((1, LANES), const)],
        out_shape=[jax.ShapeDtypeStruct((n, TOP_K), I32), jax.ShapeDtypeStruct((n, TOP_K), F32),
                   jax.ShapeDtypeStruct((n, TOP_K), I32), jax.ShapeDtypeStruct((1, LANES), F32)],
        scratch_shapes=[pltpu.VMEM((1, LANES), F32)],
        compiler_params=_cparams("arbitrary"),
        name="moe_router",
    )(h2d, w, b, tri)


def _gather_rows(src, idx):
    n_out = idx.shape[0]
    width = src.shape[1]
    win = SC_GATHER_SLOT_BYTES // (width * src.dtype.itemsize)
    mesh = plsc.VectorSubcoreMesh(core_axis_name="core", subcore_axis_name="subcore")
    n_workers = mesh.num_cores * mesh.num_subcores
    per_worker = n_out // n_workers
    steps = per_worker // win
    assert per_worker * n_workers == n_out and steps * win == per_worker and steps % 2 == 0

    @functools.partial(
        pl.kernel, out_type=jax.ShapeDtypeStruct((n_out, width), src.dtype), mesh=mesh,
        scratch_types=[pltpu.VMEM((per_worker,), I32), pltpu.VMEM((2, win, width), src.dtype),
                       pltpu.SemaphoreType.DMA, pltpu.SemaphoreType.DMA])
    def gather_kernel(src_hbm, idx_hbm, dst_hbm, idx_v, rows_v, sem0, sem1):
        worker = lax.axis_index("subcore") * mesh.num_cores + lax.axis_index("core")
        base = worker * per_worker
        sems = (sem0, sem1)
        pltpu.sync_copy(idx_hbm.at[pl.ds(base, per_worker)], idx_v)

        def gather(step, slot):
            return pltpu.make_async_copy(src_hbm.at[idx_v.at[pl.ds(step * win, win)]], rows_v.at[slot], sems[slot])

        gather(0, 0).start()

        @pl.loop(0, steps, step=2)
        def _(s):
            for slot in range(2):
                step = s + slot
                gather(step, slot).wait()

                @pl.when(step + 1 < steps)
                def _():
                    gather(step + 1, 1 - slot).start()

                pltpu.sync_copy(rows_v.at[slot], dst_hbm.at[pl.ds(base + step * win, win)])

    return gather_kernel(src, idx)


def _expert_kernel(blk_exp_ref, n_used_ref, x_ref, wgu_ref, bgu_ref, wd_ref, bd_ref, o_ref, wgu_bf, wd_bf):
    i = pl.program_id(0)

    @pl.when(jnp.logical_or(i == 0, blk_exp_ref[i] != blk_exp_ref[jnp.maximum(i - 1, 0)]))
    def _():
        wgu_bf[...] = wgu_ref[0, 0].astype(BF16)
        wd_bf[...] = wd_ref[0, 0].astype(BF16)

    @pl.when(i < n_used_ref[0])
    def _():
        x_lo, x_hi = _unpack_bf16_pairs(x_ref[...])
        hgu = (_dot(x_lo.astype(BF16), wgu_bf[:HALF_D, :]) + _dot(x_hi.astype(BF16), wgu_bf[HALF_D:, :])
               + bgu_ref[0])
        gate = jnp.minimum(hgu[:, :D_EXPERT], SWIGLU_LIMIT)
        up = jnp.clip(hgu[:, D_EXPERT:], -SWIGLU_LIMIT, SWIGLU_LIMIT)
        act = gate * jax.nn.sigmoid(gate * SWIGLU_ALPHA) * (up + 1.0)
        o_ref[...] = _pack_bf16_pairs(_dot(act.astype(BF16), wd_bf[...]) + bd_ref[0])

    @pl.when(i >= n_used_ref[0])
    def _():
        o_ref[...] = jnp.zeros_like(o_ref)


def _expert_mlp(xs, block_exp, n_used, layer, w_gu, b_gu, w_down, b_down):
    n_rows = xs.shape[0]
    bm = MOE_BLOCK_ROWS
    row = lambda i, be, nu: (i, 0)
    exp3 = lambda i, be, nu: (be[i], 0, 0)
    exp4 = lambda i, be, nu: (layer, be[i], 0, 0)
    grid_spec = pltpu.PrefetchScalarGridSpec(
        num_scalar_prefetch=2,
        grid=(n_rows // bm,),
        in_specs=[pl.BlockSpec((bm, HALF_D), row),
                  pl.BlockSpec((1, 1, D_MODEL, 2 * D_EXPERT), exp4), pl.BlockSpec((1, 1, 2 * D_EXPERT), exp3),
                  pl.BlockSpec((1, 1, D_EXPERT, D_MODEL), exp4), pl.BlockSpec((1, 1, D_MODEL), exp3)],
        out_specs=pl.BlockSpec((bm, HALF_D), row),
        scratch_shapes=[pltpu.VMEM((D_MODEL, 2 * D_EXPERT), BF16), pltpu.VMEM((D_EXPERT, D_MODEL), BF16)],
    )
    return pl.pallas_call(
        _expert_kernel,
        grid_spec=grid_spec,
        out_shape=jax.ShapeDtypeStruct((n_rows, HALF_D), I32),
        compiler_params=_cparams("arbitrary"),
        name="moe_experts",
    )(block_exp, n_used, xs, w_gu, b_gu.reshape(N_EXPERTS, 1, 2 * D_EXPERT),
      w_down, b_down.reshape(N_EXPERTS, 1, D_MODEL))


def _combine_kernel(y0_ref, y1_ref, y2_ref, y3_ref, gate_ref, res_ref, g_ref, b_ref, o_ref):
    gates = gate_ref[...]
    acc_lo, acc_hi = None, None
    for k, y_ref in enumerate((y0_ref, y1_ref, y2_ref, y3_ref)):
        lo, hi = _unpack_bf16_pairs(y_ref[...])
        gk = gates[:, k:k + 1]
        acc_lo = lo * gk if acc_lo is None else acc_lo + lo * gk
        acc_hi = hi * gk if acc_hi is None else acc_hi + hi * gk
    acc = jnp.concatenate([acc_lo, acc_hi], axis=1)
    o_ref[...] = _layer_norm_rows(DN_ALPHA * res_ref[...] + acc, g_ref[...], b_ref[...])


def _moe_block(h2d, h_packed, layer, w_router, b_router, w_gu, b_gu, w_down, b_down, g, b):
    n = h2d.shape[0]
    n_slots = n * TOP_K
    bm = MOE_BLOCK_ROWS
    top_idx, gates, rank, totals = _router(h2d, w_router, b_router)

    e_flat = top_idx.reshape(-1)
    assert N_EXPERTS * n_slots < 2 ** 31
    order = jnp.sort(e_flat * n_slots + jnp.arange(n_slots, dtype=I32)) % n_slots
    counts = totals[0, :N_EXPERTS].astype(I32)
    padded = (counts + bm - 1) // bm * bm
    start = jnp.cumsum(counts) - counts
    ends_p = jnp.cumsum(padded)
    pstart = ends_p - padded
    n_rows = n_slots + N_EXPERTS * bm
    n_blocks = n_rows // bm
    r = jnp.arange(n_rows, dtype=I32)
    e_r = jnp.minimum(jnp.searchsorted(ends_p, r, side="right"), N_EXPERTS - 1).astype(I32)
    j = r - pstart[e_r]
    valid = j < counts[e_r]
    slot_of_row = order[jnp.where(valid, start[e_r] + j, 0)]
    rows_tok = jnp.where(valid, slot_of_row // TOP_K, r % n).astype(I32)
    slot_pos = pstart[e_flat] + rank.reshape(-1)
    block_exp = e_r[::bm]
    n_used = (ends_p[-1] // bm).astype(I32).reshape(1)

    xs = _gather_rows(h_packed, rows_tok)
    ys = _expert_mlp(xs, block_exp, n_used, layer, w_gu, b_gu, w_down, b_down)
    yk = _gather_rows(ys, slot_pos.reshape(n, TOP_K).T.reshape(-1))

    tm = ROW_TILE
    row = lambda i: (i, 0)
    const = lambda i: (0, 0)
    choice = lambda k: (lambda i: (k * (n // tm) + i, 0))
    return pl.pallas_call(
        _combine_kernel,
        grid=(n // tm,),
        in_specs=[pl.BlockSpec((tm, HALF_D), choice(k)) for k in range(TOP_K)] + [
                  pl.BlockSpec((tm, TOP_K), row),
                  pl.BlockSpec((tm, D_MODEL), row), pl.BlockSpec((1, D_MODEL), const),
                  pl.BlockSpec((1, D_MODEL), const)],
        out_specs=pl.BlockSpec((tm, D_MODEL), row),
        out_shape=jax.ShapeDtypeStruct((n, D_MODEL), F32),
        compiler_params=_cparams("parallel"),
        name="moe_combine",
    )(yk, yk, yk, yk, gates, h2d, g.reshape(1, D_MODEL), b.reshape(1, D_MODEL))


def _s5_kernel(u_ref, bre_ref, bim_ref, cre_ref, cim_ref, are_ref, aim_ref, d_ref, y_ref,
               bu_re, bu_im, st_re, st_im, h_re, h_im, *, bsz):
    @pl.when(pl.program_id(0) == 0)
    def _():
        h_re[...] = jnp.zeros_like(h_re)
        h_im[...] = jnp.zeros_like(h_im)

    rows = u_ref.shape[0]
    first = lax.broadcasted_iota(I32, (SUBLANES, S5_ST_BLK), 0) < bsz
    for j in range(S5_LANE_BLOCKS):
        cin = slice(j * S5_IN_BLK, (j + 1) * S5_IN_BLK)
        cst = slice(j * S5_ST_BLK, (j + 1) * S5_ST_BLK)
        uj = u_ref[:, cin]
        ujb = uj.astype(BF16)
        bu_re[...] = _dot(ujb, bre_ref[j])
        bu_im[...] = _dot(ujb, bim_ref[j])
        ar = jnp.broadcast_to(are_ref[:, cst], (SUBLANES, S5_ST_BLK))
        ai = jnp.broadcast_to(aim_ref[:, cst], (SUBLANES, S5_ST_BLK))

        def step(i, carry):
            hr, hi = carry
            r0 = pl.multiple_of(i * SUBLANES, SUBLANES)
            vr = bu_re[pl.ds(r0, SUBLANES), :]
            vi = bu_im[pl.ds(r0, SUBLANES), :]
            h1r = ar * hr - ai * hi + vr
            h1i = ar * hi + ai * hr + vi
            h1rs = pltpu.roll(h1r, bsz, 0)
            h1is = pltpu.roll(h1i, bsz, 0)
            h2r = ar * h1rs - ai * h1is + vr
            h2i = ar * h1is + ai * h1rs + vi
            st_re[pl.ds(r0, SUBLANES), :] = jnp.where(first, h1r, h2r)
            st_im[pl.ds(r0, SUBLANES), :] = jnp.where(first, h1i, h2i)
            return pltpu.roll(h2r, bsz, 0), pltpu.roll(h2i, bsz, 0)

        hr, hi = lax.fori_loop(0, rows // SUBLANES, step, (h_re[:, cst], h_im[:, cst]))
        h_re[:, cst] = hr
        h_im[:, cst] = hi
        yj = _dot(st_re[...].astype(BF16), cre_ref[j]) + _dot(st_im[...].astype(BF16), cim_ref[j])
        yj = yj + d_ref[:, cin] * uj
        y_ref[:, cin] = jax.nn.gelu(yj).astype(BF16)


def _s5_block_diag(w, n_in, n_out):
    gpb = SSM_GROUPS // S5_LANE_BLOCKS
    w4 = w.reshape(S5_LANE_BLOCKS, gpb, n_in, n_out)
    eye = jnp.eye(gpb, dtype=w.dtype)
    return jnp.einsum("jgio,gh->jgiho", w4, eye).reshape(S5_LANE_BLOCKS, gpb * n_in, gpb * n_out)


def _s5_mixer_block(h2d, bsz, seq, w_in, log_dt, lam_re, lam_im, b_re, b_im, c_re, c_im, d, w_out, g, b):
    assert 2 * bsz == SUBLANES, "the scan packs two time steps of bsz rows into one 8-row tile"
    tm = ROW_TILE
    per_seq = seq // tm
    u_t = _matmul(h2d, w_in.astype(BF16), tm=tm, out_dtype=F32, grid=(bsz, per_seq),
                  x_map=lambda bb, i: (bb * per_seq + i, 0), out_map=lambda bb, i: (i, bb),
                  out_shape=(seq, bsz * D_MODEL)).reshape(seq * bsz, D_MODEL)

    dt = jnp.exp(log_dt)[:, None]
    mag = jnp.exp(lam_re * dt)
    a_re, a_im = mag * jnp.cos(lam_im * dt), mag * jnp.sin(lam_im * dt)
    den = lam_re * lam_re + lam_im * lam_im
    coef_re = ((a_re - 1.0) * lam_re + a_im * lam_im) / den
    coef_im = (a_im * lam_re - (a_re - 1.0) * lam_im) / den
    bb_re = coef_re[..., None] * b_re - coef_im[..., None] * b_im
    bb_im = coef_re[..., None] * b_im + coef_im[..., None] * b_re
    bre = _s5_block_diag(jnp.swapaxes(bb_re, 1, 2), SSM_GROUP, SSM_STATE).astype(BF16)
    bim = _s5_block_diag(jnp.swapaxes(bb_im, 1, 2), SSM_GROUP, SSM_STATE).astype(BF16)
    cre = _s5_block_diag(jnp.swapaxes(c_re, 1, 2), SSM_STATE, SSM_GROUP).astype(BF16)
    cim = _s5_block_diag(jnp.swapaxes(-c_im, 1, 2), SSM_STATE, SSM_GROUP).astype(BF16)
    n_state = SSM_GROUPS * SSM_STATE

    rows = S5_CHUNK * bsz
    row = lambda c: (c, 0)
    c2 = lambda c: (0, 0)
    c3 = lambda c: (0, 0, 0)
    y_t = pl.pallas_call(
        functools.partial(_s5_kernel, bsz=bsz),
        grid=(seq // S5_CHUNK,),
        in_specs=[pl.BlockSpec((rows, D_MODEL), row),
                  pl.BlockSpec(bre.shape, c3), pl.BlockSpec(bim.shape, c3),
                  pl.BlockSpec(cre.shape, c3), pl.BlockSpec(cim.shape, c3),
                  pl.BlockSpec((1, n_state), c2), pl.BlockSpec((1, n_state), c2), pl.BlockSpec((1, D_MODEL), c2)],
        out_specs=pl.BlockSpec((rows, D_MODEL), row),
        out_shape=jax.ShapeDtypeStruct((seq * bsz, D_MODEL), BF16),
        scratch_shapes=[pltpu.VMEM((rows, S5_ST_BLK), F32)] * 4 + [pltpu.VMEM((SUBLANES, n_state), F32)] * 2,
        compiler_params=_cparams("arbitrary"),
        name="s5_scan",
    )(u_t, bre, bim, cre, cim, a_re.reshape(1, n_state), a_im.reshape(1, n_state), d.reshape(1, D_MODEL))

    y2 = y_t.reshape(seq, bsz * D_MODEL)
    return _linear_residual_ln(
        [y2], [w_out.astype(BF16)], h2d, g, b, tm=tm, glu=True, grid=(bsz, per_seq),
        x_maps=[lambda bb, i: (i, bb)], res_map=lambda bb, i: (bb * per_seq + i, 0))


def _even_mixer_block(h2d, bsz, seq, w_in, qnorm_g, w_uq, w_uq_idx, kidx_g, kidx_b, w_out, g, b):
    qa, qi, ka, va, ki, wi, qb, kb, vb = _even_proj(h2d, bsz, seq, w_in, qnorm_g, w_uq, w_uq_idx, kidx_g, kidx_b)
    o_a = _dsa_attention(qa, qi, wi, ki, ka, va, bsz, seq)
    o_b = _stick_breaking(qb, kb, vb, bsz, seq)
    w_out = w_out.astype(BF16)
    return _linear_residual_ln([o_a, o_b], [w_out[:A_WIDTH], w_out[A_WIDTH:]], h2d, g, b, tm=ROW_TILE)


def kernel(x, mem, ev_w_in, ev_qnorm_g, ev_w_uq, ev_w_uq_idx, ev_kidx_ln_g, ev_kidx_ln_b, ev_w_out, od_w_in, od_log_dt, od_lambda_re, od_lambda_im, od_b_re, od_b_im, od_c_re, od_c_im, od_d, od_w_out, mix_ln_g, mix_ln_b, xa_w_q, xa_w_kv, xa_w_o, xa_ln_g, xa_ln_b, moe_w_router, moe_b_router, moe_w_gu, moe_b_gu, moe_w_down, moe_b_down, ffn_ln_g, ffn_ln_b):
    bsz, seq, _ = x.shape
    h = x.reshape(bsz * seq, D_MODEL)
    for layer in range(DEPTH):
        j = layer // 2
        if layer % 2 == 0:
            h = _even_mixer_block(h, bsz, seq, ev_w_in[j], ev_qnorm_g[j], ev_w_uq[j], ev_w_uq_idx[j],
                                  ev_kidx_ln_g[j], ev_kidx_ln_b[j], ev_w_out[j], mix_ln_g[layer], mix_ln_b[layer])
        else:
            h = _s5_mixer_block(h, bsz, seq, od_w_in[j], od_log_dt[j], od_lambda_re[j], od_lambda_im[j],
                                od_b_re[j], od_b_im[j], od_c_re[j], od_c_im[j], od_d[j], od_w_out[j],
                                mix_ln_g[layer], mix_ln_b[layer])
        h, h_packed = _cross_attention_block(h, mem, bsz, seq, xa_w_q[layer], xa_w_kv[layer], xa_w_o[layer],
                                             xa_ln_g[layer], xa_ln_b[layer])
        h = _moe_block(h, h_packed, layer, moe_w_router[layer], moe_b_router[layer], moe_w_gu, moe_b_gu[layer],
                       moe_w_down, moe_b_down[layer], ffn_ln_g[layer], ffn_ln_b[layer])
    return h.reshape(bsz, seq, D_MODEL)
```

```python
import functools
import math

import jax
import jax.numpy as jnp
from jax import lax
from jax.experimental import pallas as pl
from jax.experimental.pallas import tpu as pltpu
from jax.experimental.pallas import tpu_sc as plsc

F32 = jnp.float32
BF16 = jnp.bfloat16
I32 = jnp.int32

D_MODEL = 1024
DEPTH = 2
HEAD_DIM = 64
A_HEADS = 8
A_KV_HEADS = 2
A_REP = A_HEADS // A_KV_HEADS
Q_RANK = 256
IDX_HEADS = 8
IDX_DIM = 64
IDX_TOPK = 256
B_HEADS = 8
A_WIDTH = A_HEADS * HEAD_DIM
B_WIDTH = B_HEADS * HEAD_DIM
SSM_GROUP = 16
SSM_GROUPS = D_MODEL // SSM_GROUP
SSM_STATE = 64
XA_HEADS = 4
XA_HEAD_DIM = D_MODEL // XA_HEADS
N_EXPERTS = 32
TOP_K = 4
D_EXPERT = D_MODEL
SWIGLU_LIMIT = 7.0
SWIGLU_ALPHA = 1.702
ROPE_THETA = 500000.0
ROPE_HALF = HEAD_DIM // 8
LN_EPS = 1e-5
DN_ALPHA = (2 * DEPTH) ** 0.25

LANES = 128
SUBLANES = 8
VMEM_LIMIT_BYTES = 56 * 1024 * 1024

Q_BLOCK = 256
DSA_KEY_TILE = 512
DSA_ATT_TILE = 1024
DSA_ATT_PAR = 1
DSA_COUNT_ROWS = 8 * SUBLANES
SB_KEY_TILE = 256
ROW_TILE = 512
MOE_BLOCK_ROWS = 512
SC_GATHER_SLOT_BYTES = 128 * 1024
S5_CHUNK = 256
S5_LANE_BLOCKS = 4
S5_IN_BLK = D_MODEL // S5_LANE_BLOCKS
S5_ST_BLK = SSM_GROUPS * SSM_STATE // S5_LANE_BLOCKS

SB_EXIT_LOG = -104.0
NEG_BIG = -1e30
INT_MIN = -(2 ** 31)


def _cparams(*sem):
    return pltpu.CompilerParams(dimension_semantics=sem, vmem_limit_bytes=VMEM_LIMIT_BYTES)


def _dot(a, b):
    return jnp.dot(a, b, preferred_element_type=F32)


def _dot_nt(a, b):
    return lax.dot_general(a, b, (((1,), (1,)), ((), ())), preferred_element_type=F32)


def _layer_norm_rows(y, g, b):
    mu = jnp.mean(y, axis=-1, keepdims=True)
    d = y - mu
    var = jnp.mean(d * d, axis=-1, keepdims=True)
    return d * lax.rsqrt(var + LN_EPS) * g + b


def _mm_kernel(x_ref, w_ref, o_ref):
    o_ref[...] = _dot(x_ref[...].astype(BF16), w_ref[...]).astype(o_ref.dtype)


def _matmul(x, w, *, tm, out_dtype, x_map=None, out_map=None, grid=None, out_shape=None):
    m, k = x.shape
    n = w.shape[1]
    grid = grid or (m // tm,)
    x_map = x_map or (lambda i: (i, 0))
    out_map = out_map or (lambda i: (i, 0))
    out_shape = out_shape or (m, n)
    return pl.pallas_call(
        _mm_kernel,
        grid=grid,
        in_specs=[pl.BlockSpec((tm, k), x_map), pl.BlockSpec((k, n), lambda *a: (0, 0))],
        out_specs=pl.BlockSpec((tm, n), out_map),
        out_shape=jax.ShapeDtypeStruct(out_shape, out_dtype),
        compiler_params=_cparams(*(("parallel",) * len(grid))),
        name="matmul",
    )(x, w)


def _lin_ln_kernel(*refs, n_in, glu):
    xs, ws = refs[:n_in], refs[n_in:2 * n_in]
    res_ref, g_ref, b_ref, o_ref = refs[2 * n_in:]
    acc = _dot(xs[0][...].astype(BF16), ws[0][...])
    for x_ref, w_ref in zip(xs[1:], ws[1:]):
        acc = acc + _dot(x_ref[...].astype(BF16), w_ref[...])
    if glu:
        acc = acc[:, :D_MODEL] * jax.nn.sigmoid(acc[:, D_MODEL:])
    y = DN_ALPHA * res_ref[...] + acc
    o_ref[...] = _layer_norm_rows(y, g_ref[...], b_ref[...])


def _linear_residual_ln(xs, ws, res, g, b, *, tm, glu=False, grid=None, x_maps=None, res_map=None):
    n_rows = res.shape[0]
    grid = grid or (n_rows // tm,)
    x_maps = x_maps or [lambda i: (i, 0)] * len(xs)
    res_map = res_map or (lambda i: (i, 0))
    const = lambda *a: (0, 0)
    in_specs = [pl.BlockSpec((tm, w.shape[0]), m) for w, m in zip(ws, x_maps)]
    in_specs += [pl.BlockSpec(w.shape, const) for w in ws]
    in_specs += [pl.BlockSpec((tm, D_MODEL), res_map), pl.BlockSpec((1, D_MODEL), const),
                 pl.BlockSpec((1, D_MODEL), const)]
    return pl.pallas_call(
        functools.partial(_lin_ln_kernel, n_in=len(xs), glu=glu),
        grid=grid,
        in_specs=in_specs,
        out_specs=pl.BlockSpec((tm, D_MODEL), res_map),
        out_shape=jax.ShapeDtypeStruct((n_rows, D_MODEL), F32),
        compiler_params=_cparams(*(("parallel",) * len(grid))),
        name="linear_residual_ln",
    )(*xs, *ws, res, g.reshape(1, D_MODEL), b.reshape(1, D_MODEL))


_EV_CQ, _EV_KA, _EV_VA, _EV_KI, _EV_QB = 0, 256, 384, 512, 640
_EV_KB = _EV_QB + B_WIDTH
_EV_VB = _EV_KB + B_WIDTH
_EV_COLS = _EV_VB + B_WIDTH


def _rope_tables(seq):
    inv = ROPE_THETA ** (-jnp.arange(ROPE_HALF, dtype=F32) / ROPE_HALF)
    ang = jnp.arange(seq, dtype=F32)[:, None] * inv[None, :]
    cos, sin = jnp.cos(ang), jnp.sin(ang)
    rest = HEAD_DIM - 2 * ROPE_HALF
    zh = jnp.zeros((seq, ROPE_HALF), F32)
    c = jnp.concatenate([cos, cos, jnp.ones((seq, rest), F32)], axis=1)
    s1 = jnp.concatenate([-sin, zh, jnp.zeros((seq, rest), F32)], axis=1)
    s2 = jnp.concatenate([zh, sin, jnp.zeros((seq, rest), F32)], axis=1)
    rep = LANES // HEAD_DIM
    return jnp.tile(c, (1, rep)), jnp.tile(s1, (1, rep)), jnp.tile(s2, (1, rep))


def _even_proj_kernel(x_ref, w_ref, qg_ref, wuq_ref, wuqi_ref, lg_ref, lb_ref, c_ref, s1_ref, s2_ref,
                      qa_ref, qi_ref, ka_ref, va_ref, ki_ref, wi_ref, qb_ref, kb_ref, vb_ref):
    p = _dot(x_ref[...].astype(BF16), w_ref[...])
    c, s1, s2 = c_ref[...], s1_ref[...], s2_ref[...]

    def rope(t):
        return (t * c + pltpu.roll(t, LANES - ROPE_HALF, 1) * s1 + pltpu.roll(t, ROPE_HALF, 1) * s2)

    cq = p[:, _EV_CQ:_EV_CQ + Q_RANK]
    cn = cq * lax.rsqrt(jnp.mean(cq * cq, axis=-1, keepdims=True) + LN_EPS) * qg_ref[...]
    cnb = cn.astype(BF16)
    qa = _dot(cnb, wuq_ref[...])
    qi = _dot(cnb, wuqi_ref[...])
    low = lax.broadcasted_iota(I32, c.shape, 1) < HEAD_DIM
    for j in range(A_WIDTH // LANES):
        sl = slice(j * LANES, (j + 1) * LANES)
        pair = rope(qa[:, sl]) * (HEAD_DIM ** -0.5)
        for e, src in enumerate((pair, pltpu.roll(pair, HEAD_DIM, 1))):
            h = 2 * j + e
            qa_ref[:, h * LANES:(h + 1) * LANES] = jnp.where(low, src, 0.0).astype(BF16)
        qi_ref[:, sl] = (rope(qi[:, sl]) * (IDX_DIM ** -0.5)).astype(BF16)
    kpair = rope(p[:, _EV_KA:_EV_KA + LANES])
    vpair = p[:, _EV_VA:_EV_VA + LANES]
    v_pad = jnp.where(lax.broadcasted_iota(I32, c.shape, 1) == HEAD_DIM, 1.0, 0.0)
    for g, (ks, vs) in enumerate(((kpair, vpair), (pltpu.roll(kpair, HEAD_DIM, 1), pltpu.roll(vpair, HEAD_DIM, 1)))):
        ka_ref[:, g * LANES:(g + 1) * LANES] = jnp.where(low, ks, 0.0).astype(BF16)
        va_ref[:, g * LANES:(g + 1) * LANES] = jnp.where(low, vs, v_pad).astype(BF16)

    t = p[:, _EV_KI:_EV_KI + LANES]
    lane = lax.broadcasted_iota(I32, t.shape, 1)
    is_k = lane < IDX_DIM
    mu = jnp.sum(jnp.where(is_k, t, 0.0), axis=-1, keepdims=True) * (1.0 / IDX_DIM)
    d = jnp.where(is_k, t - mu, 0.0)
    var = jnp.sum(d * d, axis=-1, keepdims=True) * (1.0 / IDX_DIM)
    kin = d * lax.rsqrt(var + LN_EPS) * lg_ref[...] + lb_ref[...]
    ki_ref[...] = rope(kin)[:, :IDX_DIM].astype(BF16)
    wi_ref[...] = t[:, IDX_DIM:IDX_DIM + IDX_HEADS] * (IDX_HEADS ** -0.5)

    qb_ref[...] = (p[:, _EV_QB:_EV_KB] * (HEAD_DIM ** -0.5)).astype(BF16)
    kb_ref[...] = p[:, _EV_KB:_EV_VB].astype(BF16)
    vb_ref[...] = p[:, _EV_VB:_EV_COLS].astype(BF16)


def _even_proj(x2d, bsz, seq, w_in, qnorm_g, w_uq, w_uq_idx, kidx_g, kidx_b):
    n = x2d.shape[0]
    tm = ROW_TILE
    per_seq = seq // tm
    c0 = Q_RANK + 2 * A_KV_HEADS * HEAD_DIM + IDX_DIM + IDX_HEADS
    w_pack = jnp.concatenate(
        [w_in[:, :c0], jnp.zeros((D_MODEL, _EV_QB - c0), w_in.dtype), w_in[:, c0:]], axis=1).astype(BF16)
    pad = LANES - IDX_DIM
    lg = jnp.concatenate([kidx_g, jnp.zeros((pad,), F32)]).reshape(1, LANES)
    lb = jnp.concatenate([kidx_b, jnp.zeros((pad,), F32)]).reshape(1, LANES)
    c, s1, s2 = _rope_tables(seq)
    row = lambda i: (i, 0)
    const = lambda i: (0, 0)
    pos = lambda i: (i % per_seq, 0)
    head_shape = jax.ShapeDtypeStruct((n, B_WIDTH), BF16)
    head_spec = pl.BlockSpec((tm, B_WIDTH), row)
    return pl.pallas_call(
        _even_proj_kernel,
        grid=(n // tm,),
        in_specs=[pl.BlockSpec((tm, D_MODEL), row), pl.BlockSpec((D_MODEL, _EV_COLS), const),
                  pl.BlockSpec((1, Q_RANK), const), pl.BlockSpec((Q_RANK, A_WIDTH), const),
                  pl.BlockSpec((Q_RANK, IDX_HEADS * IDX_DIM), const),
                  pl.BlockSpec((1, LANES), const), pl.BlockSpec((1, LANES), const),
                  pl.BlockSpec((tm, LANES), pos), pl.BlockSpec((tm, LANES), pos), pl.BlockSpec((tm, LANES), pos)],
        out_specs=[pl.BlockSpec((tm, A_HEADS * LANES), row), pl.BlockSpec((tm, IDX_HEADS * IDX_DIM), row),
                   pl.BlockSpec((tm, A_KV_HEADS * LANES), row), pl.BlockSpec((tm, A_KV_HEADS * LANES), row),
                   pl.BlockSpec((tm, IDX_DIM), row), pl.BlockSpec((tm, IDX_HEADS), row),
                   head_spec, head_spec, head_spec],
        out_shape=[jax.ShapeDtypeStruct((n, A_HEADS * LANES), BF16), jax.ShapeDtypeStruct((n, IDX_HEADS * IDX_DIM), BF16),
                   jax.ShapeDtypeStruct((n, A_KV_HEADS * LANES), BF16),
                   jax.ShapeDtypeStruct((n, A_KV_HEADS * LANES), BF16),
                   jax.ShapeDtypeStruct((n, IDX_DIM), BF16), jax.ShapeDtypeStruct((n, IDX_HEADS), F32),
                   head_shape, head_shape, head_shape],
        compiler_params=_cparams("parallel"),
        name="even_proj",
    )(x2d, w_pack, qnorm_g.reshape(1, Q_RANK), w_uq.astype(BF16), w_uq_idx.astype(BF16), lg, lb, c, s1, s2)


def _key_to_float(key):
    bits = key ^ ((key >> 31) & jnp.int32(0x7FFFFFFF))
    return lax.bitcast_convert_type(bits, F32)


def _high_half(x):
    bits = lax.bitcast_convert_type(x, jnp.uint32) & jnp.uint32(0xFFFF0000)
    return lax.bitcast_convert_type(bits, F32).astype(BF16)


def _dsa_kernel(qa_ref, qi_ref, wit_ref, ki_ref, ka_ref, vat_ref, o_ref, sc_scr, hi_scr, *, topk, ts, ta):
    seq = sc_scr.shape[0]
    qb = pl.program_id(1)
    q0 = qb * Q_BLOCK
    nkt = (q0 + Q_BLOCK - 1) // ts + 1
    t_row = q0 + lax.broadcasted_iota(I32, (1, Q_BLOCK), 1)
    key = lax.broadcasted_iota(I32, (ts, Q_BLOCK), 0)
    kf = jnp.float32(topk)

    qi = qi_ref[0]
    qs = jnp.concatenate([qi[:, h * IDX_DIM:(h + 1) * IDX_DIM] for h in range(IDX_HEADS)], axis=0)
    wit = wit_ref[0]

    def score_tile(kt, carry):
        off = pl.multiple_of(kt * ts, ts)
        s_all = _dot_nt(ki_ref[0, pl.ds(off, ts), :], qs)
        acc = jnp.zeros((ts, Q_BLOCK), F32)
        for h in range(IDX_HEADS):
            acc = acc + jnp.maximum(s_all[:, h * Q_BLOCK:(h + 1) * Q_BLOCK], 0.0) * wit[h:h + 1, :]
        val = jnp.where(off + key <= t_row, acc, -jnp.inf)
        sc_scr[pl.ds(off, ts), :] = val
        hi_scr[pl.ds(off, ts), :] = _high_half(val)
        return carry

    lax.fori_loop(0, nkt, score_tile, 0)

    def count(pred):
        def body(kt, acc):
            off = pl.multiple_of(kt * ts, ts)
            ind = pred(sc_scr[pl.ds(off, ts), :], off + key)
            return acc + jnp.sum(ind.reshape(ts // DSA_COUNT_ROWS, DSA_COUNT_ROWS, Q_BLOCK), axis=0)
        acc = lax.fori_loop(0, nkt, body, jnp.zeros((DSA_COUNT_ROWS, Q_BLOCK), F32))
        return jnp.sum(acc, axis=0, keepdims=True)

    one_h, zero_h = jnp.ones((), BF16), jnp.zeros((), BF16)

    def count_high(c_hi):
        def body(kt, acc):
            off = pl.multiple_of(kt * ts, ts)
            ind = jnp.where(hi_scr[pl.ds(off, ts), :] >= c_hi, one_h, zero_h)
            part = ind[:DSA_COUNT_ROWS]
            for j in range(1, ts // DSA_COUNT_ROWS):
                part = part + ind[j * DSA_COUNT_ROWS:(j + 1) * DSA_COUNT_ROWS]
            return acc + part.astype(F32)
        acc = lax.fori_loop(0, nkt, body, jnp.zeros((DSA_COUNT_ROWS, Q_BLOCK), F32))
        return jnp.sum(acc, axis=0, keepdims=True)

    def high_step(i, base):
        cand = base + jnp.left_shift(jnp.int32(1), 31 - i)
        cnt = count_high(_high_half(_key_to_float(cand)))
        return jnp.where(cnt >= kf, cand, base)

    def bit_step(i, base):
        cand = base + jnp.left_shift(jnp.int32(1), 31 - i)
        cf = _key_to_float(cand)
        cnt = count(lambda sc, idx: jnp.where(sc >= cf, 1.0, 0.0))
        return jnp.where(cnt >= kf, cand, base)

    base = lax.fori_loop(0, 16, high_step, jnp.full((1, Q_BLOCK), INT_MIN, I32))
    base = lax.fori_loop(16, 32, bit_step, base)
    thr = jnp.where(base == INT_MIN, -jnp.inf, _key_to_float(base))

    cnt_ge = count(lambda sc, idx: jnp.where(sc >= thr, 1.0, 0.0))
    tied = jnp.logical_and(cnt_ge > kf, thr > -jnp.inf)
    any_tied = jnp.max(jnp.where(tied, 1.0, 0.0)) > 0.0
    seq_bits = max(1, int(math.ceil(math.log2(seq))))

    def tie_cut():
        cnt_gt = count(lambda sc, idx: jnp.where(sc > thr, 1.0, 0.0))
        need = kf - cnt_gt

        def idx_step(i, pos):
            cand = pos + jnp.left_shift(jnp.int32(1), seq_bits - 1 - i)
            cnt = count(lambda sc, idx: jnp.where(sc == thr, jnp.where(idx < cand, 1.0, 0.0), 0.0))
            return jnp.where(cnt < need, cand, pos)

        return lax.fori_loop(0, seq_bits, idx_step, jnp.zeros((1, Q_BLOCK), I32))

    cut = lax.cond(any_tied, tie_cut, lambda: jnp.full((1, Q_BLOCK), seq, I32))
    cut = jnp.where(tied, cut, seq)

    nkt_a = (q0 + Q_BLOCK - 1) // ta + 1
    key_a = lax.broadcasted_iota(I32, (ta, Q_BLOCK), 0)
    cols = A_REP * Q_BLOCK
    qg = [jnp.concatenate([qa_ref[0, :, (g * A_REP + r) * LANES:(g * A_REP + r + 1) * LANES]
                           for r in range(A_REP)], axis=0) for g in range(A_KV_HEADS)]

    def att_step(i, carry):
        offs = [pl.multiple_of((DSA_ATT_PAR * i + e) * ta, ta) for e in range(DSA_ATT_PAR)]
        logits = [_dot_nt(ka_ref[0, pl.ds(offs[e], ta), g * LANES:(g + 1) * LANES], qg[g])
                  for e in range(DSA_ATT_PAR) for g in range(A_KV_HEADS)]
        out = []
        for e in range(DSA_ATT_PAR):
            sc = sc_scr[pl.ds(offs[e], ta), :]
            idx = offs[e] + key_a
            keep = jnp.where(sc > thr, 0.0, jnp.where(sc == thr, jnp.where(idx <= cut, 0.0, NEG_BIG), NEG_BIG))
            bias = jnp.where(idx <= t_row, keep, NEG_BIG)
            bias = jnp.concatenate([bias] * A_REP, axis=1)
            for g in range(A_KV_HEADS):
                m, acc = carry[e * A_KV_HEADS + g]
                s = logits[e * A_KV_HEADS + g] + bias
                m_new = jnp.maximum(m, jnp.max(s, axis=0, keepdims=True))
                p = jnp.exp(s - m_new)
                vt = vat_ref[0, g * LANES:(g + 1) * LANES, pl.ds(offs[e], ta)]
                out.append((m_new, jnp.exp(m - m_new) * acc + _dot(vt, p.astype(BF16))))
        return tuple(out)

    init = tuple((jnp.full((1, cols), NEG_BIG, F32), jnp.zeros((LANES, cols), F32))
                 for _ in range(DSA_ATT_PAR * A_KV_HEADS))
    final = lax.fori_loop(0, (nkt_a + DSA_ATT_PAR - 1) // DSA_ATT_PAR, att_step, init)
    low = lax.broadcasted_iota(I32, (Q_BLOCK, LANES), 1) < HEAD_DIM
    outs = []
    for g in range(A_KV_HEADS):
        m, acc = final[g]
        for e in range(1, DSA_ATT_PAR):
            m1, acc1 = final[e * A_KV_HEADS + g]
            m_all = jnp.maximum(m, m1)
            m, acc = m_all, jnp.exp(m - m_all) * acc + jnp.exp(m1 - m_all) * acc1
        og = acc / acc[HEAD_DIM:HEAD_DIM + 1, :]
        outs += [og[:, r * Q_BLOCK:(r + 1) * Q_BLOCK].T for r in range(A_REP)]
    for j in range(A_HEADS // 2):
        pair = jnp.where(low, outs[2 * j], pltpu.roll(outs[2 * j + 1], HEAD_DIM, 1))
        o_ref[0, :, j * LANES:(j + 1) * LANES] = pair.astype(BF16)


def _dsa_attention(qa, qi, wi, ki, ka, va, bsz, seq):
    topk = min(IDX_TOPK, seq // 4)
    ts = min(DSA_KEY_TILE, seq)
    ta = min(DSA_ATT_TILE, seq // DSA_ATT_PAR)
    assert seq % (DSA_ATT_PAR * ta) == 0 and seq % ts == 0, "the attention loop walks whole groups of key tiles"
    blk = lambda b, i: (b, i, 0)
    full = lambda b, i: (b, 0, 0)
    r3 = lambda a: a.reshape(bsz, seq, a.shape[-1])
    wit = jnp.swapaxes(r3(wi), 1, 2)
    vat = jnp.swapaxes(r3(va), 1, 2)
    return pl.pallas_call(
        functools.partial(_dsa_kernel, topk=topk, ts=ts, ta=ta),
        grid=(bsz, seq // Q_BLOCK),
        in_specs=[pl.BlockSpec((1, Q_BLOCK, A_HEADS * LANES), blk),
                  pl.BlockSpec((1, Q_BLOCK, IDX_HEADS * IDX_DIM), blk),
                  pl.BlockSpec((1, IDX_HEADS, Q_BLOCK), lambda b, i: (b, 0, i)),
                  pl.BlockSpec((1, seq, IDX_DIM), full),
                  pl.BlockSpec((1, seq, A_KV_HEADS * LANES), full),
                  pl.BlockSpec((1, A_KV_HEADS * LANES, seq), full)],
        out_specs=pl.BlockSpec((1, Q_BLOCK, A_WIDTH), blk),
        out_shape=jax.ShapeDtypeStruct((bsz, seq, A_WIDTH), BF16),
        scratch_shapes=[pltpu.VMEM((seq, Q_BLOCK), F32), pltpu.VMEM((seq, Q_BLOCK), BF16)],
        compiler_params=_cparams("parallel", "parallel"),
        name="dsa_attention",
    )(r3(qa), r3(qi), wit, r3(ki), r3(ka), vat).reshape(bsz * seq, A_WIDTH)


def _sb_kernel(q_ref, k_ref, v_ref, u_ref, o_ref, acc_scr, run_scr, *, tk):
    q0 = pl.program_id(1) * Q_BLOCK
    t_col = q0 + lax.broadcasted_iota(I32, (Q_BLOCK, 1), 0)
    lane = lax.broadcasted_iota(I32, (Q_BLOCK, tk), 1)
    low = lax.broadcasted_iota(I32, (Q_BLOCK, LANES), 1) < HEAD_DIM
    upper = u_ref[...]
    nkt = (q0 + Q_BLOCK - 1) // tk + 1
    q = q_ref[0]
    zero = jnp.zeros((Q_BLOCK, LANES), BF16)
    qm = []
    for p in range(B_HEADS // 2):
        pair = q[:, p * LANES:(p + 1) * LANES]
        qm.append(jnp.concatenate([jnp.where(low, pair, zero), jnp.where(low, zero, pair)], axis=0))
    acc_scr[...] = jnp.zeros_like(acc_scr)
    run_scr[...] = jnp.zeros_like(run_scr)

    def cond(carry):
        i, worst = carry
        return jnp.logical_and(i < nkt, worst >= SB_EXIT_LOG)

    def body(carry):
        i, _ = carry
        off = pl.multiple_of((nkt - 1 - i) * tk, tk)
        strict = off + lane < t_col
        strict = jnp.concatenate([strict, strict], axis=0)
        worst = None
        for p in range(B_HEADS // 2):
            cols = slice(p * LANES, (p + 1) * LANES)
            kp = k_ref[0, pl.ds(off, tk), cols]
            vp = v_ref[0, pl.ds(off, tk), cols]
            run = run_scr[p]
            z = _dot_nt(qm[p], kp)
            softplus = jnp.maximum(z, 0.0) + jnp.log(1.0 + jnp.exp(-jnp.abs(z)))
            log_1mb = jnp.where(strict, -softplus, 0.0)
            hi = log_1mb.astype(BF16)
            lo = (log_1mb - hi.astype(F32)).astype(BF16)
            after = _dot(hi, upper) + _dot(lo, upper) + run
            a = jnp.where(strict, jnp.exp(z - softplus + after), 0.0)
            out = _dot(a.astype(BF16), vp)
            run = run + jnp.sum(log_1mb, axis=1, keepdims=True)
            run_scr[p] = run
            worst = run if worst is None else jnp.maximum(worst, run)
            acc_scr[:, cols] += jnp.where(low, out[:Q_BLOCK], out[Q_BLOCK:])
        return i + 1, jnp.max(worst)

    lax.while_loop(cond, body, (jnp.int32(0), jnp.float32(0.0)))
    o_ref[0] = acc_scr[...].astype(BF16)


def _stick_breaking(qb, kb, vb, bsz, seq):
    tk = min(SB_KEY_TILE, seq)
    r = lax.broadcasted_iota(I32, (tk, tk), 0)
    c = lax.broadcasted_iota(I32, (tk, tk), 1)
    upper = jnp.where(r > c, 1.0, 0.0).astype(BF16)
    blk = lambda b, i: (b, i, 0)
    full = lambda b, i: (b, 0, 0)
    r3 = lambda a: a.reshape(bsz, seq, B_WIDTH)
    return pl.pallas_call(
        functools.partial(_sb_kernel, tk=tk),
        grid=(bsz, seq // Q_BLOCK),
        in_specs=[pl.BlockSpec((1, Q_BLOCK, B_WIDTH), blk), pl.BlockSpec((1, seq, B_WIDTH), full),
                  pl.BlockSpec((1, seq, B_WIDTH), full), pl.BlockSpec((tk, tk), lambda b, i: (0, 0))],
        out_specs=pl.BlockSpec((1, Q_BLOCK, B_WIDTH), blk),
        out_shape=jax.ShapeDtypeStruct((bsz, seq, B_WIDTH), BF16),
        scratch_shapes=[pltpu.VMEM((Q_BLOCK, B_WIDTH), F32), pltpu.VMEM((B_HEADS // 2, 2 * Q_BLOCK, 1), F32)],
        compiler_params=_cparams("parallel", "arbitrary"),
        name="stick_breaking",
    )(r3(qb), r3(kb), r3(vb), upper).reshape(bsz * seq, B_WIDTH)


HALF_D = D_MODEL // 2
U32 = jnp.uint32
HIGH16 = 0xFFFF0000


def _pack_bf16_pairs(x):
    def bits(v):
        return lax.bitcast_convert_type(v.astype(BF16).astype(F32), U32)
    word = (bits(x[:, HALF_D:]) & U32(HIGH16)) | (bits(x[:, :HALF_D]) >> 16)
    return lax.bitcast_convert_type(word, I32)


def _unpack_bf16_pairs(word):
    u = lax.bitcast_convert_type(word, U32)
    return lax.bitcast_convert_type(u << 16, F32), lax.bitcast_convert_type(u & U32(HIGH16), F32)


def _xattn_kernel(h_ref, wq_ref, kv_ref, wo_ref, g_ref, b_ref, o_ref, packed_ref):
    h = h_ref[...]
    q = (_dot(h.astype(BF16), wq_ref[...]) * (XA_HEAD_DIM ** -0.5)).astype(BF16)
    kv = kv_ref[0]
    outs = []
    for hd in range(XA_HEADS):
        sl = slice(hd * XA_HEAD_DIM, (hd + 1) * XA_HEAD_DIM)
        s = _dot_nt(q[:, sl], kv[:, sl])
        p = jnp.exp(s - jnp.max(s, axis=1, keepdims=True))
        vh = kv[:, D_MODEL + hd * XA_HEAD_DIM:D_MODEL + (hd + 1) * XA_HEAD_DIM]
        outs.append((_dot(p.astype(BF16), vh) / jnp.sum(p, axis=1, keepdims=True)).astype(BF16))
    y = _dot(jnp.concatenate(outs, axis=1), wo_ref[...])
    out = _layer_norm_rows(DN_ALPHA * h + y, g_ref[...], b_ref[...])
    o_ref[...] = out
    packed_ref[...] = _pack_bf16_pairs(out)


def _cross_attention_block(h2d, mem, bsz, seq, w_q, w_kv, w_o, g, b):
    tm = ROW_TILE
    per_seq = seq // tm
    mem_len = mem.shape[1]
    kv = _matmul(mem.reshape(bsz * mem_len, D_MODEL), w_kv.astype(BF16), tm=mem_len, out_dtype=BF16)
    kv = kv.reshape(bsz, mem_len, 2 * D_MODEL)
    row = lambda i: (i, 0)
    const = lambda i: (0, 0)
    return pl.pallas_call(
        _xattn_kernel,
        grid=(bsz * per_seq,),
        in_specs=[pl.BlockSpec((tm, D_MODEL), row), pl.BlockSpec((D_MODEL, D_MODEL), const),
                  pl.BlockSpec((1, mem_len, 2 * D_MODEL), lambda i: (i // per_seq, 0, 0)),
                  pl.BlockSpec((D_MODEL, D_MODEL), const),
                  pl.BlockSpec((1, D_MODEL), const), pl.BlockSpec((1, D_MODEL), const)],
        out_specs=[pl.BlockSpec((tm, D_MODEL), row), pl.BlockSpec((tm, HALF_D), row)],
        out_shape=[jax.ShapeDtypeStruct(h2d.shape, F32), jax.ShapeDtypeStruct((h2d.shape[0], HALF_D), I32)],
        compiler_params=_cparams("parallel"),
        name="cross_attention",
    )(h2d, w_q.astype(BF16), kv, w_o.astype(BF16), g.reshape(1, D_MODEL), b.reshape(1, D_MODEL))


def _router_kernel(h_ref, w_ref, b_ref, tri_ref, idx_ref, gate_ref, rank_ref, cnt_ref, run_scr):
    @pl.when(pl.program_id(0) == 0)
    def _():
        run_scr[...] = jnp.zeros_like(run_scr)

    h = h_ref[...]
    hh = h.astype(BF16)
    hl = (h - hh.astype(F32)).astype(BF16)
    w = w_ref[...]
    wh = w.astype(BF16)
    wl = (w - wh.astype(F32)).astype(BF16)
    logits = _dot(hh, wh) + _dot(hl, wh) + _dot(hh, wl) + b_ref[...]
    lane = lax.broadcasted_iota(I32, logits.shape, 1).astype(F32)
    vals, sels = [], []
    onehot = jnp.zeros(logits.shape, F32)
    for k in range(TOP_K):
        m = jnp.max(logits, axis=1, keepdims=True)
        sel = jnp.min(jnp.where(logits == m, lane, float(LANES)), axis=1, keepdims=True)
        idx_ref[:, k:k + 1] = sel.astype(I32)
        vals.append(m)
        sels.append(sel)
        onehot = onehot + jnp.where(lane == sel, 1.0, 0.0)
        logits = jnp.where(lane == sel, -jnp.inf, logits)
    es = [jnp.exp(v - vals[0]) for v in vals]
    tot = es[0] + es[1] + es[2] + es[3]
    for k in range(TOP_K):
        gate_ref[:, k:k + 1] = es[k] / tot

    earlier = _dot(tri_ref[...], onehot.astype(BF16)) + run_scr[...]
    for k in range(TOP_K):
        rank = jnp.sum(jnp.where(lane == sels[k], earlier, 0.0), axis=1, keepdims=True)
        rank_ref[:, k:k + 1] = rank.astype(I32)
    run = run_scr[...] + jnp.sum(onehot, axis=0, keepdims=True)
    run_scr[...] = run
    cnt_ref[...] = run


def _router(h2d, w_router, b_router):
    n = h2d.shape[0]
    tm = 2 * ROW_TILE
    pad = LANES - N_EXPERTS
    w = jnp.concatenate([w_router, jnp.zeros((D_MODEL, pad), F32)], axis=1)
    b = jnp.concatenate([b_router, jnp.full((pad,), NEG_BIG, F32)]).reshape(1, LANES)
    r = lax.broadcasted_iota(I32, (tm, tm), 0)
    c = lax.broadcasted_iota(I32, (tm, tm), 1)
    tri = jnp.where(c < r, 1.0, 0.0).astype(BF16)
    row = lambda i: (i, 0)
    const = lambda i: (0, 0)
    return pl.pallas_call(
        _router_kernel,
        grid=(n // tm,),
        in_specs=[pl.BlockSpec((tm, D_MODEL), row), pl.BlockSpec((D_MODEL, LANES), const),
                  pl.BlockSpec((1, LANES), const), pl.BlockSpec((tm, tm), const)],
        out_specs=[pl.BlockSpec((tm, TOP_K), row), pl.BlockSpec((tm, TOP_K), row),
                   pl.BlockSpec((tm, TOP_K), row), pl.BlockSpec((1, LANES), const)],
        out_shape=[jax.ShapeDtypeStruct((n, TOP_K), I32), jax.ShapeDtypeStruct((n, TOP_K), F32),
                   jax.ShapeDtypeStruct((n, TOP_K), I32), jax.ShapeDtypeStruct((1, LANES), F32)],
        scratch_shapes=[pltpu.VMEM((1, LANES), F32)],
        compiler_params=_cparams("arbitrary"),
        name="moe_router",
    )(h2d, w, b, tri)


def _gather_rows(src, idx):
    n_out = idx.shape[0]
    width = src.shape[1]
    win = SC_GATHER_SLOT_BYTES // (width * src.dtype.itemsize)
    mesh = plsc.VectorSubcoreMesh(core_axis_name="core", subcore_axis_name="subcore")
    n_workers = mesh.num_cores * mesh.num_subcores
    per_worker = n_out // n_workers
    steps = per_worker // win
    assert per_worker * n_workers == n_out and steps * win == per_worker and steps % 2 == 0

    @functools.partial(
        pl.kernel, out_type=jax.ShapeDtypeStruct((n_out, width), src.dtype), mesh=mesh,
        scratch_types=[pltpu.VMEM((per_worker,), I32), pltpu.VMEM((2, win, width), src.dtype),
                       pltpu.SemaphoreType.DMA, pltpu.SemaphoreType.DMA])
    def gather_kernel(src_hbm, idx_hbm, dst_hbm, idx_v, rows_v, sem0, sem1):
        worker = lax.axis_index("subcore") * mesh.num_cores + lax.axis_index("core")
        base = worker * per_worker
        sems = (sem0, sem1)
        pltpu.sync_copy(idx_hbm.at[pl.ds(base, per_worker)], idx_v)

        def gather(step, slot):
            return pltpu.make_async_copy(src_hbm.at[idx_v.at[pl.ds(step * win, win)]], rows_v.at[slot], sems[slot])

        gather(0, 0).start()

        @pl.loop(0, steps, step=2)
        def _(s):
            for slot in range(2):
                step = s + slot
                gather(step, slot).wait()

                @pl.when(step + 1 < steps)
                def _():
                    gather(step + 1, 1 - slot).start()

                pltpu.sync_copy(rows_v.at[slot], dst_hbm.at[pl.ds(base + step * win, win)])

    return gather_kernel(src, idx)


def _expert_kernel(blk_exp_ref, n_used_ref, x_ref, wgu_ref, bgu_ref, wd_ref, bd_ref, o_ref, wgu_bf, wd_bf):
    i = pl.program_id(0)

    @pl.when(jnp.logical_or(i == 0, blk_exp_ref[i] != blk_exp_ref[jnp.maximum(i - 1, 0)]))
    def _():
        wgu_bf[...] = wgu_ref[0, 0].astype(BF16)
        wd_bf[...] = wd_ref[0, 0].astype(BF16)

    @pl.when(i < n_used_ref[0])
    def _():
        x_lo, x_hi = _unpack_bf16_pairs(x_ref[...])
        hgu = (_dot(x_lo.astype(BF16), wgu_bf[:HALF_D, :]) + _dot(x_hi.astype(BF16), wgu_bf[HALF_D:, :])
               + bgu_ref[0])
        gate = jnp.minimum(hgu[:, :D_EXPERT], SWIGLU_LIMIT)
        up = jnp.clip(hgu[:, D_EXPERT:], -SWIGLU_LIMIT, SWIGLU_LIMIT)
        act = gate * jax.nn.sigmoid(gate * SWIGLU_ALPHA) * (up + 1.0)
        o_ref[...] = _pack_bf16_pairs(_dot(act.astype(BF16), wd_bf[...]) + bd_ref[0])

    @pl.when(i >= n_used_ref[0])
    def _():
        o_ref[...] = jnp.zeros_like(o_ref)


def _expert_mlp(xs, block_exp, n_used, layer, w_gu, b_gu, w_down, b_down):
    n_rows = xs.shape[0]
    bm = MOE_BLOCK_ROWS
    row = lambda i, be, nu: (i, 0)
    exp3 = lambda i, be, nu: (be[i], 0, 0)
    exp4 = lambda i, be, nu: (layer, be[i], 0, 0)
    grid_spec = pltpu.PrefetchScalarGridSpec(
        num_scalar_prefetch=2,
        grid=(n_rows // bm,),
        in_specs=[pl.BlockSpec((bm, HALF_D), row),
                  pl.BlockSpec((1, 1, D_MODEL, 2 * D_EXPERT), exp4), pl.BlockSpec((1, 1, 2 * D_EXPERT), exp3),
                  pl.BlockSpec((1, 1, D_EXPERT, D_MODEL), exp4), pl.BlockSpec((1, 1, D_MODEL), exp3)],
        out_specs=pl.BlockSpec((bm, HALF_D), row),
        scratch_shapes=[pltpu.VMEM((D_MODEL, 2 * D_EXPERT), BF16), pltpu.VMEM((D_EXPERT, D_MODEL), BF16)],
    )
    return pl.pallas_call(
        _expert_kernel,
        grid_spec=grid_spec,
        out_shape=jax.ShapeDtypeStruct((n_rows, HALF_D), I32),
        compiler_params=_cparams("arbitrary"),
        name="moe_experts",
    )(block_exp, n_used, xs, w_gu, b_gu.reshape(N_EXPERTS, 1, 2 * D_EXPERT),
      w_down, b_down.reshape(N_EXPERTS, 1, D_MODEL))


def _combine_kernel(y0_ref, y1_ref, y2_ref, y3_ref, gate_ref, res_ref, g_ref, b_ref, o_ref):
    gates = gate_ref[...]
    acc_lo, acc_hi = None, None
    for k, y_ref in enumerate((y0_ref, y1_ref, y2_ref, y3_ref)):
        lo, hi = _unpack_bf16_pairs(y_ref[...])
        gk = gates[:, k:k + 1]
        acc_lo = lo * gk if acc_lo is None else acc_lo + lo * gk
        acc_hi = hi * gk if acc_hi is None else acc_hi + hi * gk
    acc = jnp.concatenate([acc_lo, acc_hi], axis=1)
    o_ref[...] = _layer_norm_rows(DN_ALPHA * res_ref[...] + acc, g_ref[...], b_ref[...])


def _moe_block(h2d, h_packed, layer, w_router, b_router, w_gu, b_gu, w_down, b_down, g, b):
    n = h2d.shape[0]
    n_slots = n * TOP_K
    bm = MOE_BLOCK_ROWS
    top_idx, gates, rank, totals = _router(h2d, w_router, b_router)

    e_flat = top_idx.reshape(-1)
    assert N_EXPERTS * n_slots < 2 ** 31
    order = jnp.sort(e_flat * n_slots + jnp.arange(n_slots, dtype=I32)) % n_slots
    counts = totals[0, :N_EXPERTS].astype(I32)
    padded = (counts + bm - 1) // bm * bm
    start = jnp.cumsum(counts) - counts
    ends_p = jnp.cumsum(padded)
    pstart = ends_p - padded
    n_rows = n_slots + N_EXPERTS * bm
    n_blocks = n_rows // bm
    r = jnp.arange(n_rows, dtype=I32)
    e_r = jnp.minimum(jnp.searchsorted(ends_p, r, side="right", method="compare_all"), N_EXPERTS - 1).astype(I32)
    j = r - pstart[e_r]
    valid = j < counts[e_r]
    slot_of_row = order[jnp.where(valid, start[e_r] + j, 0)]
    rows_tok = jnp.where(valid, slot_of_row // TOP_K, r % n).astype(I32)
    slot_pos = pstart[e_flat] + rank.reshape(-1)
    block_exp = e_r[::bm]
    n_used = (ends_p[-1] // bm).astype(I32).reshape(1)

    xs = _gather_rows(h_packed, rows_tok)
    ys = _expert_mlp(xs, block_exp, n_used, layer, w_gu, b_gu, w_down, b_down)
    yk = _gather_rows(ys, slot_pos.reshape(n, TOP_K).T.reshape(-1))

    tm = ROW_TILE
    row = lambda i: (i, 0)
    const = lambda i: (0, 0)
    choice = lambda k: (lambda i: (k * (n // tm) + i, 0))
    return pl.pallas_call(
        _combine_kernel,
        grid=(n // tm,),
        in_specs=[pl.BlockSpec((tm, HALF_D), choice(k)) for k in range(TOP_K)] + [
                  pl.BlockSpec((tm, TOP_K), row),
                  pl.BlockSpec((tm, D_MODEL), row), pl.BlockSpec((1, D_MODEL), const),
                  pl.BlockSpec((1, D_MODEL), const)],
        out_specs=pl.BlockSpec((tm, D_MODEL), row),
        out_shape=jax.ShapeDtypeStruct((n, D_MODEL), F32),
        compiler_params=_cparams("parallel"),
        name="moe_combine",
    )(yk, yk, yk, yk, gates, h2d, g.reshape(1, D_MODEL), b.reshape(1, D_MODEL))


def _s5_kernel(u_ref, bre_ref, bim_ref, cre_ref, cim_ref, are_ref, aim_ref, d_ref, y_ref,
               bu_re, bu_im, st_re, st_im, h_re, h_im, *, bsz):
    @pl.when(pl.program_id(0) == 0)
    def _():
        h_re[...] = jnp.zeros_like(h_re)
        h_im[...] = jnp.zeros_like(h_im)

    rows = u_ref.shape[0]
    first = lax.broadcasted_iota(I32, (SUBLANES, S5_ST_BLK), 0) < bsz
    for j in range(S5_LANE_BLOCKS):
        cin = slice(j * S5_IN_BLK, (j + 1) * S5_IN_BLK)
        cst = slice(j * S5_ST_BLK, (j + 1) * S5_ST_BLK)
        uj = u_ref[:, cin]
        ujb = uj.astype(BF16)
        bu_re[...] = _dot(ujb, bre_ref[j])
        bu_im[...] = _dot(ujb, bim_ref[j])
        ar = jnp.broadcast_to(are_ref[:, cst], (SUBLANES, S5_ST_BLK))
        ai = jnp.broadcast_to(aim_ref[:, cst], (SUBLANES, S5_ST_BLK))

        def step(i, carry):
            hr, hi = carry
            r0 = pl.multiple_of(i * SUBLANES, SUBLANES)
            vr = bu_re[pl.ds(r0, SUBLANES), :]
            vi = bu_im[pl.ds(r0, SUBLANES), :]
            h1r = ar * hr - ai * hi + vr
            h1i = ar * hi + ai * hr + vi
            h1rs = pltpu.roll(h1r, bsz, 0)
            h1is = pltpu.roll(h1i, bsz, 0)
            h2r = ar * h1rs - ai * h1is + vr
            h2i = ar * h1is + ai * h1rs + vi
            st_re[pl.ds(r0, SUBLANES), :] = jnp.where(first, h1r, h2r)
            st_im[pl.ds(r0, SUBLANES), :] = jnp.where(first, h1i, h2i)
            return pltpu.roll(h2r, bsz, 0), pltpu.roll(h2i, bsz, 0)

        hr, hi = lax.fori_loop(0, rows // SUBLANES, step, (h_re[:, cst], h_im[:, cst]))
        h_re[:, cst] = hr
        h_im[:, cst] = hi
        yj = _dot(st_re[...].astype(BF16), cre_ref[j]) + _dot(st_im[...].astype(BF16), cim_ref[j])
        yj = yj + d_ref[:, cin] * uj
        y_ref[:, cin] = jax.nn.gelu(yj).astype(BF16)


def _s5_block_diag(w, n_in, n_out):
    gpb = SSM_GROUPS // S5_LANE_BLOCKS
    w4 = w.reshape(S5_LANE_BLOCKS, gpb, n_in, n_out)
    eye = jnp.eye(gpb, dtype=w.dtype)
    return jnp.einsum("jgio,gh->jgiho", w4, eye).reshape(S5_LANE_BLOCKS, gpb * n_in, gpb * n_out)


def _s5_mixer_block(h2d, bsz, seq, w_in, log_dt, lam_re, lam_im, b_re, b_im, c_re, c_im, d, w_out, g, b):
    assert 2 * bsz == SUBLANES, "the scan packs two time steps of bsz rows into one 8-row tile"
    tm = ROW_TILE
    per_seq = seq // tm
    u_t = _matmul(h2d, w_in.astype(BF16), tm=tm, out_dtype=F32, grid=(bsz, per_seq),
                  x_map=lambda bb, i: (bb * per_seq + i, 0), out_map=lambda bb, i: (i, bb),
                  out_shape=(seq, bsz * D_MODEL)).reshape(seq * bsz, D_MODEL)

    dt = jnp.exp(log_dt)[:, None]
    mag = jnp.exp(lam_re * dt)
    a_re, a_im = mag * jnp.cos(lam_im * dt), mag * jnp.sin(lam_im * dt)
    den = lam_re * lam_re + lam_im * lam_im
    coef_re = ((a_re - 1.0) * lam_re + a_im * lam_im) / den
    coef_im = (a_im * lam_re - (a_re - 1.0) * lam_im) / den
    bb_re = coef_re[..., None] * b_re - coef_im[..., None] * b_im
    bb_im = coef_re[..., None] * b_im + coef_im[..., None] * b_re
    bre = _s5_block_diag(jnp.swapaxes(bb_re, 1, 2), SSM_GROUP, SSM_STATE).astype(BF16)
    bim = _s5_block_diag(jnp.swapaxes(bb_im, 1, 2), SSM_GROUP, SSM_STATE).astype(BF16)
    cre = _s5_block_diag(jnp.swapaxes(c_re, 1, 2), SSM_STATE, SSM_GROUP).astype(BF16)
    cim = _s5_block_diag(jnp.swapaxes(-c_im, 1, 2), SSM_STATE, SSM_GROUP).astype(BF16)
    n_state = SSM_GROUPS * SSM_STATE

    rows = S5_CHUNK * bsz
    row = lambda c: (c, 0)
    c2 = lambda c: (0, 0)
    c3 = lambda c: (0, 0, 0)
    y_t = pl.pallas_call(
        functools.partial(_s5_kernel, bsz=bsz),
        grid=(seq // S5_CHUNK,),
        in_specs=[pl.BlockSpec((rows, D_MODEL), row),
                  pl.BlockSpec(bre.shape, c3), pl.BlockSpec(bim.shape, c3),
                  pl.BlockSpec(cre.shape, c3), pl.BlockSpec(cim.shape, c3),
                  pl.BlockSpec((1, n_state), c2), pl.BlockSpec((1, n_state), c2), pl.BlockSpec((1, D_MODEL), c2)],
        out_specs=pl.BlockSpec((rows, D_MODEL), row),
        out_shape=jax.ShapeDtypeStruct((seq * bsz, D_MODEL), BF16),
        scratch_shapes=[pltpu.VMEM((rows, S5_ST_BLK), F32)] * 4 + [pltpu.VMEM((SUBLANES, n_state), F32)] * 2,
        compiler_params=_cparams("arbitrary"),
        name="s5_scan",
    )(u_t, bre, bim, cre, cim, a_re.reshape(1, n_state), a_im.reshape(1, n_state), d.reshape(1, D_MODEL))

    y2 = y_t.reshape(seq, bsz * D_MODEL)
    return _linear_residual_ln(
        [y2], [w_out.astype(BF16)], h2d, g, b, tm=tm, glu=True, grid=(bsz, per_seq),
        x_maps=[lambda bb, i: (i, bb)], res_map=lambda bb, i: (bb * per_seq + i, 0))


def _even_mixer_block(h2d, bsz, seq, w_in, qnorm_g, w_uq, w_uq_idx, kidx_g, kidx_b, w_out, g, b):
    qa, qi, ka, va, ki, wi, qb, kb, vb = _even_proj(h2d, bsz, seq, w_in, qnorm_g, w_uq, w_uq_idx, kidx_g, kidx_b)
    o_a = _dsa_attention(qa, qi, wi, ki, ka, va, bsz, seq)
    o_b = _stick_breaking(qb, kb, vb, bsz, seq)
    w_out = w_out.astype(BF16)
    return _linear_residual_ln([o_a, o_b], [w_out[:A_WIDTH], w_out[A_WIDTH:]], h2d, g, b, tm=ROW_TILE)


def kernel(x, mem, ev_w_in, ev_qnorm_g, ev_w_uq, ev_w_uq_idx, ev_kidx_ln_g, ev_kidx_ln_b, ev_w_out, od_w_in, od_log_dt, od_lambda_re, od_lambda_im, od_b_re, od_b_im, od_c_re, od_c_im, od_d, od_w_out, mix_ln_g, mix_ln_b, xa_w_q, xa_w_kv, xa_w_o, xa_ln_g, xa_ln_b, moe_w_router, moe_b_router, moe_w_gu, moe_b_gu, moe_w_down, moe_b_down, ffn_ln_g, ffn_ln_b):
    bsz, seq, _ = x.shape
    h = x.reshape(bsz * seq, D_MODEL)
    for layer in range(DEPTH):
        j = layer // 2
        if layer % 2 == 0:
            h = _even_mixer_block(h, bsz, seq, ev_w_in[j], ev_qnorm_g[j], ev_w_uq[j], ev_w_uq_idx[j],
                                  ev_kidx_ln_g[j], ev_kidx_ln_b[j], ev_w_out[j], mix_ln_g[layer], mix_ln_b[layer])
        else:
            h = _s5_mixer_block(h, bsz, seq, od_w_in[j], od_log_dt[j], od_lambda_re[j], od_lambda_im[j],
                                od_b_re[j], od_b_im[j], od_c_re[j], od_c_im[j], od_d[j], od_w_out[j],
                                mix_ln_g[layer], mix_ln_b[layer])
        h, h_packed = _cross_attention_block(h, mem, bsz, seq, xa_w_q[layer], xa_w_kv[layer], xa_w_o[layer],
                                             xa_ln_g[layer], xa_ln_b[layer])
        h = _moe_block(h, h_packed, layer, moe_w_router[layer], moe_b_router[layer], moe_w_gu, moe_b_gu[layer],
                       moe_w_down, moe_b_down[layer], ffn_ln_g[layer], ffn_ln_b[layer])
    return h.reshape(bsz, seq, D_MODEL)
```

```python
import functools
import math

import jax
import jax.numpy as jnp
from jax import lax
from jax.experimental import pallas as pl
from jax.experimental.pallas import tpu as pltpu
from jax.experimental.pallas import tpu_sc as plsc

F32 = jnp.float32
BF16 = jnp.bfloat16
I32 = jnp.int32
U32 = jnp.uint32
HIGH16 = 0xFFFF0000

D_MODEL = 1024
HALF_D = D_MODEL // 2
DEPTH = 2
HEAD_DIM = 64
A_HEADS = 8
A_KV_HEADS = 2
A_REP = A_HEADS // A_KV_HEADS
Q_RANK = 256
IDX_HEADS = 8
IDX_DIM = 64
IDX_TOPK = 256
B_HEADS = 8
A_WIDTH = A_HEADS * HEAD_DIM
B_WIDTH = B_HEADS * HEAD_DIM
SSM_GROUP = 16
SSM_GROUPS = D_MODEL // SSM_GROUP
SSM_STATE = 64
XA_HEADS = 4
XA_HEAD_DIM = D_MODEL // XA_HEADS
N_EXPERTS = 32
TOP_K = 4
D_EXPERT = D_MODEL
SWIGLU_LIMIT = 7.0
SWIGLU_ALPHA = 1.702
ROPE_THETA = 500000.0
ROPE_HALF = HEAD_DIM // 8
LN_EPS = 1e-5
DN_ALPHA = (2 * DEPTH) ** 0.25

LANES = 128
SUBLANES = 8
VMEM_LIMIT_BYTES = 56 * 1024 * 1024

Q_BLOCK = 256
DSA_KEY_TILE = 512
DSA_ATT_TILE = 1024
DSA_ATT_PAR = 1
DSA_COUNT_ROWS = 8 * SUBLANES
SB_KEY_TILE = 256
ROW_TILE = 512
MOE_BLOCK_ROWS = 512
SC_GATHER_SLOT_BYTES = 128 * 1024
S5_CHUNK = 256
S5_LANE_BLOCKS = 4
S5_IN_BLK = D_MODEL // S5_LANE_BLOCKS
S5_ST_BLK = SSM_GROUPS * SSM_STATE // S5_LANE_BLOCKS

SB_EXIT_LOG = -104.0
NEG_BIG = -1e30
INT_MIN = -(2 ** 31)


def _cparams(*sem):
    return pltpu.CompilerParams(dimension_semantics=sem, vmem_limit_bytes=VMEM_LIMIT_BYTES)


def _dot(a, b):
    return jnp.dot(a, b, preferred_element_type=F32)


def _dot_nt(a, b):
    return lax.dot_general(a, b, (((1,), (1,)), ((), ())), preferred_element_type=F32)


def _layer_norm_rows(y, g, b):
    mu = jnp.mean(y, axis=-1, keepdims=True)
    d = y - mu
    var = jnp.mean(d * d, axis=-1, keepdims=True)
    return d * lax.rsqrt(var + LN_EPS) * g + b


def _mm_kernel(x_ref, w_ref, o_ref):
    o_ref[...] = _dot(x_ref[...].astype(BF16), w_ref[...]).astype(o_ref.dtype)


def _matmul(x, w, *, tm, out_dtype, x_map=None, out_map=None, grid=None, out_shape=None):
    m, k = x.shape
    n = w.shape[1]
    grid = grid or (m // tm,)
    x_map = x_map or (lambda i: (i, 0))
    out_map = out_map or (lambda i: (i, 0))
    out_shape = out_shape or (m, n)
    return pl.pallas_call(
        _mm_kernel,
        grid=grid,
        in_specs=[pl.BlockSpec((tm, k), x_map), pl.BlockSpec((k, n), lambda *a: (0, 0))],
        out_specs=pl.BlockSpec((tm, n), out_map),
        out_shape=jax.ShapeDtypeStruct(out_shape, out_dtype),
        compiler_params=_cparams(*(("parallel",) * len(grid))),
        name="matmul",
    )(x, w)


def _lin_ln_kernel(*refs, n_in, glu):
    xs, ws = refs[:n_in], refs[n_in:2 * n_in]
    res_ref, g_ref, b_ref, o_ref = refs[2 * n_in:]
    acc = _dot(xs[0][...].astype(BF16), ws[0][...])
    for x_ref, w_ref in zip(xs[1:], ws[1:]):
        acc = acc + _dot(x_ref[...].astype(BF16), w_ref[...])
    if glu:
        acc = acc[:, :D_MODEL] * jax.nn.sigmoid(acc[:, D_MODEL:])
    y = DN_ALPHA * res_ref[...] + acc
    o_ref[...] = _layer_norm_rows(y, g_ref[...], b_ref[...])


def _linear_residual_ln(xs, ws, res, g, b, *, tm, glu=False, grid=None, x_maps=None, res_map=None):
    n_rows = res.shape[0]
    grid = grid or (n_rows // tm,)
    x_maps = x_maps or [lambda i: (i, 0)] * len(xs)
    res_map = res_map or (lambda i: (i, 0))
    const = lambda *a: (0, 0)
    in_specs = [pl.BlockSpec((tm, w.shape[0]), m) for w, m in zip(ws, x_maps)]
    in_specs += [pl.BlockSpec(w.shape, const) for w in ws]
    in_specs += [pl.BlockSpec((tm, D_MODEL), res_map), pl.BlockSpec((1, D_MODEL), const),
                 pl.BlockSpec((1, D_MODEL), const)]
    return pl.pallas_call(
        functools.partial(_lin_ln_kernel, n_in=len(xs), glu=glu),
        grid=grid,
        in_specs=in_specs,
        out_specs=pl.BlockSpec((tm, D_MODEL), res_map),
        out_shape=jax.ShapeDtypeStruct((n_rows, D_MODEL), F32),
        compiler_params=_cparams(*(("parallel",) * len(grid))),
        name="linear_residual_ln",
    )(*xs, *ws, res, g.reshape(1, D_MODEL), b.reshape(1, D_MODEL))


_EV_CQ, _EV_KA, _EV_VA, _EV_KI, _EV_QB = 0, 256, 384, 512, 640
_EV_KB = _EV_QB + B_WIDTH
_EV_VB = _EV_KB + B_WIDTH
_EV_COLS = _EV_VB + B_WIDTH


def _rope_tables(seq):
    inv = ROPE_THETA ** (-jnp.arange(ROPE_HALF, dtype=F32) / ROPE_HALF)
    ang = jnp.arange(seq, dtype=F32)[:, None] * inv[None, :]
    cos, sin = jnp.cos(ang), jnp.sin(ang)
    rest = HEAD_DIM - 2 * ROPE_HALF
    zh = jnp.zeros((seq, ROPE_HALF), F32)
    c = jnp.concatenate([cos, cos, jnp.ones((seq, rest), F32)], axis=1)
    s1 = jnp.concatenate([-sin, zh, jnp.zeros((seq, rest), F32)], axis=1)
    s2 = jnp.concatenate([zh, sin, jnp.zeros((seq, rest), F32)], axis=1)
    rep = LANES // HEAD_DIM
    return jnp.tile(c, (1, rep)), jnp.tile(s1, (1, rep)), jnp.tile(s2, (1, rep))


def _even_proj_kernel(x_ref, w_ref, qg_ref, wuq_ref, wuqi_ref, lg_ref, lb_ref, c_ref, s1_ref, s2_ref,
                      qa_ref, qi_ref, ka_ref, va_ref, ki_ref, wi_ref, qb_ref, kb_ref, vb_ref):
    p = _dot(x_ref[...].astype(BF16), w_ref[...])
    c, s1, s2 = c_ref[...], s1_ref[...], s2_ref[...]

    def rope(t):
        return (t * c + pltpu.roll(t, LANES - ROPE_HALF, 1) * s1 + pltpu.roll(t, ROPE_HALF, 1) * s2)

    cq = p[:, _EV_CQ:_EV_CQ + Q_RANK]
    cn = cq * lax.rsqrt(jnp.mean(cq * cq, axis=-1, keepdims=True) + LN_EPS) * qg_ref[...]
    cnb = cn.astype(BF16)
    qa = _dot(cnb, wuq_ref[...])
    qi = _dot(cnb, wuqi_ref[...])
    low = lax.broadcasted_iota(I32, c.shape, 1) < HEAD_DIM
    for j in range(A_WIDTH // LANES):
        sl = slice(j * LANES, (j + 1) * LANES)
        pair = rope(qa[:, sl]) * (HEAD_DIM ** -0.5)
        for e, src in enumerate((pair, pltpu.roll(pair, HEAD_DIM, 1))):
            h = 2 * j + e
            qa_ref[:, h * LANES:(h + 1) * LANES] = jnp.where(low, src, 0.0).astype(BF16)
        qi_ref[:, sl] = (rope(qi[:, sl]) * (IDX_DIM ** -0.5)).astype(BF16)
    kpair = rope(p[:, _EV_KA:_EV_KA + LANES])
    vpair = p[:, _EV_VA:_EV_VA + LANES]
    v_pad = jnp.where(lax.broadcasted_iota(I32, c.shape, 1) == HEAD_DIM, 1.0, 0.0)
    for g, (ks, vs) in enumerate(((kpair, vpair), (pltpu.roll(kpair, HEAD_DIM, 1), pltpu.roll(vpair, HEAD_DIM, 1)))):
        ka_ref[:, g * LANES:(g + 1) * LANES] = jnp.where(low, ks, 0.0).astype(BF16)
        va_ref[:, g * LANES:(g + 1) * LANES] = jnp.where(low, vs, v_pad).astype(BF16)

    t = p[:, _EV_KI:_EV_KI + LANES]
    lane = lax.broadcasted_iota(I32, t.shape, 1)
    is_k = lane < IDX_DIM
    mu = jnp.sum(jnp.where(is_k, t, 0.0), axis=-1, keepdims=True) * (1.0 / IDX_DIM)
    d = jnp.where(is_k, t - mu, 0.0)
    var = jnp.sum(d * d, axis=-1, keepdims=True) * (1.0 / IDX_DIM)
    kin = d * lax.rsqrt(var + LN_EPS) * lg_ref[...] + lb_ref[...]
    ki_ref[...] = rope(kin)[:, :IDX_DIM].astype(BF16)
    wi_ref[...] = t[:, IDX_DIM:IDX_DIM + IDX_HEADS] * (IDX_HEADS ** -0.5)

    qb_ref[...] = (p[:, _EV_QB:_EV_KB] * (HEAD_DIM ** -0.5)).astype(BF16)
    kb_ref[...] = p[:, _EV_KB:_EV_VB].astype(BF16)
    vb_ref[...] = p[:, _EV_VB:_EV_COLS].astype(BF16)


def _even_proj(x2d, bsz, seq, w_in, qnorm_g, w_uq, w_uq_idx, kidx_g, kidx_b):
    n = x2d.shape[0]
    tm = ROW_TILE
    per_seq = seq // tm
    c0 = Q_RANK + 2 * A_KV_HEADS * HEAD_DIM + IDX_DIM + IDX_HEADS
    w_pack = jnp.concatenate(
        [w_in[:, :c0], jnp.zeros((D_MODEL, _EV_QB - c0), w_in.dtype), w_in[:, c0:]], axis=1).astype(BF16)
    pad = LANES - IDX_DIM
    lg = jnp.concatenate([kidx_g, jnp.zeros((pad,), F32)]).reshape(1, LANES)
    lb = jnp.concatenate([kidx_b, jnp.zeros((pad,), F32)]).reshape(1, LANES)
    c, s1, s2 = _rope_tables(seq)
    row = lambda i: (i, 0)
    const = lambda i: (0, 0)
    pos = lambda i: (i % per_seq, 0)
    head_shape = jax.ShapeDtypeStruct((n, B_WIDTH), BF16)
    head_spec = pl.BlockSpec((tm, B_WIDTH), row)
    return pl.pallas_call(
        _even_proj_kernel,
        grid=(n // tm,),
        in_specs=[pl.BlockSpec((tm, D_MODEL), row), pl.BlockSpec((D_MODEL, _EV_COLS), const),
                  pl.BlockSpec((1, Q_RANK), const), pl.BlockSpec((Q_RANK, A_WIDTH), const),
                  pl.BlockSpec((Q_RANK, IDX_HEADS * IDX_DIM), const),
                  pl.BlockSpec((1, LANES), const), pl.BlockSpec((1, LANES), const),
                  pl.BlockSpec((tm, LANES), pos), pl.BlockSpec((tm, LANES), pos), pl.BlockSpec((tm, LANES), pos)],
        out_specs=[pl.BlockSpec((tm, A_HEADS * LANES), row), pl.BlockSpec((tm, IDX_HEADS * IDX_DIM), row),
                   pl.BlockSpec((tm, A_KV_HEADS * LANES), row), pl.BlockSpec((tm, A_KV_HEADS * LANES), row),
                   pl.BlockSpec((tm, IDX_DIM), row), pl.BlockSpec((tm, IDX_HEADS), row),
                   head_spec, head_spec, head_spec],
        out_shape=[jax.ShapeDtypeStruct((n, A_HEADS * LANES), BF16), jax.ShapeDtypeStruct((n, IDX_HEADS * IDX_DIM), BF16),
                   jax.ShapeDtypeStruct((n, A_KV_HEADS * LANES), BF16),
                   jax.ShapeDtypeStruct((n, A_KV_HEADS * LANES), BF16),
                   jax.ShapeDtypeStruct((n, IDX_DIM), BF16), jax.ShapeDtypeStruct((n, IDX_HEADS), F32),
                   head_shape, head_shape, head_shape],
        compiler_params=_cparams("parallel"),
        name="even_proj",
    )(x2d, w_pack, qnorm_g.reshape(1, Q_RANK), w_uq.astype(BF16), w_uq_idx.astype(BF16), lg, lb, c, s1, s2)


def _key_to_float(key):
    bits = key ^ ((key >> 31) & jnp.int32(0x7FFFFFFF))
    return lax.bitcast_convert_type(bits, F32)


def _high_half(x):
    bits = lax.bitcast_convert_type(x, U32) & U32(HIGH16)
    return lax.bitcast_convert_type(bits, F32).astype(BF16)


def _dsa_kernel(qa_ref, qi_ref, wit_ref, ki_ref, ka_ref, vat_ref, o_ref, sc_scr, hi_scr, *, topk, ts, ta):
    seq = sc_scr.shape[0]
    qb = pl.program_id(1)
    q0 = qb * Q_BLOCK
    nkt = (q0 + Q_BLOCK - 1) // ts + 1
    t_row = q0 + lax.broadcasted_iota(I32, (1, Q_BLOCK), 1)
    key = lax.broadcasted_iota(I32, (ts, Q_BLOCK), 0)
    kf = jnp.float32(topk)

    qi = qi_ref[0]
    qs = jnp.concatenate([qi[:, h * IDX_DIM:(h + 1) * IDX_DIM] for h in range(IDX_HEADS)], axis=0)
    wit = wit_ref[0]

    def score_tile(kt, carry):
        off = pl.multiple_of(kt * ts, ts)
        s_all = _dot_nt(ki_ref[0, pl.ds(off, ts), :], qs)
        acc = jnp.zeros((ts, Q_BLOCK), F32)
        for h in range(IDX_HEADS):
            acc = acc + jnp.maximum(s_all[:, h * Q_BLOCK:(h + 1) * Q_BLOCK], 0.0) * wit[h:h + 1, :]
        val = jnp.where(off + key <= t_row, acc, -jnp.inf)
        sc_scr[pl.ds(off, ts), :] = val
        hi_scr[pl.ds(off, ts), :] = _high_half(val)
        return carry

    lax.fori_loop(0, nkt, score_tile, 0)

    def count(pred):
        def body(kt, acc):
            off = pl.multiple_of(kt * ts, ts)
            ind = pred(sc_scr[pl.ds(off, ts), :], off + key)
            return acc + jnp.sum(ind.reshape(ts // DSA_COUNT_ROWS, DSA_COUNT_ROWS, Q_BLOCK), axis=0)
        acc = lax.fori_loop(0, nkt, body, jnp.zeros((DSA_COUNT_ROWS, Q_BLOCK), F32))
        return jnp.sum(acc, axis=0, keepdims=True)

    one_h, zero_h = jnp.ones((), BF16), jnp.zeros((), BF16)

    def count_high(c_hi):
        def body(kt, acc):
            off = pl.multiple_of(kt * ts, ts)
            ind = jnp.where(hi_scr[pl.ds(off, ts), :] >= c_hi, one_h, zero_h)
            part = ind[:DSA_COUNT_ROWS]
            for j in range(1, ts // DSA_COUNT_ROWS):
                part = part + ind[j * DSA_COUNT_ROWS:(j + 1) * DSA_COUNT_ROWS]
            return acc + part.astype(F32)
        acc = lax.fori_loop(0, nkt, body, jnp.zeros((DSA_COUNT_ROWS, Q_BLOCK), F32))
        return jnp.sum(acc, axis=0, keepdims=True)

    def high_step(i, base):
        cand = base + jnp.left_shift(jnp.int32(1), 31 - i)
        cnt = count_high(_high_half(_key_to_float(cand)))
        return jnp.where(cnt >= kf, cand, base)

    def bit_step(i, base):
        cand = base + jnp.left_shift(jnp.int32(1), 31 - i)
        cf = _key_to_float(cand)
        cnt = count(lambda sc, idx: jnp.where(sc >= cf, 1.0, 0.0))
        return jnp.where(cnt >= kf, cand, base)

    base = lax.fori_loop(0, 16, high_step, jnp.full((1, Q_BLOCK), INT_MIN, I32))
    base = lax.fori_loop(16, 32, bit_step, base)
    thr = jnp.where(base == INT_MIN, -jnp.inf, _key_to_float(base))

    cnt_ge = count(lambda sc, idx: jnp.where(sc >= thr, 1.0, 0.0))
    tied = jnp.logical_and(cnt_ge > kf, thr > -jnp.inf)
    any_tied = jnp.max(jnp.where(tied, 1.0, 0.0)) > 0.0
    seq_bits = max(1, int(math.ceil(math.log2(seq))))

    def tie_cut():
        cnt_gt = count(lambda sc, idx: jnp.where(sc > thr, 1.0, 0.0))
        need = kf - cnt_gt

        def idx_step(i, pos):
            cand = pos + jnp.left_shift(jnp.int32(1), seq_bits - 1 - i)
            cnt = count(lambda sc, idx: jnp.where(sc == thr, jnp.where(idx < cand, 1.0, 0.0), 0.0))
            return jnp.where(cnt < need, cand, pos)

        return lax.fori_loop(0, seq_bits, idx_step, jnp.zeros((1, Q_BLOCK), I32))

    cut = lax.cond(any_tied, tie_cut, lambda: jnp.full((1, Q_BLOCK), seq, I32))
    cut = jnp.where(tied, cut, seq)

    nkt_a = (q0 + Q_BLOCK - 1) // ta + 1
    key_a = lax.broadcasted_iota(I32, (ta, Q_BLOCK), 0)
    cols = A_REP * Q_BLOCK
    qg = [jnp.concatenate([qa_ref[0, :, (g * A_REP + r) * LANES:(g * A_REP + r + 1) * LANES]
                           for r in range(A_REP)], axis=0) for g in range(A_KV_HEADS)]

    def att_step(i, carry):
        offs = [pl.multiple_of((DSA_ATT_PAR * i + e) * ta, ta) for e in range(DSA_ATT_PAR)]
        logits = [_dot_nt(ka_ref[0, pl.ds(offs[e], ta), g * LANES:(g + 1) * LANES], qg[g])
                  for e in range(DSA_ATT_PAR) for g in range(A_KV_HEADS)]
        out = []
        for e in range(DSA_ATT_PAR):
            sc = sc_scr[pl.ds(offs[e], ta), :]
            idx = offs[e] + key_a
            keep = jnp.where(sc > thr, 0.0, jnp.where(sc == thr, jnp.where(idx <= cut, 0.0, NEG_BIG), NEG_BIG))
            bias = jnp.where(idx <= t_row, keep, NEG_BIG)
            bias = jnp.concatenate([bias] * A_REP, axis=1)
            for g in range(A_KV_HEADS):
                m, acc = carry[e * A_KV_HEADS + g]
                s = logits[e * A_KV_HEADS + g] + bias
                m_new = jnp.maximum(m, jnp.max(s, axis=0, keepdims=True))
                p = jnp.exp(s - m_new)
                vt = vat_ref[0, g * LANES:(g + 1) * LANES, pl.ds(offs[e], ta)]
                out.append((m_new, jnp.exp(m - m_new) * acc + _dot(vt, p.astype(BF16))))
        return tuple(out)

    init = tuple((jnp.full((1, cols), NEG_BIG, F32), jnp.zeros((LANES, cols), F32))
                 for _ in range(DSA_ATT_PAR * A_KV_HEADS))
    final = lax.fori_loop(0, (nkt_a + DSA_ATT_PAR - 1) // DSA_ATT_PAR, att_step, init)
    low = lax.broadcasted_iota(I32, (Q_BLOCK, LANES), 1) < HEAD_DIM
    outs = []
    for g in range(A_KV_HEADS):
        m, acc = final[g]
        for e in range(1, DSA_ATT_PAR):
            m1, acc1 = final[e * A_KV_HEADS + g]
            m_all = jnp.maximum(m, m1)
            m, acc = m_all, jnp.exp(m - m_all) * acc + jnp.exp(m1 - m_all) * acc1
        og = acc / acc[HEAD_DIM:HEAD_DIM + 1, :]
        outs += [og[:, r * Q_BLOCK:(r + 1) * Q_BLOCK].T for r in range(A_REP)]
    for j in range(A_HEADS // 2):
        pair = jnp.where(low, outs[2 * j], pltpu.roll(outs[2 * j + 1], HEAD_DIM, 1))
        o_ref[0, :, j * LANES:(j + 1) * LANES] = pair.astype(BF16)


def _dsa_attention(qa, qi, wi, ki, ka, va, bsz, seq):
    topk = min(IDX_TOPK, seq // 4)
    ts = min(DSA_KEY_TILE, seq)
    ta = min(DSA_ATT_TILE, seq // DSA_ATT_PAR)
    assert seq % (DSA_ATT_PAR * ta) == 0 and seq % ts == 0, "the attention loop walks whole groups of key tiles"
    blk = lambda b, i: (b, i, 0)
    full = lambda b, i: (b, 0, 0)
    r3 = lambda a: a.reshape(bsz, seq, a.shape[-1])
    wit = jnp.swapaxes(r3(wi), 1, 2)
    vat = jnp.swapaxes(r3(va), 1, 2)
    return pl.pallas_call(
        functools.partial(_dsa_kernel, topk=topk, ts=ts, ta=ta),
        grid=(bsz, seq // Q_BLOCK),
        in_specs=[pl.BlockSpec((1, Q_BLOCK, A_HEADS * LANES), blk),
                  pl.BlockSpec((1, Q_BLOCK, IDX_HEADS * IDX_DIM), blk),
                  pl.BlockSpec((1, IDX_HEADS, Q_BLOCK), lambda b, i: (b, 0, i)),
                  pl.BlockSpec((1, seq, IDX_DIM), full),
                  pl.BlockSpec((1, seq, A_KV_HEADS * LANES), full),
                  pl.BlockSpec((1, A_KV_HEADS * LANES, seq), full)],
        out_specs=pl.BlockSpec((1, Q_BLOCK, A_WIDTH), blk),
        out_shape=jax.ShapeDtypeStruct((bsz, seq, A_WIDTH), BF16),
        scratch_shapes=[pltpu.VMEM((seq, Q_BLOCK), F32), pltpu.VMEM((seq, Q_BLOCK), BF16)],
        compiler_params=_cparams("parallel", "parallel"),
        name="dsa_attention",
    )(r3(qa), r3(qi), wit, r3(ki), r3(ka), vat).reshape(bsz * seq, A_WIDTH)


def _sb_kernel(q_ref, k_ref, v_ref, u_ref, o_ref, acc_scr, run_scr, *, tk):
    q0 = pl.program_id(1) * Q_BLOCK
    t_col = q0 + lax.broadcasted_iota(I32, (Q_BLOCK, 1), 0)
    lane = lax.broadcasted_iota(I32, (Q_BLOCK, tk), 1)
    low = lax.broadcasted_iota(I32, (Q_BLOCK, LANES), 1) < HEAD_DIM
    upper = u_ref[...]
    nkt = (q0 + Q_BLOCK - 1) // tk + 1
    q = q_ref[0]
    zero = jnp.zeros((Q_BLOCK, LANES), BF16)
    qm = []
    for p in range(B_HEADS // 2):
        pair = q[:, p * LANES:(p + 1) * LANES]
        qm.append(jnp.concatenate([jnp.where(low, pair, zero), jnp.where(low, zero, pair)], axis=0))
    acc_scr[...] = jnp.zeros_like(acc_scr)
    run_scr[...] = jnp.zeros_like(run_scr)

    def cond(carry):
        i, worst = carry
        return jnp.logical_and(i < nkt, worst >= SB_EXIT_LOG)

    def body(carry):
        i, _ = carry
        off = pl.multiple_of((nkt - 1 - i) * tk, tk)
        strict = off + lane < t_col
        strict = jnp.concatenate([strict, strict], axis=0)
        worst = None
        for p in range(B_HEADS // 2):
            cols = slice(p * LANES, (p + 1) * LANES)
            kp = k_ref[0, pl.ds(off, tk), cols]
            vp = v_ref[0, pl.ds(off, tk), cols]
            run = run_scr[p]
            z = _dot_nt(qm[p], kp)
            softplus = jnp.maximum(z, 0.0) + jnp.log(1.0 + jnp.exp(-jnp.abs(z)))
            log_1mb = jnp.where(strict, -softplus, 0.0)
            hi = log_1mb.astype(BF16)
            lo = (log_1mb - hi.astype(F32)).astype(BF16)
            after = _dot(hi, upper) + _dot(lo, upper) + run
            a = jnp.where(strict, jnp.exp(z - softplus + after), 0.0)
            out = _dot(a.astype(BF16), vp)
            run = run + jnp.sum(log_1mb, axis=1, keepdims=True)
            run_scr[p] = run
            worst = run if worst is None else jnp.maximum(worst, run)
            acc_scr[:, cols] += jnp.where(low, out[:Q_BLOCK], out[Q_BLOCK:])
        return i + 1, jnp.max(worst)

    lax.while_loop(cond, body, (jnp.int32(0), jnp.float32(0.0)))
    o_ref[0] = acc_scr[...].astype(BF16)


def _stick_breaking(qb, kb, vb, bsz, seq):
    tk = min(SB_KEY_TILE, seq)
    r = lax.broadcasted_iota(I32, (tk, tk), 0)
    c = lax.broadcasted_iota(I32, (tk, tk), 1)
    upper = jnp.where(r > c, 1.0, 0.0).astype(BF16)
    blk = lambda b, i: (b, i, 0)
    full = lambda b, i: (b, 0, 0)
    r3 = lambda a: a.reshape(bsz, seq, B_WIDTH)
    return pl.pallas_call(
        functools.partial(_sb_kernel, tk=tk),
        grid=(bsz, seq // Q_BLOCK),
        in_specs=[pl.BlockSpec((1, Q_BLOCK, B_WIDTH), blk), pl.BlockSpec((1, seq, B_WIDTH), full),
                  pl.BlockSpec((1, seq, B_WIDTH), full), pl.BlockSpec((tk, tk), lambda b, i: (0, 0))],
        out_specs=pl.BlockSpec((1, Q_BLOCK, B_WIDTH), blk),
        out_shape=jax.ShapeDtypeStruct((bsz, seq, B_WIDTH), BF16),
        scratch_shapes=[pltpu.VMEM((Q_BLOCK, B_WIDTH), F32), pltpu.VMEM((B_HEADS // 2, 2 * Q_BLOCK, 1), F32)],
        compiler_params=_cparams("parallel", "arbitrary"),
        name="stick_breaking",
    )(r3(qb), r3(kb), r3(vb), upper).reshape(bsz * seq, B_WIDTH)


def _pack_bf16_pairs(x):
    def bits(v):
        return lax.bitcast_convert_type(v.astype(BF16).astype(F32), U32)
    word = (bits(x[:, HALF_D:]) & U32(HIGH16)) | (bits(x[:, :HALF_D]) >> 16)
    return lax.bitcast_convert_type(word, I32)


def _unpack_bf16_pairs(word):
    u = lax.bitcast_convert_type(word, U32)
    return lax.bitcast_convert_type(u << 16, F32), lax.bitcast_convert_type(u & U32(HIGH16), F32)


def _xattn_kernel(h_ref, wq_ref, kv_ref, wo_ref, g_ref, b_ref, o_ref, packed_ref):
    h = h_ref[...]
    q = (_dot(h.astype(BF16), wq_ref[...]) * (XA_HEAD_DIM ** -0.5)).astype(BF16)
    kv = kv_ref[0]
    outs = []
    for hd in range(XA_HEADS):
        sl = slice(hd * XA_HEAD_DIM, (hd + 1) * XA_HEAD_DIM)
        s = _dot_nt(q[:, sl], kv[:, sl])
        p = jnp.exp(s - jnp.max(s, axis=1, keepdims=True))
        vh = kv[:, D_MODEL + hd * XA_HEAD_DIM:D_MODEL + (hd + 1) * XA_HEAD_DIM]
        outs.append((_dot(p.astype(BF16), vh) / jnp.sum(p, axis=1, keepdims=True)).astype(BF16))
    y = _dot(jnp.concatenate(outs, axis=1), wo_ref[...])
    out = _layer_norm_rows(DN_ALPHA * h + y, g_ref[...], b_ref[...])
    o_ref[...] = out
    packed_ref[...] = _pack_bf16_pairs(out)


def _cross_attention_block(h2d, mem, bsz, seq, w_q, w_kv, w_o, g, b):
    tm = ROW_TILE
    per_seq = seq // tm
    mem_len = mem.shape[1]
    kv = _matmul(mem.reshape(bsz * mem_len, D_MODEL), w_kv.astype(BF16), tm=mem_len, out_dtype=BF16)
    kv = kv.reshape(bsz, mem_len, 2 * D_MODEL)
    row = lambda i: (i, 0)
    const = lambda i: (0, 0)
    return pl.pallas_call(
        _xattn_kernel,
        grid=(bsz * per_seq,),
        in_specs=[pl.BlockSpec((tm, D_MODEL), row), pl.BlockSpec((D_MODEL, D_MODEL), const),
                  pl.BlockSpec((1, mem_len, 2 * D_MODEL), lambda i: (i // per_seq, 0, 0)),
                  pl.BlockSpec((D_MODEL, D_MODEL), const),
                  pl.BlockSpec((1, D_MODEL), const), pl.BlockSpec((1, D_MODEL), const)],
        out_specs=[pl.BlockSpec((tm, D_MODEL), row), pl.BlockSpec((tm, HALF_D), row)],
        out_shape=[jax.ShapeDtypeStruct(h2d.shape, F32), jax.ShapeDtypeStruct((h2d.shape[0], HALF_D), I32)],
        compiler_params=_cparams("parallel"),
        name="cross_attention",
    )(h2d, w_q.astype(BF16), kv, w_o.astype(BF16), g.reshape(1, D_MODEL), b.reshape(1, D_MODEL))


def _router_kernel(h_ref, w_ref, b_ref, tri_ref, idx_ref, gate_ref, rank_ref, cnt_ref, run_scr):
    @pl.when(pl.program_id(0) == 0)
    def _():
        run_scr[...] = jnp.zeros_like(run_scr)

    h = h_ref[...]
    hh = h.astype(BF16)
    hl = (h - hh.astype(F32)).astype(BF16)
    w = w_ref[...]
    wh = w.astype(BF16)
    wl = (w - wh.astype(F32)).astype(BF16)
    logits = _dot(hh, wh) + _dot(hl, wh) + _dot(hh, wl) + b_ref[...]
    lane = lax.broadcasted_iota(I32, logits.shape, 1).astype(F32)
    vals, sels = [], []
    onehot = jnp.zeros(logits.shape, F32)
    for k in range(TOP_K):
        m = jnp.max(logits, axis=1, keepdims=True)
        sel = jnp.min(jnp.where(logits == m, lane, float(LANES)), axis=1, keepdims=True)
        idx_ref[:, k:k + 1] = sel.astype(I32)
        vals.append(m)
        sels.append(sel)
        onehot = onehot + jnp.where(lane == sel, 1.0, 0.0)
        logits = jnp.where(lane == sel, -jnp.inf, logits)
    es = [jnp.exp(v - vals[0]) for v in vals]
    tot = es[0] + es[1] + es[2] + es[3]
    for k in range(TOP_K):
        gate_ref[:, k:k + 1] = es[k] / tot

    earlier = _dot(tri_ref[...], onehot.astype(BF16)) + run_scr[...]
    for k in range(TOP_K):
        rank = jnp.sum(jnp.where(lane == sels[k], earlier, 0.0), axis=1, keepdims=True)
        rank_ref[:, k:k + 1] = rank.astype(I32)
    run = run_scr[...] + jnp.sum(onehot, axis=0, keepdims=True)
    run_scr[...] = run
    cnt_ref[...] = run


def _router(h2d, w_router, b_router):
    n = h2d.shape[0]
    tm = 2 * ROW_TILE
    pad = LANES - N_EXPERTS
    w = jnp.concatenate([w_router, jnp.zeros((D_MODEL, pad), F32)], axis=1)
    b = jnp.concatenate([b_router, jnp.full((pad,), NEG_BIG, F32)]).reshape(1, LANES)
    r = lax.broadcasted_iota(I32, (tm, tm), 0)
    c = lax.broadcasted_iota(I32, (tm, tm), 1)
    tri = jnp.where(c < r, 1.0, 0.0).astype(BF16)
    row = lambda i: (i, 0)
    const = lambda i: (0, 0)
    return pl.pallas_call(
        _router_kernel,
        grid=(n // tm,),
        in_specs=[pl.BlockSpec((tm, D_MODEL), row), pl.BlockSpec((D_MODEL, LANES), const),
                  pl.BlockSpec((1, LANES), const), pl.BlockSpec((tm, tm), const)],
        out_specs=[pl.BlockSpec((tm, TOP_K), row), pl.BlockSpec((tm, TOP_K), row),
                   pl.BlockSpec((tm, TOP_K), row), pl.BlockSpec((1, LANES), const)],
        out_shape=[jax.ShapeDtypeStruct((n, TOP_K), I32), jax.ShapeDtypeStruct((n, TOP_K), F32),
                   jax.ShapeDtypeStruct((n, TOP_K), I32), jax.ShapeDtypeStruct((1, LANES), F32)],
        scratch_shapes=[pltpu.VMEM((1, LANES), F32)],
        compiler_params=_cparams("arbitrary"),
        name="moe_router",
    )(h2d, w, b, tri)


def _gather_rows(src, idx):
    n_out = idx.shape[0]
    width = src.shape[1]
    win = SC_GATHER_SLOT_BYTES // (width * src.dtype.itemsize)
    mesh = plsc.VectorSubcoreMesh(core_axis_name="core", subcore_axis_name="subcore")
    n_workers = mesh.num_cores * mesh.num_subcores
    per_worker = n_out // n_workers
    steps = per_worker // win
    assert per_worker * n_workers == n_out and steps * win == per_worker and steps % 2 == 0

    @functools.partial(
        pl.kernel, out_type=jax.ShapeDtypeStruct((n_out, width), src.dtype), mesh=mesh,
        scratch_types=[pltpu.VMEM((per_worker,), I32), pltpu.VMEM((2, win, width), src.dtype),
                       pltpu.SemaphoreType.DMA, pltpu.SemaphoreType.DMA])
    def gather_kernel(src_hbm, idx_hbm, dst_hbm, idx_v, rows_v, sem0, sem1):
        worker = lax.axis_index("subcore") * mesh.num_cores + lax.axis_index("core")
        base = worker * per_worker
        sems = (sem0, sem1)
        pltpu.sync_copy(idx_hbm.at[pl.ds(base, per_worker)], idx_v)

        def gather(step, slot):
            return pltpu.make_async_copy(src_hbm.at[idx_v.at[pl.ds(step * win, win)]], rows_v.at[slot], sems[slot])

        gather(0, 0).start()

        @pl.loop(0, steps, step=2)
        def _(s):
            for slot in range(2):
                step = s + slot
                gather(step, slot).wait()

                @pl.when(step + 1 < steps)
                def _():
                    gather(step + 1, 1 - slot).start()

                pltpu.sync_copy(rows_v.at[slot], dst_hbm.at[pl.ds(base + step * win, win)])

    return gather_kernel(src, idx)


def _expert_kernel(blk_exp_ref, n_used_ref, x_ref, wgu_ref, bgu_ref, wd_ref, bd_ref, o_ref, wgu_bf, wd_bf):
    i = pl.program_id(0)

    @pl.when(jnp.logical_or(i == 0, blk_exp_ref[i] != blk_exp_ref[jnp.maximum(i - 1, 0)]))
    def _():
        wgu_bf[...] = wgu_ref[0, 0].astype(BF16)
        wd_bf[...] = wd_ref[0, 0].astype(BF16)

    @pl.when(i < n_used_ref[0])
    def _():
        x_lo, x_hi = _unpack_bf16_pairs(x_ref[...])
        hgu = (_dot(x_lo.astype(BF16), wgu_bf[:HALF_D, :]) + _dot(x_hi.astype(BF16), wgu_bf[HALF_D:, :])
               + bgu_ref[0])
        gate = jnp.minimum(hgu[:, :D_EXPERT], SWIGLU_LIMIT)
        up = jnp.clip(hgu[:, D_EXPERT:], -SWIGLU_LIMIT, SWIGLU_LIMIT)
        act = gate * jax.nn.sigmoid(gate * SWIGLU_ALPHA) * (up + 1.0)
        o_ref[...] = _pack_bf16_pairs(_dot(act.astype(BF16), wd_bf[...]) + bd_ref[0])

    @pl.when(i >= n_used_ref[0])
    def _():
        o_ref[...] = jnp.zeros_like(o_ref)


def _expert_mlp(xs, block_exp, n_used, layer, w_gu, b_gu, w_down, b_down):
    n_rows = xs.shape[0]
    bm = MOE_BLOCK_ROWS
    row = lambda i, be, nu: (i, 0)
    exp3 = lambda i, be, nu: (be[i], 0, 0)
    exp4 = lambda i, be, nu: (layer, be[i], 0, 0)
    grid_spec = pltpu.PrefetchScalarGridSpec(
        num_scalar_prefetch=2,
        grid=(n_rows // bm,),
        in_specs=[pl.BlockSpec((bm, HALF_D), row),
                  pl.BlockSpec((1, 1, D_MODEL, 2 * D_EXPERT), exp4), pl.BlockSpec((1, 1, 2 * D_EXPERT), exp3),
                  pl.BlockSpec((1, 1, D_EXPERT, D_MODEL), exp4), pl.BlockSpec((1, 1, D_MODEL), exp3)],
        out_specs=pl.BlockSpec((bm, HALF_D), row),
        scratch_shapes=[pltpu.VMEM((D_MODEL, 2 * D_EXPERT), BF16), pltpu.VMEM((D_EXPERT, D_MODEL), BF16)],
    )
    return pl.pallas_call(
        _expert_kernel,
        grid_spec=grid_spec,
        out_shape=jax.ShapeDtypeStruct((n_rows, HALF_D), I32),
        compiler_params=_cparams("arbitrary"),
        name="moe_experts",
    )(block_exp, n_used, xs, w_gu, b_gu.reshape(N_EXPERTS, 1, 2 * D_EXPERT),
      w_down, b_down.reshape(N_EXPERTS, 1, D_MODEL))


def _combine_kernel(y0_ref, y1_ref, y2_ref, y3_ref, gate_ref, res_ref, g_ref, b_ref, o_ref):
    gates = gate_ref[...]
    acc_lo, acc_hi = None, None
    for k, y_ref in enumerate((y0_ref, y1_ref, y2_ref, y3_ref)):
        lo, hi = _unpack_bf16_pairs(y_ref[...])
        gk = gates[:, k:k + 1]
        acc_lo = lo * gk if acc_lo is None else acc_lo + lo * gk
        acc_hi = hi * gk if acc_hi is None else acc_hi + hi * gk
    acc = jnp.concatenate([acc_lo, acc_hi], axis=1)
    o_ref[...] = _layer_norm_rows(DN_ALPHA * res_ref[...] + acc, g_ref[...], b_ref[...])


def _moe_block(h2d, h_packed, layer, w_router, b_router, w_gu, b_gu, w_down, b_down, g, b):
    n = h2d.shape[0]
    n_slots = n * TOP_K
    bm = MOE_BLOCK_ROWS
    top_idx, gates, rank, totals = _router(h2d, w_router, b_router)

    e_flat = top_idx.reshape(-1)
    assert N_EXPERTS * n_slots < 2 ** 31
    order = jnp.sort(e_flat * n_slots + jnp.arange(n_slots, dtype=I32)) % n_slots
    counts = totals[0, :N_EXPERTS].astype(I32)
    padded = (counts + bm - 1) // bm * bm
    start = jnp.cumsum(counts) - counts
    ends_p = jnp.cumsum(padded)
    pstart = ends_p - padded
    n_rows = n_slots + N_EXPERTS * bm
    n_blocks = n_rows // bm
    r = jnp.arange(n_rows, dtype=I32)
    e_r = jnp.minimum(jnp.searchsorted(ends_p, r, side="right", method="compare_all"), N_EXPERTS - 1).astype(I32)
    j = r - pstart[e_r]
    valid = j < counts[e_r]
    slot_of_row = order[jnp.where(valid, start[e_r] + j, 0)]
    rows_tok = jnp.where(valid, slot_of_row // TOP_K, r % n).astype(I32)
    slot_pos = pstart[e_flat] + rank.reshape(-1)
    block_exp = e_r[::bm]
    n_used = (ends_p[-1] // bm).astype(I32).reshape(1)

    xs = _gather_rows(h_packed, rows_tok)
    ys = _expert_mlp(xs, block_exp, n_used, layer, w_gu, b_gu, w_down, b_down)
    yk = _gather_rows(ys, slot_pos.reshape(n, TOP_K).T.reshape(-1))

    tm = ROW_TILE
    row = lambda i: (i, 0)
    const = lambda i: (0, 0)
    choice = lambda k: (lambda i: (k * (n // tm) + i, 0))
    return pl.pallas_call(
        _combine_kernel,
        grid=(n // tm,),
        in_specs=[pl.BlockSpec((tm, HALF_D), choice(k)) for k in range(TOP_K)] + [
                  pl.BlockSpec((tm, TOP_K), row),
                  pl.BlockSpec((tm, D_MODEL), row), pl.BlockSpec((1, D_MODEL), const),
                  pl.BlockSpec((1, D_MODEL), const)],
        out_specs=pl.BlockSpec((tm, D_MODEL), row),
        out_shape=jax.ShapeDtypeStruct((n, D_MODEL), F32),
        compiler_params=_cparams("parallel"),
        name="moe_combine",
    )(yk, yk, yk, yk, gates, h2d, g.reshape(1, D_MODEL), b.reshape(1, D_MODEL))


def _s5_kernel(u_ref, bre_ref, bim_ref, cre_ref, cim_ref, are_ref, aim_ref, d_ref, y_ref,
               bu_re, bu_im, st_re, st_im, h_re, h_im, *, bsz):
    @pl.when(pl.program_id(0) == 0)
    def _():
        h_re[...] = jnp.zeros_like(h_re)
        h_im[...] = jnp.zeros_like(h_im)

    rows = u_ref.shape[0]
    first = lax.broadcasted_iota(I32, (SUBLANES, S5_ST_BLK), 0) < bsz
    for j in range(S5_LANE_BLOCKS):
        cin = slice(j * S5_IN_BLK, (j + 1) * S5_IN_BLK)
        cst = slice(j * S5_ST_BLK, (j + 1) * S5_ST_BLK)
        uj = u_ref[:, cin]
        ujb = uj.astype(BF16)
        bu_re[...] = _dot(ujb, bre_ref[j])
        bu_im[...] = _dot(ujb, bim_ref[j])
        ar = jnp.broadcast_to(are_ref[:, cst], (SUBLANES, S5_ST_BLK))
        ai = jnp.broadcast_to(aim_ref[:, cst], (SUBLANES, S5_ST_BLK))

        def step(i, carry):
            hr, hi = carry
            r0 = pl.multiple_of(i * SUBLANES, SUBLANES)
            vr = bu_re[pl.ds(r0, SUBLANES), :]
            vi = bu_im[pl.ds(r0, SUBLANES), :]
            h1r = ar * hr - ai * hi + vr
            h1i = ar * hi + ai * hr + vi
            h1rs = pltpu.roll(h1r, bsz, 0)
            h1is = pltpu.roll(h1i, bsz, 0)
            h2r = ar * h1rs - ai * h1is + vr
            h2i = ar * h1is + ai * h1rs + vi
            st_re[pl.ds(r0, SUBLANES), :] = jnp.where(first, h1r, h2r)
            st_im[pl.ds(r0, SUBLANES), :] = jnp.where(first, h1i, h2i)
            return pltpu.roll(h2r, bsz, 0), pltpu.roll(h2i, bsz, 0)

        hr, hi = lax.fori_loop(0, rows // SUBLANES, step, (h_re[:, cst], h_im[:, cst]))
        h_re[:, cst] = hr
        h_im[:, cst] = hi
        yj = _dot(st_re[...].astype(BF16), cre_ref[j]) + _dot(st_im[...].astype(BF16), cim_ref[j])
        yj = yj + d_ref[:, cin] * uj
        y_ref[:, cin] = jax.nn.gelu(yj).astype(BF16)


def _s5_block_diag(w, n_in, n_out):
    gpb = SSM_GROUPS // S5_LANE_BLOCKS
    w4 = w.reshape(S5_LANE_BLOCKS, gpb, n_in, n_out)
    eye = jnp.eye(gpb, dtype=w.dtype)
    return jnp.einsum("jgio,gh->jgiho", w4, eye).reshape(S5_LANE_BLOCKS, gpb * n_in, gpb * n_out)


def _s5_mixer_block(h2d, bsz, seq, w_in, log_dt, lam_re, lam_im, b_re, b_im, c_re, c_im, d, w_out, g, b):
    assert 2 * bsz == SUBLANES, "the scan packs two time steps of bsz rows into one 8-row tile"
    tm = ROW_TILE
    per_seq = seq // tm
    u_t = _matmul(h2d, w_in.astype(BF16), tm=tm, out_dtype=F32, grid=(bsz, per_seq),
                  x_map=lambda bb, i: (bb * per_seq + i, 0), out_map=lambda bb, i: (i, bb),
                  out_shape=(seq, bsz * D_MODEL)).reshape(seq * bsz, D_MODEL)

    dt = jnp.exp(log_dt)[:, None]
    mag = jnp.exp(lam_re * dt)
    a_re, a_im = mag * jnp.cos(lam_im * dt), mag * jnp.sin(lam_im * dt)
    den = lam_re * lam_re + lam_im * lam_im
    coef_re = ((a_re - 1.0) * lam_re + a_im * lam_im) / den
    coef_im = (a_im * lam_re - (a_re - 1.0) * lam_im) / den
    bb_re = coef_re[..., None] * b_re - coef_im[..., None] * b_im
    bb_im = coef_re[..., None] * b_im + coef_im[..., None] * b_re
    bre = _s5_block_diag(jnp.swapaxes(bb_re, 1, 2), SSM_GROUP, SSM_STATE).astype(BF16)
    bim = _s5_block_diag(jnp.swapaxes(bb_im, 1, 2), SSM_GROUP, SSM_STATE).astype(BF16)
    cre = _s5_block_diag(jnp.swapaxes(c_re, 1, 2), SSM_STATE, SSM_GROUP).astype(BF16)
    cim = _s5_block_diag(jnp.swapaxes(-c_im, 1, 2), SSM_STATE, SSM_GROUP).astype(BF16)
    n_state = SSM_GROUPS * SSM_STATE

    rows = S5_CHUNK * bsz
    row = lambda c: (c, 0)
    c2 = lambda c: (0, 0)
    c3 = lambda c: (0, 0, 0)
    y_t = pl.pallas_call(
        functools.partial(_s5_kernel, bsz=bsz),
        grid=(seq // S5_CHUNK,),
        in_specs=[pl.BlockSpec((rows, D_MODEL), row),
                  pl.BlockSpec(bre.shape, c3), pl.BlockSpec(bim.shape, c3),
                  pl.BlockSpec(cre.shape, c3), pl.BlockSpec(cim.shape, c3),
                  pl.BlockSpec((1, n_state), c2), pl.BlockSpec((1, n_state), c2), pl.BlockSpec((1, D_MODEL), c2)],
        out_specs=pl.BlockSpec((rows, D_MODEL), row),
        out_shape=jax.ShapeDtypeStruct((seq * bsz, D_MODEL), BF16),
        scratch_shapes=[pltpu.VMEM((rows, S5_ST_BLK), F32)] * 4 + [pltpu.VMEM((SUBLANES, n_state), F32)] * 2,
        compiler_params=_cparams("arbitrary"),
        name="s5_scan",
    )(u_t, bre, bim, cre, cim, a_re.reshape(1, n_state), a_im.reshape(1, n_state), d.reshape(1, D_MODEL))

    y2 = y_t.reshape(seq, bsz * D_MODEL)
    return _linear_residual_ln(
        [y2], [w_out.astype(BF16)], h2d, g, b, tm=tm, glu=True, grid=(bsz, per_seq),
        x_maps=[lambda bb, i: (i, bb)], res_map=lambda bb, i: (bb * per_seq + i, 0))


def _even_mixer_block(h2d, bsz, seq, w_in, qnorm_g, w_uq, w_uq_idx, kidx_g, kidx_b, w_out, g, b):
    qa, qi, ka, va, ki, wi, qb, kb, vb = _even_proj(h2d, bsz, seq, w_in, qnorm_g, w_uq, w_uq_idx, kidx_g, kidx_b)
    o_a = _dsa_attention(qa, qi, wi, ki, ka, va, bsz, seq)
    o_b = _stick_breaking(qb, kb, vb, bsz, seq)
    w_out = w_out.astype(BF16)
    return _linear_residual_ln([o_a, o_b], [w_out[:A_WIDTH], w_out[A_WIDTH:]], h2d, g, b, tm=ROW_TILE)


def kernel(x, mem, ev_w_in, ev_qnorm_g, ev_w_uq, ev_w_uq_idx, ev_kidx_ln_g, ev_kidx_ln_b, ev_w_out, od_w_in, od_log_dt, od_lambda_re, od_lambda_im, od_b_re, od_b_im, od_c_re, od_c_im, od_d, od_w_out, mix_ln_g, mix_ln_b, xa_w_q, xa_w_kv, xa_w_o, xa_ln_g, xa_ln_b, moe_w_router, moe_b_router, moe_w_gu, moe_b_gu, moe_w_down, moe_b_down, ffn_ln_g, ffn_ln_b):
    bsz, seq, _ = x.shape
    h = x.reshape(bsz * seq, D_MODEL)
    for layer in range(DEPTH):
        j = layer // 2
        if layer % 2 == 0:
            h = _even_mixer_block(h, bsz, seq, ev_w_in[j], ev_qnorm_g[j], ev_w_uq[j], ev_w_uq_idx[j],
                                  ev_kidx_ln_g[j], ev_kidx_ln_b[j], ev_w_out[j], mix_ln_g[layer], mix_ln_b[layer])
        else:
            h = _s5_mixer_block(h, bsz, seq, od_w_in[j], od_log_dt[j], od_lambda_re[j], od_lambda_im[j],
                                od_b_re[j], od_b_im[j], od_c_re[j], od_c_im[j], od_d[j], od_w_out[j],
                                mix_ln_g[layer], mix_ln_b[layer])
        h, h_packed = _cross_attention_block(h, mem, bsz, seq, xa_w_q[layer], xa_w_kv[layer], xa_w_o[layer],
                                             xa_ln_g[layer], xa_ln_b[layer])
        h = _moe_block(h, h_packed, layer, moe_w_router[layer], moe_b_router[layer], moe_w_gu, moe_b_gu[layer],
                       moe_w_down, moe_b_down[layer], ffn_ln_g[layer], ffn_ln_b[layer])
    return h.reshape(bsz, seq, D_MODEL)
```

```python
import functools
import math

import jax
import jax.numpy as jnp
from jax import lax
from jax.experimental import pallas as pl
from jax.experimental.pallas import tpu as pltpu
from jax.experimental.pallas import tpu_sc as plsc

F32 = jnp.float32
BF16 = jnp.bfloat16
I32 = jnp.int32
U32 = jnp.uint32
HIGH16 = 0xFFFF0000

D_MODEL = 1024
HALF_D = D_MODEL // 2
DEPTH = 2
HEAD_DIM = 64
A_HEADS = 8
A_KV_HEADS = 2
A_REP = A_HEADS // A_KV_HEADS
Q_RANK = 256
IDX_HEADS = 8
IDX_DIM = 64
IDX_TOPK = 256
B_HEADS = 8
A_WIDTH = A_HEADS * HEAD_DIM
B_WIDTH = B_HEADS * HEAD_DIM
SSM_GROUP = 16
SSM_GROUPS = D_MODEL // SSM_GROUP
SSM_STATE = 64
XA_HEADS = 4
XA_HEAD_DIM = D_MODEL // XA_HEADS
N_EXPERTS = 32
TOP_K = 4
D_EXPERT = D_MODEL
SWIGLU_LIMIT = 7.0
SWIGLU_ALPHA = 1.702
ROPE_THETA = 500000.0
ROPE_HALF = HEAD_DIM // 8
LN_EPS = 1e-5
DN_ALPHA = (2 * DEPTH) ** 0.25

LANES = 128
SUBLANES = 8
VMEM_LIMIT_BYTES = 56 * 1024 * 1024

Q_BLOCK = 256
DSA_KEY_TILE = 512
DSA_ATT_TILE = 1024
DSA_ATT_PAR = 1
DSA_COUNT_ROWS = 8 * SUBLANES
SB_KEY_TILE = 256
ROW_TILE = 512
MOE_BLOCK_ROWS = 512
SC_GATHER_SLOT_BYTES = 128 * 1024
S5_CHUNK = 256
S5_LANE_BLOCKS = 4
S5_IN_BLK = D_MODEL // S5_LANE_BLOCKS
S5_ST_BLK = SSM_GROUPS * SSM_STATE // S5_LANE_BLOCKS

SB_EXIT_LOG = -104.0
NEG_BIG = -1e30
INT_MIN = -(2 ** 31)


def _cparams(*sem):
    return pltpu.CompilerParams(dimension_semantics=sem, vmem_limit_bytes=VMEM_LIMIT_BYTES)


def _dot(a, b):
    return jnp.dot(a, b, preferred_element_type=F32)


def _dot_nt(a, b):
    return lax.dot_general(a, b, (((1,), (1,)), ((), ())), preferred_element_type=F32)


def _layer_norm_rows(y, g, b):
    mu = jnp.mean(y, axis=-1, keepdims=True)
    d = y - mu
    var = jnp.mean(d * d, axis=-1, keepdims=True)
    return d * lax.rsqrt(var + LN_EPS) * g + b


def _mm_kernel(x_ref, w_ref, o_ref):
    o_ref[...] = _dot(x_ref[...].astype(BF16), w_ref[...]).astype(o_ref.dtype)


def _matmul(x, w, *, tm, out_dtype, x_map=None, out_map=None, grid=None, out_shape=None):
    m, k = x.shape
    n = w.shape[1]
    grid = grid or (m // tm,)
    x_map = x_map or (lambda i: (i, 0))
    out_map = out_map or (lambda i: (i, 0))
    out_shape = out_shape or (m, n)
    return pl.pallas_call(
        _mm_kernel,
        grid=grid,
        in_specs=[pl.BlockSpec((tm, k), x_map), pl.BlockSpec((k, n), lambda *a: (0, 0))],
        out_specs=pl.BlockSpec((tm, n), out_map),
        out_shape=jax.ShapeDtypeStruct(out_shape, out_dtype),
        compiler_params=_cparams(*(("parallel",) * len(grid))),
        name="matmul",
    )(x, w)


def _lin_ln_kernel(*refs, n_in, glu):
    xs, ws = refs[:n_in], refs[n_in:2 * n_in]
    res_ref, g_ref, b_ref, o_ref = refs[2 * n_in:]
    acc = _dot(xs[0][...].astype(BF16), ws[0][...])
    for x_ref, w_ref in zip(xs[1:], ws[1:]):
        acc = acc + _dot(x_ref[...].astype(BF16), w_ref[...])
    if glu:
        acc = acc[:, :D_MODEL] * jax.nn.sigmoid(acc[:, D_MODEL:])
    y = DN_ALPHA * res_ref[...] + acc
    o_ref[...] = _layer_norm_rows(y, g_ref[...], b_ref[...])


def _linear_residual_ln(xs, ws, res, g, b, *, tm, glu=False, grid=None, x_maps=None, res_map=None):
    n_rows = res.shape[0]
    grid = grid or (n_rows // tm,)
    x_maps = x_maps or [lambda i: (i, 0)] * len(xs)
    res_map = res_map or (lambda i: (i, 0))
    const = lambda *a: (0, 0)
    in_specs = [pl.BlockSpec((tm, w.shape[0]), m) for w, m in zip(ws, x_maps)]
    in_specs += [pl.BlockSpec(w.shape, const) for w in ws]
    in_specs += [pl.BlockSpec((tm, D_MODEL), res_map), pl.BlockSpec((1, D_MODEL), const),
                 pl.BlockSpec((1, D_MODEL), const)]
    return pl.pallas_call(
        functools.partial(_lin_ln_kernel, n_in=len(xs), glu=glu),
        grid=grid,
        in_specs=in_specs,
        out_specs=pl.BlockSpec((tm, D_MODEL), res_map),
        out_shape=jax.ShapeDtypeStruct((n_rows, D_MODEL), F32),
        compiler_params=_cparams(*(("parallel",) * len(grid))),
        name="linear_residual_ln",
    )(*xs, *ws, res, g.reshape(1, D_MODEL), b.reshape(1, D_MODEL))


_EV_CQ, _EV_KA, _EV_VA, _EV_KI, _EV_QB = 0, 256, 384, 512, 640
_EV_KB = _EV_QB + B_WIDTH
_EV_VB = _EV_KB + B_WIDTH
_EV_COLS = _EV_VB + B_WIDTH


def _rope_tables(seq):
    inv = ROPE_THETA ** (-jnp.arange(ROPE_HALF, dtype=F32) / ROPE_HALF)
    ang = jnp.arange(seq, dtype=F32)[:, None] * inv[None, :]
    cos, sin = jnp.cos(ang), jnp.sin(ang)
    rest = HEAD_DIM - 2 * ROPE_HALF
    zh = jnp.zeros((seq, ROPE_HALF), F32)
    c = jnp.concatenate([cos, cos, jnp.ones((seq, rest), F32)], axis=1)
    s1 = jnp.concatenate([-sin, zh, jnp.zeros((seq, rest), F32)], axis=1)
    s2 = jnp.concatenate([zh, sin, jnp.zeros((seq, rest), F32)], axis=1)
    rep = LANES // HEAD_DIM
    return jnp.tile(c, (1, rep)), jnp.tile(s1, (1, rep)), jnp.tile(s2, (1, rep))


def _even_proj_kernel(x_ref, w_ref, qg_ref, wuq_ref, wuqi_ref, lg_ref, lb_ref, c_ref, s1_ref, s2_ref,
                      qa_ref, qi_ref, ka_ref, va_ref, ki_ref, wi_ref, qb_ref, kb_ref, vb_ref):
    p = _dot(x_ref[...].astype(BF16), w_ref[...])
    c, s1, s2 = c_ref[...], s1_ref[...], s2_ref[...]

    def rope(t):
        return (t * c + pltpu.roll(t, LANES - ROPE_HALF, 1) * s1 + pltpu.roll(t, ROPE_HALF, 1) * s2)

    cq = p[:, _EV_CQ:_EV_CQ + Q_RANK]
    cn = cq * lax.rsqrt(jnp.mean(cq * cq, axis=-1, keepdims=True) + LN_EPS) * qg_ref[...]
    cnb = cn.astype(BF16)
    qa = _dot(cnb, wuq_ref[...])
    qi = _dot(cnb, wuqi_ref[...])
    low = lax.broadcasted_iota(I32, c.shape, 1) < HEAD_DIM
    for j in range(A_WIDTH // LANES):
        sl = slice(j * LANES, (j + 1) * LANES)
        pair = rope(qa[:, sl]) * (HEAD_DIM ** -0.5)
        for e, src in enumerate((pair, pltpu.roll(pair, HEAD_DIM, 1))):
            h = 2 * j + e
            qa_ref[:, h * LANES:(h + 1) * LANES] = jnp.where(low, src, 0.0).astype(BF16)
        qi_ref[:, sl] = (rope(qi[:, sl]) * (IDX_DIM ** -0.5)).astype(BF16)
    kpair = rope(p[:, _EV_KA:_EV_KA + LANES])
    vpair = p[:, _EV_VA:_EV_VA + LANES]
    v_pad = jnp.where(lax.broadcasted_iota(I32, c.shape, 1) == HEAD_DIM, 1.0, 0.0)
    for g, (ks, vs) in enumerate(((kpair, vpair), (pltpu.roll(kpair, HEAD_DIM, 1), pltpu.roll(vpair, HEAD_DIM, 1)))):
        ka_ref[:, g * LANES:(g + 1) * LANES] = jnp.where(low, ks, 0.0).astype(BF16)
        va_ref[:, g * LANES:(g + 1) * LANES] = jnp.where(low, vs, v_pad).astype(BF16)

    t = p[:, _EV_KI:_EV_KI + LANES]
    lane = lax.broadcasted_iota(I32, t.shape, 1)
    is_k = lane < IDX_DIM
    mu = jnp.sum(jnp.where(is_k, t, 0.0), axis=-1, keepdims=True) * (1.0 / IDX_DIM)
    d = jnp.where(is_k, t - mu, 0.0)
    var = jnp.sum(d * d, axis=-1, keepdims=True) * (1.0 / IDX_DIM)
    kin = d * lax.rsqrt(var + LN_EPS) * lg_ref[...] + lb_ref[...]
    ki_ref[...] = rope(kin)[:, :IDX_DIM].astype(BF16)
    wi_ref[...] = t[:, IDX_DIM:IDX_DIM + IDX_HEADS] * (IDX_HEADS ** -0.5)

    qb_ref[...] = (p[:, _EV_QB:_EV_KB] * (HEAD_DIM ** -0.5)).astype(BF16)
    kb_ref[...] = p[:, _EV_KB:_EV_VB].astype(BF16)
    vb_ref[...] = p[:, _EV_VB:_EV_COLS].astype(BF16)


def _even_proj(x2d, bsz, seq, w_in, qnorm_g, w_uq, w_uq_idx, kidx_g, kidx_b):
    n = x2d.shape[0]
    tm = ROW_TILE
    per_seq = seq // tm
    c0 = Q_RANK + 2 * A_KV_HEADS * HEAD_DIM + IDX_DIM + IDX_HEADS
    w_pack = jnp.concatenate(
        [w_in[:, :c0], jnp.zeros((D_MODEL, _EV_QB - c0), w_in.dtype), w_in[:, c0:]], axis=1).astype(BF16)
    pad = LANES - IDX_DIM
    lg = jnp.concatenate([kidx_g, jnp.zeros((pad,), F32)]).reshape(1, LANES)
    lb = jnp.concatenate([kidx_b, jnp.zeros((pad,), F32)]).reshape(1, LANES)
    c, s1, s2 = _rope_tables(seq)
    row = lambda i: (i, 0)
    const = lambda i: (0, 0)
    pos = lambda i: (i % per_seq, 0)
    head_shape = jax.ShapeDtypeStruct((n, B_WIDTH), BF16)
    head_spec = pl.BlockSpec((tm, B_WIDTH), row)
    return pl.pallas_call(
        _even_proj_kernel,
        grid=(n // tm,),
        in_specs=[pl.BlockSpec((tm, D_MODEL), row), pl.BlockSpec((D_MODEL, _EV_COLS), const),
                  pl.BlockSpec((1, Q_RANK), const), pl.BlockSpec((Q_RANK, A_WIDTH), const),
                  pl.BlockSpec((Q_RANK, IDX_HEADS * IDX_DIM), const),
                  pl.BlockSpec((1, LANES), const), pl.BlockSpec((1, LANES), const),
                  pl.BlockSpec((tm, LANES), pos), pl.BlockSpec((tm, LANES), pos), pl.BlockSpec((tm, LANES), pos)],
        out_specs=[pl.BlockSpec((tm, A_HEADS * LANES), row), pl.BlockSpec((tm, IDX_HEADS * IDX_DIM), row),
                   pl.BlockSpec((tm, A_KV_HEADS * LANES), row), pl.BlockSpec((tm, A_KV_HEADS * LANES), row),
                   pl.BlockSpec((tm, IDX_DIM), row), pl.BlockSpec((tm, IDX_HEADS), row),
                   head_spec, head_spec, head_spec],
        out_shape=[jax.ShapeDtypeStruct((n, A_HEADS * LANES), BF16), jax.ShapeDtypeStruct((n, IDX_HEADS * IDX_DIM), BF16),
                   jax.ShapeDtypeStruct((n, A_KV_HEADS * LANES), BF16),
                   jax.ShapeDtypeStruct((n, A_KV_HEADS * LANES), BF16),
                   jax.ShapeDtypeStruct((n, IDX_DIM), BF16), jax.ShapeDtypeStruct((n, IDX_HEADS), F32),
                   head_shape, head_shape, head_shape],
        compiler_params=_cparams("parallel"),
        name="even_proj",
    )(x2d, w_pack, qnorm_g.reshape(1, Q_RANK), w_uq.astype(BF16), w_uq_idx.astype(BF16), lg, lb, c, s1, s2)


def _key_to_float(key):
    bits = key ^ ((key >> 31) & jnp.int32(0x7FFFFFFF))
    return lax.bitcast_convert_type(bits, F32)


def _high_half(x):
    bits = lax.bitcast_convert_type(x, U32) & U32(HIGH16)
    return lax.bitcast_convert_type(bits, F32).astype(BF16)


def _dsa_kernel(qa_ref, qi_ref, wit_ref, ki_ref, ka_ref, vat_ref, o_ref, sc_scr, hi_scr, *, topk, ts, ta):
    seq = sc_scr.shape[0]
    qb = pl.program_id(1)
    q0 = qb * Q_BLOCK
    nkt = (q0 + Q_BLOCK - 1) // ts + 1
    t_row = q0 + lax.broadcasted_iota(I32, (1, Q_BLOCK), 1)
    key = lax.broadcasted_iota(I32, (ts, Q_BLOCK), 0)
    kf = jnp.float32(topk)

    qi = qi_ref[0]
    qs = jnp.concatenate([qi[:, h * IDX_DIM:(h + 1) * IDX_DIM] for h in range(IDX_HEADS)], axis=0)
    wit = wit_ref[0]

    def score_tile(kt, carry):
        off = pl.multiple_of(kt * ts, ts)
        s_all = _dot_nt(ki_ref[0, pl.ds(off, ts), :], qs)
        acc = jnp.zeros((ts, Q_BLOCK), F32)
        for h in range(IDX_HEADS):
            acc = acc + jnp.maximum(s_all[:, h * Q_BLOCK:(h + 1) * Q_BLOCK], 0.0) * wit[h:h + 1, :]
        val = jnp.where(off + key <= t_row, acc, -jnp.inf)
        sc_scr[pl.ds(off, ts), :] = val
        hi_scr[pl.ds(off, ts), :] = _high_half(val)
        return carry

    lax.fori_loop(0, nkt, score_tile, 0)

    def count(pred):
        def body(kt, acc):
            off = pl.multiple_of(kt * ts, ts)
            ind = pred(sc_scr[pl.ds(off, ts), :], off + key)
            return acc + jnp.sum(ind.reshape(ts // DSA_COUNT_ROWS, DSA_COUNT_ROWS, Q_BLOCK), axis=0)
        acc = lax.fori_loop(0, nkt, body, jnp.zeros((DSA_COUNT_ROWS, Q_BLOCK), F32))
        return jnp.sum(acc, axis=0, keepdims=True)

    one_h, zero_h = jnp.ones((), BF16), jnp.zeros((), BF16)

    def count_high(c_hi):
        def body(kt, acc):
            off = pl.multiple_of(kt * ts, ts)
            ind = jnp.where(hi_scr[pl.ds(off, ts), :] >= c_hi, one_h, zero_h)
            part = ind[:DSA_COUNT_ROWS]
            for j in range(1, ts // DSA_COUNT_ROWS):
                part = part + ind[j * DSA_COUNT_ROWS:(j + 1) * DSA_COUNT_ROWS]
            return acc + part.astype(F32)
        acc = lax.fori_loop(0, nkt, body, jnp.zeros((DSA_COUNT_ROWS, Q_BLOCK), F32))
        return jnp.sum(acc, axis=0, keepdims=True)

    def high_step(i, base):
        cand = base + jnp.left_shift(jnp.int32(1), 31 - i)
        cnt = count_high(_high_half(_key_to_float(cand)))
        return jnp.where(cnt >= kf, cand, base)

    def bit_step(i, base):
        cand = base + jnp.left_shift(jnp.int32(1), 31 - i)
        cf = _key_to_float(cand)
        cnt = count(lambda sc, idx: jnp.where(sc >= cf, 1.0, 0.0))
        return jnp.where(cnt >= kf, cand, base)

    base = lax.fori_loop(0, 16, high_step, jnp.full((1, Q_BLOCK), INT_MIN, I32))
    base = lax.fori_loop(16, 32, bit_step, base)
    thr = jnp.where(base == INT_MIN, -jnp.inf, _key_to_float(base))

    cnt_ge = count(lambda sc, idx: jnp.where(sc >= thr, 1.0, 0.0))
    tied = jnp.logical_and(cnt_ge > kf, thr > -jnp.inf)
    any_tied = jnp.max(jnp.where(tied, 1.0, 0.0)) > 0.0
    seq_bits = max(1, int(math.ceil(math.log2(seq))))

    def tie_cut():
        cnt_gt = count(lambda sc, idx: jnp.where(sc > thr, 1.0, 0.0))
        need = kf - cnt_gt

        def idx_step(i, pos):
            cand = pos + jnp.left_shift(jnp.int32(1), seq_bits - 1 - i)
            cnt = count(lambda sc, idx: jnp.where(sc == thr, jnp.where(idx < cand, 1.0, 0.0), 0.0))
            return jnp.where(cnt < need, cand, pos)

        return lax.fori_loop(0, seq_bits, idx_step, jnp.zeros((1, Q_BLOCK), I32))

    cut = lax.cond(any_tied, tie_cut, lambda: jnp.full((1, Q_BLOCK), seq, I32))
    cut = jnp.where(tied, cut, seq)

    nkt_a = (q0 + Q_BLOCK - 1) // ta + 1
    key_a = lax.broadcasted_iota(I32, (ta, Q_BLOCK), 0)
    cols = A_REP * Q_BLOCK
    qg = [jnp.concatenate([qa_ref[0, :, (g * A_REP + r) * LANES:(g * A_REP + r + 1) * LANES]
                           for r in range(A_REP)], axis=0) for g in range(A_KV_HEADS)]

    def att_step(i, carry):
        offs = [pl.multiple_of((DSA_ATT_PAR * i + e) * ta, ta) for e in range(DSA_ATT_PAR)]
        logits = [_dot_nt(ka_ref[0, pl.ds(offs[e], ta), g * LANES:(g + 1) * LANES], qg[g])
                  for e in range(DSA_ATT_PAR) for g in range(A_KV_HEADS)]
        out = []
        for e in range(DSA_ATT_PAR):
            sc = sc_scr[pl.ds(offs[e], ta), :]
            idx = offs[e] + key_a
            keep = jnp.where(sc > thr, 0.0, jnp.where(sc == thr, jnp.where(idx <= cut, 0.0, NEG_BIG), NEG_BIG))
            bias = jnp.where(idx <= t_row, keep, NEG_BIG)
            bias = jnp.concatenate([bias] * A_REP, axis=1)
            for g in range(A_KV_HEADS):
                m, acc = carry[e * A_KV_HEADS + g]
                s = logits[e * A_KV_HEADS + g] + bias
                m_new = jnp.maximum(m, jnp.max(s, axis=0, keepdims=True))
                p = jnp.exp(s - m_new)
                vt = vat_ref[0, g * LANES:(g + 1) * LANES, pl.ds(offs[e], ta)]
                out.append((m_new, jnp.exp(m - m_new) * acc + _dot(vt, p.astype(BF16))))
        return tuple(out)

    init = tuple((jnp.full((1, cols), NEG_BIG, F32), jnp.zeros((LANES, cols), F32))
                 for _ in range(DSA_ATT_PAR * A_KV_HEADS))
    final = lax.fori_loop(0, (nkt_a + DSA_ATT_PAR - 1) // DSA_ATT_PAR, att_step, init)
    low = lax.broadcasted_iota(I32, (Q_BLOCK, LANES), 1) < HEAD_DIM
    outs = []
    for g in range(A_KV_HEADS):
        m, acc = final[g]
        for e in range(1, DSA_ATT_PAR):
            m1, acc1 = final[e * A_KV_HEADS + g]
            m_all = jnp.maximum(m, m1)
            m, acc = m_all, jnp.exp(m - m_all) * acc + jnp.exp(m1 - m_all) * acc1
        og = acc / acc[HEAD_DIM:HEAD_DIM + 1, :]
        outs += [og[:, r * Q_BLOCK:(r + 1) * Q_BLOCK].T for r in range(A_REP)]
    for j in range(A_HEADS // 2):
        pair = jnp.where(low, outs[2 * j], pltpu.roll(outs[2 * j + 1], HEAD_DIM, 1))
        o_ref[0, :, j * LANES:(j + 1) * LANES] = pair.astype(BF16)


def _dsa_attention(qa, qi, wi, ki, ka, va, bsz, seq):
    topk = min(IDX_TOPK, seq // 4)
    ts = min(DSA_KEY_TILE, seq)
    ta = min(DSA_ATT_TILE, seq // DSA_ATT_PAR)
    assert seq % (DSA_ATT_PAR * ta) == 0 and seq % ts == 0, "the attention loop walks whole groups of key tiles"
    blk = lambda b, i: (b, i, 0)
    full = lambda b, i: (b, 0, 0)
    r3 = lambda a: a.reshape(bsz, seq, a.shape[-1])
    wit = jnp.swapaxes(r3(wi), 1, 2)
    vat = jnp.swapaxes(r3(va), 1, 2)
    return pl.pallas_call(
        functools.partial(_dsa_kernel, topk=topk, ts=ts, ta=ta),
        grid=(bsz, seq // Q_BLOCK),
        in_specs=[pl.BlockSpec((1, Q_BLOCK, A_HEADS * LANES), blk),
                  pl.BlockSpec((1, Q_BLOCK, IDX_HEADS * IDX_DIM), blk),
                  pl.BlockSpec((1, IDX_HEADS, Q_BLOCK), lambda b, i: (b, 0, i)),
                  pl.BlockSpec((1, seq, IDX_DIM), full),
                  pl.BlockSpec((1, seq, A_KV_HEADS * LANES), full),
                  pl.BlockSpec((1, A_KV_HEADS * LANES, seq), full)],
        out_specs=pl.BlockSpec((1, Q_BLOCK, A_WIDTH), blk),
        out_shape=jax.ShapeDtypeStruct((bsz, seq, A_WIDTH), BF16),
        scratch_shapes=[pltpu.VMEM((seq, Q_BLOCK), F32), pltpu.VMEM((seq, Q_BLOCK), BF16)],
        compiler_params=_cparams("parallel", "parallel"),
        name="dsa_attention",
    )(r3(qa), r3(qi), wit, r3(ki), r3(ka), vat).reshape(bsz * seq, A_WIDTH)


def _sb_kernel(q_ref, k_ref, v_ref, u_ref, o_ref, acc_scr, run_scr, *, tk):
    q0 = pl.program_id(1) * Q_BLOCK
    t_col = q0 + lax.broadcasted_iota(I32, (Q_BLOCK, 1), 0)
    lane = lax.broadcasted_iota(I32, (Q_BLOCK, tk), 1)
    low = lax.broadcasted_iota(I32, (Q_BLOCK, LANES), 1) < HEAD_DIM
    upper = u_ref[...]
    nkt = (q0 + Q_BLOCK - 1) // tk + 1
    q = q_ref[0]
    zero = jnp.zeros((Q_BLOCK, LANES), BF16)
    qm = []
    for p in range(B_HEADS // 2):
        pair = q[:, p * LANES:(p + 1) * LANES]
        qm.append(jnp.concatenate([jnp.where(low, pair, zero), jnp.where(low, zero, pair)], axis=0))
    acc_scr[...] = jnp.zeros_like(acc_scr)
    run_scr[...] = jnp.zeros_like(run_scr)

    def cond(carry):
        i, worst = carry
        return jnp.logical_and(i < nkt, worst >= SB_EXIT_LOG)

    def body(carry):
        i, _ = carry
        off = pl.multiple_of((nkt - 1 - i) * tk, tk)
        strict = off + lane < t_col
        strict = jnp.concatenate([strict, strict], axis=0)
        worst = None
        for p in range(B_HEADS // 2):
            cols = slice(p * LANES, (p + 1) * LANES)
            kp = k_ref[0, pl.ds(off, tk), cols]
            vp = v_ref[0, pl.ds(off, tk), cols]
            run = run_scr[p]
            z = _dot_nt(qm[p], kp)
            softplus = jnp.maximum(z, 0.0) + jnp.log(1.0 + jnp.exp(-jnp.abs(z)))
            log_1mb = jnp.where(strict, -softplus, 0.0)
            hi = log_1mb.astype(BF16)
            lo = (log_1mb - hi.astype(F32)).astype(BF16)
            after = _dot(hi, upper) + _dot(lo, upper) + run
            a = jnp.where(strict, jnp.exp(z - softplus + after), 0.0)
            out = _dot(a.astype(BF16), vp)
            run = run + jnp.sum(log_1mb, axis=1, keepdims=True)
            run_scr[p] = run
            worst = run if worst is None else jnp.maximum(worst, run)
            acc_scr[:, cols] += jnp.where(low, out[:Q_BLOCK], out[Q_BLOCK:])
        return i + 1, jnp.max(worst)

    lax.while_loop(cond, body, (jnp.int32(0), jnp.float32(0.0)))
    o_ref[0] = acc_scr[...].astype(BF16)


def _stick_breaking(qb, kb, vb, bsz, seq):
    tk = min(SB_KEY_TILE, seq)
    r = lax.broadcasted_iota(I32, (tk, tk), 0)
    c = lax.broadcasted_iota(I32, (tk, tk), 1)
    upper = jnp.where(r > c, 1.0, 0.0).astype(BF16)
    blk = lambda b, i: (b, i, 0)
    full = lambda b, i: (b, 0, 0)
    r3 = lambda a: a.reshape(bsz, seq, B_WIDTH)
    return pl.pallas_call(
        functools.partial(_sb_kernel, tk=tk),
        grid=(bsz, seq // Q_BLOCK),
        in_specs=[pl.BlockSpec((1, Q_BLOCK, B_WIDTH), blk), pl.BlockSpec((1, seq, B_WIDTH), full),
                  pl.BlockSpec((1, seq, B_WIDTH), full), pl.BlockSpec((tk, tk), lambda b, i: (0, 0))],
        out_specs=pl.BlockSpec((1, Q_BLOCK, B_WIDTH), blk),
        out_shape=jax.ShapeDtypeStruct((bsz, seq, B_WIDTH), BF16),
        scratch_shapes=[pltpu.VMEM((Q_BLOCK, B_WIDTH), F32), pltpu.VMEM((B_HEADS // 2, 2 * Q_BLOCK, 1), F32)],
        compiler_params=_cparams("parallel", "arbitrary"),
        name="stick_breaking",
    )(r3(qb), r3(kb), r3(vb), upper).reshape(bsz * seq, B_WIDTH)


def _pack_bf16_pairs(x):
    def bits(v):
        return lax.bitcast_convert_type(v.astype(BF16).astype(F32), U32)
    word = (bits(x[:, HALF_D:]) & U32(HIGH16)) | (bits(x[:, :HALF_D]) >> 16)
    return lax.bitcast_convert_type(word, I32)


def _unpack_bf16_pairs(word):
    u = lax.bitcast_convert_type(word, U32)
    return lax.bitcast_convert_type(u << 16, F32), lax.bitcast_convert_type(u & U32(HIGH16), F32)


def _xattn_kernel(h_ref, wq_ref, kv_ref, wo_ref, g_ref, b_ref, o_ref, packed_ref):
    h = h_ref[...]
    q = (_dot(h.astype(BF16), wq_ref[...]) * (XA_HEAD_DIM ** -0.5)).astype(BF16)
    kv = kv_ref[0]
    outs = []
    for hd in range(XA_HEADS):
        sl = slice(hd * XA_HEAD_DIM, (hd + 1) * XA_HEAD_DIM)
        s = _dot_nt(q[:, sl], kv[:, sl])
        p = jnp.exp(s - jnp.max(s, axis=1, keepdims=True))
        vh = kv[:, D_MODEL + hd * XA_HEAD_DIM:D_MODEL + (hd + 1) * XA_HEAD_DIM]
        outs.append((_dot(p.astype(BF16), vh) / jnp.sum(p, axis=1, keepdims=True)).astype(BF16))
    y = _dot(jnp.concatenate(outs, axis=1), wo_ref[...])
    out = _layer_norm_rows(DN_ALPHA * h + y, g_ref[...], b_ref[...])
    o_ref[...] = out
    packed_ref[...] = _pack_bf16_pairs(out)


def _cross_attention_block(h2d, mem, bsz, seq, w_q, w_kv, w_o, g, b):
    tm = ROW_TILE
    per_seq = seq // tm
    mem_len = mem.shape[1]
    kv = _matmul(mem.reshape(bsz * mem_len, D_MODEL), w_kv.astype(BF16), tm=mem_len, out_dtype=BF16)
    kv = kv.reshape(bsz, mem_len, 2 * D_MODEL)
    row = lambda i: (i, 0)
    const = lambda i: (0, 0)
    return pl.pallas_call(
        _xattn_kernel,
        grid=(bsz * per_seq,),
        in_specs=[pl.BlockSpec((tm, D_MODEL), row), pl.BlockSpec((D_MODEL, D_MODEL), const),
                  pl.BlockSpec((1, mem_len, 2 * D_MODEL), lambda i: (i // per_seq, 0, 0)),
                  pl.BlockSpec((D_MODEL, D_MODEL), const),
                  pl.BlockSpec((1, D_MODEL), const), pl.BlockSpec((1, D_MODEL), const)],
        out_specs=[pl.BlockSpec((tm, D_MODEL), row), pl.BlockSpec((tm, HALF_D), row)],
        out_shape=[jax.ShapeDtypeStruct(h2d.shape, F32), jax.ShapeDtypeStruct((h2d.shape[0], HALF_D), I32)],
        compiler_params=_cparams("parallel"),
        name="cross_attention",
    )(h2d, w_q.astype(BF16), kv, w_o.astype(BF16), g.reshape(1, D_MODEL), b.reshape(1, D_MODEL))


def _router_kernel(h_ref, w_ref, b_ref, tri_ref, idx_ref, gate_ref, rank_ref, cnt_ref, run_scr):
    @pl.when(pl.program_id(0) == 0)
    def _():
        run_scr[...] = jnp.zeros_like(run_scr)

    h = h_ref[...]
    hh = h.astype(BF16)
    hl = (h - hh.astype(F32)).astype(BF16)
    w = w_ref[...]
    wh = w.astype(BF16)
    wl = (w - wh.astype(F32)).astype(BF16)
    logits = _dot(hh, wh) + _dot(hl, wh) + _dot(hh, wl) + b_ref[...]
    lane = lax.broadcasted_iota(I32, logits.shape, 1).astype(F32)
    vals, sels = [], []
    onehot = jnp.zeros(logits.shape, F32)
    for k in range(TOP_K):
        m = jnp.max(logits, axis=1, keepdims=True)
        sel = jnp.min(jnp.where(logits == m, lane, float(LANES)), axis=1, keepdims=True)
        idx_ref[:, k:k + 1] = sel.astype(I32)
        vals.append(m)
        sels.append(sel)
        onehot = onehot + jnp.where(lane == sel, 1.0, 0.0)
        logits = jnp.where(lane == sel, -jnp.inf, logits)
    es = [jnp.exp(v - vals[0]) for v in vals]
    tot = es[0] + es[1] + es[2] + es[3]
    for k in range(TOP_K):
        gate_ref[:, k:k + 1] = es[k] / tot

    earlier = _dot(tri_ref[...], onehot.astype(BF16)) + run_scr[...]
    for k in range(TOP_K):
        rank = jnp.sum(jnp.where(lane == sels[k], earlier, 0.0), axis=1, keepdims=True)
        rank_ref[:, k:k + 1] = rank.astype(I32)
    run = run_scr[...] + jnp.sum(onehot, axis=0, keepdims=True)
    run_scr[...] = run
    cnt_ref[...] = run


def _router(h2d, w_router, b_router):
    n = h2d.shape[0]
    tm = 2 * ROW_TILE
    pad = LANES - N_EXPERTS
    w = jnp.concatenate([w_router, jnp.zeros((D_MODEL, pad), F32)], axis=1)
    b = jnp.concatenate([b_router, jnp.full((pad,), NEG_BIG, F32)]).reshape(1, LANES)
    r = lax.broadcasted_iota(I32, (tm, tm), 0)
    c = lax.broadcasted_iota(I32, (tm, tm), 1)
    tri = jnp.where(c < r, 1.0, 0.0).astype(BF16)
    row = lambda i: (i, 0)
    const = lambda i: (0, 0)
    return pl.pallas_call(
        _router_kernel,
        grid=(n // tm,),
        in_specs=[pl.BlockSpec((tm, D_MODEL), row), pl.BlockSpec((D_MODEL, LANES), const),
                  pl.BlockSpec((1, LANES), const), pl.BlockSpec((tm, tm), const)],
        out_specs=[pl.BlockSpec((tm, TOP_K), row), pl.BlockSpec((tm, TOP_K), row),
                   pl.BlockSpec((tm, TOP_K), row), pl.BlockSpec((1, LANES), const)],
        out_shape=[jax.ShapeDtypeStruct((n, TOP_K), I32), jax.ShapeDtypeStruct((n, TOP_K), F32),
                   jax.ShapeDtypeStruct((n, TOP_K), I32), jax.ShapeDtypeStruct((1, LANES), F32)],
        scratch_shapes=[pltpu.VMEM((1, LANES), F32)],
        compiler_params=_cparams("arbitrary"),
        name="moe_router",
    )(h2d, w, b, tri)


def _gather_rows(src, idx):
    n_out = idx.shape[0]
    width = src.shape[1]
    win = SC_GATHER_SLOT_BYTES // (width * src.dtype.itemsize)
    mesh = plsc.VectorSubcoreMesh(core_axis_name="core", subcore_axis_name="subcore")
    n_workers = mesh.num_cores * mesh.num_subcores
    per_worker = n_out // n_workers
    steps = per_worker // win
    assert per_worker * n_workers == n_out and steps * win == per_worker and steps % 2 == 0

    @functools.partial(
        pl.kernel, out_type=jax.ShapeDtypeStruct((n_out, width), src.dtype), mesh=mesh,
        scratch_types=[pltpu.VMEM((per_worker,), I32), pltpu.VMEM((2, win, width), src.dtype),
                       pltpu.SemaphoreType.DMA, pltpu.SemaphoreType.DMA])
    def gather_kernel(src_hbm, idx_hbm, dst_hbm, idx_v, rows_v, sem0, sem1):
        worker = lax.axis_index("subcore") * mesh.num_cores + lax.axis_index("core")
        base = worker * per_worker
        sems = (sem0, sem1)
        pltpu.sync_copy(idx_hbm.at[pl.ds(base, per_worker)], idx_v)

        def gather(step, slot):
            return pltpu.make_async_copy(src_hbm.at[idx_v.at[pl.ds(step * win, win)]], rows_v.at[slot], sems[slot])

        gather(0, 0).start()

        @pl.loop(0, steps, step=2)
        def _(s):
            for slot in range(2):
                step = s + slot
                gather(step, slot).wait()

                @pl.when(step + 1 < steps)
                def _():
                    gather(step + 1, 1 - slot).start()

                pltpu.sync_copy(rows_v.at[slot], dst_hbm.at[pl.ds(base + step * win, win)])

    return gather_kernel(src, idx)


def _expert_kernel(blk_exp_ref, n_used_ref, x_ref, wgu_ref, bgu_ref, wd_ref, bd_ref, o_ref, wgu_bf, wd_bf):
    i = pl.program_id(0)

    @pl.when(jnp.logical_or(i == 0, blk_exp_ref[i] != blk_exp_ref[jnp.maximum(i - 1, 0)]))
    def _():
        wgu_bf[...] = wgu_ref[0, 0].astype(BF16)
        wd_bf[...] = wd_ref[0, 0].astype(BF16)

    @pl.when(i < n_used_ref[0])
    def _():
        x_lo, x_hi = _unpack_bf16_pairs(x_ref[...])
        hgu = (_dot(x_lo.astype(BF16), wgu_bf[:HALF_D, :]) + _dot(x_hi.astype(BF16), wgu_bf[HALF_D:, :])
               + bgu_ref[0])
        gate = jnp.minimum(hgu[:, :D_EXPERT], SWIGLU_LIMIT)
        up = jnp.clip(hgu[:, D_EXPERT:], -SWIGLU_LIMIT, SWIGLU_LIMIT)
        act = gate * jax.nn.sigmoid(gate * SWIGLU_ALPHA) * (up + 1.0)
        o_ref[...] = _pack_bf16_pairs(_dot(act.astype(BF16), wd_bf[...]) + bd_ref[0])

    @pl.when(i >= n_used_ref[0])
    def _():
        o_ref[...] = jnp.zeros_like(o_ref)


def _expert_mlp(xs, block_exp, n_used, layer, w_gu, b_gu, w_down, b_down):
    n_rows = xs.shape[0]
    bm = MOE_BLOCK_ROWS
    row = lambda i, be, nu: (i, 0)
    exp3 = lambda i, be, nu: (be[i], 0, 0)
    exp4 = lambda i, be, nu: (layer, be[i], 0, 0)
    grid_spec = pltpu.PrefetchScalarGridSpec(
        num_scalar_prefetch=2,
        grid=(n_rows // bm,),
        in_specs=[pl.BlockSpec((bm, HALF_D), row),
                  pl.BlockSpec((1, 1, D_MODEL, 2 * D_EXPERT), exp4), pl.BlockSpec((1, 1, 2 * D_EXPERT), exp3),
                  pl.BlockSpec((1, 1, D_EXPERT, D_MODEL), exp4), pl.BlockSpec((1, 1, D_MODEL), exp3)],
        out_specs=pl.BlockSpec((bm, HALF_D), row),
        scratch_shapes=[pltpu.VMEM((D_MODEL, 2 * D_EXPERT), BF16), pltpu.VMEM((D_EXPERT, D_MODEL), BF16)],
    )
    return pl.pallas_call(
        _expert_kernel,
        grid_spec=grid_spec,
        out_shape=jax.ShapeDtypeStruct((n_rows, HALF_D), I32),
        compiler_params=_cparams("arbitrary"),
        name="moe_experts",
    )(block_exp, n_used, xs, w_gu, b_gu.reshape(N_EXPERTS, 1, 2 * D_EXPERT),
      w_down, b_down.reshape(N_EXPERTS, 1, D_MODEL))


def _combine_kernel(y0_ref, y1_ref, y2_ref, y3_ref, gate_ref, res_ref, g_ref, b_ref, o_ref):
    gates = gate_ref[...]
    acc_lo, acc_hi = None, None
    for k, y_ref in enumerate((y0_ref, y1_ref, y2_ref, y3_ref)):
        lo, hi = _unpack_bf16_pairs(y_ref[...])
        gk = gates[:, k:k + 1]
        acc_lo = lo * gk if acc_lo is None else acc_lo + lo * gk
        acc_hi = hi * gk if acc_hi is None else acc_hi + hi * gk
    acc = jnp.concatenate([acc_lo, acc_hi], axis=1)
    o_ref[...] = _layer_norm_rows(DN_ALPHA * res_ref[...] + acc, g_ref[...], b_ref[...])


def _moe_block(h2d, h_packed, layer, w_router, b_router, w_gu, b_gu, w_down, b_down, g, b):
    n = h2d.shape[0]
    n_slots = n * TOP_K
    bm = MOE_BLOCK_ROWS
    top_idx, gates, rank, totals = _router(h2d, w_router, b_router)

    e_flat = top_idx.reshape(-1)
    assert N_EXPERTS * n_slots < 2 ** 31
    order = jnp.sort(e_flat * n_slots + jnp.arange(n_slots, dtype=I32)) % n_slots
    counts = totals[0, :N_EXPERTS].astype(I32)
    padded = (counts + bm - 1) // bm * bm
    start = jnp.cumsum(counts) - counts
    ends_p = jnp.cumsum(padded)
    pstart = ends_p - padded
    n_rows = n_slots + N_EXPERTS * bm
    n_blocks = n_rows // bm
    r = jnp.arange(n_rows, dtype=I32)
    e_r = jnp.minimum(jnp.searchsorted(ends_p, r, side="right", method="compare_all"), N_EXPERTS - 1).astype(I32)
    j = r - pstart[e_r]
    valid = j < counts[e_r]
    slot_of_row = order[jnp.where(valid, start[e_r] + j, 0)]
    rows_tok = jnp.where(valid, slot_of_row // TOP_K, r % n).astype(I32)
    slot_pos = pstart[e_flat] + rank.reshape(-1)
    block_exp = e_r[::bm]
    n_used = (ends_p[-1] // bm).astype(I32).reshape(1)

    xs = _gather_rows(h_packed, rows_tok)
    ys = _expert_mlp(xs, block_exp, n_used, layer, w_gu, b_gu, w_down, b_down)
    yk = _gather_rows(ys, slot_pos.reshape(n, TOP_K).T.reshape(-1))

    tm = ROW_TILE
    row = lambda i: (i, 0)
    const = lambda i: (0, 0)
    choice = lambda k: (lambda i: (k * (n // tm) + i, 0))
    return pl.pallas_call(
        _combine_kernel,
        grid=(n // tm,),
        in_specs=[pl.BlockSpec((tm, HALF_D), choice(k)) for k in range(TOP_K)] + [
                  pl.BlockSpec((tm, TOP_K), row),
                  pl.BlockSpec((tm, D_MODEL), row), pl.BlockSpec((1, D_MODEL), const),
                  pl.BlockSpec((1, D_MODEL), const)],
        out_specs=pl.BlockSpec((tm, D_MODEL), row),
        out_shape=jax.ShapeDtypeStruct((n, D_MODEL), F32),
        compiler_params=_cparams("parallel"),
        name="moe_combine",
    )(yk, yk, yk, yk, gates, h2d, g.reshape(1, D_MODEL), b.reshape(1, D_MODEL))


def _s5_kernel(u_ref, bre_ref, bim_ref, cre_ref, cim_ref, are_ref, aim_ref, d_ref, y_ref,
               bu_re, bu_im, st_re, st_im, h_re, h_im, uj_scr, yj_scr, *, bsz):
    @pl.when(pl.program_id(0) == 0)
    def _():
        h_re[...] = jnp.zeros_like(h_re)
        h_im[...] = jnp.zeros_like(h_im)

    steps = u_ref.shape[0]
    rows = steps * bsz
    first = lax.broadcasted_iota(I32, (SUBLANES, S5_ST_BLK), 0) < bsz
    for j in range(S5_LANE_BLOCKS):
        cin = slice(j * S5_IN_BLK, (j + 1) * S5_IN_BLK)
        cst = slice(j * S5_ST_BLK, (j + 1) * S5_ST_BLK)
        n_lt = S5_IN_BLK // LANES
        for bb in range(bsz):
            for c in range(n_lt):
                col = bb * D_MODEL + j * S5_IN_BLK + c * LANES
                uj_scr[c, pl.ds(bb, steps, stride=bsz), :] = u_ref[:, col:col + LANES]
        uj = jnp.concatenate([uj_scr[c] for c in range(n_lt)], axis=1)
        ujb = uj.astype(BF16)
        bu_re[...] = _dot(ujb, bre_ref[j])
        bu_im[...] = _dot(ujb, bim_ref[j])
        ar = jnp.broadcast_to(are_ref[:, cst], (SUBLANES, S5_ST_BLK))
        ai = jnp.broadcast_to(aim_ref[:, cst], (SUBLANES, S5_ST_BLK))

        def step(i, carry):
            hr, hi = carry
            r0 = pl.multiple_of(i * SUBLANES, SUBLANES)
            vr = bu_re[pl.ds(r0, SUBLANES), :]
            vi = bu_im[pl.ds(r0, SUBLANES), :]
            h1r = ar * hr - ai * hi + vr
            h1i = ar * hi + ai * hr + vi
            h1rs = pltpu.roll(h1r, bsz, 0)
            h1is = pltpu.roll(h1i, bsz, 0)
            h2r = ar * h1rs - ai * h1is + vr
            h2i = ar * h1is + ai * h1rs + vi
            st_re[pl.ds(r0, SUBLANES), :] = jnp.where(first, h1r, h2r)
            st_im[pl.ds(r0, SUBLANES), :] = jnp.where(first, h1i, h2i)
            return pltpu.roll(h2r, bsz, 0), pltpu.roll(h2i, bsz, 0)

        hr, hi = lax.fori_loop(0, rows // SUBLANES, step, (h_re[:, cst], h_im[:, cst]))
        h_re[:, cst] = hr
        h_im[:, cst] = hi
        yj = _dot(st_re[...].astype(BF16), cre_ref[j]) + _dot(st_im[...].astype(BF16), cim_ref[j])
        yj = jax.nn.gelu(yj + d_ref[:, cin] * uj)
        for c in range(n_lt):
            yj_scr[c] = yj[:, c * LANES:(c + 1) * LANES]
        for bb in range(bsz):
            for c in range(n_lt):
                col = bb * D_MODEL + j * S5_IN_BLK + c * LANES
                y_ref[:, col:col + LANES] = yj_scr[c, pl.ds(bb, steps, stride=bsz), :].astype(BF16)


def _s5_block_diag(w, n_in, n_out):
    gpb = SSM_GROUPS // S5_LANE_BLOCKS
    w4 = w.reshape(S5_LANE_BLOCKS, gpb, n_in, n_out)
    eye = jnp.eye(gpb, dtype=w.dtype)
    return jnp.einsum("jgio,gh->jgiho", w4, eye).reshape(S5_LANE_BLOCKS, gpb * n_in, gpb * n_out)


def _s5_mixer_block(h2d, bsz, seq, w_in, log_dt, lam_re, lam_im, b_re, b_im, c_re, c_im, d, w_out, g, b):
    assert 2 * bsz == SUBLANES, "the scan packs two time steps of bsz rows into one 8-row tile"
    tm = ROW_TILE
    per_seq = seq // tm
    u_t = _matmul(h2d, w_in.astype(BF16), tm=tm, out_dtype=F32, grid=(bsz, per_seq),
                  x_map=lambda bb, i: (bb * per_seq + i, 0), out_map=lambda bb, i: (i, bb),
                  out_shape=(seq, bsz * D_MODEL))

    dt = jnp.exp(log_dt)[:, None]
    mag = jnp.exp(lam_re * dt)
    a_re, a_im = mag * jnp.cos(lam_im * dt), mag * jnp.sin(lam_im * dt)
    den = lam_re * lam_re + lam_im * lam_im
    coef_re = ((a_re - 1.0) * lam_re + a_im * lam_im) / den
    coef_im = (a_im * lam_re - (a_re - 1.0) * lam_im) / den
    bb_re = coef_re[..., None] * b_re - coef_im[..., None] * b_im
    bb_im = coef_re[..., None] * b_im + coef_im[..., None] * b_re
    bre = _s5_block_diag(jnp.swapaxes(bb_re, 1, 2), SSM_GROUP, SSM_STATE).astype(BF16)
    bim = _s5_block_diag(jnp.swapaxes(bb_im, 1, 2), SSM_GROUP, SSM_STATE).astype(BF16)
    cre = _s5_block_diag(jnp.swapaxes(c_re, 1, 2), SSM_STATE, SSM_GROUP).astype(BF16)
    cim = _s5_block_diag(jnp.swapaxes(-c_im, 1, 2), SSM_STATE, SSM_GROUP).astype(BF16)
    n_state = SSM_GROUPS * SSM_STATE

    rows = S5_CHUNK * bsz
    row = lambda c: (c, 0)
    c2 = lambda c: (0, 0)
    c3 = lambda c: (0, 0, 0)
    y_t = pl.pallas_call(
        functools.partial(_s5_kernel, bsz=bsz),
        grid=(seq // S5_CHUNK,),
        in_specs=[pl.BlockSpec((S5_CHUNK, bsz * D_MODEL), row),
                  pl.BlockSpec(bre.shape, c3), pl.BlockSpec(bim.shape, c3),
                  pl.BlockSpec(cre.shape, c3), pl.BlockSpec(cim.shape, c3),
                  pl.BlockSpec((1, n_state), c2), pl.BlockSpec((1, n_state), c2), pl.BlockSpec((1, D_MODEL), c2)],
        out_specs=pl.BlockSpec((S5_CHUNK, bsz * D_MODEL), row),
        out_shape=jax.ShapeDtypeStruct((seq, bsz * D_MODEL), BF16),
        scratch_shapes=([pltpu.VMEM((rows, S5_ST_BLK), F32)] * 4 + [pltpu.VMEM((SUBLANES, n_state), F32)] * 2
                        + [pltpu.VMEM((S5_IN_BLK // LANES, rows, LANES), F32)] * 2),
        compiler_params=_cparams("arbitrary"),
        name="s5_scan",
    )(u_t, bre, bim, cre, cim, a_re.reshape(1, n_state), a_im.reshape(1, n_state), d.reshape(1, D_MODEL))

    return _linear_residual_ln(
        [y_t], [w_out.astype(BF16)], h2d, g, b, tm=tm, glu=True, grid=(bsz, per_seq),
        x_maps=[lambda bb, i: (i, bb)], res_map=lambda bb, i: (bb * per_seq + i, 0))


def _even_mixer_block(h2d, bsz, seq, w_in, qnorm_g, w_uq, w_uq_idx, kidx_g, kidx_b, w_out, g, b):
    qa, qi, ka, va, ki, wi, qb, kb, vb = _even_proj(h2d, bsz, seq, w_in, qnorm_g, w_uq, w_uq_idx, kidx_g, kidx_b)
    o_a = _dsa_attention(qa, qi, wi, ki, ka, va, bsz, seq)
    o_b = _stick_breaking(qb, kb, vb, bsz, seq)
    w_out = w_out.astype(BF16)
    return _linear_residual_ln([o_a, o_b], [w_out[:A_WIDTH], w_out[A_WIDTH:]], h2d, g, b, tm=ROW_TILE)


def kernel(x, mem, ev_w_in, ev_qnorm_g, ev_w_uq, ev_w_uq_idx, ev_kidx_ln_g, ev_kidx_ln_b, ev_w_out, od_w_in, od_log_dt, od_lambda_re, od_lambda_im, od_b_re, od_b_im, od_c_re, od_c_im, od_d, od_w_out, mix_ln_g, mix_ln_b, xa_w_q, xa_w_kv, xa_w_o, xa_ln_g, xa_ln_b, moe_w_router, moe_b_router, moe_w_gu, moe_b_gu, moe_w_down, moe_b_down, ffn_ln_g, ffn_ln_b):
    bsz, seq, _ = x.shape
    h = x.reshape(bsz * seq, D_MODEL)
    for layer in range(DEPTH):
        j = layer // 2
        if layer % 2 == 0:
            h = _even_mixer_block(h, bsz, seq, ev_w_in[j], ev_qnorm_g[j], ev_w_uq[j], ev_w_uq_idx[j],
                                  ev_kidx_ln_g[j], ev_kidx_ln_b[j], ev_w_out[j], mix_ln_g[layer], mix_ln_b[layer])
        else:
            h = _s5_mixer_block(h, bsz, seq, od_w_in[j], od_log_dt[j], od_lambda_re[j], od_lambda_im[j],
                                od_b_re[j], od_b_im[j], od_c_re[j], od_c_im[j], od_d[j], od_w_out[j],
                                mix_ln_g[layer], mix_ln_b[layer])
        h, h_packed = _cross_attention_block(h, mem, bsz, seq, xa_w_q[layer], xa_w_kv[layer], xa_w_o[layer],
                                             xa_ln_g[layer], xa_ln_b[layer])
        h = _moe_block(h, h_packed, layer, moe_w_router[layer], moe_b_router[layer], moe_w_gu, moe_b_gu[layer],
                       moe_w_down, moe_b_down[layer], ffn_ln_g[layer], ffn_ln_b[layer])
    return h.reshape(bsz, seq, D_MODEL)
```
